```python
import math
import jax, jax.numpy as jnp
from jax import lax
import numpy as np

D_MODEL = 1024
BATCH = 2
SEQ = 8192
DEPTH = 1

CHUNK = 64
Q_BLOCK = 128
ATT_WIDTH = D_MODEL // 2
ATT_HEAD_DIM = 64
ATT_HEADS = ATT_WIDTH // (2 * ATT_HEAD_DIM)
LRU_WIDTH = D_MODEL - ATT_WIDTH
LRU_BLOCKS = 8
LRU_BLOCK_DIM = LRU_WIDTH // LRU_BLOCKS
LRU_C = 8.0
CONV_WIDTH = 4
IN_COLS = 3 * ATT_WIDTH + 2 * LRU_WIDTH
N_EXPERTS = 32
TOP_K = 4
D_EXPERT = D_MODEL
SWIGLU_LIMIT = 7.0
SWIGLU_ALPHA = 1.702
EXPERT_BLOCK = 256
ALPHA = (2.0 * DEPTH) ** 0.25
BETA = (8.0 * DEPTH) ** -0.25
LN_EPS = 1e-5
SUBLN_EPS = 1e-5

kernel_name = "hybrid_diffattn_rglru_moe_deepnorm"


def layer_norm(x, g, b):
    xf = x.astype(jnp.float32)
    mu = jnp.mean(xf, axis=-1, keepdims=True)
    var = jnp.mean(jnp.square(xf - mu), axis=-1, keepdims=True)
    y = (xf - mu) * lax.rsqrt(var + LN_EPS) * g.astype(jnp.float32) + b.astype(jnp.float32)
    return y.astype(x.dtype)


def alibi_slopes(n):
    return jnp.asarray([2.0 ** (-8.0 * (i + 1) / n) for i in range(n)], dtype=jnp.float32)


def diff_attention(q, k, v, lam, subln_g, lam_init):
    B, S = q.shape[0], q.shape[1]
    nb = S // Q_BLOCK
    f32 = jnp.float32
    qf = (q.astype(f32) * (ATT_HEAD_DIM ** -0.5)).reshape(
        B, nb, Q_BLOCK, ATT_HEADS, 2, ATT_HEAD_DIM).transpose(1, 0, 3, 4, 2, 5)
    kf = k.astype(f32).transpose(0, 2, 3, 1, 4)
    vf = v.astype(f32).transpose(0, 2, 1, 3)
    slopes = alibi_slopes(ATT_HEADS)
    kpos = jnp.arange(S)

    def block(args):
        qb, bi = args
        qpos = bi * Q_BLOCK + jnp.arange(Q_BLOCK)
        s = jnp.einsum('bhmqd,bhmkd->bhmqk', qb, kf)
        dist = jnp.abs(qpos[:, None] - kpos[None, :]).astype(f32)
        bias = -slopes[:, None, None] * dist
        allowed = (kpos[None, :] // CHUNK) <= (qpos[:, None] // CHUNK)
        s = jnp.where(allowed, s + bias[None, :, None], -jnp.inf)
        p = jax.nn.softmax(s, axis=-1)
        a = p[:, :, 0] - lam * p[:, :, 1]
        return jnp.einsum('bhqk,bhkd->bhqd', a, vf)

    o = lax.map(block, (qf, jnp.arange(nb)))
    o = o.transpose(1, 0, 3, 2, 4).reshape(B, S, ATT_HEADS, 2 * ATT_HEAD_DIM)
    o = o * lax.rsqrt(jnp.mean(jnp.square(o), axis=-1, keepdims=True) + SUBLN_EPS)
    o = o * subln_g.astype(f32) * (1.0 - lam_init)
    return o.reshape(B, S, ATT_WIDTH)


def rg_lru_branch(xb, gb, conv_w, conv_b, w_a, b_a, w_x, b_x, lru_lambda):
    B, S = xb.shape[0], xb.shape[1]
    f32 = jnp.float32
    xc = lax.conv_general_dilated(
        xb, conv_w.astype(xb.dtype)[:, None, :], window_strides=(1,),
        padding=[(CONV_WIDTH - 1, 0)], dimension_numbers=('NWC', 'WIO', 'NWC'),
        feature_group_count=LRU_WIDTH)
    xc = xc.astype(f32) + conv_b.astype(f32)
    xh = xc.reshape(B, S, LRU_BLOCKS, LRU_BLOCK_DIM)
    r = jax.nn.sigmoid(jnp.einsum('bsnc,ncd->bsnd', xh, w_a.astype(f32)) + b_a.astype(f32)).reshape(B, S, LRU_WIDTH)
    i = jax.nn.sigmoid(jnp.einsum('bsnc,ncd->bsnd', xh, w_x.astype(f32)) + b_x.astype(f32)).reshape(B, S, LRU_WIDTH)
    log_a = -LRU_C * r * jax.nn.softplus(-lru_lambda.astype(f32))
    a = jnp.exp(log_a)
    u = jnp.sqrt(-jnp.expm1(2.0 * log_a)) * (i * xc)

    def combine(left, right):
        a1, b1 = left
        a2, b2 = right
        return a1 * a2, a2 * b1 + b2

    _, h = lax.associative_scan(combine, (a, u), axis=1)
    return (h * jax.nn.gelu(gb.astype(f32))).astype(xb.dtype)


def clamped_swiglu(h):
    x_glu = jnp.minimum(h[..., ::2], SWIGLU_LIMIT)
    x_lin = jnp.clip(h[..., 1::2], -SWIGLU_LIMIT, SWIGLU_LIMIT)
    return x_glu * jax.nn.sigmoid(SWIGLU_ALPHA * x_glu) * (x_lin + 1.0)


def moe(x, w_router, b_router, w_gu, b_gu, w_down, b_down):
    B, S, D = x.shape
    T = B * S
    A = T * TOP_K
    xt = x.reshape(T, D)
    logits = (xt @ w_router).astype(jnp.float32) + b_router.astype(jnp.float32)
    top_v, top_e = lax.top_k(logits, TOP_K)
    gates = jax.nn.softmax(top_v, axis=-1)
    flat_e = top_e.reshape(A)
    flat_tok = jnp.repeat(jnp.arange(T, dtype=jnp.int32), TOP_K)
    flat_g = gates.reshape(A)
    order = jnp.argsort(flat_e)
    sorted_e = flat_e[order]
    counts = jnp.bincount(flat_e, length=N_EXPERTS)
    starts = jnp.cumsum(counts) - counts
    padded = (counts + EXPERT_BLOCK - 1) // EXPERT_BLOCK * EXPERT_BLOCK
    pends = jnp.cumsum(padded)
    pstarts = pends - padded
    dest = pstarts[sorted_e] + (jnp.arange(A) - starts[sorted_e])
    num_blocks = (A + N_EXPERTS * (EXPERT_BLOCK - 1) + EXPERT_BLOCK - 1) // EXPERT_BLOCK
    R = num_blocks * EXPERT_BLOCK
    row_tok = jnp.zeros((R,), jnp.int32).at[dest].set(flat_tok[order])
    row_gate = jnp.zeros((R,), jnp.float32).at[dest].set(flat_g[order])
    block_e = jnp.minimum(
        jnp.searchsorted(pends, jnp.arange(num_blocks) * EXPERT_BLOCK, side='right'), N_EXPERTS - 1)

    def expert_block(args):
        tok, g, e = args
        xb = xt[tok]
        h = xb @ w_gu[e] + b_gu[e]
        y = clamped_swiglu(h) @ w_down[e] + b_down[e]
        return y * g[:, None].astype(y.dtype)

    ys = lax.map(expert_block, (row_tok.reshape(num_blocks, EXPERT_BLOCK),
                                row_gate.reshape(num_blocks, EXPERT_BLOCK), block_e))
    out = jax.ops.segment_sum(ys.reshape(R, D), row_tok, num_segments=T)
    return out.reshape(B, S, D).astype(x.dtype)


def setup_inputs(seed: int = 0) -> dict:
    key = jax.random.key(seed)
    ks = jax.random.split(key, 32)
    f32 = jnp.float32
    L = DEPTH
    nrm = lambda k, s, sc: jax.random.normal(k, s, f32) * sc
    x = jax.random.normal(ks[0], (BATCH, SEQ, D_MODEL), f32)
    ln0_g = 1.0 + nrm(ks[1], (D_MODEL,), 0.02)
    ln0_b = nrm(ks[2], (D_MODEL,), 0.02)
    w_in = nrm(ks[3], (L, D_MODEL, IN_COLS), D_MODEL ** -0.5)
    col_scale = jnp.ones((IN_COLS,), f32).at[2 * ATT_WIDTH:3 * ATT_WIDTH].set(BETA)
    w_in = w_in * col_scale
    conv_w = nrm(ks[4], (L, CONV_WIDTH, LRU_WIDTH), CONV_WIDTH ** -0.5)
    conv_b = nrm(ks[5], (L, LRU_WIDTH), 0.01)
    w_rg_a = nrm(ks[6], (L, LRU_BLOCKS, LRU_BLOCK_DIM, LRU_BLOCK_DIM), LRU_BLOCK_DIM ** -0.5)
    b_rg_a = nrm(ks[7], (L, LRU_BLOCKS, LRU_BLOCK_DIM), 0.01)
    w_rg_x = nrm(ks[8], (L, LRU_BLOCKS, LRU_BLOCK_DIM, LRU_BLOCK_DIM), LRU_BLOCK_DIM ** -0.5)
    b_rg_x = nrm(ks[9], (L, LRU_BLOCKS, LRU_BLOCK_DIM), 0.01)
    u = jax.random.uniform(ks[10], (L, LRU_WIDTH), f32, 0.9, 0.999)
    s = u ** (1.0 / LRU_C)
    lru_lambda = jnp.log(s) - jnp.log1p(-s)
    lam_q1 = nrm(ks[11], (L, ATT_HEAD_DIM), 0.1)
    lam_k1 = nrm(ks[12], (L, ATT_HEAD_DIM), 0.1)
    lam_q2 = nrm(ks[13], (L, ATT_HEAD_DIM), 0.1)
    lam_k2 = nrm(ks[14], (L, ATT_HEAD_DIM), 0.1)
    subln_g = 1.0 + nrm(ks[15], (L, 2 * ATT_HEAD_DIM), 0.02)
    w_out = nrm(ks[16], (L, D_MODEL, D_MODEL), BETA * D_MODEL ** -0.5)
    ln1_g = 1.0 + nrm(ks[17], (L, D_MODEL), 0.02)
    ln1_b = nrm(ks[18], (L, D_MODEL), 0.02)
    w_router = nrm(ks[19], (L, D_MODEL, N_EXPERTS), D_MODEL ** -0.5)
    b_router = nrm(ks[20], (L, N_EXPERTS), 0.01)
    w_gu = nrm(ks[21], (L, N_EXPERTS, D_MODEL, 2 * D_EXPERT), BETA * D_MODEL ** -0.5)
    b_gu = nrm(ks[22], (L, N_EXPERTS, 2 * D_EXPERT), 0.01)
    w_down = nrm(ks[23], (L, N_EXPERTS, D_EXPERT, D_MODEL), BETA * D_EXPERT ** -0.5)
    b_down = nrm(ks[24], (L, N_EXPERTS, D_MODEL), 0.01)
    ln2_g = 1.0 + nrm(ks[25], (L, D_MODEL), 0.02)
    ln2_b = nrm(ks[26], (L, D_MODEL), 0.02)
    return {"x": x, "ln0_g": ln0_g, "ln0_b": ln0_b, "w_in": w_in,
            "conv_w": conv_w, "conv_b": conv_b, "w_rg_a": w_rg_a, "b_rg_a": b_rg_a,
            "w_rg_x": w_rg_x, "b_rg_x": b_rg_x, "lru_lambda": lru_lambda,
            "lam_q1": lam_q1, "lam_k1": lam_k1, "lam_q2": lam_q2, "lam_k2": lam_k2,
            "subln_g": subln_g, "w_out": w_out, "ln1_g": ln1_g, "ln1_b": ln1_b,
            "w_router": w_router, "b_router": b_router, "w_gu": w_gu, "b_gu": b_gu,
            "w_down": w_down, "b_down": b_down, "ln2_g": ln2_g, "ln2_b": ln2_b}


def reference(x, ln0_g, ln0_b, w_in, conv_w, conv_b, w_rg_a, b_rg_a, w_rg_x, b_rg_x,
              lru_lambda, lam_q1, lam_k1, lam_q2, lam_k2, subln_g, w_out, ln1_g, ln1_b,
              w_router, b_router, w_gu, b_gu, w_down, b_down, ln2_g, ln2_b):
    B, S, _ = x.shape
    f32 = jnp.float32
    x = layer_norm(x, ln0_g, ln0_b)
    for l in range(DEPTH):
        lam_init = 0.8 - 0.6 * math.exp(-0.3 * l)
        proj = x @ w_in[l]
        q, k, v, xl, gl = jnp.split(
            proj, [ATT_WIDTH, 2 * ATT_WIDTH, 3 * ATT_WIDTH, 3 * ATT_WIDTH + LRU_WIDTH], axis=-1)
        q = q.reshape(B, S, ATT_HEADS, 2, ATT_HEAD_DIM)
        k = k.reshape(B, S, ATT_HEADS, 2, ATT_HEAD_DIM)
        v = v.reshape(B, S, ATT_HEADS, 2 * ATT_HEAD_DIM)
        lam = (jnp.exp(jnp.sum(lam_q1[l].astype(f32) * lam_k1[l].astype(f32)))
               - jnp.exp(jnp.sum(lam_q2[l].astype(f32) * lam_k2[l].astype(f32))) + lam_init)
        att = diff_attention(q, k, v, lam, subln_g[l], lam_init).astype(x.dtype)
        rec = rg_lru_branch(xl, gl, conv_w[l], conv_b[l], w_rg_a[l], b_rg_a[l],
                            w_rg_x[l], b_rg_x[l], lru_lambda[l])
        mixed = jnp.concatenate([att, rec], axis=-1) @ w_out[l]
        x = layer_norm(ALPHA * x + mixed, ln1_g[l], ln1_b[l])
        ffn = moe(x, w_router[l], b_router[l], w_gu[l], b_gu[l], w_down[l], b_down[l])
        x = layer_norm(ALPHA * x + ffn, ln2_g[l], ln2_b[l])
    return x
```

```python
import functools
import math

import jax
import jax.numpy as jnp
from jax import lax
from jax.experimental import pallas as pl
from jax.experimental.pallas import tpu as pltpu

F32 = jnp.float32
BF16 = jnp.bfloat16
I32 = jnp.int32

D_MODEL = 1024
ATT_WIDTH = 512
ATT_HEAD_DIM = 64
ATT_HEADS = 4
VAL_DIM = 2 * ATT_HEAD_DIM
LRU_WIDTH = 512
LRU_BLOCKS = 8
LRU_C = 8.0
CONV_WIDTH = 4
CHUNK_SHIFT = 6
N_EXPERTS = 32
TOP_K = 4
SWIGLU_LIMIT = 7.0
SWIGLU_ALPHA = 1.702
DEPTH = 1
ALPHA = (2.0 * DEPTH) ** 0.25
LN_EPS = 1e-5
SUBLN_EPS = 1e-5
LAM_INIT = 0.8 - 0.6 * math.exp(-0.3 * 0)
NEG_BIG = -1e30

TM_PROJ = 512
TQ = 512
TC_LRU = 512
ROW_BLOCK = 256
TM_COMB = 128

VMEM_LIMIT = 48 * 1024 * 1024


def _layer_norm(x, g, b):
    mu = jnp.mean(x, axis=-1, keepdims=True)
    xc = x - mu
    var = jnp.mean(xc * xc, axis=-1, keepdims=True)
    return xc * lax.rsqrt(var + LN_EPS) * g + b


def _inproj_kernel(x_ref, g_ref, b_ref, w_ref, q_ref, k_ref, v_ref, xl_ref, gl_ref):
    x0 = _layer_norm(x_ref[...], g_ref[...], b_ref[...])
    p = jnp.dot(x0.astype(BF16), w_ref[...], preferred_element_type=F32)
    a = ATT_WIDTH
    q_ref[...] = (p[:, 0:a] * (ATT_HEAD_DIM ** -0.5)).astype(BF16)
    k_ref[...] = p[:, a:2 * a].astype(BF16)
    v_ref[...] = p[:, 2 * a:3 * a].astype(BF16)
    xl_ref[...] = p[:, 3 * a:3 * a + LRU_WIDTH]
    gl_ref[...] = p[:, 3 * a + LRU_WIDTH:]


def _inproj(x2, g, b, w_bf):
    t = x2.shape[0]
    n = w_bf.shape[1]
    tm = TM_PROJ
    row = lambda i: (i, 0)
    fixed = lambda i: (0, 0)
    return pl.pallas_call(
        _inproj_kernel,
        grid=(t // tm,),
        in_specs=[
            pl.BlockSpec((tm, D_MODEL), row),
            pl.BlockSpec((1, D_MODEL), fixed),
            pl.BlockSpec((1, D_MODEL), fixed),
            pl.BlockSpec((D_MODEL, n), fixed),
        ],
        out_specs=[
            pl.BlockSpec((tm, ATT_WIDTH), row),
            pl.BlockSpec((tm, ATT_WIDTH), row),
            pl.BlockSpec((tm, ATT_WIDTH), row),
            pl.BlockSpec((tm, LRU_WIDTH), row),
            pl.BlockSpec((tm, LRU_WIDTH), row),
        ],
        out_shape=[
            jax.ShapeDtypeStruct((t, ATT_WIDTH), BF16),
            jax.ShapeDtypeStruct((t, ATT_WIDTH), BF16),
            jax.ShapeDtypeStruct((t, ATT_WIDTH), BF16),
            jax.ShapeDtypeStruct((t, LRU_WIDTH), F32),
            jax.ShapeDtypeStruct((t, LRU_WIDTH), F32),
        ],
        compiler_params=pltpu.CompilerParams(
            dimension_semantics=("parallel",), vmem_limit_bytes=VMEM_LIMIT),
        name="ln_inproj",
    )(x2, g, b, w_bf)


def _attn_kernel(qi_tab, ki_tab, q_ref, k_ref, v_ref, slope_ref, lam_ref, g_ref,
                 o_ref, m_ref, l_ref, acc_ref):
    s = pl.program_id(2)
    qi = qi_tab[s]
    ki = ki_tab[s]
    tq = q_ref.shape[0]
    tk = k_ref.shape[0]

    @pl.when(ki == 0)
    def _():
        m_ref[...] = jnp.full(m_ref.shape, NEG_BIG, F32)
        l_ref[...] = jnp.zeros(l_ref.shape, F32)
        acc_ref[...] = jnp.zeros(acc_ref.shape, F32)

    q = q_ref[...]
    lane = lax.broadcasted_iota(I32, q.shape, 1)
    zero = jnp.zeros_like(q)
    q2 = jnp.concatenate([jnp.where(lane < ATT_HEAD_DIM, q, zero),
                          jnp.where(lane >= ATT_HEAD_DIM, q, zero)], axis=0)
    sc = lax.dot_general(q2, k_ref[...], (((1,), (1,)), ((), ())),
                         preferred_element_type=F32)

    rows = lax.broadcasted_iota(I32, (tq, tk), 0) + qi * tq
    cols = lax.broadcasted_iota(I32, (tq, tk), 1) + ki * tk
    dist = jnp.abs(rows - cols).astype(F32)
    bias = dist * (-slope_ref[0:1, 0:1])
    allowed = (cols >> CHUNK_SHIFT) <= (rows >> CHUNK_SHIFT)
    bias = jnp.where(allowed, bias, NEG_BIG)
    s3 = sc.reshape(2, tq, tk) + bias[None]

    m_prev = m_ref[...]
    m_new = jnp.maximum(m_prev, jnp.max(s3, axis=-1, keepdims=True))
    alpha = jnp.exp(m_prev - m_new)
    p = jnp.exp(s3 - m_new)
    l_ref[...] = alpha * l_ref[...] + jnp.sum(p, axis=-1, keepdims=True)
    pv = jnp.dot(p.reshape(2 * tq, tk).astype(BF16), v_ref[...],
                 preferred_element_type=F32)
    acc_ref[...] = alpha * acc_ref[...] + pv.reshape(2, tq, VAL_DIM)
    m_ref[...] = m_new

    @pl.when(ki == qi)
    def _():
        lv = lam_ref[...]
        lam = (jnp.exp(jnp.sum(lv[0:1] * lv[1:2], axis=-1, keepdims=True))
               - jnp.exp(jnp.sum(lv[2:3] * lv[3:4], axis=-1, keepdims=True)) + LAM_INIT)
        o2 = acc_ref[...] / l_ref[...]
        o = o2[0] - lam * o2[1]
        o = o * lax.rsqrt(jnp.mean(o * o, axis=-1, keepdims=True) + SUBLN_EPS)
        o_ref[...] = (o * g_ref[...] * (1.0 - LAM_INIT)).astype(o_ref.dtype)


def _attention(q, k, v, slopes, lamvec, subln_g):
    b, s, _ = q.shape
    nq = s // TQ
    pairs = [(i, j) for i in range(nq) for j in range(i + 1)]
    qi_tab = jnp.asarray([p[0] for p in pairs], I32)
    ki_tab = jnp.asarray([p[1] for p in pairs], I32)
    grid_spec = pltpu.PrefetchScalarGridSpec(
        num_scalar_prefetch=2,
        grid=(b, ATT_HEADS, len(pairs)),
        in_specs=[
            pl.BlockSpec((None, TQ, VAL_DIM), lambda bi, h, st, qt, kt: (bi, qt[st], h)),
            pl.BlockSpec((None, TQ, VAL_DIM), lambda bi, h, st, qt, kt: (bi, kt[st], h)),
            pl.BlockSpec((None, TQ, VAL_DIM), lambda bi, h, st, qt, kt: (bi, kt[st], h)),
            pl.BlockSpec((None, 1, 128), lambda bi, h, st, qt, kt: (h, 0, 0)),
            pl.BlockSpec((4, ATT_HEAD_DIM), lambda bi, h, st, qt, kt: (0, 0)),
            pl.BlockSpec((1, VAL_DIM), lambda bi, h, st, qt, kt: (0, 0)),
        ],
        out_specs=pl.BlockSpec((None, TQ, VAL_DIM), lambda bi, h, st, qt, kt: (bi, qt[st], h)),
        scratch_shapes=[
            pltpu.VMEM((2, TQ, 1), F32),
            pltpu.VMEM((2, TQ, 1), F32),
            pltpu.VMEM((2, TQ, VAL_DIM), F32),
        ],
    )
    return pl.pallas_call(
        _attn_kernel,
        grid_spec=grid_spec,
        out_shape=jax.ShapeDtypeStruct((b, s, ATT_WIDTH), BF16),
        compiler_params=pltpu.CompilerParams(
            dimension_semantics=("parallel", "parallel", "arbitrary"),
            vmem_limit_bytes=VMEM_LIMIT),
        name="diff_attention",
    )(qi_tab, ki_tab, q, k, v, slopes, lamvec, subln_g)


def _lru_kernel(xl_ref, gl_ref, cw_ref, cb_ref, wg_ref, bg_ref, lam_ref,
                o_ref, ext_ref, h_ref):
    c = pl.program_id(1)
    tc = xl_ref.shape[0]
    w = LRU_WIDTH

    @pl.when(c == 0)
    def _():
        ext_ref[0:8, :] = jnp.zeros((8, w), F32)
        h_ref[...] = jnp.zeros(h_ref.shape, F32)

    ext_ref[8:8 + tc, :] = xl_ref[...]
    cw = cw_ref[...]
    xc = cb_ref[...] + cw[0:1] * ext_ref[5:5 + tc, :]
    for j in range(1, CONV_WIDTH):
        xc = xc + cw[j:j + 1] * ext_ref[5 + j:5 + j + tc, :]
    tail = ext_ref[tc:tc + 8, :]

    gates = jnp.dot(xc.astype(BF16), wg_ref[...], preferred_element_type=F32) + bg_ref[...]
    r = jax.nn.sigmoid(gates[:, 0:w])
    ig = jax.nn.sigmoid(gates[:, w:2 * w])
    z = -lam_ref[...]
    softplus = jnp.maximum(z, 0.0) + jnp.log1p(jnp.exp(-jnp.abs(z)))
    log_a = (-LRU_C) * r * softplus
    a = jnp.exp(log_a)
    u = jnp.sqrt(-jnp.tanh(log_a) * (a * a + 1.0)) * (ig * xc)

    row = lax.broadcasted_iota(I32, (tc, w), 0)
    d = 1
    while d < tc:
        a_sh = pltpu.roll(a, d, axis=0)
        u_sh = pltpu.roll(u, d, axis=0)
        valid = row >= d
        u = jnp.where(valid, a * u_sh + u, u)
        a = jnp.where(valid, a * a_sh, a)
        d *= 2
    h = u + a * h_ref[0:1, :]

    gl = gl_ref[...]
    gelu = 0.5 * gl * (1.0 + jnp.tanh(math.sqrt(2.0 / math.pi) * (gl + 0.044715 * (gl * gl * gl))))
    o_ref[...] = (h * gelu).astype(o_ref.dtype)

    h_ref[0:1, :] = h[tc - 1:tc, :]
    ext_ref[0:8, :] = tail


def _lru(xl, gl, conv_w, conv_b, w_gate_bf, b_gate, lam):
    b, s, w = xl.shape
    tc = TC_LRU
    blk = lambda bi, c: (bi, c, 0)
    fixed = lambda bi, c: (0, 0)
    return pl.pallas_call(
        _lru_kernel,
        grid=(b, s // tc),
        in_specs=[
            pl.BlockSpec((None, tc, w), blk),
            pl.BlockSpec((None, tc, w), blk),
            pl.BlockSpec((CONV_WIDTH, w), fixed),
            pl.BlockSpec((1, w), fixed),
            pl.BlockSpec((w, 2 * w), fixed),
            pl.BlockSpec((1, 2 * w), fixed),
            pl.BlockSpec((1, w), fixed),
        ],
        out_specs=pl.BlockSpec((None, tc, w), blk),
        out_shape=jax.ShapeDtypeStruct((b, s, w), BF16),
        scratch_shapes=[pltpu.VMEM((tc + 8, w), F32), pltpu.VMEM((8, w), F32)],
        compiler_params=pltpu.CompilerParams(
            dimension_semantics=("parallel", "arbitrary"), vmem_limit_bytes=VMEM_LIMIT),
        name="rg_lru",
    )(xl, gl, conv_w, conv_b, w_gate_bf, b_gate, lam)


def _outproj_kernel(x_ref, g0_ref, b0_ref, att_ref, rec_ref, wo_ref, g1_ref, b1_ref,
                    wr_ref, br_ref, x1_ref, e4_ref, g4_ref, r4_ref, cnt_ref, carry_ref):
    i = pl.program_id(0)
    tm = x_ref.shape[0]

    @pl.when(i == 0)
    def _():
        carry_ref[...] = jnp.zeros(carry_ref.shape, F32)

    x0 = _layer_norm(x_ref[...], g0_ref[...], b0_ref[...])
    mixed = (jnp.dot(att_ref[...], wo_ref[0:ATT_WIDTH, :], preferred_element_type=F32)
             + jnp.dot(rec_ref[...], wo_ref[ATT_WIDTH:, :], preferred_element_type=F32))
    x1 = _layer_norm(ALPHA * x0 + mixed, g1_ref[...], b1_ref[...])
    x1_ref[...] = x1

    logits = jnp.dot(x1, wr_ref[...], preferred_element_type=F32,
                     precision=lax.Precision.HIGHEST) + br_ref[...]
    lane = lax.broadcasted_iota(I32, logits.shape, 1)
    work = logits
    vals, idxs, hots = [], [], []
    for _ in range(TOP_K):
        mx = jnp.max(work, axis=-1, keepdims=True)
        idx = jnp.min(jnp.where(work == mx, lane, N_EXPERTS), axis=-1, keepdims=True)
        hot = lane == idx
        vals.append(mx)
        idxs.append(idx)
        hots.append(hot)
        work = jnp.where(hot, -jnp.inf, work)
    exps = [jnp.exp(v - vals[0]) for v in vals]
    denom = exps[0] + exps[1] + exps[2] + exps[3]

    mask = jnp.zeros(logits.shape, F32)
    for hot in hots:
        mask = mask + hot.astype(F32)
    tr = lax.broadcasted_iota(I32, (tm, tm), 0)
    tc = lax.broadcasted_iota(I32, (tm, tm), 1)
    lower = jnp.where(tc < tr, 1.0, 0.0).astype(BF16)
    excl = jnp.dot(lower, mask.astype(BF16), preferred_element_type=F32) + carry_ref[0:1, :]
    for kk in range(TOP_K):
        e4_ref[:, kk:kk + 1] = idxs[kk]
        g4_ref[:, kk:kk + 1] = exps[kk] / denom
        rank = jnp.sum(jnp.where(hots[kk], excl, 0.0), axis=-1, keepdims=True)
        r4_ref[:, kk:kk + 1] = rank.astype(I32)
    total = carry_ref[0:1, :] + jnp.sum(mask, axis=0, keepdims=True)
    carry_ref[0:1, :] = total
    cnt_ref[...] = total


def _outproj_router(x2, g0, b0, att, rec, wo_bf, g1, b1, w_router, b_router):
    t = x2.shape[0]
    tm = TM_PROJ
    row = lambda i: (i, 0)
    fixed = lambda i: (0, 0)
    return pl.pallas_call(
        _outproj_kernel,
        grid=(t // tm,),
        in_specs=[
            pl.BlockSpec((tm, D_MODEL), row),
            pl.BlockSpec((1, D_MODEL), fixed),
            pl.BlockSpec((1, D_MODEL), fixed),
            pl.BlockSpec((tm, ATT_WIDTH), row),
            pl.BlockSpec((tm, LRU_WIDTH), row),
            pl.BlockSpec((D_MODEL, D_MODEL), fixed),
            pl.BlockSpec((1, D_MODEL), fixed),
            pl.BlockSpec((1, D_MODEL), fixed),
            pl.BlockSpec((D_MODEL, N_EXPERTS), fixed),
            pl.BlockSpec((1, N_EXPERTS), fixed),
        ],
        out_specs=[
            pl.BlockSpec((tm, D_MODEL), row),
            pl.BlockSpec((tm, TOP_K), row),
            pl.BlockSpec((tm, TOP_K), row),
            pl.BlockSpec((tm, TOP_K), row),
            pl.BlockSpec((1, N_EXPERTS), fixed),
        ],
        out_shape=[
            jax.ShapeDtypeStruct((t, D_MODEL), F32),
            jax.ShapeDtypeStruct((t, TOP_K), I32),
            jax.ShapeDtypeStruct((t, TOP_K), F32),
            jax.ShapeDtypeStruct((t, TOP_K), I32),
            jax.ShapeDtypeStruct((1, N_EXPERTS), F32),
        ],
        scratch_shapes=[pltpu.VMEM((8, N_EXPERTS), F32)],
        compiler_params=pltpu.CompilerParams(
            dimension_semantics=("arbitrary",), vmem_limit_bytes=VMEM_LIMIT),
        name="outproj_ln_router",
    )(x2, g0, b0, att, rec, wo_bf, g1, b1, w_router, b_router)


def _row_gather_start(src_hbm, idx_ref, dst_ref, sem, n_rows):
    def body(r, carry):
        tok = idx_ref[0, r]
        pltpu.make_async_copy(src_hbm.at[pl.ds(tok, 1), :],
                              dst_ref.at[pl.ds(r, 1), :], sem).start()
        return carry
    lax.fori_loop(0, n_rows, body, 0, unroll=8)


def _row_gather_wait(src_hbm, dst_ref, sem, n_rows):
    pltpu.make_async_copy(src_hbm.at[pl.ds(0, n_rows), :], dst_ref, sem).wait()


def _dispatch_kernel(idx_cur, idx_nxt, x_hbm, o_ref, buf_ref, sem):
    i = pl.program_id(0)
    n = pl.num_programs(0)
    slot = i % 2
    rows = o_ref.shape[0]

    @pl.when(i == 0)
    def _():
        _row_gather_start(x_hbm, idx_cur, buf_ref.at[0], sem.at[0], rows)

    @pl.when(i + 1 < n)
    def _():
        _row_gather_start(x_hbm, idx_nxt, buf_ref.at[1 - slot], sem.at[1 - slot], rows)

    _row_gather_wait(x_hbm, buf_ref.at[slot], sem.at[slot], rows)
    o_ref[...] = buf_ref[slot].astype(o_ref.dtype)


def _dispatch(x1, row_tok3):
    nb = row_tok3.shape[0]
    rows = row_tok3.shape[2]
    d = x1.shape[1]
    return pl.pallas_call(
        _dispatch_kernel,
        grid=(nb,),
        in_specs=[
            pl.BlockSpec((None, 1, rows), lambda i: (i, 0, 0), memory_space=pltpu.SMEM),
            pl.BlockSpec((None, 1, rows), lambda i: (jnp.minimum(i + 1, nb - 1), 0, 0),
                         memory_space=pltpu.SMEM),
            pl.BlockSpec(memory_space=pl.ANY),
        ],
        out_specs=pl.BlockSpec((rows, d), lambda i: (i, 0)),
        out_shape=jax.ShapeDtypeStruct((nb * rows, d), BF16),
        scratch_shapes=[pltpu.VMEM((2, rows, d), F32), pltpu.SemaphoreType.DMA((2,))],
        compiler_params=pltpu.CompilerParams(
            dimension_semantics=("arbitrary",), vmem_limit_bytes=VMEM_LIMIT),
        name="moe_dispatch",
    )(row_tok3, row_tok3, x1)


def _expert_kernel(be_ref, xs_ref, wg_ref, wl_ref, bg_ref, bl_ref, wd_ref, bd_ref, y_ref):
    x = xs_ref[...]
    hg = jnp.dot(x, wg_ref[...], preferred_element_type=F32) + bg_ref[...]
    hl = jnp.dot(x, wl_ref[...], preferred_element_type=F32) + bl_ref[...]
    xg = jnp.minimum(hg, SWIGLU_LIMIT)
    xl = jnp.clip(hl, -SWIGLU_LIMIT, SWIGLU_LIMIT)
    act = xg * jax.nn.sigmoid(SWIGLU_ALPHA * xg) * (xl + 1.0)
    y_ref[...] = jnp.dot(act.astype(BF16), wd_ref[...], preferred_element_type=F32) + bd_ref[...]


def _experts(block_e, xs, wg, wl, bg, bl, wd, bd):
    r, d = xs.shape
    f = wg.shape[2]
    nb = r // ROW_BLOCK
    wmap = lambda i, be: (be[i], 0, 0)
    grid_spec = pltpu.PrefetchScalarGridSpec(
        num_scalar_prefetch=1,
        grid=(nb,),
        in_specs=[
            pl.BlockSpec((ROW_BLOCK, d), lambda i, be: (i, 0)),
            pl.BlockSpec((None, d, f), wmap),
            pl.BlockSpec((None, d, f), wmap),
            pl.BlockSpec((None, 1, f), wmap),
            pl.BlockSpec((None, 1, f), wmap),
            pl.BlockSpec((None, f, d), wmap),
            pl.BlockSpec((None, 1, d), wmap),
        ],
        out_specs=pl.BlockSpec((ROW_BLOCK, d), lambda i, be: (i, 0)),
    )
    return pl.pallas_call(
        _expert_kernel,
        grid_spec=grid_spec,
        out_shape=jax.ShapeDtypeStruct((r, d), F32),
        compiler_params=pltpu.CompilerParams(
            dimension_semantics=("arbitrary",), vmem_limit_bytes=VMEM_LIMIT),
        name="moe_experts",
    )(block_e, xs, wg, wl, bg, bl, wd, bd)


def _combine_start(ys_hbm, idx_ref, buf_ref, sem, tm):
    def body(t, carry):
        for kk in range(TOP_K):
            row = idx_ref[0, t * TOP_K + kk]
            pltpu.make_async_copy(ys_hbm.at[pl.ds(row, 1), :],
                                  buf_ref.at[kk, pl.ds(t, 1), :], sem).start()
        return carry
    lax.fori_loop(0, tm, body, 0, unroll=2)


def _combine_wait(ys_hbm, buf_ref, sem, tm):
    for kk in range(TOP_K):
        pltpu.make_async_copy(ys_hbm.at[pl.ds(0, tm), :], buf_ref.at[kk], sem).wait()


def _combine_kernel(idx_cur, idx_nxt, x1_ref, g4_ref, g2_ref, b2_ref, ys_hbm,
                    o_ref, buf_ref, sem):
    i = pl.program_id(0)
    n = pl.num_programs(0)
    slot = i % 2
    tm = o_ref.shape[0]

    @pl.when(i == 0)
    def _():
        _combine_start(ys_hbm, idx_cur, buf_ref.at[0], sem.at[0], tm)

    @pl.when(i + 1 < n)
    def _():
        _combine_start(ys_hbm, idx_nxt, buf_ref.at[1 - slot], sem.at[1 - slot], tm)

    _combine_wait(ys_hbm, buf_ref.at[slot], sem.at[slot], tm)
    g4 = g4_ref[...]
    ffn = g4[:, 0:1] * buf_ref[slot, 0]
    for kk in range(1, TOP_K):
        ffn = ffn + g4[:, kk:kk + 1] * buf_ref[slot, kk]
    o_ref[...] = _layer_norm(ALPHA * x1_ref[...] + ffn, g2_ref[...], b2_ref[...])


def _combine(dest3, x1, g4, g2, b2, ys):
    t, d = x1.shape
    tm = TM_COMB
    nt = t // tm
    row = lambda i: (i, 0)
    fixed = lambda i: (0, 0)
    return pl.pallas_call(
        _combine_kernel,
        grid=(nt,),
        in_specs=[
            pl.BlockSpec((None, 1, tm * TOP_K), lambda i: (i, 0, 0), memory_space=pltpu.SMEM),
            pl.BlockSpec((None, 1, tm * TOP_K), lambda i: (jnp.minimum(i + 1, nt - 1), 0, 0),
                         memory_space=pltpu.SMEM),
            pl.BlockSpec((tm, d), row),
            pl.BlockSpec((tm, TOP_K), row),
            pl.BlockSpec((1, d), fixed),
            pl.BlockSpec((1, d), fixed),
            pl.BlockSpec(memory_space=pl.ANY),
        ],
        out_specs=pl.BlockSpec((tm, d), row),
        out_shape=jax.ShapeDtypeStruct((t, d), F32),
        scratch_shapes=[pltpu.VMEM((2, TOP_K, tm, d), F32), pltpu.SemaphoreType.DMA((2,))],
        compiler_params=pltpu.CompilerParams(
            dimension_semantics=("arbitrary",), vmem_limit_bytes=VMEM_LIMIT),
        name="moe_combine_ln",
    )(dest3, dest3, x1, g4, g2, b2, ys)


def kernel(x, ln0_g, ln0_b, w_in, conv_w, conv_b, w_rg_a, b_rg_a, w_rg_x, b_rg_x, lru_lambda, lam_q1, lam_k1, lam_q2, lam_k2, subln_g, w_out, ln1_g, ln1_b, w_router, b_router, w_gu, b_gu, w_down, b_down, ln2_g, ln2_b):
    bsz, seq, d = x.shape
    t = bsz * seq
    x2 = x.reshape(t, d)
    g0 = ln0_g.reshape(1, d)
    b0 = ln0_b.reshape(1, d)
    l = 0

    w_in_bf = w_in[l].astype(BF16)
    w_out_bf = w_out[l].astype(BF16)
    wa = jax.scipy.linalg.block_diag(*[w_rg_a[l, n] for n in range(LRU_BLOCKS)])
    wx = jax.scipy.linalg.block_diag(*[w_rg_x[l, n] for n in range(LRU_BLOCKS)])
    w_gate_bf = jnp.concatenate([wa, wx], axis=1).astype(BF16)
    b_gate = jnp.concatenate([b_rg_a[l].reshape(1, -1), b_rg_x[l].reshape(1, -1)], axis=1)
    lamvec = jnp.stack([lam_q1[l], lam_k1[l], lam_q2[l], lam_k2[l]]).astype(F32)
    slopes = jnp.asarray([2.0 ** (-8.0 * (i + 1) / ATT_HEADS) for i in range(ATT_HEADS)], F32)
    slopes = jnp.broadcast_to(slopes[:, None, None], (ATT_HEADS, 1, 128))
    wg_bf = w_gu[l, :, :, 0::2].astype(BF16)
    wl_bf = w_gu[l, :, :, 1::2].astype(BF16)
    bg = b_gu[l, :, None, 0::2]
    bl = b_gu[l, :, None, 1::2]
    wd_bf = w_down[l].astype(BF16)
    bd = b_down[l][:, None, :]

    q, k, v, xl, gl = _inproj(x2, g0, b0, w_in_bf)
    att = _attention(q.reshape(bsz, seq, -1), k.reshape(bsz, seq, -1), v.reshape(bsz, seq, -1),
                     slopes, lamvec, subln_g[l].reshape(1, -1))
    rec = _lru(xl.reshape(bsz, seq, -1), gl.reshape(bsz, seq, -1), conv_w[l],
               conv_b[l].reshape(1, -1), w_gate_bf, b_gate, lru_lambda[l].reshape(1, -1))
    x1, e4, g4, r4, cnt = _outproj_router(
        x2, g0, b0, att.reshape(t, -1), rec.reshape(t, -1), w_out_bf,
        ln1_g[l].reshape(1, d), ln1_b[l].reshape(1, d), w_router[l], b_router[l].reshape(1, -1))

    n_assign = t * TOP_K
    nb = (n_assign + N_EXPERTS * (ROW_BLOCK - 1)) // ROW_BLOCK + 1
    counts = cnt[0].astype(I32)
    padded = (counts + ROW_BLOCK - 1) // ROW_BLOCK * ROW_BLOCK
    pends = jnp.cumsum(padded)
    pstarts = pends - padded
    dest = pstarts[e4] + r4
    tok = jnp.broadcast_to(jnp.arange(t, dtype=I32)[:, None], (t, TOP_K))
    row_tok = jnp.zeros((nb * ROW_BLOCK,), I32).at[dest.reshape(-1)].set(tok.reshape(-1))
    block_start = jnp.arange(nb, dtype=I32) * ROW_BLOCK
    block_e = jnp.minimum(
        jnp.sum((pends[None, :] <= block_start[:, None]).astype(I32), axis=1), N_EXPERTS - 1)

    xs = _dispatch(x1, row_tok.reshape(nb, 1, ROW_BLOCK))
    ys = _experts(block_e, xs, wg_bf, wl_bf, bg, bl, wd_bf, bd)
    out = _combine(dest.reshape(t // TM_COMB, 1, TM_COMB * TOP_K), x1, g4,
                   ln2_g[l].reshape(1, d), ln2_b[l].reshape(1, d), ys)
    return out.reshape(bsz, seq, d)
```

```python
import math

import jax
import jax.numpy as jnp
from jax import lax
from jax.experimental import pallas as pl
from jax.experimental.pallas import tpu as pltpu

F32 = jnp.float32
BF16 = jnp.bfloat16
I32 = jnp.int32

D_MODEL = 1024
ATT_WIDTH = 512
ATT_HEAD_DIM = 64
ATT_HEADS = 4
VAL_DIM = 2 * ATT_HEAD_DIM
LRU_WIDTH = 512
LRU_BLOCKS = 8
LRU_C = 8.0
CONV_WIDTH = 4
CHUNK_SHIFT = 6
N_EXPERTS = 32
TOP_K = 4
D_EXPERT = 1024
SWIGLU_LIMIT = 7.0
SWIGLU_ALPHA = 1.702
DEPTH = 1
ALPHA = (2.0 * DEPTH) ** 0.25
LN_EPS = 1e-5
SUBLN_EPS = 1e-5
LAM_INIT = 0.8 - 0.6 * math.exp(-0.3 * 0)
NEG_BIG = -1e30

TM_PROJ = 512
TQ = 512
TC_LRU = 512
ROW_BLOCK = 256
TM_COMB = 128
XPOSE_CHUNK = 512

VMEM_LIMIT = 48 * 1024 * 1024
VMEM_LIMIT_EXPERTS = 56 * 1024 * 1024


def _layer_norm(x, g, b):
    mu = jnp.mean(x, axis=-1, keepdims=True)
    xc = x - mu
    var = jnp.mean(xc * xc, axis=-1, keepdims=True)
    return xc * lax.rsqrt(var + LN_EPS) * g + b


def _inproj_kernel(x_ref, g_ref, b_ref, w_ref, qt_ref, k_ref, vt_ref, xl_ref, gl_ref):
    x0 = _layer_norm(x_ref[...], g_ref[...], b_ref[...])
    p = jnp.dot(x0.astype(BF16), w_ref[...], preferred_element_type=F32)
    a = ATT_WIDTH
    qt_ref[...] = (p[:, 0:a] * (ATT_HEAD_DIM ** -0.5)).T.astype(BF16)
    k_ref[...] = p[:, a:2 * a].astype(BF16)
    vt_ref[...] = p[:, 2 * a:3 * a].T.astype(BF16)
    xl_ref[...] = p[:, 3 * a:3 * a + LRU_WIDTH]
    gl_ref[...] = p[:, 3 * a + LRU_WIDTH:]


def _inproj(x, g, b, w_bf):
    bsz, seq, _ = x.shape
    n = w_bf.shape[1]
    tm = TM_PROJ
    row = lambda bi, i: (bi, i, 0)
    col = lambda bi, i: (bi, 0, i)
    fixed = lambda bi, i: (0, 0)
    return pl.pallas_call(
        _inproj_kernel,
        grid=(bsz, seq // tm),
        in_specs=[
            pl.BlockSpec((None, tm, D_MODEL), row),
            pl.BlockSpec((1, D_MODEL), fixed),
            pl.BlockSpec((1, D_MODEL), fixed),
            pl.BlockSpec((D_MODEL, n), fixed),
        ],
        out_specs=[
            pl.BlockSpec((None, ATT_WIDTH, tm), col),
            pl.BlockSpec((None, tm, ATT_WIDTH), row),
            pl.BlockSpec((None, ATT_WIDTH, tm), col),
            pl.BlockSpec((None, tm, LRU_WIDTH), row),
            pl.BlockSpec((None, tm, LRU_WIDTH), row),
        ],
        out_shape=[
            jax.ShapeDtypeStruct((bsz, ATT_WIDTH, seq), BF16),
            jax.ShapeDtypeStruct((bsz, seq, ATT_WIDTH), BF16),
            jax.ShapeDtypeStruct((bsz, ATT_WIDTH, seq), BF16),
            jax.ShapeDtypeStruct((bsz, seq, LRU_WIDTH), F32),
            jax.ShapeDtypeStruct((bsz, seq, LRU_WIDTH), F32),
        ],
        compiler_params=pltpu.CompilerParams(
            dimension_semantics=("parallel", "parallel"), vmem_limit_bytes=VMEM_LIMIT),
        name="ln_inproj",
    )(x, g, b, w_bf)


def _attn_kernel(qi_tab, ki_tab, qt_ref, k_ref, vt_ref, slope_ref, lam_ref, g_ref,
                 o_ref, q2_ref, m_ref, l_ref, acc_ref, rel_ref, diag_ref):
    s = pl.program_id(2)
    qi = qi_tab[s]
    ki = ki_tab[s]
    tq = qt_ref.shape[1]
    tk = k_ref.shape[0]
    neg_slope = -slope_ref[0:1, 0:1]

    @pl.when(s == 0)
    def _():
        c = lax.broadcasted_iota(I32, (tk, tq), 0)
        r = lax.broadcasted_iota(I32, (tk, tq), 1)
        rel_ref[...] = (r - c).astype(F32) * neg_slope
        allowed = (c >> CHUNK_SHIFT) <= (r >> CHUNK_SHIFT)
        diag_ref[...] = jnp.where(allowed, jnp.abs(r - c).astype(F32) * neg_slope, NEG_BIG)

    @pl.when(ki == 0)
    def _():
        m_ref[...] = jnp.full(m_ref.shape, NEG_BIG, F32)
        l_ref[...] = jnp.zeros(l_ref.shape, F32)
        acc_ref[...] = jnp.zeros(acc_ref.shape, F32)
        qt = qt_ref[...]
        frow = lax.broadcasted_iota(I32, qt.shape, 0)
        zero = jnp.zeros_like(qt)
        q2_ref[:, 0:tq] = jnp.where(frow < ATT_HEAD_DIM, qt, zero)
        q2_ref[:, tq:2 * tq] = jnp.where(frow >= ATT_HEAD_DIM, qt, zero)

    def step(bias):
        sc = jnp.dot(k_ref[...], q2_ref[...], preferred_element_type=F32)
        sc = jnp.concatenate([sc[:, 0:tq] + bias, sc[:, tq:2 * tq] + bias], axis=1)
        m_prev = m_ref[...]
        m_new = jnp.maximum(m_prev, jnp.max(sc, axis=0, keepdims=True))
        alpha = jnp.exp(m_prev - m_new)
        p = jnp.exp(sc - m_new)
        l_ref[...] = alpha * l_ref[...] + jnp.sum(p, axis=0, keepdims=True)
        pv = jnp.dot(vt_ref[...], p.astype(BF16), preferred_element_type=F32)
        acc_ref[...] = alpha * acc_ref[...] + pv
        m_ref[...] = m_new

    @pl.when(ki < qi)
    def _():
        off = jnp.full((1, 1), (qi - ki) * tq, I32).astype(F32) * neg_slope
        step(rel_ref[...] + off)

    @pl.when(ki == qi)
    def _():
        step(diag_ref[...])
        lv = lam_ref[...]
        lam = (jnp.exp(jnp.sum(lv[0:1] * lv[1:2], axis=-1, keepdims=True))
               - jnp.exp(jnp.sum(lv[2:3] * lv[3:4], axis=-1, keepdims=True)) + LAM_INIT)
        o2 = acc_ref[...] / l_ref[...]
        o = o2[:, 0:tq] - lam * o2[:, tq:2 * tq]
        o = o * lax.rsqrt(jnp.mean(o * o, axis=0, keepdims=True) + SUBLN_EPS)
        o = o * (g_ref[...] * (1.0 - LAM_INIT))
        o_ref[...] = o.T.astype(o_ref.dtype)


def _attention(qt, k, vt, slopes, lamvec, subln_g_col):
    b, s, _ = k.shape
    nq = s // TQ
    pairs = [(i, j) for i in range(nq) for j in range(i + 1)]
    qi_tab = jnp.asarray([p[0] for p in pairs], I32)
    ki_tab = jnp.asarray([p[1] for p in pairs], I32)
    grid_spec = pltpu.PrefetchScalarGridSpec(
        num_scalar_prefetch=2,
        grid=(b, ATT_HEADS, len(pairs)),
        in_specs=[
            pl.BlockSpec((None, VAL_DIM, TQ), lambda bi, h, st, qt_, kt_: (bi, h, qt_[st])),
            pl.BlockSpec((None, TQ, VAL_DIM), lambda bi, h, st, qt_, kt_: (bi, kt_[st], h)),
            pl.BlockSpec((None, VAL_DIM, TQ), lambda bi, h, st, qt_, kt_: (bi, h, kt_[st])),
            pl.BlockSpec((None, 1, 128), lambda bi, h, st, qt_, kt_: (h, 0, 0)),
            pl.BlockSpec((4, ATT_HEAD_DIM), lambda bi, h, st, qt_, kt_: (0, 0)),
            pl.BlockSpec((VAL_DIM, 1), lambda bi, h, st, qt_, kt_: (0, 0)),
        ],
        out_specs=pl.BlockSpec((None, TQ, VAL_DIM), lambda bi, h, st, qt_, kt_: (bi, qt_[st], h)),
        scratch_shapes=[
            pltpu.VMEM((VAL_DIM, 2 * TQ), BF16),
            pltpu.VMEM((1, 2 * TQ), F32),
            pltpu.VMEM((1, 2 * TQ), F32),
            pltpu.VMEM((VAL_DIM, 2 * TQ), F32),
            pltpu.VMEM((TQ, TQ), F32),
            pltpu.VMEM((TQ, TQ), F32),
        ],
    )
    return pl.pallas_call(
        _attn_kernel,
        grid_spec=grid_spec,
        out_shape=jax.ShapeDtypeStruct((b, s, ATT_WIDTH), BF16),
        compiler_params=pltpu.CompilerParams(
            dimension_semantics=("parallel", "parallel", "arbitrary"),
            vmem_limit_bytes=VMEM_LIMIT),
        name="diff_attention",
    )(qi_tab, ki_tab, qt, k, vt, slopes, lamvec, subln_g_col)


def _lru_kernel(xl_ref, gl_ref, cw_ref, cb_ref, wg_ref, bg_ref, lam_ref,
                o_ref, ext_ref, h_ref):
    c = pl.program_id(1)
    tc = xl_ref.shape[0]
    w = LRU_WIDTH

    @pl.when(c == 0)
    def _():
        ext_ref[0:8, :] = jnp.zeros((8, w), F32)
        h_ref[...] = jnp.zeros(h_ref.shape, F32)

    ext_ref[8:8 + tc, :] = xl_ref[...]
    cw = cw_ref[...]
    xc = cb_ref[...] + cw[0:1] * ext_ref[5:5 + tc, :]
    for j in range(1, CONV_WIDTH):
        xc = xc + cw[j:j + 1] * ext_ref[5 + j:5 + j + tc, :]
    tail = ext_ref[tc:tc + 8, :]

    gates = jnp.dot(xc.astype(BF16), wg_ref[...], preferred_element_type=F32) + bg_ref[...]
    r = jax.nn.sigmoid(gates[:, 0:w])
    ig = jax.nn.sigmoid(gates[:, w:2 * w])
    z = -lam_ref[...]
    softplus = jnp.maximum(z, 0.0) + jnp.log1p(jnp.exp(-jnp.abs(z)))
    log_a = (-LRU_C) * r * softplus
    a = jnp.exp(log_a)
    u = jnp.sqrt(-jnp.tanh(log_a) * (a * a + 1.0)) * (ig * xc)

    row = lax.broadcasted_iota(I32, (tc, w), 0)
    d = 1
    while d < tc:
        a_sh = pltpu.roll(a, d, axis=0)
        u_sh = pltpu.roll(u, d, axis=0)
        valid = row >= d
        u = jnp.where(valid, a * u_sh + u, u)
        a = jnp.where(valid, a * a_sh, a)
        d *= 2
    h = u + a * h_ref[0:1, :]

    gl = gl_ref[...]
    gelu = 0.5 * gl * (1.0 + jnp.tanh(math.sqrt(2.0 / math.pi) * (gl + 0.044715 * (gl * gl * gl))))
    o_ref[...] = (h * gelu).astype(o_ref.dtype)

    h_ref[0:1, :] = h[tc - 1:tc, :]
    ext_ref[0:8, :] = tail


def _lru(xl, gl, conv_w, conv_b, w_gate_bf, b_gate, lam):
    b, s, w = xl.shape
    tc = TC_LRU
    blk = lambda bi, c: (bi, c, 0)
    fixed = lambda bi, c: (0, 0)
    return pl.pallas_call(
        _lru_kernel,
        grid=(b, s // tc),
        in_specs=[
            pl.BlockSpec((None, tc, w), blk),
            pl.BlockSpec((None, tc, w), blk),
            pl.BlockSpec((CONV_WIDTH, w), fixed),
            pl.BlockSpec((1, w), fixed),
            pl.BlockSpec((w, 2 * w), fixed),
            pl.BlockSpec((1, 2 * w), fixed),
            pl.BlockSpec((1, w), fixed),
        ],
        out_specs=pl.BlockSpec((None, tc, w), blk),
        out_shape=jax.ShapeDtypeStruct((b, s, w), BF16),
        scratch_shapes=[pltpu.VMEM((tc + 8, w), F32), pltpu.VMEM((8, w), F32)],
        compiler_params=pltpu.CompilerParams(
            dimension_semantics=("parallel", "arbitrary"), vmem_limit_bytes=VMEM_LIMIT),
        name="rg_lru",
    )(xl, gl, conv_w, conv_b, w_gate_bf, b_gate, lam)


def _outproj_kernel(x_ref, g0_ref, b0_ref, att_ref, rec_ref, wo_ref, g1_ref, b1_ref,
                    wr_ref, br_ref, x1_ref, e4_ref, g4_ref, r4_ref, cnt_ref, carry_ref):
    i = pl.program_id(0)
    tm = x_ref.shape[0]

    @pl.when(i == 0)
    def _():
        carry_ref[...] = jnp.zeros(carry_ref.shape, F32)

    x0 = _layer_norm(x_ref[...], g0_ref[...], b0_ref[...])
    mixed = (jnp.dot(att_ref[...], wo_ref[0:ATT_WIDTH, :], preferred_element_type=F32)
             + jnp.dot(rec_ref[...], wo_ref[ATT_WIDTH:, :], preferred_element_type=F32))
    x1 = _layer_norm(ALPHA * x0 + mixed, g1_ref[...], b1_ref[...])
    x1_ref[...] = x1

    logits = jnp.dot(x1, wr_ref[...], preferred_element_type=F32,
                     precision=lax.Precision.HIGHEST) + br_ref[...]
    lane = lax.broadcasted_iota(I32, logits.shape, 1)
    work = logits
    vals, idxs, hots = [], [], []
    for _ in range(TOP_K):
        mx = jnp.max(work, axis=-1, keepdims=True)
        idx = jnp.min(jnp.where(work == mx, lane, N_EXPERTS), axis=-1, keepdims=True)
        hot = lane == idx
        vals.append(mx)
        idxs.append(idx)
        hots.append(hot)
        work = jnp.where(hot, -jnp.inf, work)
    exps = [jnp.exp(v - vals[0]) for v in vals]
    denom = exps[0] + exps[1] + exps[2] + exps[3]

    mask = jnp.zeros(logits.shape, F32)
    for hot in hots:
        mask = mask + hot.astype(F32)
    tr = lax.broadcasted_iota(I32, (tm, tm), 0)
    tc = lax.broadcasted_iota(I32, (tm, tm), 1)
    lower = jnp.where(tc < tr, 1.0, 0.0).astype(BF16)
    excl = jnp.dot(lower, mask.astype(BF16), preferred_element_type=F32) + carry_ref[0:1, :]
    for kk in range(TOP_K):
        e4_ref[:, kk:kk + 1] = idxs[kk]
        g4_ref[:, kk:kk + 1] = exps[kk] / denom
        rank = jnp.sum(jnp.where(hots[kk], excl, 0.0), axis=-1, keepdims=True)
        r4_ref[:, kk:kk + 1] = rank.astype(I32)
    total = carry_ref[0:1, :] + jnp.sum(mask, axis=0, keepdims=True)
    carry_ref[0:1, :] = total
    cnt_ref[...] = total


def _outproj_router(x2, g0, b0, att, rec, wo_bf, g1, b1, w_router, b_router):
    t = x2.shape[0]
    tm = TM_PROJ
    row = lambda i: (i, 0)
    fixed = lambda i: (0, 0)
    return pl.pallas_call(
        _outproj_kernel,
        grid=(t // tm,),
        in_specs=[
            pl.BlockSpec((tm, D_MODEL), row),
            pl.BlockSpec((1, D_MODEL), fixed),
            pl.BlockSpec((1, D_MODEL), fixed),
            pl.BlockSpec((tm, ATT_WIDTH), row),
            pl.BlockSpec((tm, LRU_WIDTH), row),
            pl.BlockSpec((D_MODEL, D_MODEL), fixed),
            pl.BlockSpec((1, D_MODEL), fixed),
            pl.BlockSpec((1, D_MODEL), fixed),
            pl.BlockSpec((D_MODEL, N_EXPERTS), fixed),
            pl.BlockSpec((1, N_EXPERTS), fixed),
        ],
        out_specs=[
            pl.BlockSpec((tm, D_MODEL), row),
            pl.BlockSpec((tm, TOP_K), row),
            pl.BlockSpec((tm, TOP_K), row),
            pl.BlockSpec((tm, TOP_K), row),
            pl.BlockSpec((1, N_EXPERTS), fixed),
        ],
        out_shape=[
            jax.ShapeDtypeStruct((t, D_MODEL), F32),
            jax.ShapeDtypeStruct((t, TOP_K), I32),
            jax.ShapeDtypeStruct((t, TOP_K), F32),
            jax.ShapeDtypeStruct((t, TOP_K), I32),
            jax.ShapeDtypeStruct((1, N_EXPERTS), F32),
        ],
        scratch_shapes=[pltpu.VMEM((8, N_EXPERTS), F32)],
        compiler_params=pltpu.CompilerParams(
            dimension_semantics=("arbitrary",), vmem_limit_bytes=VMEM_LIMIT),
        name="outproj_ln_router",
    )(x2, g0, b0, att, rec, wo_bf, g1, b1, w_router, b_router)


def _row_gather_start(src_hbm, idx_ref, dst_ref, sem, n_rows):
    def body(r, carry):
        tok = idx_ref[0, r]
        pltpu.make_async_copy(src_hbm.at[pl.ds(tok, 1), :],
                              dst_ref.at[pl.ds(r, 1), :], sem).start()
        return carry
    lax.fori_loop(0, n_rows, body, 0, unroll=8)


def _row_gather_wait(src_hbm, dst_ref, sem, n_rows):
    pltpu.make_async_copy(src_hbm.at[pl.ds(0, n_rows), :], dst_ref, sem).wait()


def _dispatch_kernel(idx_cur, idx_nxt, x_hbm, o_ref, buf_ref, sem):
    i = pl.program_id(0)
    n = pl.num_programs(0)
    slot = i % 2
    rows = o_ref.shape[0]

    @pl.when(i == 0)
    def _():
        _row_gather_start(x_hbm, idx_cur, buf_ref.at[0], sem.at[0], rows)

    @pl.when(i + 1 < n)
    def _():
        _row_gather_start(x_hbm, idx_nxt, buf_ref.at[1 - slot], sem.at[1 - slot], rows)

    _row_gather_wait(x_hbm, buf_ref.at[slot], sem.at[slot], rows)
    o_ref[...] = buf_ref[slot].astype(o_ref.dtype)


def _dispatch(x1, row_tok3):
    nb = row_tok3.shape[0]
    rows = row_tok3.shape[2]
    d = x1.shape[1]
    return pl.pallas_call(
        _dispatch_kernel,
        grid=(nb,),
        in_specs=[
            pl.BlockSpec((None, 1, rows), lambda i: (i, 0, 0), memory_space=pltpu.SMEM),
            pl.BlockSpec((None, 1, rows), lambda i: (jnp.minimum(i + 1, nb - 1), 0, 0),
                         memory_space=pltpu.SMEM),
            pl.BlockSpec(memory_space=pl.ANY),
        ],
        out_specs=pl.BlockSpec((rows, d), lambda i: (i, 0)),
        out_shape=jax.ShapeDtypeStruct((nb * rows, d), BF16),
        scratch_shapes=[pltpu.VMEM((2, rows, d), F32), pltpu.SemaphoreType.DMA((2,))],
        compiler_params=pltpu.CompilerParams(
            dimension_semantics=("arbitrary",), vmem_limit_bytes=VMEM_LIMIT),
        name="moe_dispatch",
    )(row_tok3, row_tok3, x1)


def _expert_kernel(be_ref, xs_ref, wgu_ref, bg_ref, bl_ref, wd_ref, bd_ref, y_ref,
                   wt_ref, wg_ref, wl_ref, wdb_ref):
    i = pl.program_id(0)
    d, f2 = wgu_ref.shape
    f = f2 // 2
    ch = XPOSE_CHUNK
    new_expert = jnp.logical_or(i == 0, be_ref[i] != be_ref[jnp.maximum(i - 1, 0)])

    @pl.when(new_expert)
    def _():
        for j in range(d // 128):
            rows = slice(j * 128, (j + 1) * 128)
            for c in range(f2 // ch):
                wt_ref[j, c * ch:(c + 1) * ch, :] = wgu_ref[rows, c * ch:(c + 1) * ch].T
            for c in range(f // ch):
                gate_rows = wt_ref[j, pl.ds(2 * c * ch, ch, stride=2), :]
                lin_rows = wt_ref[j, pl.ds(2 * c * ch + 1, ch, stride=2), :]
                wg_ref[rows, c * ch:(c + 1) * ch] = gate_rows.T.astype(BF16)
                wl_ref[rows, c * ch:(c + 1) * ch] = lin_rows.T.astype(BF16)
        wdb_ref[...] = wd_ref[...].astype(BF16)

    x = xs_ref[...]
    hg = jnp.dot(x, wg_ref[...], preferred_element_type=F32) + bg_ref[...]
    hl = jnp.dot(x, wl_ref[...], preferred_element_type=F32) + bl_ref[...]
    xg = jnp.minimum(hg, SWIGLU_LIMIT)
    xl = jnp.clip(hl, -SWIGLU_LIMIT, SWIGLU_LIMIT)
    act = xg * jax.nn.sigmoid(SWIGLU_ALPHA * xg) * (xl + 1.0)
    y_ref[...] = jnp.dot(act.astype(BF16), wdb_ref[...], preferred_element_type=F32) + bd_ref[...]


def _experts(block_e, xs, w_gu, bg, bl, w_down, bd):
    r, d = xs.shape
    f2 = w_gu.shape[2]
    f = f2 // 2
    nb = r // ROW_BLOCK
    wmap = lambda i, be: (be[i], 0, 0)
    grid_spec = pltpu.PrefetchScalarGridSpec(
        num_scalar_prefetch=1,
        grid=(nb,),
        in_specs=[
            pl.BlockSpec((ROW_BLOCK, d), lambda i, be: (i, 0)),
            pl.BlockSpec((None, d, f2), wmap),
            pl.BlockSpec((None, 1, f), wmap),
            pl.BlockSpec((None, 1, f), wmap),
            pl.BlockSpec((None, f, d), wmap),
            pl.BlockSpec((None, 1, d), wmap),
        ],
        out_specs=pl.BlockSpec((ROW_BLOCK, d), lambda i, be: (i, 0)),
        scratch_shapes=[
            pltpu.VMEM((d // 128, f2, 128), F32),
            pltpu.VMEM((d, f), BF16),
            pltpu.VMEM((d, f), BF16),
            pltpu.VMEM((f, d), BF16),
        ],
    )
    return pl.pallas_call(
        _expert_kernel,
        grid_spec=grid_spec,
        out_shape=jax.ShapeDtypeStruct((r, d), F32),
        compiler_params=pltpu.CompilerParams(
            dimension_semantics=("arbitrary",), vmem_limit_bytes=VMEM_LIMIT_EXPERTS),
        name="moe_experts",
    )(block_e, xs, w_gu, bg, bl, w_down, bd)


def _combine_start(ys_hbm, idx_ref, buf_ref, sem, tm):
    def body(t, carry):
        for kk in range(TOP_K):
            row = idx_ref[0, t * TOP_K + kk]
            pltpu.make_async_copy(ys_hbm.at[pl.ds(row, 1), :],
                                  buf_ref.at[kk, pl.ds(t, 1), :], sem).start()
        return carry
    lax.fori_loop(0, tm, body, 0, unroll=2)


def _combine_wait(ys_hbm, buf_ref, sem, tm):
    for kk in range(TOP_K):
        pltpu.make_async_copy(ys_hbm.at[pl.ds(0, tm), :], buf_ref.at[kk], sem).wait()


def _combine_kernel(idx_cur, idx_nxt, x1_ref, g4_ref, g2_ref, b2_ref, ys_hbm,
                    o_ref, buf_ref, sem):
    i = pl.program_id(0)
    n = pl.num_programs(0)
    slot = i % 2
    tm = o_ref.shape[0]

    @pl.when(i == 0)
    def _():
        _combine_start(ys_hbm, idx_cur, buf_ref.at[0], sem.at[0], tm)

    @pl.when(i + 1 < n)
    def _():
        _combine_start(ys_hbm, idx_nxt, buf_ref.at[1 - slot], sem.at[1 - slot], tm)

    _combine_wait(ys_hbm, buf_ref.at[slot], sem.at[slot], tm)
    g4 = g4_ref[...]
    ffn = g4[:, 0:1] * buf_ref[slot, 0]
    for kk in range(1, TOP_K):
        ffn = ffn + g4[:, kk:kk + 1] * buf_ref[slot, kk]
    o_ref[...] = _layer_norm(ALPHA * x1_ref[...] + ffn, g2_ref[...], b2_ref[...])


def _combine(dest3, x1, g4, g2, b2, ys):
    t, d = x1.shape
    tm = TM_COMB
    nt = t // tm
    row = lambda i: (i, 0)
    fixed = lambda i: (0, 0)
    return pl.pallas_call(
        _combine_kernel,
        grid=(nt,),
        in_specs=[
            pl.BlockSpec((None, 1, tm * TOP_K), lambda i: (i, 0, 0), memory_space=pltpu.SMEM),
            pl.BlockSpec((None, 1, tm * TOP_K), lambda i: (jnp.minimum(i + 1, nt - 1), 0, 0),
                         memory_space=pltpu.SMEM),
            pl.BlockSpec((tm, d), row),
            pl.BlockSpec((tm, TOP_K), row),
            pl.BlockSpec((1, d), fixed),
            pl.BlockSpec((1, d), fixed),
            pl.BlockSpec(memory_space=pl.ANY),
        ],
        out_specs=pl.BlockSpec((tm, d), row),
        out_shape=jax.ShapeDtypeStruct((t, d), F32),
        scratch_shapes=[pltpu.VMEM((2, TOP_K, tm, d), F32), pltpu.SemaphoreType.DMA((2,))],
        compiler_params=pltpu.CompilerParams(
            dimension_semantics=("arbitrary",), vmem_limit_bytes=VMEM_LIMIT),
        name="moe_combine_ln",
    )(dest3, dest3, x1, g4, g2, b2, ys)


def kernel(x, ln0_g, ln0_b, w_in, conv_w, conv_b, w_rg_a, b_rg_a, w_rg_x, b_rg_x, lru_lambda, lam_q1, lam_k1, lam_q2, lam_k2, subln_g, w_out, ln1_g, ln1_b, w_router, b_router, w_gu, b_gu, w_down, b_down, ln2_g, ln2_b):
    bsz, seq, d = x.shape
    t = bsz * seq
    x2 = x.reshape(t, d)
    g0 = ln0_g.reshape(1, d)
    b0 = ln0_b.reshape(1, d)
    l = 0

    w_in_bf = w_in[l].astype(BF16)
    w_out_bf = w_out[l].astype(BF16)
    wa = jax.scipy.linalg.block_diag(*[w_rg_a[l, n] for n in range(LRU_BLOCKS)])
    wx = jax.scipy.linalg.block_diag(*[w_rg_x[l, n] for n in range(LRU_BLOCKS)])
    w_gate_bf = jnp.concatenate([wa, wx], axis=1).astype(BF16)
    b_gate = jnp.concatenate([b_rg_a[l].reshape(1, -1), b_rg_x[l].reshape(1, -1)], axis=1)
    lamvec = jnp.stack([lam_q1[l], lam_k1[l], lam_q2[l], lam_k2[l]]).astype(F32)
    slopes = jnp.asarray([2.0 ** (-8.0 * (i + 1) / ATT_HEADS) for i in range(ATT_HEADS)], F32)
    slopes = jnp.broadcast_to(slopes[:, None, None], (ATT_HEADS, 1, 128))
    bg = b_gu[l, :, None, 0::2]
    bl = b_gu[l, :, None, 1::2]
    bd = b_down[l][:, None, :]

    qt, k, vt, xl, gl = _inproj(x, g0, b0, w_in_bf)
    att = _attention(qt, k, vt, slopes, lamvec, subln_g[l].reshape(-1, 1))
    rec = _lru(xl, gl, conv_w[l], conv_b[l].reshape(1, -1), w_gate_bf, b_gate,
               lru_lambda[l].reshape(1, -1))
    x1, e4, g4, r4, cnt = _outproj_router(
        x2, g0, b0, att.reshape(t, -1), rec.reshape(t, -1), w_out_bf,
        ln1_g[l].reshape(1, d), ln1_b[l].reshape(1, d), w_router[l], b_router[l].reshape(1, -1))

    n_assign = t * TOP_K
    nb = (n_assign + N_EXPERTS * (ROW_BLOCK - 1)) // ROW_BLOCK + 1
    counts = cnt[0].astype(I32)
    padded = (counts + ROW_BLOCK - 1) // ROW_BLOCK * ROW_BLOCK
    pends = jnp.cumsum(padded)
    pstarts = pends - padded
    dest = pstarts[e4] + r4
    tok = jnp.broadcast_to(jnp.arange(t, dtype=I32)[:, None], (t, TOP_K))
    row_tok = jnp.zeros((nb * ROW_BLOCK,), I32).at[dest.reshape(-1)].set(tok.reshape(-1))
    block_start = jnp.arange(nb, dtype=I32) * ROW_BLOCK
    block_e = jnp.minimum(
        jnp.sum((pends[None, :] <= block_start[:, None]).astype(I32), axis=1), N_EXPERTS - 1)

    xs = _dispatch(x1, row_tok.reshape(nb, 1, ROW_BLOCK))
    ys = _experts(block_e, xs, w_gu[l], bg, bl, w_down[l], bd)
    out = _combine(dest.reshape(t // TM_COMB, 1, TM_COMB * TOP_K), x1, g4,
                   ln2_g[l].reshape(1, d), ln2_b[l].reshape(1, d), ys)
    return out.reshape(bsz, seq, d)
```

```python
import math

import jax
import jax.numpy as jnp
from jax import lax
from jax.experimental import pallas as pl
from jax.experimental.pallas import tpu as pltpu

F32 = jnp.float32
BF16 = jnp.bfloat16
I32 = jnp.int32

D_MODEL = 1024
ATT_WIDTH = 512
ATT_HEAD_DIM = 64
ATT_HEADS = 4
VAL_DIM = 2 * ATT_HEAD_DIM
LRU_WIDTH = 512
LRU_BLOCKS = 8
LRU_C = 8.0
CONV_WIDTH = 4
CHUNK_SHIFT = 6
N_EXPERTS = 32
TOP_K = 4
D_EXPERT = 1024
SWIGLU_LIMIT = 7.0
SWIGLU_ALPHA = 1.702
DEPTH = 1
ALPHA = (2.0 * DEPTH) ** 0.25
LN_EPS = 1e-5
SUBLN_EPS = 1e-5
LAM_INIT = 0.8 - 0.6 * math.exp(-0.3 * 0)
NEG_BIG = -1e30

TM_PROJ = 512
TQ = 512
TC_LRU = 512
ROW_BLOCK = 256
TM_COMB = 128
TM_DISP = 512
XPOSE_CHUNK = 512

VMEM_LIMIT = 48 * 1024 * 1024
VMEM_LIMIT_EXPERTS = 56 * 1024 * 1024


LANE_TILES = D_MODEL // 128


def _store_row_tiles(ref, val):
    n = val.shape[0]
    for j in range(LANE_TILES):
        ref[pl.ds(j, n, stride=LANE_TILES), :] = val[:, j * 128:(j + 1) * 128]


def _load_row_tiles(ref, n):
    return jnp.concatenate(
        [ref[pl.ds(j, n, stride=LANE_TILES), :] for j in range(LANE_TILES)], axis=1)


def _layer_norm(x, g, b):
    mu = jnp.mean(x, axis=-1, keepdims=True)
    xc = x - mu
    var = jnp.mean(xc * xc, axis=-1, keepdims=True)
    return xc * lax.rsqrt(var + LN_EPS) * g + b


def _inproj_kernel(x_ref, g_ref, b_ref, w_ref, qt_ref, k_ref, vt_ref, xl_ref, gl_ref):
    x0 = _layer_norm(x_ref[...], g_ref[...], b_ref[...])
    p = jnp.dot(x0.astype(BF16), w_ref[...], preferred_element_type=F32)
    a = ATT_WIDTH
    qt_ref[...] = (p[:, 0:a] * (ATT_HEAD_DIM ** -0.5)).T.astype(BF16)
    k_ref[...] = p[:, a:2 * a].astype(BF16)
    vt_ref[...] = p[:, 2 * a:3 * a].T.astype(BF16)
    xl_ref[...] = p[:, 3 * a:3 * a + LRU_WIDTH]
    gl_ref[...] = p[:, 3 * a + LRU_WIDTH:]


def _inproj(x, g, b, w_bf):
    bsz, seq, _ = x.shape
    n = w_bf.shape[1]
    tm = TM_PROJ
    row = lambda bi, i: (bi, i, 0)
    col = lambda bi, i: (bi, 0, i)
    fixed = lambda bi, i: (0, 0)
    return pl.pallas_call(
        _inproj_kernel,
        grid=(bsz, seq // tm),
        in_specs=[
            pl.BlockSpec((None, tm, D_MODEL), row),
            pl.BlockSpec((1, D_MODEL), fixed),
            pl.BlockSpec((1, D_MODEL), fixed),
            pl.BlockSpec((D_MODEL, n), fixed),
        ],
        out_specs=[
            pl.BlockSpec((None, ATT_WIDTH, tm), col),
            pl.BlockSpec((None, tm, ATT_WIDTH), row),
            pl.BlockSpec((None, ATT_WIDTH, tm), col),
            pl.BlockSpec((None, tm, LRU_WIDTH), row),
            pl.BlockSpec((None, tm, LRU_WIDTH), row),
        ],
        out_shape=[
            jax.ShapeDtypeStruct((bsz, ATT_WIDTH, seq), BF16),
            jax.ShapeDtypeStruct((bsz, seq, ATT_WIDTH), BF16),
            jax.ShapeDtypeStruct((bsz, ATT_WIDTH, seq), BF16),
            jax.ShapeDtypeStruct((bsz, seq, LRU_WIDTH), F32),
            jax.ShapeDtypeStruct((bsz, seq, LRU_WIDTH), F32),
        ],
        compiler_params=pltpu.CompilerParams(
            dimension_semantics=("parallel", "parallel"), vmem_limit_bytes=VMEM_LIMIT),
        name="ln_inproj",
    )(x, g, b, w_bf)


def _attn_kernel(qi_tab, ki_tab, qt_ref, k_ref, vt_ref, slope_ref, lam_ref, g_ref,
                 o_ref, q2_ref, m_ref, l_ref, acc_ref, rel_ref, diag_ref):
    s = pl.program_id(2)
    qi = qi_tab[s]
    ki = ki_tab[s]
    tq = qt_ref.shape[1]
    tk = k_ref.shape[0]
    neg_slope = -slope_ref[0:1, 0:1]

    @pl.when(s == 0)
    def _():
        c = lax.broadcasted_iota(I32, (tk, tq), 0)
        r = lax.broadcasted_iota(I32, (tk, tq), 1)
        rel_ref[...] = (r - c).astype(F32) * neg_slope
        allowed = (c >> CHUNK_SHIFT) <= (r >> CHUNK_SHIFT)
        diag_ref[...] = jnp.where(allowed, jnp.abs(r - c).astype(F32) * neg_slope, NEG_BIG)

    @pl.when(ki == 0)
    def _():
        m_ref[...] = jnp.full(m_ref.shape, NEG_BIG, F32)
        l_ref[...] = jnp.zeros(l_ref.shape, F32)
        acc_ref[...] = jnp.zeros(acc_ref.shape, F32)
        qt = qt_ref[...]
        frow = lax.broadcasted_iota(I32, qt.shape, 0)
        zero = jnp.zeros_like(qt)
        q2_ref[:, 0:tq] = jnp.where(frow < ATT_HEAD_DIM, qt, zero)
        q2_ref[:, tq:2 * tq] = jnp.where(frow >= ATT_HEAD_DIM, qt, zero)

    def step(bias):
        sc = jnp.dot(k_ref[...], q2_ref[...], preferred_element_type=F32)
        sc = jnp.concatenate([sc[:, 0:tq] + bias, sc[:, tq:2 * tq] + bias], axis=1)
        m_prev = m_ref[...]
        m_new = jnp.maximum(m_prev, jnp.max(sc, axis=0, keepdims=True))
        alpha = jnp.exp(m_prev - m_new)
        p = jnp.exp(sc - m_new)
        l_ref[...] = alpha * l_ref[...] + jnp.sum(p, axis=0, keepdims=True)
        pv = jnp.dot(vt_ref[...], p.astype(BF16), preferred_element_type=F32)
        acc_ref[...] = alpha * acc_ref[...] + pv
        m_ref[...] = m_new

    @pl.when(ki < qi)
    def _():
        off = jnp.full((1, 1), (qi - ki) * tq, I32).astype(F32) * neg_slope
        step(rel_ref[...] + off)

    @pl.when(ki == qi)
    def _():
        step(diag_ref[...])
        lv = lam_ref[...]
        lam = (jnp.exp(jnp.sum(lv[0:1] * lv[1:2], axis=-1, keepdims=True))
               - jnp.exp(jnp.sum(lv[2:3] * lv[3:4], axis=-1, keepdims=True)) + LAM_INIT)
        o2 = acc_ref[...] / l_ref[...]
        o = o2[:, 0:tq] - lam * o2[:, tq:2 * tq]
        o = o * lax.rsqrt(jnp.mean(o * o, axis=0, keepdims=True) + SUBLN_EPS)
        o = o * (g_ref[...] * (1.0 - LAM_INIT))
        o_ref[...] = o.T.astype(o_ref.dtype)


def _attention(qt, k, vt, slopes, lamvec, subln_g_col):
    b, s, _ = k.shape
    nq = s // TQ
    pairs = [(i, j) for i in range(nq) for j in range(i + 1)]
    qi_tab = jnp.asarray([p[0] for p in pairs], I32)
    ki_tab = jnp.asarray([p[1] for p in pairs], I32)
    grid_spec = pltpu.PrefetchScalarGridSpec(
        num_scalar_prefetch=2,
        grid=(b, ATT_HEADS, len(pairs)),
        in_specs=[
            pl.BlockSpec((None, VAL_DIM, TQ), lambda bi, h, st, qt_, kt_: (bi, h, qt_[st])),
            pl.BlockSpec((None, TQ, VAL_DIM), lambda bi, h, st, qt_, kt_: (bi, kt_[st], h)),
            pl.BlockSpec((None, VAL_DIM, TQ), lambda bi, h, st, qt_, kt_: (bi, h, kt_[st])),
            pl.BlockSpec((None, 1, 128), lambda bi, h, st, qt_, kt_: (h, 0, 0)),
            pl.BlockSpec((4, ATT_HEAD_DIM), lambda bi, h, st, qt_, kt_: (0, 0)),
            pl.BlockSpec((VAL_DIM, 1), lambda bi, h, st, qt_, kt_: (0, 0)),
        ],
        out_specs=pl.BlockSpec((None, TQ, VAL_DIM), lambda bi, h, st, qt_, kt_: (bi, qt_[st], h)),
        scratch_shapes=[
            pltpu.VMEM((VAL_DIM, 2 * TQ), BF16),
            pltpu.VMEM((1, 2 * TQ), F32),
            pltpu.VMEM((1, 2 * TQ), F32),
            pltpu.VMEM((VAL_DIM, 2 * TQ), F32),
            pltpu.VMEM((TQ, TQ), F32),
            pltpu.VMEM((TQ, TQ), F32),
        ],
    )
    return pl.pallas_call(
        _attn_kernel,
        grid_spec=grid_spec,
        out_shape=jax.ShapeDtypeStruct((b, s, ATT_WIDTH), BF16),
        compiler_params=pltpu.CompilerParams(
            dimension_semantics=("parallel", "parallel", "arbitrary"),
            vmem_limit_bytes=VMEM_LIMIT),
        name="diff_attention",
    )(qi_tab, ki_tab, qt, k, vt, slopes, lamvec, subln_g_col)


def _lru_kernel(xl_ref, gl_ref, cw_ref, cb_ref, wg_ref, bg_ref, lam_ref,
                o_ref, ext_ref, h_ref):
    c = pl.program_id(1)
    tc = xl_ref.shape[0]
    w = LRU_WIDTH

    @pl.when(c == 0)
    def _():
        ext_ref[0:8, :] = jnp.zeros((8, w), F32)
        h_ref[...] = jnp.zeros(h_ref.shape, F32)

    ext_ref[8:8 + tc, :] = xl_ref[...]
    cw = cw_ref[...]
    xc = cb_ref[...] + cw[0:1] * ext_ref[5:5 + tc, :]
    for j in range(1, CONV_WIDTH):
        xc = xc + cw[j:j + 1] * ext_ref[5 + j:5 + j + tc, :]
    tail = ext_ref[tc:tc + 8, :]

    gates = jnp.dot(xc.astype(BF16), wg_ref[...], preferred_element_type=F32) + bg_ref[...]
    r = jax.nn.sigmoid(gates[:, 0:w])
    ig = jax.nn.sigmoid(gates[:, w:2 * w])
    z = -lam_ref[...]
    softplus = jnp.maximum(z, 0.0) + jnp.log1p(jnp.exp(-jnp.abs(z)))
    log_a = (-LRU_C) * r * softplus
    a = jnp.exp(log_a)
    u = jnp.sqrt(-jnp.tanh(log_a) * (a * a + 1.0)) * (ig * xc)

    row = lax.broadcasted_iota(I32, (tc, w), 0)
    d = 1
    while d < tc:
        a_sh = pltpu.roll(a, d, axis=0)
        u_sh = pltpu.roll(u, d, axis=0)
        valid = row >= d
        u = jnp.where(valid, a * u_sh + u, u)
        a = jnp.where(valid, a * a_sh, a)
        d *= 2
    h = u + a * h_ref[0:1, :]

    gl = gl_ref[...]
    gelu = 0.5 * gl * (1.0 + jnp.tanh(math.sqrt(2.0 / math.pi) * (gl + 0.044715 * (gl * gl * gl))))
    o_ref[...] = (h * gelu).astype(o_ref.dtype)

    h_ref[0:1, :] = h[tc - 1:tc, :]
    ext_ref[0:8, :] = tail


def _lru(xl, gl, conv_w, conv_b, w_gate_bf, b_gate, lam):
    b, s, w = xl.shape
    tc = TC_LRU
    blk = lambda bi, c: (bi, c, 0)
    fixed = lambda bi, c: (0, 0)
    return pl.pallas_call(
        _lru_kernel,
        grid=(b, s // tc),
        in_specs=[
            pl.BlockSpec((None, tc, w), blk),
            pl.BlockSpec((None, tc, w), blk),
            pl.BlockSpec((CONV_WIDTH, w), fixed),
            pl.BlockSpec((1, w), fixed),
            pl.BlockSpec((w, 2 * w), fixed),
            pl.BlockSpec((1, 2 * w), fixed),
            pl.BlockSpec((1, w), fixed),
        ],
        out_specs=pl.BlockSpec((None, tc, w), blk),
        out_shape=jax.ShapeDtypeStruct((b, s, w), BF16),
        scratch_shapes=[pltpu.VMEM((tc + 8, w), F32), pltpu.VMEM((8, w), F32)],
        compiler_params=pltpu.CompilerParams(
            dimension_semantics=("parallel", "arbitrary"), vmem_limit_bytes=VMEM_LIMIT),
        name="rg_lru",
    )(xl, gl, conv_w, conv_b, w_gate_bf, b_gate, lam)


def _outproj_kernel(x_ref, g0_ref, b0_ref, att_ref, rec_ref, wo_ref, g1_ref, b1_ref,
                    wr_ref, br_ref, x1_ref, e4_ref, g4_ref, r4_ref, cnt_ref, carry_ref):
    i = pl.program_id(0)
    tm = x_ref.shape[0]

    @pl.when(i == 0)
    def _():
        carry_ref[...] = jnp.zeros(carry_ref.shape, F32)

    x0 = _layer_norm(x_ref[...], g0_ref[...], b0_ref[...])
    mixed = (jnp.dot(att_ref[...], wo_ref[0:ATT_WIDTH, :], preferred_element_type=F32)
             + jnp.dot(rec_ref[...], wo_ref[ATT_WIDTH:, :], preferred_element_type=F32))
    x1 = _layer_norm(ALPHA * x0 + mixed, g1_ref[...], b1_ref[...])
    _store_row_tiles(x1_ref, x1)

    logits = jnp.dot(x1, wr_ref[...], preferred_element_type=F32,
                     precision=lax.Precision.HIGHEST) + br_ref[...]
    lane = lax.broadcasted_iota(I32, logits.shape, 1)
    work = logits
    vals, idxs, hots = [], [], []
    for _ in range(TOP_K):
        mx = jnp.max(work, axis=-1, keepdims=True)
        idx = jnp.min(jnp.where(work == mx, lane, N_EXPERTS), axis=-1, keepdims=True)
        hot = lane == idx
        vals.append(mx)
        idxs.append(idx)
        hots.append(hot)
        work = jnp.where(hot, -jnp.inf, work)
    exps = [jnp.exp(v - vals[0]) for v in vals]
    denom = exps[0] + exps[1] + exps[2] + exps[3]

    mask = jnp.zeros(logits.shape, F32)
    for hot in hots:
        mask = mask + hot.astype(F32)
    tr = lax.broadcasted_iota(I32, (tm, tm), 0)
    tc = lax.broadcasted_iota(I32, (tm, tm), 1)
    lower = jnp.where(tc < tr, 1.0, 0.0).astype(BF16)
    excl = jnp.dot(lower, mask.astype(BF16), preferred_element_type=F32) + carry_ref[0:1, :]
    for kk in range(TOP_K):
        e4_ref[:, kk:kk + 1] = idxs[kk]
        g4_ref[:, kk:kk + 1] = exps[kk] / denom
        rank = jnp.sum(jnp.where(hots[kk], excl, 0.0), axis=-1, keepdims=True)
        r4_ref[:, kk:kk + 1] = rank.astype(I32)
    total = carry_ref[0:1, :] + jnp.sum(mask, axis=0, keepdims=True)
    carry_ref[0:1, :] = total
    cnt_ref[...] = total


def _outproj_router(x2, g0, b0, att, rec, wo_bf, g1, b1, w_router, b_router):
    t = x2.shape[0]
    tm = TM_PROJ
    row = lambda i: (i, 0)
    fixed = lambda i: (0, 0)
    return pl.pallas_call(
        _outproj_kernel,
        grid=(t // tm,),
        in_specs=[
            pl.BlockSpec((tm, D_MODEL), row),
            pl.BlockSpec((1, D_MODEL), fixed),
            pl.BlockSpec((1, D_MODEL), fixed),
            pl.BlockSpec((tm, ATT_WIDTH), row),
            pl.BlockSpec((tm, LRU_WIDTH), row),
            pl.BlockSpec((D_MODEL, D_MODEL), fixed),
            pl.BlockSpec((1, D_MODEL), fixed),
            pl.BlockSpec((1, D_MODEL), fixed),
            pl.BlockSpec((D_MODEL, N_EXPERTS), fixed),
            pl.BlockSpec((1, N_EXPERTS), fixed),
        ],
        out_specs=[
            pl.BlockSpec((tm * LANE_TILES, 128), row),
            pl.BlockSpec((tm, TOP_K), row),
            pl.BlockSpec((tm, TOP_K), row),
            pl.BlockSpec((tm, TOP_K), row),
            pl.BlockSpec((1, N_EXPERTS), fixed),
        ],
        out_shape=[
            jax.ShapeDtypeStruct((t * LANE_TILES, 128), F32),
            jax.ShapeDtypeStruct((t, TOP_K), I32),
            jax.ShapeDtypeStruct((t, TOP_K), F32),
            jax.ShapeDtypeStruct((t, TOP_K), I32),
            jax.ShapeDtypeStruct((1, N_EXPERTS), F32),
        ],
        scratch_shapes=[pltpu.VMEM((8, N_EXPERTS), F32)],
        compiler_params=pltpu.CompilerParams(
            dimension_semantics=("arbitrary",), vmem_limit_bytes=VMEM_LIMIT),
        name="outproj_ln_router",
    )(x2, g0, b0, att, rec, wo_bf, g1, b1, w_router, b_router)


def _tile_rows(row):
    return pl.ds(pl.multiple_of(row * LANE_TILES, LANE_TILES), LANE_TILES)


def _dispatch_kernel(fill_ref, dest_ref, x_ref, o_hbm, zero_ref, sem, fill_sem):
    i = pl.program_id(0)
    tm = x_ref.shape[0] // LANE_TILES

    @pl.when(i == 0)
    def _():
        zero_ref[...] = jnp.zeros(zero_ref.shape, F32)

        def fill_copy(e):
            start = pl.multiple_of(fill_ref[e] * LANE_TILES, LANE_TILES)
            return pltpu.make_async_copy(
                zero_ref, o_hbm.at[pl.ds(start, ROW_BLOCK * LANE_TILES), :], fill_sem)

        for e in range(2 * N_EXPERTS):
            @pl.when(fill_ref[e] >= 0)
            def _():
                fill_copy(e).start()
        for e in range(2 * N_EXPERTS):
            @pl.when(fill_ref[e] >= 0)
            def _():
                fill_copy(e).wait()

    def body(t, carry):
        src = x_ref.at[_tile_rows(t), :]
        for kk in range(TOP_K):
            pltpu.make_async_copy(
                src, o_hbm.at[_tile_rows(dest_ref[0, t * TOP_K + kk]), :], sem).start()
        return carry
    lax.fori_loop(0, tm, body, 0, unroll=2)
    for kk in range(TOP_K):
        pltpu.make_async_copy(x_ref, o_hbm.at[pl.ds(0, tm * LANE_TILES), :], sem).wait()


def _dispatch(fill_row, dest3, x1r, n_rows):
    nt, _, per = dest3.shape
    tm = per // TOP_K
    return pl.pallas_call(
        _dispatch_kernel,
        grid_spec=pltpu.PrefetchScalarGridSpec(
            num_scalar_prefetch=1,
            grid=(nt,),
            in_specs=[
                pl.BlockSpec((None, 1, per), lambda i, fr: (i, 0, 0), memory_space=pltpu.SMEM),
                pl.BlockSpec((tm * LANE_TILES, 128), lambda i, fr: (i, 0)),
            ],
            out_specs=pl.BlockSpec(memory_space=pl.ANY),
            scratch_shapes=[
                pltpu.VMEM((ROW_BLOCK * LANE_TILES, 128), F32),
                pltpu.SemaphoreType.DMA(()),
                pltpu.SemaphoreType.DMA(()),
            ],
        ),
        out_shape=jax.ShapeDtypeStruct((n_rows * LANE_TILES, 128), F32),
        compiler_params=pltpu.CompilerParams(
            dimension_semantics=("arbitrary",), vmem_limit_bytes=VMEM_LIMIT),
        name="moe_dispatch",
    )(fill_row, dest3, x1r)


def _expert_kernel(blk_ref, be_ref, xs_ref, wgu_ref, bg_ref, bl_ref, wd_ref, bd_ref, y_ref,
                   wt_ref, wg_ref, wl_ref, wdb_ref):
    i = pl.program_id(0)
    d, f2 = wgu_ref.shape
    f = f2 // 2
    ch = XPOSE_CHUNK
    live = blk_ref[i] == i
    new_expert = jnp.logical_or(i == 0, be_ref[i] != be_ref[jnp.maximum(i - 1, 0)])

    @pl.when(jnp.logical_and(live, new_expert))
    def _():
        for j in range(d // 128):
            rows = slice(j * 128, (j + 1) * 128)
            for c in range(f2 // ch):
                wt_ref[j, c * ch:(c + 1) * ch, :] = wgu_ref[rows, c * ch:(c + 1) * ch].T
            for c in range(f // ch):
                gate_rows = wt_ref[j, pl.ds(2 * c * ch, ch, stride=2), :]
                lin_rows = wt_ref[j, pl.ds(2 * c * ch + 1, ch, stride=2), :]
                wg_ref[rows, c * ch:(c + 1) * ch] = gate_rows.T.astype(BF16)
                wl_ref[rows, c * ch:(c + 1) * ch] = lin_rows.T.astype(BF16)
        wdb_ref[...] = wd_ref[...].astype(BF16)

    @pl.when(live)
    def _():
        x = _load_row_tiles(xs_ref, ROW_BLOCK).astype(BF16)
        hg = jnp.dot(x, wg_ref[...], preferred_element_type=F32) + bg_ref[...]
        hl = jnp.dot(x, wl_ref[...], preferred_element_type=F32) + bl_ref[...]
        xg = jnp.minimum(hg, SWIGLU_LIMIT)
        xl = jnp.clip(hl, -SWIGLU_LIMIT, SWIGLU_LIMIT)
        act = xg * jax.nn.sigmoid(SWIGLU_ALPHA * xg) * (xl + 1.0)
        y = jnp.dot(act.astype(BF16), wdb_ref[...], preferred_element_type=F32) + bd_ref[...]
        _store_row_tiles(y_ref, y)

    @pl.when(jnp.logical_not(live))
    def _():
        y_ref[...] = jnp.zeros(y_ref.shape, F32)


def _experts(blk, block_e, xs_r, w_gu, bg, bl, w_down, bd):
    d = D_MODEL
    r = xs_r.shape[0] // LANE_TILES
    f2 = w_gu.shape[2]
    f = f2 // 2
    nb = r // ROW_BLOCK
    wmap = lambda i, bk, be: (be[i], 0, 0)
    rmap = lambda i, bk, be: (bk[i], 0)
    grid_spec = pltpu.PrefetchScalarGridSpec(
        num_scalar_prefetch=2,
        grid=(nb,),
        in_specs=[
            pl.BlockSpec((ROW_BLOCK * LANE_TILES, 128), rmap),
            pl.BlockSpec((None, d, f2), wmap),
            pl.BlockSpec((None, 1, f), wmap),
            pl.BlockSpec((None, 1, f), wmap),
            pl.BlockSpec((None, f, d), wmap),
            pl.BlockSpec((None, 1, d), wmap),
        ],
        out_specs=pl.BlockSpec((ROW_BLOCK * LANE_TILES, 128), lambda i, bk, be: (i, 0)),
        scratch_shapes=[
            pltpu.VMEM((d // 128, f2, 128), F32),
            pltpu.VMEM((d, f), BF16),
            pltpu.VMEM((d, f), BF16),
            pltpu.VMEM((f, d), BF16),
        ],
    )
    return pl.pallas_call(
        _expert_kernel,
        grid_spec=grid_spec,
        out_shape=jax.ShapeDtypeStruct((r * LANE_TILES, 128), F32),
        compiler_params=pltpu.CompilerParams(
            dimension_semantics=("arbitrary",), vmem_limit_bytes=VMEM_LIMIT_EXPERTS),
        name="moe_experts",
    )(blk, block_e, xs_r, w_gu, bg, bl, w_down, bd)


def _combine_start(ys_hbm, idx_ref, buf_ref, sem, tm):
    def body(t, carry):
        for kk in range(TOP_K):
            pltpu.make_async_copy(ys_hbm.at[_tile_rows(idx_ref[0, t * TOP_K + kk]), :],
                                  buf_ref.at[kk, _tile_rows(t), :], sem).start()
        return carry
    lax.fori_loop(0, tm, body, 0, unroll=2)


def _combine_wait(ys_hbm, buf_ref, sem, tm):
    for kk in range(TOP_K):
        pltpu.make_async_copy(ys_hbm.at[pl.ds(0, tm * LANE_TILES), :], buf_ref.at[kk], sem).wait()


def _combine_kernel(idx_cur, idx_nxt, x1_ref, g4_ref, g2_ref, b2_ref, ys_hbm,
                    o_ref, buf_ref, sem):
    i = pl.program_id(0)
    n = pl.num_programs(0)
    slot = i % 2
    tm = o_ref.shape[0]

    @pl.when(i == 0)
    def _():
        _combine_start(ys_hbm, idx_cur, buf_ref.at[0], sem.at[0], tm)

    @pl.when(i + 1 < n)
    def _():
        _combine_start(ys_hbm, idx_nxt, buf_ref.at[1 - slot], sem.at[1 - slot], tm)

    _combine_wait(ys_hbm, buf_ref.at[slot], sem.at[slot], tm)
    g4 = g4_ref[...]
    ffn = g4[:, 0:1] * _load_row_tiles(buf_ref.at[slot, 0], tm)
    for kk in range(1, TOP_K):
        ffn = ffn + g4[:, kk:kk + 1] * _load_row_tiles(buf_ref.at[slot, kk], tm)
    x1 = _load_row_tiles(x1_ref, tm)
    o_ref[...] = _layer_norm(ALPHA * x1 + ffn, g2_ref[...], b2_ref[...])


def _combine(dest3, x1r, g4, g2, b2, ys):
    d = D_MODEL
    t = x1r.shape[0] // LANE_TILES
    tm = TM_COMB
    nt = t // tm
    row = lambda i: (i, 0)
    fixed = lambda i: (0, 0)
    return pl.pallas_call(
        _combine_kernel,
        grid=(nt,),
        in_specs=[
            pl.BlockSpec((None, 1, tm * TOP_K), lambda i: (i, 0, 0), memory_space=pltpu.SMEM),
            pl.BlockSpec((None, 1, tm * TOP_K), lambda i: (jnp.minimum(i + 1, nt - 1), 0, 0),
                         memory_space=pltpu.SMEM),
            pl.BlockSpec((tm * LANE_TILES, 128), row),
            pl.BlockSpec((tm, TOP_K), row),
            pl.BlockSpec((1, d), fixed),
            pl.BlockSpec((1, d), fixed),
            pl.BlockSpec(memory_space=pl.ANY),
        ],
        out_specs=pl.BlockSpec((tm, d), row),
        out_shape=jax.ShapeDtypeStruct((t, d), F32),
        scratch_shapes=[pltpu.VMEM((2, TOP_K, tm * LANE_TILES, 128), F32),
                        pltpu.SemaphoreType.DMA((2,))],
        compiler_params=pltpu.CompilerParams(
            dimension_semantics=("arbitrary",), vmem_limit_bytes=VMEM_LIMIT),
        name="moe_combine_ln",
    )(dest3, dest3, x1r, g4, g2, b2, ys)


def kernel(x, ln0_g, ln0_b, w_in, conv_w, conv_b, w_rg_a, b_rg_a, w_rg_x, b_rg_x, lru_lambda, lam_q1, lam_k1, lam_q2, lam_k2, subln_g, w_out, ln1_g, ln1_b, w_router, b_router, w_gu, b_gu, w_down, b_down, ln2_g, ln2_b):
    bsz, seq, d = x.shape
    t = bsz * seq
    x2 = x.reshape(t, d)
    g0 = ln0_g.reshape(1, d)
    b0 = ln0_b.reshape(1, d)
    l = 0

    w_in_bf = w_in[l].astype(BF16)
    w_out_bf = w_out[l].astype(BF16)
    wa = jax.scipy.linalg.block_diag(*[w_rg_a[l, n] for n in range(LRU_BLOCKS)])
    wx = jax.scipy.linalg.block_diag(*[w_rg_x[l, n] for n in range(LRU_BLOCKS)])
    w_gate_bf = jnp.concatenate([wa, wx], axis=1).astype(BF16)
    b_gate = jnp.concatenate([b_rg_a[l].reshape(1, -1), b_rg_x[l].reshape(1, -1)], axis=1)
    lamvec = jnp.stack([lam_q1[l], lam_k1[l], lam_q2[l], lam_k2[l]]).astype(F32)
    slopes = jnp.asarray([2.0 ** (-8.0 * (i + 1) / ATT_HEADS) for i in range(ATT_HEADS)], F32)
    slopes = jnp.broadcast_to(slopes[:, None, None], (ATT_HEADS, 1, 128))
    bg = b_gu[l, :, None, 0::2]
    bl = b_gu[l, :, None, 1::2]
    bd = b_down[l][:, None, :]

    qt, k, vt, xl, gl = _inproj(x, g0, b0, w_in_bf)
    att = _attention(qt, k, vt, slopes, lamvec, subln_g[l].reshape(-1, 1))
    rec = _lru(xl, gl, conv_w[l], conv_b[l].reshape(1, -1), w_gate_bf, b_gate,
               lru_lambda[l].reshape(1, -1))
    x1r, e4, g4, r4, cnt = _outproj_router(
        x2, g0, b0, att.reshape(t, -1), rec.reshape(t, -1), w_out_bf,
        ln1_g[l].reshape(1, d), ln1_b[l].reshape(1, d), w_router[l], b_router[l].reshape(1, -1))

    n_assign = t * TOP_K
    nb = (n_assign + N_EXPERTS * (ROW_BLOCK - 1)) // ROW_BLOCK + 1
    counts = cnt[0].astype(I32)
    padded = (counts + ROW_BLOCK - 1) // ROW_BLOCK * ROW_BLOCK
    pends = jnp.cumsum(padded)
    pstarts = pends - padded
    dest = pstarts[e4] + r4
    n_live = pends[N_EXPERTS - 1] // ROW_BLOCK
    tail_blk = n_live + jnp.arange(N_EXPERTS, dtype=I32)
    fill_row = jnp.concatenate([
        jnp.where(padded > 0, pends - ROW_BLOCK, -1),
        jnp.where(tail_blk < nb, tail_blk * ROW_BLOCK, -1)]).astype(I32)
    blk = jnp.minimum(jnp.arange(nb, dtype=I32), n_live - 1)
    block_e = jnp.minimum(
        jnp.sum((pends[None, :] <= (blk * ROW_BLOCK)[:, None]).astype(I32), axis=1),
        N_EXPERTS - 1)

    xs = _dispatch(fill_row, dest.reshape(t // TM_DISP, 1, TM_DISP * TOP_K), x1r, nb * ROW_BLOCK)
    ys = _experts(blk, block_e, xs, w_gu[l], bg, bl, w_down[l], bd)
    out = _combine(dest.reshape(t // TM_COMB, 1, TM_COMB * TOP_K), x1r, g4,
                   ln2_g[l].reshape(1, d), ln2_b[l].reshape(1, d), ys)
    return out.reshape(bsz, seq, d)
```

```python
import math

import jax
import jax.numpy as jnp
from jax import lax
from jax.experimental import pallas as pl
from jax.experimental.pallas import tpu as pltpu

F32 = jnp.float32
BF16 = jnp.bfloat16
I32 = jnp.int32

D_MODEL = 1024
ATT_WIDTH = 512
ATT_HEAD_DIM = 64
ATT_HEADS = 4
VAL_DIM = 2 * ATT_HEAD_DIM
LRU_WIDTH = 512
LRU_BLOCKS = 8
LRU_C = 8.0
CONV_WIDTH = 4
CHUNK_SHIFT = 6
N_EXPERTS = 32
TOP_K = 4
D_EXPERT = 1024
SWIGLU_LIMIT = 7.0
SWIGLU_ALPHA = 1.702
DEPTH = 1
ALPHA = (2.0 * DEPTH) ** 0.25
LN_EPS = 1e-5
SUBLN_EPS = 1e-5
LAM_INIT = 0.8 - 0.6 * math.exp(-0.3 * 0)
NEG_BIG = -1e30
LOG2E = math.log2(math.e)

TM_PROJ = 512
TQ = 1024
TKB = 128
SW = 256
TC_LRU = 512
ROW_BLOCK = 256
TM_COMB = 128
TM_DISP = 512
XPOSE_CHUNK = 512

VMEM_LIMIT = 48 * 1024 * 1024
VMEM_LIMIT_EXPERTS = 56 * 1024 * 1024


LANE_TILES = D_MODEL // 128


def _store_row_tiles(ref, val):
    n = val.shape[0]
    for j in range(LANE_TILES):
        ref[pl.ds(j, n, stride=LANE_TILES), :] = val[:, j * 128:(j + 1) * 128]


def _load_row_tiles(ref, n):
    return jnp.concatenate(
        [ref[pl.ds(j, n, stride=LANE_TILES), :] for j in range(LANE_TILES)], axis=1)


def _layer_norm(x, g, b):
    mu = jnp.mean(x, axis=-1, keepdims=True)
    xc = x - mu
    var = jnp.mean(xc * xc, axis=-1, keepdims=True)
    return xc * lax.rsqrt(var + LN_EPS) * g + b


def _inproj_kernel(x_ref, g_ref, b_ref, w_ref, qt_ref, k_ref, vt_ref, xl_ref, gl_ref):
    x0 = _layer_norm(x_ref[...], g_ref[...], b_ref[...])
    p = jnp.dot(x0.astype(BF16), w_ref[...], preferred_element_type=F32)
    a = ATT_WIDTH
    qt_ref[...] = (p[:, 0:a] * (ATT_HEAD_DIM ** -0.5 * LOG2E)).T.astype(BF16)
    k_ref[...] = p[:, a:2 * a].astype(BF16)
    vt = p[:, 2 * a:3 * a].T.astype(BF16)
    for h in range(ATT_HEADS):
        for c in range(vt_ref.shape[1]):
            vt_ref[h, c] = vt[h * VAL_DIM:(h + 1) * VAL_DIM, c * TKB:(c + 1) * TKB]
    xl_ref[...] = p[:, 3 * a:3 * a + LRU_WIDTH]
    gl_ref[...] = p[:, 3 * a + LRU_WIDTH:]


def _inproj(x, g, b, w_bf):
    bsz, seq, _ = x.shape
    n = w_bf.shape[1]
    tm = TM_PROJ
    row = lambda bi, i: (bi, i, 0)
    col = lambda bi, i: (bi, 0, i)
    fixed = lambda bi, i: (0, 0)
    return pl.pallas_call(
        _inproj_kernel,
        grid=(bsz, seq // tm),
        in_specs=[
            pl.BlockSpec((None, tm, D_MODEL), row),
            pl.BlockSpec((1, D_MODEL), fixed),
            pl.BlockSpec((1, D_MODEL), fixed),
            pl.BlockSpec((D_MODEL, n), fixed),
        ],
        out_specs=[
            pl.BlockSpec((None, ATT_WIDTH, tm), col),
            pl.BlockSpec((None, tm, ATT_WIDTH), row),
            pl.BlockSpec((None, ATT_HEADS, tm // TKB, VAL_DIM, TKB),
                         lambda bi, i: (bi, 0, i, 0, 0)),
            pl.BlockSpec((None, tm, LRU_WIDTH), row),
            pl.BlockSpec((None, tm, LRU_WIDTH), row),
        ],
        out_shape=[
            jax.ShapeDtypeStruct((bsz, ATT_WIDTH, seq), BF16),
            jax.ShapeDtypeStruct((bsz, seq, ATT_WIDTH), BF16),
            jax.ShapeDtypeStruct((bsz, ATT_HEADS, seq // TKB, VAL_DIM, TKB), BF16),
            jax.ShapeDtypeStruct((bsz, seq, LRU_WIDTH), F32),
            jax.ShapeDtypeStruct((bsz, seq, LRU_WIDTH), F32),
        ],
        compiler_params=pltpu.CompilerParams(
            dimension_semantics=("parallel", "parallel"), vmem_limit_bytes=VMEM_LIMIT),
        name="ln_inproj",
    )(x, g, b, w_bf)


def _attn_kernel(qt_ref, k_ref, vt_ref, slope_ref, lam_ref, g_ref,
                 o_ref, q2_ref, kx_ref, qbias_ref, diag_ref, *stat_refs):
    qi = pl.program_id(2)
    tq = qt_ref.shape[1]
    n_strips = 2 * tq // SW
    blocks_per_q = tq // TKB
    cl = slope_ref[0:1, 0:1] * LOG2E

    @pl.when(qi == 0)
    def _():
        c = lax.broadcasted_iota(I32, (TKB, 128), 0)
        lane = lax.broadcasted_iota(I32, (TKB, 128), 1)
        w = c.astype(F32) * cl
        w_hi = w.astype(BF16).astype(F32)
        w_mid = (w - w_hi).astype(BF16).astype(F32)
        w_lo = w - w_hi - w_mid
        kx_ref[...] = jnp.where(
            lane == 0, w_hi, jnp.where(lane == 1, w_mid, jnp.where(lane == 2, w_lo, 0.0))
        ).astype(BF16)
        r1 = lax.broadcasted_iota(I32, (1, tq), 1)
        qbias_ref[...] = r1.astype(F32) * (-cl)
        ck0 = lax.broadcasted_iota(I32, (TKB, tq), 0)
        r = lax.broadcasted_iota(I32, (TKB, tq), 1)
        for kd in range(blocks_per_q):
            ck = ck0 + kd * TKB
            allowed = (ck >> CHUNK_SHIFT) <= (r >> CHUNK_SHIFT)
            bias = (jnp.abs(r - ck) + ck0).astype(F32) * (-cl)
            diag_ref[kd] = jnp.where(allowed, bias, NEG_BIG)

    m_refs = stat_refs[0:n_strips]
    l_refs = stat_refs[n_strips:2 * n_strips]
    acc_refs = stat_refs[2 * n_strips:3 * n_strips]
    for j in range(n_strips):
        m_refs[j][...] = jnp.full(m_refs[j].shape, NEG_BIG, F32)
        l_refs[j][...] = jnp.zeros(l_refs[j].shape, F32)
        acc_refs[j][...] = jnp.zeros(acc_refs[j].shape, F32)
    qt = qt_ref[...]
    frow = lax.broadcasted_iota(I32, qt.shape, 0)
    zero = jnp.zeros_like(qt)
    q2_ref[0:VAL_DIM, 0:tq] = jnp.where(frow < ATT_HEAD_DIM, qt, zero)
    q2_ref[0:VAL_DIM, tq:2 * tq] = jnp.where(frow >= ATT_HEAD_DIM, qt, zero)
    xrow = lax.broadcasted_iota(I32, (VAL_DIM, 2 * tq), 0)
    q2_ref[VAL_DIM:2 * VAL_DIM, :] = jnp.where(xrow < 3, 1.0, 0.0).astype(BF16)

    def load_keys(kb):
        k_blk = k_ref[pl.ds(pl.multiple_of(kb * TKB, TKB), TKB), :]
        return jnp.concatenate([k_blk, kx_ref[...]], axis=1), vt_ref[kb]

    def strip(j, k_ext, vt_blk, sc_bias, shift):
        ls = slice(j * SW, (j + 1) * SW)
        sc = jnp.dot(k_ext, q2_ref[:, ls], preferred_element_type=F32)
        if sc_bias is not None:
            sc = sc + sc_bias
        m_prev = m_refs[j][...]
        m_new = jnp.maximum(m_prev, jnp.max(sc, axis=0, keepdims=True) + shift)
        alpha = jnp.exp2(m_prev - m_new)
        p = jnp.exp2(sc - (m_new - shift))
        l_refs[j][...] = alpha * l_refs[j][...] + jnp.sum(p, axis=0, keepdims=True)
        pv = jnp.dot(vt_blk, p.astype(BF16), preferred_element_type=F32)
        acc_refs[j][...] = alpha * acc_refs[j][...] + pv
        m_refs[j][...] = m_new

    def off_diag_group(g, carry):
        for kd in range(blocks_per_q):
            kb = g * blocks_per_q + kd
            k_ext, vt_blk = load_keys(kb)
            block_shift = jnp.full((1, 1), qi * tq - kb * TKB, I32).astype(F32) * (-cl)
            for j in range(n_strips):
                jq = j % (tq // SW)
                strip(j, k_ext, vt_blk, None,
                      qbias_ref[:, jq * SW:(jq + 1) * SW] + block_shift)
        return carry
    lax.fori_loop(0, qi, off_diag_group, 0)

    for kd in range(blocks_per_q):
        k_ext, vt_blk = load_keys(qi * blocks_per_q + kd)
        for j in range(n_strips):
            jq = j % (tq // SW)
            if (jq + 1) * SW > kd * TKB:
                strip(j, k_ext, vt_blk, diag_ref[kd, :, jq * SW:(jq + 1) * SW], 0.0)

    lv = lam_ref[...]
    lam = (jnp.exp(jnp.sum(lv[0:1] * lv[1:2], axis=-1, keepdims=True))
           - jnp.exp(jnp.sum(lv[2:3] * lv[3:4], axis=-1, keepdims=True)) + LAM_INIT)
    o2 = [acc_refs[j][...] / l_refs[j][...] for j in range(n_strips)]
    half = n_strips // 2
    o = jnp.concatenate([o2[j] - lam * o2[half + j] for j in range(half)], axis=1)
    o = o * lax.rsqrt(jnp.mean(o * o, axis=0, keepdims=True) + SUBLN_EPS)
    o = o * (g_ref[...] * (1.0 - LAM_INIT))
    o_ref[...] = o.T.astype(o_ref.dtype)


def _attention(qt, k, vt, slopes, lamvec, subln_g_col):
    b, s, _ = k.shape
    nkb = s // TKB
    n_strips = 2 * TQ // SW
    return pl.pallas_call(
        _attn_kernel,
        grid=(b, ATT_HEADS, s // TQ),
        in_specs=[
            pl.BlockSpec((None, VAL_DIM, TQ), lambda bi, h, qi: (bi, h, qi)),
            pl.BlockSpec((None, s, VAL_DIM), lambda bi, h, qi: (bi, 0, h)),
            pl.BlockSpec((None, None, nkb, VAL_DIM, TKB), lambda bi, h, qi: (bi, h, 0, 0, 0)),
            pl.BlockSpec((None, 1, 128), lambda bi, h, qi: (h, 0, 0)),
            pl.BlockSpec((4, ATT_HEAD_DIM), lambda bi, h, qi: (0, 0)),
            pl.BlockSpec((VAL_DIM, 1), lambda bi, h, qi: (0, 0)),
        ],
        out_specs=pl.BlockSpec((None, TQ, VAL_DIM), lambda bi, h, qi: (bi, qi, h)),
        out_shape=jax.ShapeDtypeStruct((b, s, ATT_WIDTH), BF16),
        scratch_shapes=[
            pltpu.VMEM((2 * VAL_DIM, 2 * TQ), BF16),
            pltpu.VMEM((TKB, 128), BF16),
            pltpu.VMEM((1, TQ), F32),
            pltpu.VMEM((TQ // TKB, TKB, TQ), F32),
        ] +[pltpu.VMEM((1, SW), F32)] * (2 * n_strips) + [pltpu.VMEM((VAL_DIM, SW), F32)] * n_strips,
        compiler_params=pltpu.CompilerParams(
            dimension_semantics=("parallel", "parallel", "arbitrary"),
            vmem_limit_bytes=VMEM_LIMIT),
        name="diff_attention",
    )(qt, k, vt, slopes, lamvec, subln_g_col)


def _lru_kernel(xl_ref, gl_ref, cw_ref, cb_ref, wg_ref, bg_ref, lam_ref,
                o_ref, ext_ref, h_ref):
    c = pl.program_id(1)
    tc = xl_ref.shape[0]
    w = LRU_WIDTH

    @pl.when(c == 0)
    def _():
        ext_ref[0:8, :] = jnp.zeros((8, w), F32)
        h_ref[...] = jnp.zeros(h_ref.shape, F32)

    ext_ref[8:8 + tc, :] = xl_ref[...]
    cw = cw_ref[...]
    xc = cb_ref[...] + cw[0:1] * ext_ref[5:5 + tc, :]
    for j in range(1, CONV_WIDTH):
        xc = xc + cw[j:j + 1] * ext_ref[5 + j:5 + j + tc, :]
    tail = ext_ref[tc:tc + 8, :]

    gates = jnp.dot(xc.astype(BF16), wg_ref[...], preferred_element_type=F32) + bg_ref[...]
    r = jax.nn.sigmoid(gates[:, 0:w])
    ig = jax.nn.sigmoid(gates[:, w:2 * w])
    z = -lam_ref[...]
    softplus = jnp.maximum(z, 0.0) + jnp.log1p(jnp.exp(-jnp.abs(z)))
    log_a = (-LRU_C) * r * softplus
    a = jnp.exp(log_a)
    u = jnp.sqrt(-jnp.tanh(log_a) * (a * a + 1.0)) * (ig * xc)

    row = lax.broadcasted_iota(I32, (tc, w), 0)
    d = 1
    while d < tc:
        a_sh = pltpu.roll(a, d, axis=0)
        u_sh = pltpu.roll(u, d, axis=0)
        valid = row >= d
        u = jnp.where(valid, a * u_sh + u, u)
        a = jnp.where(valid, a * a_sh, a)
        d *= 2
    h = u + a * h_ref[0:1, :]

    gl = gl_ref[...]
    gelu = 0.5 * gl * (1.0 + jnp.tanh(math.sqrt(2.0 / math.pi) * (gl + 0.044715 * (gl * gl * gl))))
    o_ref[...] = (h * gelu).astype(o_ref.dtype)

    h_ref[0:1, :] = h[tc - 1:tc, :]
    ext_ref[0:8, :] = tail


def _lru(xl, gl, conv_w, conv_b, w_gate_bf, b_gate, lam):
    b, s, w = xl.shape
    tc = TC_LRU
    blk = lambda bi, c: (bi, c, 0)
    fixed = lambda bi, c: (0, 0)
    return pl.pallas_call(
        _lru_kernel,
        grid=(b, s // tc),
        in_specs=[
            pl.BlockSpec((None, tc, w), blk),
            pl.BlockSpec((None, tc, w), blk),
            pl.BlockSpec((CONV_WIDTH, w), fixed),
            pl.BlockSpec((1, w), fixed),
            pl.BlockSpec((w, 2 * w), fixed),
            pl.BlockSpec((1, 2 * w), fixed),
            pl.BlockSpec((1, w), fixed),
        ],
        out_specs=pl.BlockSpec((None, tc, w), blk),
        out_shape=jax.ShapeDtypeStruct((b, s, w), BF16),
        scratch_shapes=[pltpu.VMEM((tc + 8, w), F32), pltpu.VMEM((8, w), F32)],
        compiler_params=pltpu.CompilerParams(
            dimension_semantics=("parallel", "arbitrary"), vmem_limit_bytes=VMEM_LIMIT),
        name="rg_lru",
    )(xl, gl, conv_w, conv_b, w_gate_bf, b_gate, lam)


def _outproj_kernel(x_ref, g0_ref, b0_ref, att_ref, rec_ref, wo_ref, g1_ref, b1_ref,
                    wr_ref, br_ref, x1_ref, e4_ref, g4_ref, r4_ref, cnt_ref, carry_ref):
    i = pl.program_id(0)
    tm = x_ref.shape[0]

    @pl.when(i == 0)
    def _():
        carry_ref[...] = jnp.zeros(carry_ref.shape, F32)

    x0 = _layer_norm(x_ref[...], g0_ref[...], b0_ref[...])
    mixed = (jnp.dot(att_ref[...], wo_ref[0:ATT_WIDTH, :], preferred_element_type=F32)
             + jnp.dot(rec_ref[...], wo_ref[ATT_WIDTH:, :], preferred_element_type=F32))
    x1 = _layer_norm(ALPHA * x0 + mixed, g1_ref[...], b1_ref[...])
    _store_row_tiles(x1_ref, x1)

    logits = jnp.dot(x1, wr_ref[...], preferred_element_type=F32,
                     precision=lax.Precision.HIGHEST) + br_ref[...]
    lane = lax.broadcasted_iota(I32, logits.shape, 1)
    work = logits
    vals, idxs, hots = [], [], []
    for _ in range(TOP_K):
        mx = jnp.max(work, axis=-1, keepdims=True)
        idx = jnp.min(jnp.where(work == mx, lane, N_EXPERTS), axis=-1, keepdims=True)
        hot = lane == idx
        vals.append(mx)
        idxs.append(idx)
        hots.append(hot)
        work = jnp.where(hot, -jnp.inf, work)
    exps = [jnp.exp(v - vals[0]) for v in vals]
    denom = exps[0] + exps[1] + exps[2] + exps[3]

    mask = jnp.zeros(logits.shape, F32)
    for hot in hots:
        mask = mask + hot.astype(F32)
    tr = lax.broadcasted_iota(I32, (tm, tm), 0)
    tc = lax.broadcasted_iota(I32, (tm, tm), 1)
    lower = jnp.where(tc < tr, 1.0, 0.0).astype(BF16)
    excl = jnp.dot(lower, mask.astype(BF16), preferred_element_type=F32) + carry_ref[0:1, :]
    for kk in range(TOP_K):
        e4_ref[:, kk:kk + 1] = idxs[kk]
        g4_ref[:, kk:kk + 1] = exps[kk] / denom
        rank = jnp.sum(jnp.where(hots[kk], excl, 0.0), axis=-1, keepdims=True)
        r4_ref[:, kk:kk + 1] = rank.astype(I32)
    total = carry_ref[0:1, :] + jnp.sum(mask, axis=0, keepdims=True)
    carry_ref[0:1, :] = total
    cnt_ref[...] = total


def _outproj_router(x2, g0, b0, att, rec, wo_bf, g1, b1, w_router, b_router):
    t = x2.shape[0]
    tm = TM_PROJ
    row = lambda i: (i, 0)
    fixed = lambda i: (0, 0)
    return pl.pallas_call(
        _outproj_kernel,
        grid=(t // tm,),
        in_specs=[
            pl.BlockSpec((tm, D_MODEL), row),
            pl.BlockSpec((1, D_MODEL), fixed),
            pl.BlockSpec((1, D_MODEL), fixed),
            pl.BlockSpec((tm, ATT_WIDTH), row),
            pl.BlockSpec((tm, LRU_WIDTH), row),
            pl.BlockSpec((D_MODEL, D_MODEL), fixed),
            pl.BlockSpec((1, D_MODEL), fixed),
            pl.BlockSpec((1, D_MODEL), fixed),
            pl.BlockSpec((D_MODEL, N_EXPERTS), fixed),
            pl.BlockSpec((1, N_EXPERTS), fixed),
        ],
        out_specs=[
            pl.BlockSpec((tm * LANE_TILES, 128), row),
            pl.BlockSpec((tm, TOP_K), row),
            pl.BlockSpec((tm, TOP_K), row),
            pl.BlockSpec((tm, TOP_K), row),
            pl.BlockSpec((1, N_EXPERTS), fixed),
        ],
        out_shape=[
            jax.ShapeDtypeStruct((t * LANE_TILES, 128), F32),
            jax.ShapeDtypeStruct((t, TOP_K), I32),
            jax.ShapeDtypeStruct((t, TOP_K), F32),
            jax.ShapeDtypeStruct((t, TOP_K), I32),
            jax.ShapeDtypeStruct((1, N_EXPERTS), F32),
        ],
        scratch_shapes=[pltpu.VMEM((8, N_EXPERTS), F32)],
        compiler_params=pltpu.CompilerParams(
            dimension_semantics=("arbitrary",), vmem_limit_bytes=VMEM_LIMIT),
        name="outproj_ln_router",
    )(x2, g0, b0, att, rec, wo_bf, g1, b1, w_router, b_router)


def _tile_rows(row):
    return pl.ds(pl.multiple_of(row * LANE_TILES, LANE_TILES), LANE_TILES)


def _dispatch_kernel(fill_ref, dest_ref, x_ref, o_hbm, zero_ref, sem, fill_sem):
    i = pl.program_id(0)
    tm = x_ref.shape[0] // LANE_TILES

    @pl.when(i == 0)
    def _():
        zero_ref[...] = jnp.zeros(zero_ref.shape, F32)

        def fill_copy(e):
            start = pl.multiple_of(fill_ref[e] * LANE_TILES, LANE_TILES)
            return pltpu.make_async_copy(
                zero_ref, o_hbm.at[pl.ds(start, ROW_BLOCK * LANE_TILES), :], fill_sem)

        for e in range(2 * N_EXPERTS):
            @pl.when(fill_ref[e] >= 0)
            def _():
                fill_copy(e).start()
        for e in range(2 * N_EXPERTS):
            @pl.when(fill_ref[e] >= 0)
            def _():
                fill_copy(e).wait()

    def body(t, carry):
        src = x_ref.at[_tile_rows(t), :]
        for kk in range(TOP_K):
            pltpu.make_async_copy(
                src, o_hbm.at[_tile_rows(dest_ref[0, t * TOP_K + kk]), :], sem).start()
        return carry
    lax.fori_loop(0, tm, body, 0, unroll=2)
    for kk in range(TOP_K):
        pltpu.make_async_copy(x_ref, o_hbm.at[pl.ds(0, tm * LANE_TILES), :], sem).wait()


def _dispatch(fill_row, dest3, x1r, n_rows):
    nt, _, per = dest3.shape
    tm = per // TOP_K
    return pl.pallas_call(
        _dispatch_kernel,
        grid_spec=pltpu.PrefetchScalarGridSpec(
            num_scalar_prefetch=1,
            grid=(nt,),
            in_specs=[
                pl.BlockSpec((None, 1, per), lambda i, fr: (i, 0, 0), memory_space=pltpu.SMEM),
                pl.BlockSpec((tm * LANE_TILES, 128), lambda i, fr: (i, 0)),
            ],
            out_specs=pl.BlockSpec(memory_space=pl.ANY),
            scratch_shapes=[
                pltpu.VMEM((ROW_BLOCK * LANE_TILES, 128), F32),
                pltpu.SemaphoreType.DMA(()),
                pltpu.SemaphoreType.DMA(()),
            ],
        ),
        out_shape=jax.ShapeDtypeStruct((n_rows * LANE_TILES, 128), F32),
        compiler_params=pltpu.CompilerParams(
            dimension_semantics=("arbitrary",), vmem_limit_bytes=VMEM_LIMIT),
        name="moe_dispatch",
    )(fill_row, dest3, x1r)


def _expert_kernel(blk_ref, be_ref, xs_ref, wgu_ref, bg_ref, bl_ref, wd_ref, bd_ref, y_ref,
                   wt_ref, wg_ref, wl_ref, wdb_ref):
    i = pl.program_id(0)
    d, f2 = wgu_ref.shape
    f = f2 // 2
    ch = XPOSE_CHUNK
    live = blk_ref[i] == i
    new_expert = jnp.logical_or(i == 0, be_ref[i] != be_ref[jnp.maximum(i - 1, 0)])

    @pl.when(jnp.logical_and(live, new_expert))
    def _():
        for j in range(d // 128):
            rows = slice(j * 128, (j + 1) * 128)
            for c in range(f2 // ch):
                wt_ref[j, c * ch:(c + 1) * ch, :] = wgu_ref[rows, c * ch:(c + 1) * ch].T
            for c in range(f // ch):
                gate_rows = wt_ref[j, pl.ds(2 * c * ch, ch, stride=2), :]
                lin_rows = wt_ref[j, pl.ds(2 * c * ch + 1, ch, stride=2), :]
                wg_ref[rows, c * ch:(c + 1) * ch] = gate_rows.T.astype(BF16)
                wl_ref[rows, c * ch:(c + 1) * ch] = lin_rows.T.astype(BF16)
        wdb_ref[...] = wd_ref[...].astype(BF16)

    @pl.when(live)
    def _():
        x = _load_row_tiles(xs_ref, ROW_BLOCK).astype(BF16)
        hg = jnp.dot(x, wg_ref[...], preferred_element_type=F32) + bg_ref[...]
        hl = jnp.dot(x, wl_ref[...], preferred_element_type=F32) + bl_ref[...]
        xg = jnp.minimum(hg, SWIGLU_LIMIT)
        xl = jnp.clip(hl, -SWIGLU_LIMIT, SWIGLU_LIMIT)
        act = xg * jax.nn.sigmoid(SWIGLU_ALPHA * xg) * (xl + 1.0)
        y = jnp.dot(act.astype(BF16), wdb_ref[...], preferred_element_type=F32) + bd_ref[...]
        _store_row_tiles(y_ref, y)

    @pl.when(jnp.logical_not(live))
    def _():
        y_ref[...] = jnp.zeros(y_ref.shape, F32)


def _experts(blk, block_e, xs_r, w_gu, bg, bl, w_down, bd):
    d = D_MODEL
    r = xs_r.shape[0] // LANE_TILES
    f2 = w_gu.shape[2]
    f = f2 // 2
    nb = r // ROW_BLOCK
    wmap = lambda i, bk, be: (be[i], 0, 0)
    rmap = lambda i, bk, be: (bk[i], 0)
    grid_spec = pltpu.PrefetchScalarGridSpec(
        num_scalar_prefetch=2,
        grid=(nb,),
        in_specs=[
            pl.BlockSpec((ROW_BLOCK * LANE_TILES, 128), rmap),
            pl.BlockSpec((None, d, f2), wmap),
            pl.BlockSpec((None, 1, f), wmap),
            pl.BlockSpec((None, 1, f), wmap),
            pl.BlockSpec((None, f, d), wmap),
            pl.BlockSpec((None, 1, d), wmap),
        ],
        out_specs=pl.BlockSpec((ROW_BLOCK * LANE_TILES, 128), lambda i, bk, be: (i, 0)),
        scratch_shapes=[
            pltpu.VMEM((d // 128, f2, 128), F32),
            pltpu.VMEM((d, f), BF16),
            pltpu.VMEM((d, f), BF16),
            pltpu.VMEM((f, d), BF16),
        ],
    )
    return pl.pallas_call(
        _expert_kernel,
        grid_spec=grid_spec,
        out_shape=jax.ShapeDtypeStruct((r * LANE_TILES, 128), F32),
        compiler_params=pltpu.CompilerParams(
            dimension_semantics=("arbitrary",), vmem_limit_bytes=VMEM_LIMIT_EXPERTS),
        name="moe_experts",
    )(blk, block_e, xs_r, w_gu, bg, bl, w_down, bd)


def _combine_start(ys_hbm, idx_ref, buf_ref, sem, tm):
    def body(t, carry):
        for kk in range(TOP_K):
            pltpu.make_async_copy(ys_hbm.at[_tile_rows(idx_ref[0, t * TOP_K + kk]), :],
                                  buf_ref.at[kk, _tile_rows(t), :], sem).start()
        return carry
    lax.fori_loop(0, tm, body, 0, unroll=2)


def _combine_wait(ys_hbm, buf_ref, sem, tm):
    for kk in range(TOP_K):
        pltpu.make_async_copy(ys_hbm.at[pl.ds(0, tm * LANE_TILES), :], buf_ref.at[kk], sem).wait()


def _combine_kernel(idx_cur, idx_nxt, x1_ref, g4_ref, g2_ref, b2_ref, ys_hbm,
                    o_ref, buf_ref, sem):
    i = pl.program_id(0)
    n = pl.num_programs(0)
    slot = i % 2
    tm = o_ref.shape[0]

    @pl.when(i == 0)
    def _():
        _combine_start(ys_hbm, idx_cur, buf_ref.at[0], sem.at[0], tm)

    @pl.when(i + 1 < n)
    def _():
        _combine_start(ys_hbm, idx_nxt, buf_ref.at[1 - slot], sem.at[1 - slot], tm)

    _combine_wait(ys_hbm, buf_ref.at[slot], sem.at[slot], tm)
    g4 = g4_ref[...]
    ffn = g4[:, 0:1] * _load_row_tiles(buf_ref.at[slot, 0], tm)
    for kk in range(1, TOP_K):
        ffn = ffn + g4[:, kk:kk + 1] * _load_row_tiles(buf_ref.at[slot, kk], tm)
    x1 = _load_row_tiles(x1_ref, tm)
    o_ref[...] = _layer_norm(ALPHA * x1 + ffn, g2_ref[...], b2_ref[...])


def _combine(dest3, x1r, g4, g2, b2, ys):
    d = D_MODEL
    t = x1r.shape[0] // LANE_TILES
    tm = TM_COMB
    nt = t // tm
    row = lambda i: (i, 0)
    fixed = lambda i: (0, 0)
    return pl.pallas_call(
        _combine_kernel,
        grid=(nt,),
        in_specs=[
            pl.BlockSpec((None, 1, tm * TOP_K), lambda i: (i, 0, 0), memory_space=pltpu.SMEM),
            pl.BlockSpec((None, 1, tm * TOP_K), lambda i: (jnp.minimum(i + 1, nt - 1), 0, 0),
                         memory_space=pltpu.SMEM),
            pl.BlockSpec((tm * LANE_TILES, 128), row),
            pl.BlockSpec((tm, TOP_K), row),
            pl.BlockSpec((1, d), fixed),
            pl.BlockSpec((1, d), fixed),
            pl.BlockSpec(memory_space=pl.ANY),
        ],
        out_specs=pl.BlockSpec((tm, d), row),
        out_shape=jax.ShapeDtypeStruct((t, d), F32),
        scratch_shapes=[pltpu.VMEM((2, TOP_K, tm * LANE_TILES, 128), F32),
                        pltpu.SemaphoreType.DMA((2,))],
        compiler_params=pltpu.CompilerParams(
            dimension_semantics=("arbitrary",), vmem_limit_bytes=VMEM_LIMIT),
        name="moe_combine_ln",
    )(dest3, dest3, x1r, g4, g2, b2, ys)


def kernel(x, ln0_g, ln0_b, w_in, conv_w, conv_b, w_rg_a, b_rg_a, w_rg_x, b_rg_x, lru_lambda, lam_q1, lam_k1, lam_q2, lam_k2, subln_g, w_out, ln1_g, ln1_b, w_router, b_router, w_gu, b_gu, w_down, b_down, ln2_g, ln2_b):
    bsz, seq, d = x.shape
    t = bsz * seq
    x2 = x.reshape(t, d)
    g0 = ln0_g.reshape(1, d)
    b0 = ln0_b.reshape(1, d)
    l = 0

    w_in_bf = w_in[l].astype(BF16)
    w_out_bf = w_out[l].astype(BF16)
    wa = jax.scipy.linalg.block_diag(*[w_rg_a[l, n] for n in range(LRU_BLOCKS)])
    wx = jax.scipy.linalg.block_diag(*[w_rg_x[l, n] for n in range(LRU_BLOCKS)])
    w_gate_bf = jnp.concatenate([wa, wx], axis=1).astype(BF16)
    b_gate = jnp.concatenate([b_rg_a[l].reshape(1, -1), b_rg_x[l].reshape(1, -1)], axis=1)
    lamvec = jnp.stack([lam_q1[l], lam_k1[l], lam_q2[l], lam_k2[l]]).astype(F32)
    slopes = jnp.asarray([2.0 ** (-8.0 * (i + 1) / ATT_HEADS) for i in range(ATT_HEADS)], F32)
    slopes = jnp.broadcast_to(slopes[:, None, None], (ATT_HEADS, 1, 128))
    bg = b_gu[l, :, None, 0::2]
    bl = b_gu[l, :, None, 1::2]
    bd = b_down[l][:, None, :]

    qt, k, vt, xl, gl = _inproj(x, g0, b0, w_in_bf)
    att = _attention(qt, k, vt, slopes, lamvec, subln_g[l].reshape(-1, 1))
    rec = _lru(xl, gl, conv_w[l], conv_b[l].reshape(1, -1), w_gate_bf, b_gate,
               lru_lambda[l].reshape(1, -1))
    x1r, e4, g4, r4, cnt = _outproj_router(
        x2, g0, b0, att.reshape(t, -1), rec.reshape(t, -1), w_out_bf,
        ln1_g[l].reshape(1, d), ln1_b[l].reshape(1, d), w_router[l], b_router[l].reshape(1, -1))

    n_assign = t * TOP_K
    nb = (n_assign + N_EXPERTS * (ROW_BLOCK - 1)) // ROW_BLOCK + 1
    counts = cnt[0].astype(I32)
    padded = (counts + ROW_BLOCK - 1) // ROW_BLOCK * ROW_BLOCK
    pends = jnp.cumsum(padded)
    pstarts = pends - padded
    dest = pstarts[e4] + r4
    n_live = pends[N_EXPERTS - 1] // ROW_BLOCK
    tail_blk = n_live + jnp.arange(N_EXPERTS, dtype=I32)
    fill_row = jnp.concatenate([
        jnp.where(padded > 0, pends - ROW_BLOCK, -1),
        jnp.where(tail_blk < nb, tail_blk * ROW_BLOCK, -1)]).astype(I32)
    blk = jnp.minimum(jnp.arange(nb, dtype=I32), n_live - 1)
    block_e = jnp.minimum(
        jnp.sum((pends[None, :] <= (blk * ROW_BLOCK)[:, None]).astype(I32), axis=1),
        N_EXPERTS - 1)

    xs = _dispatch(fill_row, dest.reshape(t // TM_DISP, 1, TM_DISP * TOP_K), x1r, nb * ROW_BLOCK)
    ys = _experts(blk, block_e, xs, w_gu[l], bg, bl, w_down[l], bd)
    out = _combine(dest.reshape(t // TM_COMB, 1, TM_COMB * TOP_K), x1r, g4,
                   ln2_g[l].reshape(1, d), ln2_b[l].reshape(1, d), ys)
    return out.reshape(bsz, seq, d)
```

```python
import math

import jax
import jax.numpy as jnp
from jax import lax
from jax.experimental import pallas as pl
from jax.experimental.pallas import tpu as pltpu

F32 = jnp.float32
BF16 = jnp.bfloat16
I32 = jnp.int32

D_MODEL = 1024
ATT_WIDTH = 512
ATT_HEAD_DIM = 64
ATT_HEADS = 4
VAL_DIM = 2 * ATT_HEAD_DIM
LRU_WIDTH = 512
LRU_BLOCKS = 8
LRU_C = 8.0
CONV_WIDTH = 4
CHUNK_SHIFT = 6
N_EXPERTS = 32
TOP_K = 4
D_EXPERT = 1024
SWIGLU_LIMIT = 7.0
SWIGLU_ALPHA = 1.702
DEPTH = 1
ALPHA = (2.0 * DEPTH) ** 0.25
LN_EPS = 1e-5
SUBLN_EPS = 1e-5
LAM_INIT = 0.8 - 0.6 * math.exp(-0.3 * 0)
NEG_BIG = -1e30
LOG2E = math.log2(math.e)

TM_PROJ = 512
TQ = 1024
TKB = 128
SW = 256
TC_LRU = 512
ROW_BLOCK = 256
TM_COMB = 128
TM_INV = 2048
GATHER_BATCH = 16
XPOSE_CHUNK = 512

VMEM_LIMIT = 48 * 1024 * 1024
VMEM_LIMIT_EXPERTS = 56 * 1024 * 1024


LANE_TILES = D_MODEL // 128


def _store_row_tiles(ref, val):
    n = val.shape[0]
    for j in range(LANE_TILES):
        ref[pl.ds(j, n, stride=LANE_TILES), :] = val[:, j * 128:(j + 1) * 128]


def _load_row_tiles(ref, n):
    return jnp.concatenate(
        [ref[pl.ds(j, n, stride=LANE_TILES), :] for j in range(LANE_TILES)], axis=1)


def _layer_norm(x, g, b):
    mu = jnp.mean(x, axis=-1, keepdims=True)
    xc = x - mu
    var = jnp.mean(xc * xc, axis=-1, keepdims=True)
    return xc * lax.rsqrt(var + LN_EPS) * g + b


def _inproj_kernel(x_ref, g_ref, b_ref, w_ref, qt_ref, k_ref, vt_ref, xl_ref, gl_ref):
    x0 = _layer_norm(x_ref[...], g_ref[...], b_ref[...])
    p = jnp.dot(x0.astype(BF16), w_ref[...], preferred_element_type=F32)
    a = ATT_WIDTH
    qt_ref[...] = (p[:, 0:a] * (ATT_HEAD_DIM ** -0.5 * LOG2E)).T.astype(BF16)
    k_ref[...] = p[:, a:2 * a].astype(BF16)
    vt = p[:, 2 * a:3 * a].T.astype(BF16)
    for h in range(ATT_HEADS):
        for c in range(vt_ref.shape[1]):
            vt_ref[h, c] = vt[h * VAL_DIM:(h + 1) * VAL_DIM, c * TKB:(c + 1) * TKB]
    xl_ref[...] = p[:, 3 * a:3 * a + LRU_WIDTH]
    gl_ref[...] = p[:, 3 * a + LRU_WIDTH:]


def _inproj(x, g, b, w_bf):
    bsz, seq, _ = x.shape
    n = w_bf.shape[1]
    tm = TM_PROJ
    row = lambda bi, i: (bi, i, 0)
    col = lambda bi, i: (bi, 0, i)
    fixed = lambda bi, i: (0, 0)
    return pl.pallas_call(
        _inproj_kernel,
        grid=(bsz, seq // tm),
        in_specs=[
            pl.BlockSpec((None, tm, D_MODEL), row),
            pl.BlockSpec((1, D_MODEL), fixed),
            pl.BlockSpec((1, D_MODEL), fixed),
            pl.BlockSpec((D_MODEL, n), fixed),
        ],
        out_specs=[
            pl.BlockSpec((None, ATT_WIDTH, tm), col),
            pl.BlockSpec((None, tm, ATT_WIDTH), row),
            pl.BlockSpec((None, ATT_HEADS, tm // TKB, VAL_DIM, TKB),
                         lambda bi, i: (bi, 0, i, 0, 0)),
            pl.BlockSpec((None, tm, LRU_WIDTH), row),
            pl.BlockSpec((None, tm, LRU_WIDTH), row),
        ],
        out_shape=[
            jax.ShapeDtypeStruct((bsz, ATT_WIDTH, seq), BF16),
            jax.ShapeDtypeStruct((bsz, seq, ATT_WIDTH), BF16),
            jax.ShapeDtypeStruct((bsz, ATT_HEADS, seq // TKB, VAL_DIM, TKB), BF16),
            jax.ShapeDtypeStruct((bsz, seq, LRU_WIDTH), F32),
            jax.ShapeDtypeStruct((bsz, seq, LRU_WIDTH), F32),
        ],
        compiler_params=pltpu.CompilerParams(
            dimension_semantics=("parallel", "parallel"), vmem_limit_bytes=VMEM_LIMIT),
        name="ln_inproj",
    )(x, g, b, w_bf)


def _attn_kernel(qt_ref, k_ref, vt_ref, slope_ref, lam_ref, g_ref,
                 o_ref, q2_ref, kx_ref, qbias_ref, diag_ref, *stat_refs):
    qi = pl.program_id(2)
    tq = qt_ref.shape[1]
    n_strips = 2 * tq // SW
    blocks_per_q = tq // TKB
    cl = slope_ref[0:1, 0:1] * LOG2E

    @pl.when(qi == 0)
    def _():
        c = lax.broadcasted_iota(I32, (TKB, 128), 0)
        lane = lax.broadcasted_iota(I32, (TKB, 128), 1)
        w = c.astype(F32) * cl
        w_hi = w.astype(BF16).astype(F32)
        w_mid = (w - w_hi).astype(BF16).astype(F32)
        w_lo = w - w_hi - w_mid
        kx_ref[...] = jnp.where(
            lane == 0, w_hi, jnp.where(lane == 1, w_mid, jnp.where(lane == 2, w_lo, 0.0))
        ).astype(BF16)
        r1 = lax.broadcasted_iota(I32, (1, tq), 1)
        qbias_ref[...] = r1.astype(F32) * (-cl)
        ck0 = lax.broadcasted_iota(I32, (TKB, tq), 0)
        r = lax.broadcasted_iota(I32, (TKB, tq), 1)
        for kd in range(blocks_per_q):
            ck = ck0 + kd * TKB
            allowed = (ck >> CHUNK_SHIFT) <= (r >> CHUNK_SHIFT)
            bias = (jnp.abs(r - ck) + ck0).astype(F32) * (-cl)
            diag_ref[kd] = jnp.where(allowed, bias, NEG_BIG)

    m_refs = stat_refs[0:n_strips]
    l_refs = stat_refs[n_strips:2 * n_strips]
    acc_refs = stat_refs[2 * n_strips:3 * n_strips]
    for j in range(n_strips):
        m_refs[j][...] = jnp.full(m_refs[j].shape, NEG_BIG, F32)
        l_refs[j][...] = jnp.zeros(l_refs[j].shape, F32)
        acc_refs[j][...] = jnp.zeros(acc_refs[j].shape, F32)
    qt = qt_ref[...]
    frow = lax.broadcasted_iota(I32, qt.shape, 0)
    zero = jnp.zeros_like(qt)
    q2_ref[0:VAL_DIM, 0:tq] = jnp.where(frow < ATT_HEAD_DIM, qt, zero)
    q2_ref[0:VAL_DIM, tq:2 * tq] = jnp.where(frow >= ATT_HEAD_DIM, qt, zero)
    xrow = lax.broadcasted_iota(I32, (VAL_DIM, 2 * tq), 0)
    q2_ref[VAL_DIM:2 * VAL_DIM, :] = jnp.where(xrow < 3, 1.0, 0.0).astype(BF16)

    def load_keys(kb):
        k_blk = k_ref[pl.ds(pl.multiple_of(kb * TKB, TKB), TKB), :]
        return jnp.concatenate([k_blk, kx_ref[...]], axis=1), vt_ref[kb]

    def strip(j, k_ext, vt_blk, sc_bias, shift):
        ls = slice(j * SW, (j + 1) * SW)
        sc = jnp.dot(k_ext, q2_ref[:, ls], preferred_element_type=F32)
        if sc_bias is not None:
            sc = sc + sc_bias
        m_prev = m_refs[j][...]
        m_new = jnp.maximum(m_prev, jnp.max(sc, axis=0, keepdims=True) + shift)
        alpha = jnp.exp2(m_prev - m_new)
        p = jnp.exp2(sc - (m_new - shift))
        l_refs[j][...] = alpha * l_refs[j][...] + jnp.sum(p, axis=0, keepdims=True)
        pv = jnp.dot(vt_blk, p.astype(BF16), preferred_element_type=F32)
        acc_refs[j][...] = alpha * acc_refs[j][...] + pv
        m_refs[j][...] = m_new

    def off_diag_group(g, carry):
        for kd in range(blocks_per_q):
            kb = g * blocks_per_q + kd
            k_ext, vt_blk = load_keys(kb)
            block_shift = jnp.full((1, 1), qi * tq - kb * TKB, I32).astype(F32) * (-cl)
            for j in range(n_strips):
                jq = j % (tq // SW)
                strip(j, k_ext, vt_blk, None,
                      qbias_ref[:, jq * SW:(jq + 1) * SW] + block_shift)
        return carry
    lax.fori_loop(0, qi, off_diag_group, 0)

    for kd in range(blocks_per_q):
        k_ext, vt_blk = load_keys(qi * blocks_per_q + kd)
        for j in range(n_strips):
            jq = j % (tq // SW)
            if (jq + 1) * SW > kd * TKB:
                strip(j, k_ext, vt_blk, diag_ref[kd, :, jq * SW:(jq + 1) * SW], 0.0)

    lv = lam_ref[...]
    lam = (jnp.exp(jnp.sum(lv[0:1] * lv[1:2], axis=-1, keepdims=True))
           - jnp.exp(jnp.sum(lv[2:3] * lv[3:4], axis=-1, keepdims=True)) + LAM_INIT)
    o2 = [acc_refs[j][...] / l_refs[j][...] for j in range(n_strips)]
    half = n_strips // 2
    o = jnp.concatenate([o2[j] - lam * o2[half + j] for j in range(half)], axis=1)
    o = o * lax.rsqrt(jnp.mean(o * o, axis=0, keepdims=True) + SUBLN_EPS)
    o = o * (g_ref[...] * (1.0 - LAM_INIT))
    o_ref[...] = o.T.astype(o_ref.dtype)


def _attention(qt, k, vt, slopes, lamvec, subln_g_col):
    b, s, _ = k.shape
    nkb = s // TKB
    n_strips = 2 * TQ // SW
    return pl.pallas_call(
        _attn_kernel,
        grid=(b, ATT_HEADS, s // TQ),
        in_specs=[
            pl.BlockSpec((None, VAL_DIM, TQ), lambda bi, h, qi: (bi, h, qi)),
            pl.BlockSpec((None, s, VAL_DIM), lambda bi, h, qi: (bi, 0, h)),
            pl.BlockSpec((None, None, nkb, VAL_DIM, TKB), lambda bi, h, qi: (bi, h, 0, 0, 0)),
            pl.BlockSpec((None, 1, 128), lambda bi, h, qi: (h, 0, 0)),
            pl.BlockSpec((4, ATT_HEAD_DIM), lambda bi, h, qi: (0, 0)),
            pl.BlockSpec((VAL_DIM, 1), lambda bi, h, qi: (0, 0)),
        ],
        out_specs=pl.BlockSpec((None, TQ, VAL_DIM), lambda bi, h, qi: (bi, qi, h)),
        out_shape=jax.ShapeDtypeStruct((b, s, ATT_WIDTH), BF16),
        scratch_shapes=[
            pltpu.VMEM((2 * VAL_DIM, 2 * TQ), BF16),
            pltpu.VMEM((TKB, 128), BF16),
            pltpu.VMEM((1, TQ), F32),
            pltpu.VMEM((TQ // TKB, TKB, TQ), F32),
        ] +[pltpu.VMEM((1, SW), F32)] * (2 * n_strips) + [pltpu.VMEM((VAL_DIM, SW), F32)] * n_strips,
        compiler_params=pltpu.CompilerParams(
            dimension_semantics=("parallel", "parallel", "arbitrary"),
            vmem_limit_bytes=VMEM_LIMIT),
        name="diff_attention",
    )(qt, k, vt, slopes, lamvec, subln_g_col)


def _lru_kernel(xl_ref, gl_ref, cw_ref, cb_ref, wg_ref, bg_ref, lam_ref,
                o_ref, ext_ref, h_ref):
    c = pl.program_id(1)
    tc = xl_ref.shape[0]
    w = LRU_WIDTH

    @pl.when(c == 0)
    def _():
        ext_ref[0:8, :] = jnp.zeros((8, w), F32)
        h_ref[...] = jnp.zeros(h_ref.shape, F32)

    ext_ref[8:8 + tc, :] = xl_ref[...]
    cw = cw_ref[...]
    xc = cb_ref[...] + cw[0:1] * ext_ref[5:5 + tc, :]
    for j in range(1, CONV_WIDTH):
        xc = xc + cw[j:j + 1] * ext_ref[5 + j:5 + j + tc, :]
    tail = ext_ref[tc:tc + 8, :]

    gates = jnp.dot(xc.astype(BF16), wg_ref[...], preferred_element_type=F32) + bg_ref[...]
    r = jax.nn.sigmoid(gates[:, 0:w])
    ig = jax.nn.sigmoid(gates[:, w:2 * w])
    z = -lam_ref[...]
    softplus = jnp.maximum(z, 0.0) + jnp.log1p(jnp.exp(-jnp.abs(z)))
    log_a = (-LRU_C) * r * softplus
    a = jnp.exp(log_a)
    u = jnp.sqrt(-jnp.tanh(log_a) * (a * a + 1.0)) * (ig * xc)

    row = lax.broadcasted_iota(I32, (tc, w), 0)
    d = 1
    while d < tc:
        a_sh = pltpu.roll(a, d, axis=0)
        u_sh = pltpu.roll(u, d, axis=0)
        valid = row >= d
        u = jnp.where(valid, a * u_sh + u, u)
        a = jnp.where(valid, a * a_sh, a)
        d *= 2
    h = u + a * h_ref[0:1, :]

    gl = gl_ref[...]
    gelu = 0.5 * gl * (1.0 + jnp.tanh(math.sqrt(2.0 / math.pi) * (gl + 0.044715 * (gl * gl * gl))))
    o_ref[...] = (h * gelu).astype(o_ref.dtype)

    h_ref[0:1, :] = h[tc - 1:tc, :]
    ext_ref[0:8, :] = tail


def _lru(xl, gl, conv_w, conv_b, w_gate_bf, b_gate, lam):
    b, s, w = xl.shape
    tc = TC_LRU
    blk = lambda bi, c: (bi, c, 0)
    fixed = lambda bi, c: (0, 0)
    return pl.pallas_call(
        _lru_kernel,
        grid=(b, s // tc),
        in_specs=[
            pl.BlockSpec((None, tc, w), blk),
            pl.BlockSpec((None, tc, w), blk),
            pl.BlockSpec((CONV_WIDTH, w), fixed),
            pl.BlockSpec((1, w), fixed),
            pl.BlockSpec((w, 2 * w), fixed),
            pl.BlockSpec((1, 2 * w), fixed),
            pl.BlockSpec((1, w), fixed),
        ],
        out_specs=pl.BlockSpec((None, tc, w), blk),
        out_shape=jax.ShapeDtypeStruct((b, s, w), BF16),
        scratch_shapes=[pltpu.VMEM((tc + 8, w), F32), pltpu.VMEM((8, w), F32)],
        compiler_params=pltpu.CompilerParams(
            dimension_semantics=("parallel", "arbitrary"), vmem_limit_bytes=VMEM_LIMIT),
        name="rg_lru",
    )(xl, gl, conv_w, conv_b, w_gate_bf, b_gate, lam)


def _outproj_kernel(x_ref, g0_ref, b0_ref, att_ref, rec_ref, wo_ref, g1_ref, b1_ref,
                    wr_ref, br_ref, x1_ref, e4_ref, g4_ref, r4_ref, cnt_ref, carry_ref):
    i = pl.program_id(0)
    tm = x_ref.shape[0]

    @pl.when(i == 0)
    def _():
        carry_ref[...] = jnp.zeros(carry_ref.shape, F32)

    x0 = _layer_norm(x_ref[...], g0_ref[...], b0_ref[...])
    mixed = (jnp.dot(att_ref[...], wo_ref[0:ATT_WIDTH, :], preferred_element_type=F32)
             + jnp.dot(rec_ref[...], wo_ref[ATT_WIDTH:, :], preferred_element_type=F32))
    x1 = _layer_norm(ALPHA * x0 + mixed, g1_ref[...], b1_ref[...])
    _store_row_tiles(x1_ref, x1)

    logits = jnp.dot(x1, wr_ref[...], preferred_element_type=F32,
                     precision=lax.Precision.HIGHEST) + br_ref[...]
    lane = lax.broadcasted_iota(I32, logits.shape, 1)
    work = logits
    vals, idxs, hots = [], [], []
    for _ in range(TOP_K):
        mx = jnp.max(work, axis=-1, keepdims=True)
        idx = jnp.min(jnp.where(work == mx, lane, N_EXPERTS), axis=-1, keepdims=True)
        hot = lane == idx
        vals.append(mx)
        idxs.append(idx)
        hots.append(hot)
        work = jnp.where(hot, -jnp.inf, work)
    exps = [jnp.exp(v - vals[0]) for v in vals]
    denom = exps[0] + exps[1] + exps[2] + exps[3]

    mask = jnp.zeros(logits.shape, F32)
    for hot in hots:
        mask = mask + hot.astype(F32)
    tr = lax.broadcasted_iota(I32, (tm, tm), 0)
    tc = lax.broadcasted_iota(I32, (tm, tm), 1)
    lower = jnp.where(tc < tr, 1.0, 0.0).astype(BF16)
    excl = jnp.dot(lower, mask.astype(BF16), preferred_element_type=F32) + carry_ref[0:1, :]
    for kk in range(TOP_K):
        e4_ref[:, kk:kk + 1] = idxs[kk]
        g4_ref[:, kk:kk + 1] = exps[kk] / denom
        rank = jnp.sum(jnp.where(hots[kk], excl, 0.0), axis=-1, keepdims=True)
        r4_ref[:, kk:kk + 1] = rank.astype(I32)
    total = carry_ref[0:1, :] + jnp.sum(mask, axis=0, keepdims=True)
    carry_ref[0:1, :] = total
    cnt_ref[...] = total


def _outproj_router(x2, g0, b0, att, rec, wo_bf, g1, b1, w_router, b_router):
    t = x2.shape[0]
    tm = TM_PROJ
    row = lambda i: (i, 0)
    fixed = lambda i: (0, 0)
    return pl.pallas_call(
        _outproj_kernel,
        grid=(t // tm,),
        in_specs=[
            pl.BlockSpec((tm, D_MODEL), row),
            pl.BlockSpec((1, D_MODEL), fixed),
            pl.BlockSpec((1, D_MODEL), fixed),
            pl.BlockSpec((tm, ATT_WIDTH), row),
            pl.BlockSpec((tm, LRU_WIDTH), row),
            pl.BlockSpec((D_MODEL, D_MODEL), fixed),
            pl.BlockSpec((1, D_MODEL), fixed),
            pl.BlockSpec((1, D_MODEL), fixed),
            pl.BlockSpec((D_MODEL, N_EXPERTS), fixed),
            pl.BlockSpec((1, N_EXPERTS), fixed),
        ],
        out_specs=[
            pl.BlockSpec((tm * LANE_TILES, 128), row),
            pl.BlockSpec((tm, TOP_K), row),
            pl.BlockSpec((tm, TOP_K), row),
            pl.BlockSpec((tm, TOP_K), row),
            pl.BlockSpec((1, N_EXPERTS), fixed),
        ],
        out_shape=[
            jax.ShapeDtypeStruct((t * LANE_TILES, 128), F32),
            jax.ShapeDtypeStruct((t, TOP_K), I32),
            jax.ShapeDtypeStruct((t, TOP_K), F32),
            jax.ShapeDtypeStruct((t, TOP_K), I32),
            jax.ShapeDtypeStruct((1, N_EXPERTS), F32),
        ],
        scratch_shapes=[pltpu.VMEM((8, N_EXPERTS), F32)],
        compiler_params=pltpu.CompilerParams(
            dimension_semantics=("arbitrary",), vmem_limit_bytes=VMEM_LIMIT),
        name="outproj_ln_router",
    )(x2, g0, b0, att, rec, wo_bf, g1, b1, w_router, b_router)


def _tile_rows(row):
    return pl.ds(pl.multiple_of(row * LANE_TILES, LANE_TILES), LANE_TILES)


def _invert_kernel(dest_ref, rt_ref, zero_ref, sem):
    i = pl.program_id(0)
    tm = dest_ref.shape[1] // TOP_K

    @pl.when(i == 0)
    def _():
        zero_ref[...] = jnp.zeros(zero_ref.shape, I32)
        clear = pltpu.make_async_copy(zero_ref, rt_ref, sem)
        clear.start()
        clear.wait()

    def body(t, carry):
        for kk in range(TOP_K):
            rt_ref[dest_ref[0, t * TOP_K + kk]] = i * tm + t
        return carry
    lax.fori_loop(0, tm, body, 0, unroll=4)


def _invert(dest3, n_rows):
    nt, _, per = dest3.shape
    return pl.pallas_call(
        _invert_kernel,
        grid=(nt,),
        in_specs=[pl.BlockSpec((None, 1, per), lambda i: (i, 0, 0), memory_space=pltpu.SMEM)],
        out_specs=pl.BlockSpec(memory_space=pltpu.SMEM),
        out_shape=jax.ShapeDtypeStruct((n_rows,), I32),
        scratch_shapes=[pltpu.VMEM((n_rows,), I32), pltpu.SemaphoreType.DMA(())],
        compiler_params=pltpu.CompilerParams(dimension_semantics=("arbitrary",)),
        name="moe_invert",
    )(dest3)


def _expert_kernel(first_ref, nblk_ref, rt_ref, wgu_ref, bg_ref, bl_ref, wd_ref, bd_ref,
                   x_hbm, ys_hbm, wt_ref, wg_ref, wl_ref, wdb_ref, xb_ref, xbuf, ybuf, xsem, ysem):
    e = pl.program_id(0)
    n_exp = pl.num_programs(0)
    d, f2 = wgu_ref.shape
    f = f2 // 2
    ch = XPOSE_CHUNK
    first = first_ref[e]
    nblk = nblk_ref[e]
    n_live = first_ref[n_exp - 1] + nblk_ref[n_exp - 1]
    n_blocks = ys_hbm.shape[0] // (ROW_BLOCK * LANE_TILES)
    block_rows = ROW_BLOCK * LANE_TILES

    def gather_start(g, slot):
        for r0 in range(0, ROW_BLOCK, GATHER_BATCH):
            toks = [rt_ref[g * ROW_BLOCK + r0 + i] for i in range(GATHER_BATCH)]
            for i, tok in enumerate(toks):
                pltpu.make_async_copy(
                    x_hbm.at[_tile_rows(tok), :],
                    xbuf.at[slot, pl.ds((r0 + i) * LANE_TILES, LANE_TILES), :],
                    xsem.at[slot]).start()

    def gather_wait(slot):
        pltpu.make_async_copy(x_hbm.at[pl.ds(0, block_rows), :], xbuf.at[slot],
                              xsem.at[slot]).wait()

    def y_copy(g, slot):
        start = pl.multiple_of(g * block_rows, block_rows)
        return pltpu.make_async_copy(ybuf.at[slot], ys_hbm.at[pl.ds(start, block_rows), :],
                                     ysem.at[slot])

    @pl.when(jnp.logical_and(nblk > 0, first == 0))
    def _():
        gather_start(0, 0)

    @pl.when(nblk > 0)
    def _():
        for j in range(d // 128):
            rows = slice(j * 128, (j + 1) * 128)
            for c in range(f2 // ch):
                wt_ref[j, c * ch:(c + 1) * ch, :] = wgu_ref[rows, c * ch:(c + 1) * ch].T
            for c in range(f // ch):
                gate_rows = wt_ref[j, pl.ds(2 * c * ch, ch, stride=2), :]
                lin_rows = wt_ref[j, pl.ds(2 * c * ch + 1, ch, stride=2), :]
                wg_ref[rows, c * ch:(c + 1) * ch] = gate_rows.T.astype(BF16)
                wl_ref[rows, c * ch:(c + 1) * ch] = lin_rows.T.astype(BF16)
        wdb_ref[...] = wd_ref[...].astype(BF16)

    def block(b, carry):
        g = first + b
        for slot in range(2):
            @pl.when(g % 2 == slot)
            def _():
                block_body(g, slot)
        return carry

    def block_body(g, slot):
        @pl.when(g >= 2)
        def _():
            y_copy(g - 2, slot).wait()

        gather_wait(slot)
        xb_ref[...] = _load_row_tiles(xbuf.at[slot], ROW_BLOCK).astype(BF16)
        gather_start(jnp.minimum(g + 1, n_live - 1), 1 - slot)
        x = xb_ref[...]
        hg =jnp.dot(x, wg_ref[...], preferred_element_type=F32) + bg_ref[...]
        hl = jnp.dot(x, wl_ref[...], preferred_element_type=F32) + bl_ref[...]
        xg = jnp.minimum(hg, SWIGLU_LIMIT)
        xl = jnp.clip(hl, -SWIGLU_LIMIT, SWIGLU_LIMIT)
        act = xg * jax.nn.sigmoid(SWIGLU_ALPHA * xg) * (xl + 1.0)
        y = jnp.dot(act.astype(BF16), wdb_ref[...], preferred_element_type=F32) + bd_ref[...]
        _store_row_tiles(ybuf.at[slot], y)
        y_copy(g, slot).start()

    lax.fori_loop(0, nblk, block, 0)

    @pl.when(e == n_exp - 1)
    def _():
        gather_wait(n_live % 2)
        y_copy(0, 0).wait()
        y_copy(0, 1).wait()
        ybuf[0] = jnp.zeros(ybuf.shape[1:], F32)
        for t in range(N_EXPERTS):
            @pl.when(n_live + t < n_blocks)
            def _():
                y_copy(n_live + t, 0).start()
        for t in range(N_EXPERTS):
            @pl.when(n_live + t < n_blocks)
            def _():
                y_copy(n_live + t, 0).wait()


def _experts(first_blk, n_blk, row_tok, x1r, w_gu, bg, bl, w_down, bd):
    d = D_MODEL
    f2 = w_gu.shape[2]
    f = f2 // 2
    n_rows = row_tok.shape[0]
    wmap = lambda e, fb, nb_: (e, 0, 0)
    grid_spec = pltpu.PrefetchScalarGridSpec(
        num_scalar_prefetch=2,
        grid=(N_EXPERTS,),
        in_specs=[
            pl.BlockSpec(memory_space=pltpu.SMEM),
            pl.BlockSpec((None, d, f2), wmap),
            pl.BlockSpec((None, 1, f), wmap),
            pl.BlockSpec((None, 1, f), wmap),
            pl.BlockSpec((None, f, d), wmap),
            pl.BlockSpec((None, 1, d), wmap),
            pl.BlockSpec(memory_space=pl.ANY),
        ],
        out_specs=pl.BlockSpec(memory_space=pl.ANY),
        scratch_shapes=[
            pltpu.VMEM((d // 128, f2, 128), F32),
            pltpu.VMEM((d, f), BF16),
            pltpu.VMEM((d, f), BF16),
            pltpu.VMEM((f, d), BF16),
            pltpu.VMEM((ROW_BLOCK, d), BF16),
            pltpu.VMEM((2, ROW_BLOCK * LANE_TILES, 128), F32),
            pltpu.VMEM((2, ROW_BLOCK * LANE_TILES, 128), F32),
            pltpu.SemaphoreType.DMA((2,)),
            pltpu.SemaphoreType.DMA((2,)),
        ],
    )
    return pl.pallas_call(
        _expert_kernel,
        grid_spec=grid_spec,
        out_shape=jax.ShapeDtypeStruct((n_rows * LANE_TILES, 128), F32),
        compiler_params=pltpu.CompilerParams(
            dimension_semantics=("arbitrary",), vmem_limit_bytes=VMEM_LIMIT_EXPERTS),
        name="moe_experts",
    )(first_blk, n_blk, row_tok, w_gu, bg, bl, w_down, bd, x1r)


def _combine_start(ys_hbm, idx_ref, buf_ref, sem, tm):
    def body(t, carry):
        for kk in range(TOP_K):
            pltpu.make_async_copy(ys_hbm.at[_tile_rows(idx_ref[0, t * TOP_K + kk]), :],
                                  buf_ref.at[kk, _tile_rows(t), :], sem).start()
        return carry
    lax.fori_loop(0, tm, body, 0, unroll=2)


def _combine_wait(ys_hbm, buf_ref, sem, tm):
    for kk in range(TOP_K):
        pltpu.make_async_copy(ys_hbm.at[pl.ds(0, tm * LANE_TILES), :], buf_ref.at[kk], sem).wait()


def _combine_kernel(idx_cur, idx_nxt, x1_ref, g4_ref, g2_ref, b2_ref, ys_hbm,
                    o_ref, buf_ref, sem):
    i = pl.program_id(0)
    n = pl.num_programs(0)
    slot = i % 2
    tm = o_ref.shape[0]

    @pl.when(i == 0)
    def _():
        _combine_start(ys_hbm, idx_cur, buf_ref.at[0], sem.at[0], tm)

    @pl.when(i + 1 < n)
    def _():
        _combine_start(ys_hbm, idx_nxt, buf_ref.at[1 - slot], sem.at[1 - slot], tm)

    _combine_wait(ys_hbm, buf_ref.at[slot], sem.at[slot], tm)
    g4 = g4_ref[...]
    ffn = g4[:, 0:1] * _load_row_tiles(buf_ref.at[slot, 0], tm)
    for kk in range(1, TOP_K):
        ffn = ffn + g4[:, kk:kk + 1] * _load_row_tiles(buf_ref.at[slot, kk], tm)
    x1 = _load_row_tiles(x1_ref, tm)
    o_ref[...] = _layer_norm(ALPHA * x1 + ffn, g2_ref[...], b2_ref[...])


def _combine(dest3, x1r, g4, g2, b2, ys):
    d = D_MODEL
    t = x1r.shape[0] // LANE_TILES
    tm = TM_COMB
    nt = t // tm
    row = lambda i: (i, 0)
    fixed = lambda i: (0, 0)
    return pl.pallas_call(
        _combine_kernel,
        grid=(nt,),
        in_specs=[
            pl.BlockSpec((None, 1, tm * TOP_K), lambda i: (i, 0, 0), memory_space=pltpu.SMEM),
            pl.BlockSpec((None, 1, tm * TOP_K), lambda i: (jnp.minimum(i + 1, nt - 1), 0, 0),
                         memory_space=pltpu.SMEM),
            pl.BlockSpec((tm * LANE_TILES, 128), row),
            pl.BlockSpec((tm, TOP_K), row),
            pl.BlockSpec((1, d), fixed),
            pl.BlockSpec((1, d), fixed),
            pl.BlockSpec(memory_space=pl.ANY),
        ],
        out_specs=pl.BlockSpec((tm, d), row),
        out_shape=jax.ShapeDtypeStruct((t, d), F32),
        scratch_shapes=[pltpu.VMEM((2, TOP_K, tm * LANE_TILES, 128), F32),
                        pltpu.SemaphoreType.DMA((2,))],
        compiler_params=pltpu.CompilerParams(
            dimension_semantics=("arbitrary",), vmem_limit_bytes=VMEM_LIMIT),
        name="moe_combine_ln",
    )(dest3, dest3, x1r, g4, g2, b2, ys)


def kernel(x, ln0_g, ln0_b, w_in, conv_w, conv_b, w_rg_a, b_rg_a, w_rg_x, b_rg_x, lru_lambda, lam_q1, lam_k1, lam_q2, lam_k2, subln_g, w_out, ln1_g, ln1_b, w_router, b_router, w_gu, b_gu, w_down, b_down, ln2_g, ln2_b):
    bsz, seq, d = x.shape
    t = bsz * seq
    x2 = x.reshape(t, d)
    g0 = ln0_g.reshape(1, d)
    b0 = ln0_b.reshape(1, d)
    l = 0

    w_in_bf = w_in[l].astype(BF16)
    w_out_bf = w_out[l].astype(BF16)
    wa = jax.scipy.linalg.block_diag(*[w_rg_a[l, n] for n in range(LRU_BLOCKS)])
    wx = jax.scipy.linalg.block_diag(*[w_rg_x[l, n] for n in range(LRU_BLOCKS)])
    w_gate_bf = jnp.concatenate([wa, wx], axis=1).astype(BF16)
    b_gate = jnp.concatenate([b_rg_a[l].reshape(1, -1), b_rg_x[l].reshape(1, -1)], axis=1)
    lamvec = jnp.stack([lam_q1[l], lam_k1[l], lam_q2[l], lam_k2[l]]).astype(F32)
    slopes = jnp.asarray([2.0 ** (-8.0 * (i + 1) / ATT_HEADS) for i in range(ATT_HEADS)], F32)
    slopes = jnp.broadcast_to(slopes[:, None, None], (ATT_HEADS, 1, 128))
    bg = b_gu[l, :, None, 0::2]
    bl = b_gu[l, :, None, 1::2]
    bd = b_down[l][:, None, :]

    qt, k, vt, xl, gl = _inproj(x, g0, b0, w_in_bf)
    att = _attention(qt, k, vt, slopes, lamvec, subln_g[l].reshape(-1, 1))
    rec = _lru(xl, gl, conv_w[l], conv_b[l].reshape(1, -1), w_gate_bf, b_gate,
               lru_lambda[l].reshape(1, -1))
    x1r, e4, g4, r4, cnt = _outproj_router(
        x2, g0, b0, att.reshape(t, -1), rec.reshape(t, -1), w_out_bf,
        ln1_g[l].reshape(1, d), ln1_b[l].reshape(1, d), w_router[l], b_router[l].reshape(1, -1))

    n_assign = t * TOP_K
    nb = (n_assign + N_EXPERTS * (ROW_BLOCK - 1)) // ROW_BLOCK + 1
    counts = cnt[0].astype(I32)
    padded = (counts + ROW_BLOCK - 1) // ROW_BLOCK * ROW_BLOCK
    pends = jnp.cumsum(padded)
    pstarts = pends - padded
    dest = pstarts[e4] + r4
    row_tok = _invert(dest.reshape(t // TM_INV, 1, TM_INV * TOP_K), nb * ROW_BLOCK)
    ys = _experts((pstarts // ROW_BLOCK).astype(I32), (padded // ROW_BLOCK).astype(I32),
                  row_tok, x1r, w_gu[l], bg, bl, w_down[l], bd)
    out = _combine(dest.reshape(t // TM_COMB, 1, TM_COMB * TOP_K), x1r, g4,
                   ln2_g[l].reshape(1, d), ln2_b[l].reshape(1, d), ys)
    return out.reshape(bsz, seq, d)
```

```python
import math

import jax
import jax.numpy as jnp
from jax import lax
from jax.experimental import pallas as pl
from jax.experimental.pallas import tpu as pltpu

F32 = jnp.float32
BF16 = jnp.bfloat16
I32 = jnp.int32

D_MODEL = 1024
ATT_WIDTH = 512
ATT_HEAD_DIM = 64
ATT_HEADS = 4
VAL_DIM = 2 * ATT_HEAD_DIM
LRU_WIDTH = 512
LRU_BLOCKS = 8
LRU_C = 8.0
CONV_WIDTH = 4
CHUNK_SHIFT = 6
N_EXPERTS = 32
TOP_K = 4
D_EXPERT = 1024
SWIGLU_LIMIT = 7.0
SWIGLU_ALPHA = 1.702
DEPTH = 1
ALPHA = (2.0 * DEPTH) ** 0.25
LN_EPS = 1e-5
SUBLN_EPS = 1e-5
LAM_INIT = 0.8 - 0.6 * math.exp(-0.3 * 0)
NEG_BIG = -1e30
LOG2E = math.log2(math.e)

TM_PROJ = 512
TQ = 1024
TKB = 128
SW = 256
TC_LRU = 512
ROW_BLOCK = 256
TM_COMB = 128
TM_INV = 2048
XPOSE_CHUNK = 512

VMEM_LIMIT = 48 * 1024 * 1024
VMEM_LIMIT_EXPERTS = 56 * 1024 * 1024


LANE_TILES = D_MODEL // 128


def _store_row_tiles(ref, val):
    n = val.shape[0]
    for j in range(LANE_TILES):
        ref[pl.ds(j, n, stride=LANE_TILES), :] = val[:, j * 128:(j + 1) * 128]


def _load_row_tiles(ref, n):
    return jnp.concatenate(
        [ref[pl.ds(j, n, stride=LANE_TILES), :] for j in range(LANE_TILES)], axis=1)


def _layer_norm(x, g, b):
    mu = jnp.mean(x, axis=-1, keepdims=True)
    xc = x - mu
    var = jnp.mean(xc * xc, axis=-1, keepdims=True)
    return xc * lax.rsqrt(var + LN_EPS) * g + b


def _inproj_kernel(x_ref, g_ref, b_ref, w_ref, qt_ref, k_ref, vt_ref, xl_ref, gl_ref):
    x0 = _layer_norm(x_ref[...], g_ref[...], b_ref[...])
    p = jnp.dot(x0.astype(BF16), w_ref[...], preferred_element_type=F32)
    a = ATT_WIDTH
    qt_ref[...] = (p[:, 0:a] * (ATT_HEAD_DIM ** -0.5 * LOG2E)).T.astype(BF16)
    k_ref[...] = p[:, a:2 * a].astype(BF16)
    vt = p[:, 2 * a:3 * a].T.astype(BF16)
    for h in range(ATT_HEADS):
        for c in range(vt_ref.shape[1]):
            vt_ref[h, c] = vt[h * VAL_DIM:(h + 1) * VAL_DIM, c * TKB:(c + 1) * TKB]
    xl_ref[...] = p[:, 3 * a:3 * a + LRU_WIDTH]
    gl_ref[...] = p[:, 3 * a + LRU_WIDTH:]


def _inproj(x, g, b, w_bf):
    bsz, seq, _ = x.shape
    n = w_bf.shape[1]
    tm = TM_PROJ
    row = lambda bi, i: (bi, i, 0)
    col = lambda bi, i: (bi, 0, i)
    fixed = lambda bi, i: (0, 0)
    return pl.pallas_call(
        _inproj_kernel,
        grid=(bsz, seq // tm),
        in_specs=[
            pl.BlockSpec((None, tm, D_MODEL), row),
            pl.BlockSpec((1, D_MODEL), fixed),
            pl.BlockSpec((1, D_MODEL), fixed),
            pl.BlockSpec((D_MODEL, n), fixed),
        ],
        out_specs=[
            pl.BlockSpec((None, ATT_WIDTH, tm), col),
            pl.BlockSpec((None, tm, ATT_WIDTH), row),
            pl.BlockSpec((None, ATT_HEADS, tm // TKB, VAL_DIM, TKB),
                         lambda bi, i: (bi, 0, i, 0, 0)),
            pl.BlockSpec((None, tm, LRU_WIDTH), row),
            pl.BlockSpec((None, tm, LRU_WIDTH), row),
        ],
        out_shape=[
            jax.ShapeDtypeStruct((bsz, ATT_WIDTH, seq), BF16),
            jax.ShapeDtypeStruct((bsz, seq, ATT_WIDTH), BF16),
            jax.ShapeDtypeStruct((bsz, ATT_HEADS, seq // TKB, VAL_DIM, TKB), BF16),
            jax.ShapeDtypeStruct((bsz, seq, LRU_WIDTH), F32),
            jax.ShapeDtypeStruct((bsz, seq, LRU_WIDTH), F32),
        ],
        compiler_params=pltpu.CompilerParams(
            dimension_semantics=("parallel", "parallel"), vmem_limit_bytes=VMEM_LIMIT),
        name="ln_inproj",
    )(x, g, b, w_bf)


def _attn_kernel(qt_ref, k_ref, vt_ref, slope_ref, lam_ref, g_ref,
                 o_ref, q2_ref, kx_ref, qbias_ref, diag_ref, *stat_refs):
    qi = pl.program_id(2)
    tq = qt_ref.shape[1]
    n_strips = 2 * tq // SW
    blocks_per_q = tq // TKB
    cl = slope_ref[0:1, 0:1] * LOG2E

    @pl.when(qi == 0)
    def _():
        c = lax.broadcasted_iota(I32, (TKB, 128), 0)
        lane = lax.broadcasted_iota(I32, (TKB, 128), 1)
        w = c.astype(F32) * cl
        w_hi = w.astype(BF16).astype(F32)
        w_mid = (w - w_hi).astype(BF16).astype(F32)
        w_lo = w - w_hi - w_mid
        kx_ref[...] = jnp.where(
            lane == 0, w_hi, jnp.where(lane == 1, w_mid, jnp.where(lane == 2, w_lo, 0.0))
        ).astype(BF16)
        r1 = lax.broadcasted_iota(I32, (1, tq), 1)
        qbias_ref[...] = r1.astype(F32) * (-cl)
        ck0 = lax.broadcasted_iota(I32, (TKB, tq), 0)
        r = lax.broadcasted_iota(I32, (TKB, tq), 1)
        for kd in range(blocks_per_q):
            ck = ck0 + kd * TKB
            allowed = (ck >> CHUNK_SHIFT) <= (r >> CHUNK_SHIFT)
            bias = (jnp.abs(r - ck) + ck0).astype(F32) * (-cl)
            diag_ref[kd] = jnp.where(allowed, bias, NEG_BIG)

    m_refs = stat_refs[0:n_strips]
    l_refs = stat_refs[n_strips:2 * n_strips]
    acc_refs = stat_refs[2 * n_strips:3 * n_strips]
    for j in range(n_strips):
        m_refs[j][...] = jnp.full(m_refs[j].shape, NEG_BIG, F32)
        l_refs[j][...] = jnp.zeros(l_refs[j].shape, F32)
        acc_refs[j][...] = jnp.zeros(acc_refs[j].shape, F32)
    qt = qt_ref[...]
    frow = lax.broadcasted_iota(I32, qt.shape, 0)
    zero = jnp.zeros_like(qt)
    q2_ref[0:VAL_DIM, 0:tq] = jnp.where(frow < ATT_HEAD_DIM, qt, zero)
    q2_ref[0:VAL_DIM, tq:2 * tq] = jnp.where(frow >= ATT_HEAD_DIM, qt, zero)
    xrow = lax.broadcasted_iota(I32, (VAL_DIM, 2 * tq), 0)
    q2_ref[VAL_DIM:2 * VAL_DIM, :] = jnp.where(xrow < 3, 1.0, 0.0).astype(BF16)

    def load_keys(kb):
        k_blk = k_ref[pl.ds(pl.multiple_of(kb * TKB, TKB), TKB), :]
        return jnp.concatenate([k_blk, kx_ref[...]], axis=1), vt_ref[kb]

    def strip(j, k_ext, vt_blk, sc_bias, shift):
        ls = slice(j * SW, (j + 1) * SW)
        sc = jnp.dot(k_ext, q2_ref[:, ls], preferred_element_type=F32)
        if sc_bias is not None:
            sc = sc + sc_bias
        m_prev = m_refs[j][...]
        m_new = jnp.maximum(m_prev, jnp.max(sc, axis=0, keepdims=True) + shift)
        alpha = jnp.exp2(m_prev - m_new)
        p = jnp.exp2(sc - (m_new - shift))
        l_refs[j][...] = alpha * l_refs[j][...] + jnp.sum(p, axis=0, keepdims=True)
        pv = jnp.dot(vt_blk, p.astype(BF16), preferred_element_type=F32)
        acc_refs[j][...] = alpha * acc_refs[j][...] + pv
        m_refs[j][...] = m_new

    def off_diag_group(g, carry):
        for kd in range(blocks_per_q):
            kb = g * blocks_per_q + kd
            k_ext, vt_blk = load_keys(kb)
            block_shift = jnp.full((1, 1), qi * tq - kb * TKB, I32).astype(F32) * (-cl)
            for j in range(n_strips):
                jq = j % (tq // SW)
                strip(j, k_ext, vt_blk, None,
                      qbias_ref[:, jq * SW:(jq + 1) * SW] + block_shift)
        return carry
    lax.fori_loop(0, qi, off_diag_group, 0)

    for kd in range(blocks_per_q):
        k_ext, vt_blk = load_keys(qi * blocks_per_q + kd)
        for j in range(n_strips):
            jq = j % (tq // SW)
            if (jq + 1) * SW > kd * TKB:
                strip(j, k_ext, vt_blk, diag_ref[kd, :, jq * SW:(jq + 1) * SW], 0.0)

    lv = lam_ref[...]
    lam = (jnp.exp(jnp.sum(lv[0:1] * lv[1:2], axis=-1, keepdims=True))
           - jnp.exp(jnp.sum(lv[2:3] * lv[3:4], axis=-1, keepdims=True)) + LAM_INIT)
    o2 = [acc_refs[j][...] / l_refs[j][...] for j in range(n_strips)]
    half = n_strips // 2
    o = jnp.concatenate([o2[j] - lam * o2[half + j] for j in range(half)], axis=1)
    o = o * lax.rsqrt(jnp.mean(o * o, axis=0, keepdims=True) + SUBLN_EPS)
    o = o * (g_ref[...] * (1.0 - LAM_INIT))
    o_ref[...] = o.T.astype(o_ref.dtype)


def _attention(qt, k, vt, slopes, lamvec, subln_g_col):
    b, s, _ = k.shape
    nkb = s // TKB
    n_strips = 2 * TQ // SW
    return pl.pallas_call(
        _attn_kernel,
        grid=(b, ATT_HEADS, s // TQ),
        in_specs=[
            pl.BlockSpec((None, VAL_DIM, TQ), lambda bi, h, qi: (bi, h, qi)),
            pl.BlockSpec((None, s, VAL_DIM), lambda bi, h, qi: (bi, 0, h)),
            pl.BlockSpec((None, None, nkb, VAL_DIM, TKB), lambda bi, h, qi: (bi, h, 0, 0, 0)),
            pl.BlockSpec((None, 1, 128), lambda bi, h, qi: (h, 0, 0)),
            pl.BlockSpec((4, ATT_HEAD_DIM), lambda bi, h, qi: (0, 0)),
            pl.BlockSpec((VAL_DIM, 1), lambda bi, h, qi: (0, 0)),
        ],
        out_specs=pl.BlockSpec((None, TQ, VAL_DIM), lambda bi, h, qi: (bi, qi, h)),
        out_shape=jax.ShapeDtypeStruct((b, s, ATT_WIDTH), BF16),
        scratch_shapes=[
            pltpu.VMEM((2 * VAL_DIM, 2 * TQ), BF16),
            pltpu.VMEM((TKB, 128), BF16),
            pltpu.VMEM((1, TQ), F32),
            pltpu.VMEM((TQ // TKB, TKB, TQ), F32),
        ] +[pltpu.VMEM((1, SW), F32)] * (2 * n_strips) + [pltpu.VMEM((VAL_DIM, SW), F32)] * n_strips,
        compiler_params=pltpu.CompilerParams(
            dimension_semantics=("parallel", "parallel", "arbitrary"),
            vmem_limit_bytes=VMEM_LIMIT),
        name="diff_attention",
    )(qt, k, vt, slopes, lamvec, subln_g_col)


def _lru_kernel(xl_ref, gl_ref, cw_ref, cb_ref, wg_ref, bg_ref, lam_ref,
                o_ref, ext_ref, h_ref):
    c = pl.program_id(1)
    tc = xl_ref.shape[0]
    w = LRU_WIDTH

    @pl.when(c == 0)
    def _():
        ext_ref[0:8, :] = jnp.zeros((8, w), F32)
        h_ref[...] = jnp.zeros(h_ref.shape, F32)

    ext_ref[8:8 + tc, :] = xl_ref[...]
    cw = cw_ref[...]
    xc = cb_ref[...] + cw[0:1] * ext_ref[5:5 + tc, :]
    for j in range(1, CONV_WIDTH):
        xc = xc + cw[j:j + 1] * ext_ref[5 + j:5 + j + tc, :]
    tail = ext_ref[tc:tc + 8, :]

    gates = jnp.dot(xc.astype(BF16), wg_ref[...], preferred_element_type=F32) + bg_ref[...]
    r = jax.nn.sigmoid(gates[:, 0:w])
    ig = jax.nn.sigmoid(gates[:, w:2 * w])
    z = -lam_ref[...]
    softplus = jnp.maximum(z, 0.0) + jnp.log1p(jnp.exp(-jnp.abs(z)))
    log_a = (-LRU_C) * r * softplus
    a = jnp.exp(log_a)
    u = jnp.sqrt(-jnp.tanh(log_a) * (a * a + 1.0)) * (ig * xc)

    row = lax.broadcasted_iota(I32, (tc, w), 0)
    d = 1
    while d < tc:
        a_sh = pltpu.roll(a, d, axis=0)
        u_sh = pltpu.roll(u, d, axis=0)
        valid = row >= d
        u = jnp.where(valid, a * u_sh + u, u)
        a = jnp.where(valid, a * a_sh, a)
        d *= 2
    h = u + a * h_ref[0:1, :]

    gl = gl_ref[...]
    gelu = 0.5 * gl * (1.0 + jnp.tanh(math.sqrt(2.0 / math.pi) * (gl + 0.044715 * (gl * gl * gl))))
    o_ref[...] = (h * gelu).astype(o_ref.dtype)

    h_ref[0:1, :] = h[tc - 1:tc, :]
    ext_ref[0:8, :] = tail


def _lru(xl, gl, conv_w, conv_b, w_gate_bf, b_gate, lam):
    b, s, w = xl.shape
    tc = TC_LRU
    blk = lambda bi, c: (bi, c, 0)
    fixed = lambda bi, c: (0, 0)
    return pl.pallas_call(
        _lru_kernel,
        grid=(b, s // tc),
        in_specs=[
            pl.BlockSpec((None, tc, w), blk),
            pl.BlockSpec((None, tc, w), blk),
            pl.BlockSpec((CONV_WIDTH, w), fixed),
            pl.BlockSpec((1, w), fixed),
            pl.BlockSpec((w, 2 * w), fixed),
            pl.BlockSpec((1, 2 * w), fixed),
            pl.BlockSpec((1, w), fixed),
        ],
        out_specs=pl.BlockSpec((None, tc, w), blk),
        out_shape=jax.ShapeDtypeStruct((b, s, w), BF16),
        scratch_shapes=[pltpu.VMEM((tc + 8, w), F32), pltpu.VMEM((8, w), F32)],
        compiler_params=pltpu.CompilerParams(
            dimension_semantics=("parallel", "arbitrary"), vmem_limit_bytes=VMEM_LIMIT),
        name="rg_lru",
    )(xl, gl, conv_w, conv_b, w_gate_bf, b_gate, lam)


def _outproj_kernel(x_ref, g0_ref, b0_ref, att_ref, rec_ref, wo_ref, g1_ref, b1_ref,
                    wr_ref, br_ref, x1_ref, e4_ref, g4_ref, r4_ref, cnt_ref, carry_ref):
    i = pl.program_id(0)
    tm = x_ref.shape[0]

    @pl.when(i == 0)
    def _():
        carry_ref[...] = jnp.zeros(carry_ref.shape, F32)

    x0 = _layer_norm(x_ref[...], g0_ref[...], b0_ref[...])
    mixed = (jnp.dot(att_ref[...], wo_ref[0:ATT_WIDTH, :], preferred_element_type=F32)
             + jnp.dot(rec_ref[...], wo_ref[ATT_WIDTH:, :], preferred_element_type=F32))
    x1 = _layer_norm(ALPHA * x0 + mixed, g1_ref[...], b1_ref[...])
    _store_row_tiles(x1_ref, x1)

    x1_hi = x1.astype(BF16)
    x1_lo = (x1 - x1_hi.astype(F32)).astype(BF16)
    nt = (((1,), (1,)), ((), ()))
    ne = N_EXPERTS
    hi_terms = lax.dot_general(wr_ref[...], x1_hi, nt, preferred_element_type=F32)
    lo_term = lax.dot_general(wr_ref[0:ne, :], x1_lo, nt, preferred_element_type=F32)
    logits = hi_terms[0:ne] + hi_terms[ne:2 * ne] + lo_term + br_ref[...]

    erow = lax.broadcasted_iota(I32, logits.shape, 0)
    work = logits
    vals, idxs, hots = [], [], []
    for _ in range(TOP_K):
        mx = jnp.max(work, axis=0, keepdims=True)
        idx = jnp.min(jnp.where(work == mx, erow, ne), axis=0, keepdims=True)
        hot = erow == idx
        vals.append(mx)
        idxs.append(idx)
        hots.append(hot)
        work = jnp.where(hot, -jnp.inf, work)
    exps = [jnp.exp(v - vals[0]) for v in vals]
    denom = exps[0] + exps[1] + exps[2] + exps[3]

    mask = jnp.zeros(logits.shape, F32)
    for hot in hots:
        mask = mask + hot.astype(F32)
    tr = lax.broadcasted_iota(I32, (tm, tm), 0)
    tc = lax.broadcasted_iota(I32, (tm, tm), 1)
    earlier = jnp.where(tr < tc, 1.0, 0.0).astype(BF16)
    carry = carry_ref[:, 0:1]
    excl = jnp.dot(mask.astype(BF16), earlier, preferred_element_type=F32) + carry
    ranks = [jnp.sum(jnp.where(hot, excl, 0.0), axis=0, keepdims=True) for hot in hots]
    pad_i = jnp.zeros((8 - TOP_K, tm), I32)
    pad_f = jnp.zeros((8 - TOP_K, tm), F32)
    e4_ref[...] = jnp.concatenate(idxs + [pad_i], axis=0)
    g4_ref[...] = jnp.concatenate([ex / denom for ex in exps] + [pad_f], axis=0)
    r4_ref[...] = jnp.concatenate([r.astype(I32) for r in ranks] + [pad_i], axis=0)
    total = carry + jnp.sum(mask, axis=1, keepdims=True)
    carry_ref[...] = jnp.broadcast_to(total, carry_ref.shape)
    cnt_ref[...] = jnp.broadcast_to(total, cnt_ref.shape)


def _outproj_router(x2, g0, b0, att, rec, wo_bf, g1, b1, w_router, b_router):
    t = x2.shape[0]
    tm = TM_PROJ
    row = lambda i: (i, 0)
    col = lambda i: (0, i)
    fixed = lambda i: (0, 0)
    return pl.pallas_call(
        _outproj_kernel,
        grid=(t // tm,),
        in_specs=[
            pl.BlockSpec((tm, D_MODEL), row),
            pl.BlockSpec((1, D_MODEL), fixed),
            pl.BlockSpec((1, D_MODEL), fixed),
            pl.BlockSpec((tm, ATT_WIDTH), row),
            pl.BlockSpec((tm, LRU_WIDTH), row),
            pl.BlockSpec((D_MODEL, D_MODEL), fixed),
            pl.BlockSpec((1, D_MODEL), fixed),
            pl.BlockSpec((1, D_MODEL), fixed),
            pl.BlockSpec((2 * N_EXPERTS, D_MODEL), fixed),
            pl.BlockSpec((N_EXPERTS, 1), fixed),
        ],
        out_specs=[
            pl.BlockSpec((tm * LANE_TILES, 128), row),
            pl.BlockSpec((8, tm), col),
            pl.BlockSpec((8, tm), col),
            pl.BlockSpec((8, tm), col),
            pl.BlockSpec((N_EXPERTS, 128), fixed),
        ],
        out_shape=[
            jax.ShapeDtypeStruct((t * LANE_TILES, 128), F32),
            jax.ShapeDtypeStruct((8, t), I32),
            jax.ShapeDtypeStruct((8, t), F32),
            jax.ShapeDtypeStruct((8, t), I32),
            jax.ShapeDtypeStruct((N_EXPERTS, 128), F32),
        ],
        scratch_shapes=[pltpu.VMEM((N_EXPERTS, 128), F32)],
        compiler_params=pltpu.CompilerParams(
            dimension_semantics=("arbitrary",), vmem_limit_bytes=VMEM_LIMIT),
        name="outproj_ln_router",
    )(x2, g0, b0, att, rec, wo_bf, g1, b1, w_router, b_router)


def _tile_rows(row):
    return pl.ds(pl.multiple_of(row * LANE_TILES, LANE_TILES), LANE_TILES)


def _invert_kernel(dest_ref, rt_ref, zero_ref, sem):
    i = pl.program_id(0)
    tm = dest_ref.shape[1] // TOP_K

    @pl.when(i == 0)
    def _():
        zero_ref[...] = jnp.zeros(zero_ref.shape, I32)
        clear = pltpu.make_async_copy(zero_ref, rt_ref, sem)
        clear.start()
        clear.wait()

    def body(t, carry):
        for kk in range(TOP_K):
            rt_ref[dest_ref[0, t * TOP_K + kk]] = i * tm + t
        return carry
    lax.fori_loop(0, tm, body, 0, unroll=4)


def _invert(dest3, n_rows):
    nt, _, per = dest3.shape
    return pl.pallas_call(
        _invert_kernel,
        grid=(nt,),
        in_specs=[pl.BlockSpec((None, 1, per), lambda i: (i, 0, 0), memory_space=pltpu.SMEM)],
        out_specs=pl.BlockSpec(memory_space=pltpu.SMEM),
        out_shape=jax.ShapeDtypeStruct((n_rows,), I32),
        scratch_shapes=[pltpu.VMEM((n_rows,), I32), pltpu.SemaphoreType.DMA(())],
        compiler_params=pltpu.CompilerParams(dimension_semantics=("arbitrary",)),
        name="moe_invert",
    )(dest3)


def _expert_kernel(first_ref, nblk_ref, rt_ref, wgu_ref, bg_ref, bl_ref, wd_ref, bd_ref,
                   x_hbm, ys_hbm, wt_ref, wg_ref, wl_ref, wdb_ref, xb_ref, xbuf, ybuf, xsem, ysem):
    e = pl.program_id(0)
    n_exp = pl.num_programs(0)
    d, f2 = wgu_ref.shape
    f = f2 // 2
    ch = XPOSE_CHUNK
    first = first_ref[e]
    nblk = nblk_ref[e]
    n_live = first_ref[n_exp - 1] + nblk_ref[n_exp - 1]
    n_blocks = ys_hbm.shape[0] // (ROW_BLOCK * LANE_TILES)
    block_rows = ROW_BLOCK * LANE_TILES

    def gather_start(g, slot):
        for r in range(ROW_BLOCK):
            pltpu.make_async_copy(
                x_hbm.at[_tile_rows(rt_ref[g * ROW_BLOCK + r]), :],
                xbuf.at[slot, pl.ds(r * LANE_TILES, LANE_TILES), :],
                xsem.at[slot]).start()

    def gather_wait(slot):
        pltpu.make_async_copy(x_hbm.at[pl.ds(0, block_rows), :], xbuf.at[slot],
                              xsem.at[slot]).wait()

    def y_copy(g, slot):
        start = pl.multiple_of(g * block_rows, block_rows)
        return pltpu.make_async_copy(ybuf.at[slot], ys_hbm.at[pl.ds(start, block_rows), :],
                                     ysem.at[slot])

    @pl.when(jnp.logical_and(nblk > 0, first == 0))
    def _():
        gather_start(0, 0)

    @pl.when(nblk > 0)
    def _():
        for j in range(d // 128):
            rows = slice(j * 128, (j + 1) * 128)
            for c in range(f2 // ch):
                wt_ref[j, c * ch:(c + 1) * ch, :] = wgu_ref[rows, c * ch:(c + 1) * ch].T
            for c in range(f // ch):
                gate_rows = wt_ref[j, pl.ds(2 * c * ch, ch, stride=2), :]
                lin_rows = wt_ref[j, pl.ds(2 * c * ch + 1, ch, stride=2), :]
                wg_ref[rows, c * ch:(c + 1) * ch] = gate_rows.T.astype(BF16)
                wl_ref[rows, c * ch:(c + 1) * ch] = lin_rows.T.astype(BF16)
        wdb_ref[...] = wd_ref[...].astype(BF16)

    def block(b, carry):
        g = first + b
        for slot in range(2):
            @pl.when(g % 2 == slot)
            def _():
                block_body(g, slot)
        return carry

    def block_body(g, slot):
        @pl.when(g >= 2)
        def _():
            y_copy(g - 2, slot).wait()

        gather_wait(slot)
        xb_ref[...] = _load_row_tiles(xbuf.at[slot], ROW_BLOCK).astype(BF16)
        gather_start(jnp.minimum(g + 1, n_live - 1), 1 - slot)
        x = xb_ref[...]
        hg =jnp.dot(x, wg_ref[...], preferred_element_type=F32) + bg_ref[...]
        hl = jnp.dot(x, wl_ref[...], preferred_element_type=F32) + bl_ref[...]
        xg = jnp.minimum(hg, SWIGLU_LIMIT)
        xl = jnp.clip(hl, -SWIGLU_LIMIT, SWIGLU_LIMIT)
        act = xg * jax.nn.sigmoid(SWIGLU_ALPHA * xg) * (xl + 1.0)
        y = jnp.dot(act.astype(BF16), wdb_ref[...], preferred_element_type=F32) + bd_ref[...]
        _store_row_tiles(ybuf.at[slot], y)
        y_copy(g, slot).start()

    lax.fori_loop(0, nblk, block, 0)

    @pl.when(e == n_exp - 1)
    def _():
        gather_wait(n_live % 2)
        y_copy(0, 0).wait()
        y_copy(0, 1).wait()
        ybuf[0] = jnp.zeros(ybuf.shape[1:], F32)
        for t in range(N_EXPERTS):
            @pl.when(n_live + t < n_blocks)
            def _():
                y_copy(n_live + t, 0).start()
        for t in range(N_EXPERTS):
            @pl.when(n_live + t < n_blocks)
            def _():
                y_copy(n_live + t, 0).wait()


def _experts(first_blk, n_blk, row_tok, x1r, w_gu, bg, bl, w_down, bd):
    d = D_MODEL
    f2 = w_gu.shape[2]
    f = f2 // 2
    n_rows = row_tok.shape[0]
    wmap = lambda e, fb, nb_: (e, 0, 0)
    grid_spec = pltpu.PrefetchScalarGridSpec(
        num_scalar_prefetch=2,
        grid=(N_EXPERTS,),
        in_specs=[
            pl.BlockSpec(memory_space=pltpu.SMEM),
            pl.BlockSpec((None, d, f2), wmap),
            pl.BlockSpec((None, 1, f), wmap),
            pl.BlockSpec((None, 1, f), wmap),
            pl.BlockSpec((None, f, d), wmap),
            pl.BlockSpec((None, 1, d), wmap),
            pl.BlockSpec(memory_space=pl.ANY),
        ],
        out_specs=pl.BlockSpec(memory_space=pl.ANY),
        scratch_shapes=[
            pltpu.VMEM((d // 128, f2, 128), F32),
            pltpu.VMEM((d, f), BF16),
            pltpu.VMEM((d, f), BF16),
            pltpu.VMEM((f, d), BF16),
            pltpu.VMEM((ROW_BLOCK, d), BF16),
            pltpu.VMEM((2, ROW_BLOCK * LANE_TILES, 128), F32),
            pltpu.VMEM((2, ROW_BLOCK * LANE_TILES, 128), F32),
            pltpu.SemaphoreType.DMA((2,)),
            pltpu.SemaphoreType.DMA((2,)),
        ],
    )
    return pl.pallas_call(
        _expert_kernel,
        grid_spec=grid_spec,
        out_shape=jax.ShapeDtypeStruct((n_rows * LANE_TILES, 128), F32),
        compiler_params=pltpu.CompilerParams(
            dimension_semantics=("arbitrary",), vmem_limit_bytes=VMEM_LIMIT_EXPERTS),
        name="moe_experts",
    )(first_blk, n_blk, row_tok, w_gu, bg, bl, w_down, bd, x1r)


def _combine_start(ys_hbm, idx_ref, buf_ref, sem, tm):
    def body(t, carry):
        for kk in range(TOP_K):
            pltpu.make_async_copy(ys_hbm.at[_tile_rows(idx_ref[0, t * TOP_K + kk]), :],
                                  buf_ref.at[kk, _tile_rows(t), :], sem).start()
        return carry
    lax.fori_loop(0, tm, body, 0, unroll=2)


def _combine_wait(ys_hbm, buf_ref, sem, tm):
    for kk in range(TOP_K):
        pltpu.make_async_copy(ys_hbm.at[pl.ds(0, tm * LANE_TILES), :], buf_ref.at[kk], sem).wait()


def _combine_kernel(idx_cur, idx_nxt, x1_ref, g4_ref, g2_ref, b2_ref, ys_hbm,
                    o_ref, buf_ref, sem):
    i = pl.program_id(0)
    n = pl.num_programs(0)
    slot = i % 2
    tm = o_ref.shape[0]

    @pl.when(i == 0)
    def _():
        _combine_start(ys_hbm, idx_cur, buf_ref.at[0], sem.at[0], tm)

    @pl.when(i + 1 < n)
    def _():
        _combine_start(ys_hbm, idx_nxt, buf_ref.at[1 - slot], sem.at[1 - slot], tm)

    _combine_wait(ys_hbm, buf_ref.at[slot], sem.at[slot], tm)
    g4 = g4_ref[...]
    ffn = g4[:, 0:1] * _load_row_tiles(buf_ref.at[slot, 0], tm)
    for kk in range(1, TOP_K):
        ffn = ffn + g4[:, kk:kk + 1] * _load_row_tiles(buf_ref.at[slot, kk], tm)
    x1 = _load_row_tiles(x1_ref, tm)
    o_ref[...] = _layer_norm(ALPHA * x1 + ffn, g2_ref[...], b2_ref[...])


def _combine(dest3, x1r, g4, g2, b2, ys):
    d = D_MODEL
    t = x1r.shape[0] // LANE_TILES
    tm = TM_COMB
    nt = t // tm
    row = lambda i: (i, 0)
    fixed = lambda i: (0, 0)
    return pl.pallas_call(
        _combine_kernel,
        grid=(nt,),
        in_specs=[
            pl.BlockSpec((None, 1, tm * TOP_K), lambda i: (i, 0, 0), memory_space=pltpu.SMEM),
            pl.BlockSpec((None, 1, tm * TOP_K), lambda i: (jnp.minimum(i + 1, nt - 1), 0, 0),
                         memory_space=pltpu.SMEM),
            pl.BlockSpec((tm * LANE_TILES, 128), row),
            pl.BlockSpec((tm, TOP_K), row),
            pl.BlockSpec((1, d), fixed),
            pl.BlockSpec((1, d), fixed),
            pl.BlockSpec(memory_space=pl.ANY),
        ],
        out_specs=pl.BlockSpec((tm, d), row),
        out_shape=jax.ShapeDtypeStruct((t, d), F32),
        scratch_shapes=[pltpu.VMEM((2, TOP_K, tm * LANE_TILES, 128), F32),
                        pltpu.SemaphoreType.DMA((2,))],
        compiler_params=pltpu.CompilerParams(
            dimension_semantics=("arbitrary",), vmem_limit_bytes=VMEM_LIMIT),
        name="moe_combine_ln",
    )(dest3, dest3, x1r, g4, g2, b2, ys)


def kernel(x, ln0_g, ln0_b, w_in, conv_w, conv_b, w_rg_a, b_rg_a, w_rg_x, b_rg_x, lru_lambda, lam_q1, lam_k1, lam_q2, lam_k2, subln_g, w_out, ln1_g, ln1_b, w_router, b_router, w_gu, b_gu, w_down, b_down, ln2_g, ln2_b):
    bsz, seq, d = x.shape
    t = bsz * seq
    x2 = x.reshape(t, d)
    g0 = ln0_g.reshape(1, d)
    b0 = ln0_b.reshape(1, d)
    l = 0

    w_in_bf = w_in[l].astype(BF16)
    w_out_bf = w_out[l].astype(BF16)
    wa = jax.scipy.linalg.block_diag(*[w_rg_a[l, n] for n in range(LRU_BLOCKS)])
    wx = jax.scipy.linalg.block_diag(*[w_rg_x[l, n] for n in range(LRU_BLOCKS)])
    w_gate_bf = jnp.concatenate([wa, wx], axis=1).astype(BF16)
    b_gate = jnp.concatenate([b_rg_a[l].reshape(1, -1), b_rg_x[l].reshape(1, -1)], axis=1)
    lamvec = jnp.stack([lam_q1[l], lam_k1[l], lam_q2[l], lam_k2[l]]).astype(F32)
    slopes = jnp.asarray([2.0 ** (-8.0 * (i + 1) / ATT_HEADS) for i in range(ATT_HEADS)], F32)
    slopes = jnp.broadcast_to(slopes[:, None, None], (ATT_HEADS, 1, 128))
    bg = b_gu[l, :, None, 0::2]
    bl = b_gu[l, :, None, 1::2]
    bd = b_down[l][:, None, :]

    qt, k, vt, xl, gl = _inproj(x, g0, b0, w_in_bf)
    att = _attention(qt, k, vt, slopes, lamvec, subln_g[l].reshape(-1, 1))
    rec = _lru(xl, gl, conv_w[l], conv_b[l].reshape(1, -1), w_gate_bf, b_gate,
               lru_lambda[l].reshape(1, -1))
    wr_t = w_router[l].T.astype(F32)
    wr_hi = wr_t.astype(BF16)
    wr_lo = (wr_t - wr_hi.astype(F32)).astype(BF16)
    x1r, e4t, g4t, r4t, cnt = _outproj_router(
        x2, g0, b0, att.reshape(t, -1), rec.reshape(t, -1), w_out_bf,
        ln1_g[l].reshape(1, d), ln1_b[l].reshape(1, d),
        jnp.concatenate([wr_hi, wr_lo], axis=0), b_router[l].reshape(-1, 1).astype(F32))
    e4 = e4t[0:TOP_K].T
    g4 = g4t[0:TOP_K].T
    r4 = r4t[0:TOP_K].T

    n_assign = t * TOP_K
    nb = (n_assign + N_EXPERTS * (ROW_BLOCK - 1)) // ROW_BLOCK + 1
    counts = cnt[:, 0].astype(I32)
    padded = (counts + ROW_BLOCK - 1) // ROW_BLOCK * ROW_BLOCK
    pends = jnp.cumsum(padded)
    pstarts = pends - padded
    dest = pstarts[e4] + r4
    row_tok = _invert(dest.reshape(t // TM_INV, 1, TM_INV * TOP_K), nb * ROW_BLOCK)
    ys = _experts((pstarts // ROW_BLOCK).astype(I32), (padded // ROW_BLOCK).astype(I32),
                  row_tok, x1r, w_gu[l], bg, bl, w_down[l], bd)
    out = _combine(dest.reshape(t // TM_COMB, 1, TM_COMB * TOP_K), x1r, g4,
                   ln2_g[l].reshape(1, d), ln2_b[l].reshape(1, d), ys)
    return out.reshape(bsz, seq, d)
```

```python
import math

import jax
import jax.numpy as jnp
from jax import lax
from jax.experimental import pallas as pl
from jax.experimental.pallas import tpu as pltpu

F32 = jnp.float32
BF16 = jnp.bfloat16
I32 = jnp.int32

D_MODEL = 1024
ATT_WIDTH = 512
ATT_HEAD_DIM = 64
ATT_HEADS = 4
VAL_DIM = 2 * ATT_HEAD_DIM
LRU_WIDTH = 512
LRU_BLOCKS = 8
LRU_C = 8.0
CONV_WIDTH = 4
CHUNK_SHIFT = 6
N_EXPERTS = 32
TOP_K = 4
D_EXPERT = 1024
SWIGLU_LIMIT = 7.0
SWIGLU_ALPHA = 1.702
DEPTH = 1
ALPHA = (2.0 * DEPTH) ** 0.25
LN_EPS = 1e-5
SUBLN_EPS = 1e-5
LAM_INIT = 0.8 - 0.6 * math.exp(-0.3 * 0)
NEG_BIG = -1e30
LOG2E = math.log2(math.e)

TM_PROJ = 512
TQ = 1024
TKB = 128
SW = 256
TC_LRU = 512
ROW_BLOCK = 256
TM_COMB = 128
TM_INV = 2048
N_DMA_QUEUES = 2
XPOSE_CHUNK = 512

VMEM_LIMIT = 48 * 1024 * 1024
VMEM_LIMIT_EXPERTS = 56 * 1024 * 1024


LANE_TILES = D_MODEL // 128


def _store_row_tiles(ref, val):
    n = val.shape[0]
    for j in range(LANE_TILES):
        ref[pl.ds(j, n, stride=LANE_TILES), :] = val[:, j * 128:(j + 1) * 128]


def _load_row_tiles(ref, n):
    return jnp.concatenate(
        [ref[pl.ds(j, n, stride=LANE_TILES), :] for j in range(LANE_TILES)], axis=1)


def _layer_norm(x, g, b):
    mu = jnp.mean(x, axis=-1, keepdims=True)
    xc = x - mu
    var = jnp.mean(xc * xc, axis=-1, keepdims=True)
    return xc * lax.rsqrt(var + LN_EPS) * g + b


def _inproj_kernel(x_ref, g_ref, b_ref, w_ref, qt_ref, k_ref, vt_ref, xl_ref, gl_ref):
    x0 = _layer_norm(x_ref[...], g_ref[...], b_ref[...])
    p = jnp.dot(x0.astype(BF16), w_ref[...], preferred_element_type=F32)
    a = ATT_WIDTH
    qt_ref[...] = (p[:, 0:a] * (ATT_HEAD_DIM ** -0.5 * LOG2E)).T.astype(BF16)
    k_ref[...] = p[:, a:2 * a].astype(BF16)
    vt = p[:, 2 * a:3 * a].T.astype(BF16)
    for h in range(ATT_HEADS):
        for c in range(vt_ref.shape[1]):
            vt_ref[h, c] = vt[h * VAL_DIM:(h + 1) * VAL_DIM, c * TKB:(c + 1) * TKB]
    xl_ref[...] = p[:, 3 * a:3 * a + LRU_WIDTH]
    gl_ref[...] = p[:, 3 * a + LRU_WIDTH:]


def _inproj(x, g, b, w_bf):
    bsz, seq, _ = x.shape
    n = w_bf.shape[1]
    tm = TM_PROJ
    row = lambda bi, i: (bi, i, 0)
    col = lambda bi, i: (bi, 0, i)
    fixed = lambda bi, i: (0, 0)
    return pl.pallas_call(
        _inproj_kernel,
        grid=(bsz, seq // tm),
        in_specs=[
            pl.BlockSpec((None, tm, D_MODEL), row),
            pl.BlockSpec((1, D_MODEL), fixed),
            pl.BlockSpec((1, D_MODEL), fixed),
            pl.BlockSpec((D_MODEL, n), fixed),
        ],
        out_specs=[
            pl.BlockSpec((None, ATT_WIDTH, tm), col),
            pl.BlockSpec((None, tm, ATT_WIDTH), row),
            pl.BlockSpec((None, ATT_HEADS, tm // TKB, VAL_DIM, TKB),
                         lambda bi, i: (bi, 0, i, 0, 0)),
            pl.BlockSpec((None, tm, LRU_WIDTH), row),
            pl.BlockSpec((None, tm, LRU_WIDTH), row),
        ],
        out_shape=[
            jax.ShapeDtypeStruct((bsz, ATT_WIDTH, seq), BF16),
            jax.ShapeDtypeStruct((bsz, seq, ATT_WIDTH), BF16),
            jax.ShapeDtypeStruct((bsz, ATT_HEADS, seq // TKB, VAL_DIM, TKB), BF16),
            jax.ShapeDtypeStruct((bsz, seq, LRU_WIDTH), F32),
            jax.ShapeDtypeStruct((bsz, seq, LRU_WIDTH), F32),
        ],
        compiler_params=pltpu.CompilerParams(
            dimension_semantics=("parallel", "parallel"), vmem_limit_bytes=VMEM_LIMIT),
        name="ln_inproj",
    )(x, g, b, w_bf)


def _attn_kernel(qt_ref, k_ref, vt_ref, slope_ref, lam_ref, g_ref,
                 o_ref, q2_ref, kx_ref, qbias_ref, diag_ref, *stat_refs):
    qi = pl.program_id(2)
    tq = qt_ref.shape[1]
    n_strips = 2 * tq // SW
    blocks_per_q = tq // TKB
    cl = slope_ref[0:1, 0:1] * LOG2E

    @pl.when(qi == 0)
    def _():
        c = lax.broadcasted_iota(I32, (TKB, 128), 0)
        lane = lax.broadcasted_iota(I32, (TKB, 128), 1)
        w = c.astype(F32) * cl
        w_hi = w.astype(BF16).astype(F32)
        w_mid = (w - w_hi).astype(BF16).astype(F32)
        w_lo = w - w_hi - w_mid
        kx_ref[...] = jnp.where(
            lane == 0, w_hi, jnp.where(lane == 1, w_mid, jnp.where(lane == 2, w_lo, 0.0))
        ).astype(BF16)
        r1 = lax.broadcasted_iota(I32, (1, tq), 1)
        qbias_ref[...] = r1.astype(F32) * (-cl)
        ck0 = lax.broadcasted_iota(I32, (TKB, tq), 0)
        r = lax.broadcasted_iota(I32, (TKB, tq), 1)
        for kd in range(blocks_per_q):
            ck = ck0 + kd * TKB
            allowed = (ck >> CHUNK_SHIFT) <= (r >> CHUNK_SHIFT)
            bias = (jnp.abs(r - ck) + ck0).astype(F32) * (-cl)
            diag_ref[kd] = jnp.where(allowed, bias, NEG_BIG)

    m_refs = stat_refs[0:n_strips]
    l_refs = stat_refs[n_strips:2 * n_strips]
    acc_refs = stat_refs[2 * n_strips:3 * n_strips]
    for j in range(n_strips):
        m_refs[j][...] = jnp.full(m_refs[j].shape, NEG_BIG, F32)
        l_refs[j][...] = jnp.zeros(l_refs[j].shape, F32)
        acc_refs[j][...] = jnp.zeros(acc_refs[j].shape, F32)
    qt = qt_ref[...]
    frow = lax.broadcasted_iota(I32, qt.shape, 0)
    zero = jnp.zeros_like(qt)
    q2_ref[0:VAL_DIM, 0:tq] = jnp.where(frow < ATT_HEAD_DIM, qt, zero)
    q2_ref[0:VAL_DIM, tq:2 * tq] = jnp.where(frow >= ATT_HEAD_DIM, qt, zero)
    xrow = lax.broadcasted_iota(I32, (VAL_DIM, 2 * tq), 0)
    q2_ref[VAL_DIM:2 * VAL_DIM, :] = jnp.where(xrow < 3, 1.0, 0.0).astype(BF16)

    def load_keys(kb):
        k_blk = k_ref[pl.ds(pl.multiple_of(kb * TKB, TKB), TKB), :]
        return jnp.concatenate([k_blk, kx_ref[...]], axis=1), vt_ref[kb]

    def strip(j, k_ext, vt_blk, sc_bias, shift):
        ls = slice(j * SW, (j + 1) * SW)
        sc = jnp.dot(k_ext, q2_ref[:, ls], preferred_element_type=F32)
        if sc_bias is not None:
            sc = sc + sc_bias
        m_prev = m_refs[j][...]
        m_new = jnp.maximum(m_prev, jnp.max(sc, axis=0, keepdims=True) + shift)
        alpha = jnp.exp2(m_prev - m_new)
        p = jnp.exp2(sc - (m_new - shift))
        l_refs[j][...] = alpha * l_refs[j][...] + jnp.sum(p, axis=0, keepdims=True)
        pv = jnp.dot(vt_blk, p.astype(BF16), preferred_element_type=F32)
        acc_refs[j][...] = alpha * acc_refs[j][...] + pv
        m_refs[j][...] = m_new

    def off_diag_group(g, carry):
        for kd in range(blocks_per_q):
            kb = g * blocks_per_q + kd
            k_ext, vt_blk = load_keys(kb)
            block_shift = jnp.full((1, 1), qi * tq - kb * TKB, I32).astype(F32) * (-cl)
            for j in range(n_strips):
                jq = j % (tq // SW)
                strip(j, k_ext, vt_blk, None,
                      qbias_ref[:, jq * SW:(jq + 1) * SW] + block_shift)
        return carry
    lax.fori_loop(0, qi, off_diag_group, 0)

    for kd in range(blocks_per_q):
        k_ext, vt_blk = load_keys(qi * blocks_per_q + kd)
        for j in range(n_strips):
            jq = j % (tq // SW)
            if (jq + 1) * SW > kd * TKB:
                strip(j, k_ext, vt_blk, diag_ref[kd, :, jq * SW:(jq + 1) * SW], 0.0)

    lv = lam_ref[...]
    lam = (jnp.exp(jnp.sum(lv[0:1] * lv[1:2], axis=-1, keepdims=True))
           - jnp.exp(jnp.sum(lv[2:3] * lv[3:4], axis=-1, keepdims=True)) + LAM_INIT)
    o2 = [acc_refs[j][...] / l_refs[j][...] for j in range(n_strips)]
    half = n_strips // 2
    o = jnp.concatenate([o2[j] - lam * o2[half + j] for j in range(half)], axis=1)
    o = o * lax.rsqrt(jnp.mean(o * o, axis=0, keepdims=True) + SUBLN_EPS)
    o = o * (g_ref[...] * (1.0 - LAM_INIT))
    o_ref[...] = o.T.astype(o_ref.dtype)


def _attention(qt, k, vt, slopes, lamvec, subln_g_col):
    b, s, _ = k.shape
    nkb = s // TKB
    n_strips = 2 * TQ // SW
    return pl.pallas_call(
        _attn_kernel,
        grid=(b, ATT_HEADS, s // TQ),
        in_specs=[
            pl.BlockSpec((None, VAL_DIM, TQ), lambda bi, h, qi: (bi, h, qi)),
            pl.BlockSpec((None, s, VAL_DIM), lambda bi, h, qi: (bi, 0, h)),
            pl.BlockSpec((None, None, nkb, VAL_DIM, TKB), lambda bi, h, qi: (bi, h, 0, 0, 0)),
            pl.BlockSpec((None, 1, 128), lambda bi, h, qi: (h, 0, 0)),
            pl.BlockSpec((4, ATT_HEAD_DIM), lambda bi, h, qi: (0, 0)),
            pl.BlockSpec((VAL_DIM, 1), lambda bi, h, qi: (0, 0)),
        ],
        out_specs=pl.BlockSpec((None, TQ, VAL_DIM), lambda bi, h, qi: (bi, qi, h)),
        out_shape=jax.ShapeDtypeStruct((b, s, ATT_WIDTH), BF16),
        scratch_shapes=[
            pltpu.VMEM((2 * VAL_DIM, 2 * TQ), BF16),
            pltpu.VMEM((TKB, 128), BF16),
            pltpu.VMEM((1, TQ), F32),
            pltpu.VMEM((TQ // TKB, TKB, TQ), F32),
        ] +[pltpu.VMEM((1, SW), F32)] * (2 * n_strips) + [pltpu.VMEM((VAL_DIM, SW), F32)] * n_strips,
        compiler_params=pltpu.CompilerParams(
            dimension_semantics=("parallel", "parallel", "arbitrary"),
            vmem_limit_bytes=VMEM_LIMIT),
        name="diff_attention",
    )(qt, k, vt, slopes, lamvec, subln_g_col)


def _lru_kernel(xl_ref, gl_ref, cw_ref, cb_ref, wg_ref, bg_ref, lam_ref,
                o_ref, ext_ref, h_ref):
    c = pl.program_id(1)
    tc = xl_ref.shape[0]
    w = LRU_WIDTH

    @pl.when(c == 0)
    def _():
        ext_ref[0:8, :] = jnp.zeros((8, w), F32)
        h_ref[...] = jnp.zeros(h_ref.shape, F32)

    ext_ref[8:8 + tc, :] = xl_ref[...]
    cw = cw_ref[...]
    xc = cb_ref[...] + cw[0:1] * ext_ref[5:5 + tc, :]
    for j in range(1, CONV_WIDTH):
        xc = xc + cw[j:j + 1] * ext_ref[5 + j:5 + j + tc, :]
    tail = ext_ref[tc:tc + 8, :]

    gates = jnp.dot(xc.astype(BF16), wg_ref[...], preferred_element_type=F32) + bg_ref[...]
    r = jax.nn.sigmoid(gates[:, 0:w])
    ig = jax.nn.sigmoid(gates[:, w:2 * w])
    z = -lam_ref[...]
    softplus = jnp.maximum(z, 0.0) + jnp.log1p(jnp.exp(-jnp.abs(z)))
    log_a = (-LRU_C) * r * softplus
    a = jnp.exp(log_a)
    u = jnp.sqrt(-jnp.tanh(log_a) * (a * a + 1.0)) * (ig * xc)

    row = lax.broadcasted_iota(I32, (tc, w), 0)
    d = 1
    while d < tc:
        a_sh = pltpu.roll(a, d, axis=0)
        u_sh = pltpu.roll(u, d, axis=0)
        valid = row >= d
        u = jnp.where(valid, a * u_sh + u, u)
        a = jnp.where(valid, a * a_sh, a)
        d *= 2
    h = u + a * h_ref[0:1, :]

    gl = gl_ref[...]
    gelu = 0.5 * gl * (1.0 + jnp.tanh(math.sqrt(2.0 / math.pi) * (gl + 0.044715 * (gl * gl * gl))))
    o_ref[...] = (h * gelu).astype(o_ref.dtype)

    h_ref[0:1, :] = h[tc - 1:tc, :]
    ext_ref[0:8, :] = tail


def _lru(xl, gl, conv_w, conv_b, w_gate_bf, b_gate, lam):
    b, s, w = xl.shape
    tc = TC_LRU
    blk = lambda bi, c: (bi, c, 0)
    fixed = lambda bi, c: (0, 0)
    return pl.pallas_call(
        _lru_kernel,
        grid=(b, s // tc),
        in_specs=[
            pl.BlockSpec((None, tc, w), blk),
            pl.BlockSpec((None, tc, w), blk),
            pl.BlockSpec((CONV_WIDTH, w), fixed),
            pl.BlockSpec((1, w), fixed),
            pl.BlockSpec((w, 2 * w), fixed),
            pl.BlockSpec((1, 2 * w), fixed),
            pl.BlockSpec((1, w), fixed),
        ],
        out_specs=pl.BlockSpec((None, tc, w), blk),
        out_shape=jax.ShapeDtypeStruct((b, s, w), BF16),
        scratch_shapes=[pltpu.VMEM((tc + 8, w), F32), pltpu.VMEM((8, w), F32)],
        compiler_params=pltpu.CompilerParams(
            dimension_semantics=("parallel", "arbitrary"), vmem_limit_bytes=VMEM_LIMIT),
        name="rg_lru",
    )(xl, gl, conv_w, conv_b, w_gate_bf, b_gate, lam)


def _outproj_kernel(x_ref, g0_ref, b0_ref, att_ref, rec_ref, wo_ref, g1_ref, b1_ref,
                    wr_ref, br_ref, x1_ref, e4_ref, g4_ref, r4_ref, cnt_ref, carry_ref):
    i = pl.program_id(0)
    tm = x_ref.shape[0]

    @pl.when(i == 0)
    def _():
        carry_ref[...] = jnp.zeros(carry_ref.shape, F32)

    x0 = _layer_norm(x_ref[...], g0_ref[...], b0_ref[...])
    mixed = (jnp.dot(att_ref[...], wo_ref[0:ATT_WIDTH, :], preferred_element_type=F32)
             + jnp.dot(rec_ref[...], wo_ref[ATT_WIDTH:, :], preferred_element_type=F32))
    x1 = _layer_norm(ALPHA * x0 + mixed, g1_ref[...], b1_ref[...])
    _store_row_tiles(x1_ref, x1)

    x1_hi = x1.astype(BF16)
    x1_lo = (x1 - x1_hi.astype(F32)).astype(BF16)
    nt = (((1,), (1,)), ((), ()))
    ne = N_EXPERTS
    hi_terms = lax.dot_general(wr_ref[...], x1_hi, nt, preferred_element_type=F32)
    lo_term = lax.dot_general(wr_ref[0:ne, :], x1_lo, nt, preferred_element_type=F32)
    logits = hi_terms[0:ne] + hi_terms[ne:2 * ne] + lo_term + br_ref[...]

    erow = lax.broadcasted_iota(I32, logits.shape, 0)
    work = logits
    vals, idxs, hots = [], [], []
    for _ in range(TOP_K):
        mx = jnp.max(work, axis=0, keepdims=True)
        idx = jnp.min(jnp.where(work == mx, erow, ne), axis=0, keepdims=True)
        hot = erow == idx
        vals.append(mx)
        idxs.append(idx)
        hots.append(hot)
        work = jnp.where(hot, -jnp.inf, work)
    exps = [jnp.exp(v - vals[0]) for v in vals]
    denom = exps[0] + exps[1] + exps[2] + exps[3]

    mask = jnp.zeros(logits.shape, F32)
    for hot in hots:
        mask = mask + hot.astype(F32)
    tr = lax.broadcasted_iota(I32, (tm, tm), 0)
    tc = lax.broadcasted_iota(I32, (tm, tm), 1)
    earlier = jnp.where(tr < tc, 1.0, 0.0).astype(BF16)
    carry = carry_ref[:, 0:1]
    excl = jnp.dot(mask.astype(BF16), earlier, preferred_element_type=F32) + carry
    ranks = [jnp.sum(jnp.where(hot, excl, 0.0), axis=0, keepdims=True) for hot in hots]
    pad_i = jnp.zeros((8 - TOP_K, tm), I32)
    pad_f = jnp.zeros((8 - TOP_K, tm), F32)
    e4_ref[...] = jnp.concatenate(idxs + [pad_i], axis=0)
    g4_ref[...] = jnp.concatenate([ex / denom for ex in exps] + [pad_f], axis=0)
    r4_ref[...] = jnp.concatenate([r.astype(I32) for r in ranks] + [pad_i], axis=0)
    total = carry + jnp.sum(mask, axis=1, keepdims=True)
    carry_ref[...] = jnp.broadcast_to(total, carry_ref.shape)
    cnt_ref[...] = jnp.broadcast_to(total, cnt_ref.shape)


def _outproj_router(x2, g0, b0, att, rec, wo_bf, g1, b1, w_router, b_router):
    t = x2.shape[0]
    tm = TM_PROJ
    row = lambda i: (i, 0)
    col = lambda i: (0, i)
    fixed = lambda i: (0, 0)
    return pl.pallas_call(
        _outproj_kernel,
        grid=(t // tm,),
        in_specs=[
            pl.BlockSpec((tm, D_MODEL), row),
            pl.BlockSpec((1, D_MODEL), fixed),
            pl.BlockSpec((1, D_MODEL), fixed),
            pl.BlockSpec((tm, ATT_WIDTH), row),
            pl.BlockSpec((tm, LRU_WIDTH), row),
            pl.BlockSpec((D_MODEL, D_MODEL), fixed),
            pl.BlockSpec((1, D_MODEL), fixed),
            pl.BlockSpec((1, D_MODEL), fixed),
            pl.BlockSpec((2 * N_EXPERTS, D_MODEL), fixed),
            pl.BlockSpec((N_EXPERTS, 1), fixed),
        ],
        out_specs=[
            pl.BlockSpec((tm * LANE_TILES, 128), row),
            pl.BlockSpec((8, tm), col),
            pl.BlockSpec((8, tm), col),
            pl.BlockSpec((8, tm), col),
            pl.BlockSpec((N_EXPERTS, 128), fixed),
        ],
        out_shape=[
            jax.ShapeDtypeStruct((t * LANE_TILES, 128), F32),
            jax.ShapeDtypeStruct((8, t), I32),
            jax.ShapeDtypeStruct((8, t), F32),
            jax.ShapeDtypeStruct((8, t), I32),
            jax.ShapeDtypeStruct((N_EXPERTS, 128), F32),
        ],
        scratch_shapes=[pltpu.VMEM((N_EXPERTS, 128), F32)],
        compiler_params=pltpu.CompilerParams(
            dimension_semantics=("arbitrary",), vmem_limit_bytes=VMEM_LIMIT),
        name="outproj_ln_router",
    )(x2, g0, b0, att, rec, wo_bf, g1, b1, w_router, b_router)


def _tile_rows(row):
    return pl.ds(pl.multiple_of(row * LANE_TILES, LANE_TILES), LANE_TILES)


def _invert_kernel(dest_ref, rt_ref, zero_ref, sem):
    i = pl.program_id(0)
    tm = dest_ref.shape[1] // TOP_K

    @pl.when(i == 0)
    def _():
        zero_ref[...] = jnp.zeros(zero_ref.shape, I32)
        clear = pltpu.make_async_copy(zero_ref, rt_ref, sem)
        clear.start()
        clear.wait()

    def body(t, carry):
        for kk in range(TOP_K):
            rt_ref[dest_ref[0, t * TOP_K + kk]] = i * tm + t
        return carry
    lax.fori_loop(0, tm, body, 0, unroll=4)


def _invert(dest3, n_rows):
    nt, _, per = dest3.shape
    return pl.pallas_call(
        _invert_kernel,
        grid=(nt,),
        in_specs=[pl.BlockSpec((None, 1, per), lambda i: (i, 0, 0), memory_space=pltpu.SMEM)],
        out_specs=pl.BlockSpec(memory_space=pltpu.SMEM),
        out_shape=jax.ShapeDtypeStruct((n_rows,), I32),
        scratch_shapes=[pltpu.VMEM((n_rows,), I32), pltpu.SemaphoreType.DMA(())],
        compiler_params=pltpu.CompilerParams(dimension_semantics=("arbitrary",)),
        name="moe_invert",
    )(dest3)


def _expert_kernel(first_ref, nblk_ref, rt_ref, wgu_ref, bg_ref, bl_ref, wd_ref, bd_ref,
                   x_hbm, ys_hbm, wt_ref, wg_ref, wl_ref, wdb_ref, xb_ref, xbuf, ybuf, xsem, ysem):
    e = pl.program_id(0)
    n_exp = pl.num_programs(0)
    d, f2 = wgu_ref.shape
    f = f2 // 2
    ch = XPOSE_CHUNK
    first = first_ref[e]
    nblk = nblk_ref[e]
    n_live = first_ref[n_exp - 1] + nblk_ref[n_exp - 1]
    n_blocks = ys_hbm.shape[0] // (ROW_BLOCK * LANE_TILES)
    block_rows = ROW_BLOCK * LANE_TILES

    def gather_start(g, slot):
        for r in range(ROW_BLOCK):
            pltpu.make_async_copy(
                x_hbm.at[_tile_rows(rt_ref[g * ROW_BLOCK + r]), :],
                xbuf.at[slot, pl.ds(r * LANE_TILES, LANE_TILES), :],
                xsem.at[slot]).start(priority=r % N_DMA_QUEUES)

    def gather_wait(slot):
        pltpu.make_async_copy(x_hbm.at[pl.ds(0, block_rows), :], xbuf.at[slot],
                              xsem.at[slot]).wait()

    def y_copy(g, slot):
        start = pl.multiple_of(g * block_rows, block_rows)
        return pltpu.make_async_copy(ybuf.at[slot], ys_hbm.at[pl.ds(start, block_rows), :],
                                     ysem.at[slot])

    @pl.when(jnp.logical_and(nblk > 0, first == 0))
    def _():
        gather_start(0, 0)

    @pl.when(nblk > 0)
    def _():
        for j in range(d // 128):
            rows = slice(j * 128, (j + 1) * 128)
            for c in range(f2 // ch):
                wt_ref[j, c * ch:(c + 1) * ch, :] = wgu_ref[rows, c * ch:(c + 1) * ch].T
            for c in range(f // ch):
                gate_rows = wt_ref[j, pl.ds(2 * c * ch, ch, stride=2), :]
                lin_rows = wt_ref[j, pl.ds(2 * c * ch + 1, ch, stride=2), :]
                wg_ref[rows, c * ch:(c + 1) * ch] = gate_rows.T.astype(BF16)
                wl_ref[rows, c * ch:(c + 1) * ch] = lin_rows.T.astype(BF16)
        wdb_ref[...] = wd_ref[...].astype(BF16)

    def block(b, carry):
        g = first + b
        for slot in range(2):
            @pl.when(g % 2 == slot)
            def _():
                block_body(g, slot)
        return carry

    def block_body(g, slot):
        @pl.when(g >= 2)
        def _():
            y_copy(g - 2, slot).wait()

        gather_wait(slot)
        xb_ref[...] = _load_row_tiles(xbuf.at[slot], ROW_BLOCK).astype(BF16)
        gather_start(jnp.minimum(g + 1, n_live - 1), 1 - slot)
        x = xb_ref[...]
        hg = jnp.dot(x, wg_ref[...], preferred_element_type=F32) + bg_ref[...]
        hl = jnp.dot(x, wl_ref[...], preferred_element_type=F32) + bl_ref[...]
        xg = jnp.minimum(hg, SWIGLU_LIMIT)
        xl = jnp.clip(hl, -SWIGLU_LIMIT, SWIGLU_LIMIT)
        act = xg * jax.nn.sigmoid(SWIGLU_ALPHA * xg) * (xl + 1.0)
        y = jnp.dot(act.astype(BF16), wdb_ref[...], preferred_element_type=F32) + bd_ref[...]
        _store_row_tiles(ybuf.at[slot], y)
        y_copy(g, slot).start()

    lax.fori_loop(0, nblk, block, 0)

    @pl.when(e == n_exp - 1)
    def _():
        gather_wait(n_live % 2)
        y_copy(0, 0).wait()
        y_copy(0, 1).wait()
        ybuf[0] = jnp.zeros(ybuf.shape[1:], F32)
        for t in range(N_EXPERTS):
            @pl.when(n_live + t < n_blocks)
            def _():
                y_copy(n_live + t, 0).start()
        for t in range(N_EXPERTS):
            @pl.when(n_live + t < n_blocks)
            def _():
                y_copy(n_live + t, 0).wait()


def _experts(first_blk, n_blk, row_tok, x1r, w_gu, bg, bl, w_down, bd):
    d = D_MODEL
    f2 = w_gu.shape[2]
    f = f2 // 2
    n_rows = row_tok.shape[0]
    wmap = lambda e, fb, nb_: (e, 0, 0)
    grid_spec = pltpu.PrefetchScalarGridSpec(
        num_scalar_prefetch=2,
        grid=(N_EXPERTS,),
        in_specs=[
            pl.BlockSpec(memory_space=pltpu.SMEM),
            pl.BlockSpec((None, d, f2), wmap),
            pl.BlockSpec((None, 1, f), wmap),
            pl.BlockSpec((None, 1, f), wmap),
            pl.BlockSpec((None, f, d), wmap),
            pl.BlockSpec((None, 1, d), wmap),
            pl.BlockSpec(memory_space=pl.ANY),
        ],
        out_specs=pl.BlockSpec(memory_space=pl.ANY),
        scratch_shapes=[
            pltpu.VMEM((d // 128, f2, 128), F32),
            pltpu.VMEM((d, f), BF16),
            pltpu.VMEM((d, f), BF16),
            pltpu.VMEM((f, d), BF16),
            pltpu.VMEM((ROW_BLOCK, d), BF16),
            pltpu.VMEM((2, ROW_BLOCK * LANE_TILES, 128), F32),
            pltpu.VMEM((2, ROW_BLOCK * LANE_TILES, 128), F32),
            pltpu.SemaphoreType.DMA((2,)),
            pltpu.SemaphoreType.DMA((2,)),
        ],
    )
    return pl.pallas_call(
        _expert_kernel,
        grid_spec=grid_spec,
        out_shape=jax.ShapeDtypeStruct((n_rows * LANE_TILES, 128), F32),
        compiler_params=pltpu.CompilerParams(
            dimension_semantics=("arbitrary",), vmem_limit_bytes=VMEM_LIMIT_EXPERTS),
        name="moe_experts",
    )(first_blk, n_blk, row_tok, w_gu, bg, bl, w_down, bd, x1r)


def _combine_start(ys_hbm, idx_ref, buf_ref, sem, tm):
    def body(t, carry):
        for kk in range(TOP_K):
            pltpu.make_async_copy(ys_hbm.at[_tile_rows(idx_ref[0, t * TOP_K + kk]), :],
                                  buf_ref.at[kk, _tile_rows(t), :],
                                  sem).start(priority=kk % N_DMA_QUEUES)
        return carry
    lax.fori_loop(0, tm, body, 0, unroll=2)


def _combine_wait(ys_hbm, buf_ref, sem, tm):
    for kk in range(TOP_K):
        pltpu.make_async_copy(ys_hbm.at[pl.ds(0, tm * LANE_TILES), :], buf_ref.at[kk], sem).wait()


def _combine_kernel(idx_cur, idx_nxt, x1_ref, g4_ref, g2_ref, b2_ref, ys_hbm,
                    o_ref, buf_ref, sem):
    i = pl.program_id(0)
    n = pl.num_programs(0)
    slot = i % 2
    tm = o_ref.shape[0]

    @pl.when(i == 0)
    def _():
        _combine_start(ys_hbm, idx_cur, buf_ref.at[0], sem.at[0], tm)

    @pl.when(i + 1 < n)
    def _():
        _combine_start(ys_hbm, idx_nxt, buf_ref.at[1 - slot], sem.at[1 - slot], tm)

    _combine_wait(ys_hbm, buf_ref.at[slot], sem.at[slot], tm)
    g4 = g4_ref[...]
    ffn = g4[:, 0:1] * _load_row_tiles(buf_ref.at[slot, 0], tm)
    for kk in range(1, TOP_K):
        ffn = ffn + g4[:, kk:kk + 1] * _load_row_tiles(buf_ref.at[slot, kk], tm)
    x1 = _load_row_tiles(x1_ref, tm)
    o_ref[...] = _layer_norm(ALPHA * x1 + ffn, g2_ref[...], b2_ref[...])


def _combine(dest3, x1r, g4, g2, b2, ys):
    d = D_MODEL
    t = x1r.shape[0] // LANE_TILES
    tm = TM_COMB
    nt = t // tm
    row = lambda i: (i, 0)
    fixed = lambda i: (0, 0)
    return pl.pallas_call(
        _combine_kernel,
        grid=(nt,),
        in_specs=[
            pl.BlockSpec((None, 1, tm * TOP_K), lambda i: (i, 0, 0), memory_space=pltpu.SMEM),
            pl.BlockSpec((None, 1, tm * TOP_K), lambda i: (jnp.minimum(i + 1, nt - 1), 0, 0),
                         memory_space=pltpu.SMEM),
            pl.BlockSpec((tm * LANE_TILES, 128), row),
            pl.BlockSpec((tm, TOP_K), row),
            pl.BlockSpec((1, d), fixed),
            pl.BlockSpec((1, d), fixed),
            pl.BlockSpec(memory_space=pl.ANY),
        ],
        out_specs=pl.BlockSpec((tm, d), row),
        out_shape=jax.ShapeDtypeStruct((t, d), F32),
        scratch_shapes=[pltpu.VMEM((2, TOP_K, tm * LANE_TILES, 128), F32),
                        pltpu.SemaphoreType.DMA((2,))],
        compiler_params=pltpu.CompilerParams(
            dimension_semantics=("arbitrary",), vmem_limit_bytes=VMEM_LIMIT),
        name="moe_combine_ln",
    )(dest3, dest3, x1r, g4, g2, b2, ys)


def kernel(x, ln0_g, ln0_b, w_in, conv_w, conv_b, w_rg_a, b_rg_a, w_rg_x, b_rg_x, lru_lambda, lam_q1, lam_k1, lam_q2, lam_k2, subln_g, w_out, ln1_g, ln1_b, w_router, b_router, w_gu, b_gu, w_down, b_down, ln2_g, ln2_b):
    bsz, seq, d = x.shape
    t = bsz * seq
    x2 = x.reshape(t, d)
    g0 = ln0_g.reshape(1, d)
    b0 = ln0_b.reshape(1, d)
    l = 0

    w_in_bf = w_in[l].astype(BF16)
    w_out_bf = w_out[l].astype(BF16)
    wa = jax.scipy.linalg.block_diag(*[w_rg_a[l, n] for n in range(LRU_BLOCKS)])
    wx = jax.scipy.linalg.block_diag(*[w_rg_x[l, n] for n in range(LRU_BLOCKS)])
    w_gate_bf = jnp.concatenate([wa, wx], axis=1).astype(BF16)
    b_gate = jnp.concatenate([b_rg_a[l].reshape(1, -1), b_rg_x[l].reshape(1, -1)], axis=1)
    lamvec = jnp.stack([lam_q1[l], lam_k1[l], lam_q2[l], lam_k2[l]]).astype(F32)
    slopes = jnp.asarray([2.0 ** (-8.0 * (i + 1) / ATT_HEADS) for i in range(ATT_HEADS)], F32)
    slopes = jnp.broadcast_to(slopes[:, None, None], (ATT_HEADS, 1, 128))
    bg = b_gu[l, :, None, 0::2]
    bl = b_gu[l, :, None, 1::2]
    bd = b_down[l][:, None, :]

    qt, k, vt, xl, gl = _inproj(x, g0, b0, w_in_bf)
    att = _attention(qt, k, vt, slopes, lamvec, subln_g[l].reshape(-1, 1))
    rec = _lru(xl, gl, conv_w[l], conv_b[l].reshape(1, -1), w_gate_bf, b_gate,
               lru_lambda[l].reshape(1, -1))
    wr_t = w_router[l].T.astype(F32)
    wr_hi = wr_t.astype(BF16)
    wr_lo = (wr_t - wr_hi.astype(F32)).astype(BF16)
    x1r, e4t, g4t, r4t, cnt = _outproj_router(
        x2, g0, b0, att.reshape(t, -1), rec.reshape(t, -1), w_out_bf,
        ln1_g[l].reshape(1, d), ln1_b[l].reshape(1, d),
        jnp.concatenate([wr_hi, wr_lo], axis=0), b_router[l].reshape(-1, 1).astype(F32))
    e4 = e4t[0:TOP_K].T
    g4 = g4t[0:TOP_K].T
    r4 = r4t[0:TOP_K].T

    n_assign = t * TOP_K
    nb = (n_assign + N_EXPERTS * (ROW_BLOCK - 1)) // ROW_BLOCK + 1
    counts = cnt[:, 0].astype(I32)
    padded = (counts + ROW_BLOCK - 1) // ROW_BLOCK * ROW_BLOCK
    pends = jnp.cumsum(padded)
    pstarts = pends - padded
    dest = pstarts[e4] + r4
    row_tok = _invert(dest.reshape(t // TM_INV, 1, TM_INV * TOP_K), nb * ROW_BLOCK)
    ys = _experts((pstarts // ROW_BLOCK).astype(I32), (padded // ROW_BLOCK).astype(I32),
                  row_tok, x1r, w_gu[l], bg, bl, w_down[l], bd)
    out = _combine(dest.reshape(t // TM_COMB, 1, TM_COMB * TOP_K), x1r, g4,
                   ln2_g[l].reshape(1, d), ln2_b[l].reshape(1, d), ys)
    return out.reshape(bsz, seq, d)
```

```python
import math

import jax
import jax.numpy as jnp
from jax import lax
from jax.experimental import pallas as pl
from jax.experimental.pallas import tpu as pltpu

F32 = jnp.float32
BF16 = jnp.bfloat16
I32 = jnp.int32

D_MODEL = 1024
ATT_WIDTH = 512
ATT_HEAD_DIM = 64
ATT_HEADS = 4
VAL_DIM = 2 * ATT_HEAD_DIM
LRU_WIDTH = 512
LRU_BLOCKS = 8
LRU_C = 8.0
CONV_WIDTH = 4
CHUNK_SHIFT = 6
N_EXPERTS = 32
TOP_K = 4
D_EXPERT = 1024
SWIGLU_LIMIT = 7.0
SWIGLU_ALPHA = 1.702
DEPTH = 1
ALPHA = (2.0 * DEPTH) ** 0.25
LN_EPS = 1e-5
SUBLN_EPS = 1e-5
LAM_INIT = 0.8 - 0.6 * math.exp(-0.3 * 0)
NEG_BIG = -1e30
LOG2E = math.log2(math.e)

TM_PROJ = 512
TQ = 1024
TKB = 128
SW = 256
TC_LRU = 512
SCAN_GROUP = 8
ROW_BLOCK = 256
TM_COMB = 128
TM_INV = 2048
XPOSE_CHUNK = 512

VMEM_LIMIT = 48 * 1024 * 1024
VMEM_LIMIT_EXPERTS = 56 * 1024 * 1024


LANE_TILES = D_MODEL // 128


def _store_row_tiles(ref, val):
    n = val.shape[0]
    for j in range(LANE_TILES):
        ref[pl.ds(j, n, stride=LANE_TILES), :] = val[:, j * 128:(j + 1) * 128]


def _load_row_tiles(ref, n):
    return jnp.concatenate(
        [ref[pl.ds(j, n, stride=LANE_TILES), :] for j in range(LANE_TILES)], axis=1)


def _layer_norm(x, g, b):
    mu = jnp.mean(x, axis=-1, keepdims=True)
    xc = x - mu
    var = jnp.mean(xc * xc, axis=-1, keepdims=True)
    return xc * lax.rsqrt(var + LN_EPS) * g + b


def _inproj_kernel(x_ref, g_ref, b_ref, w_ref, qt_ref, k_ref, vt_ref, xl_ref, gl_ref):
    x0 = _layer_norm(x_ref[...], g_ref[...], b_ref[...])
    p = jnp.dot(x0.astype(BF16), w_ref[...], preferred_element_type=F32)
    a = ATT_WIDTH
    qt_ref[...] = (p[:, 0:a] * (ATT_HEAD_DIM ** -0.5 * LOG2E)).T.astype(BF16)
    k_ref[...] = p[:, a:2 * a].astype(BF16)
    vt = p[:, 2 * a:3 * a].T.astype(BF16)
    for h in range(ATT_HEADS):
        for c in range(vt_ref.shape[1]):
            vt_ref[h, c] = vt[h * VAL_DIM:(h + 1) * VAL_DIM, c * TKB:(c + 1) * TKB]
    xl_ref[...] = p[:, 3 * a:3 * a + LRU_WIDTH]
    gl_ref[...] = p[:, 3 * a + LRU_WIDTH:]


def _inproj(x, g, b, w_bf):
    bsz, seq, _ = x.shape
    n = w_bf.shape[1]
    tm = TM_PROJ
    row = lambda bi, i: (bi, i, 0)
    col = lambda bi, i: (bi, 0, i)
    fixed = lambda bi, i: (0, 0)
    return pl.pallas_call(
        _inproj_kernel,
        grid=(bsz, seq // tm),
        in_specs=[
            pl.BlockSpec((None, tm, D_MODEL), row),
            pl.BlockSpec((1, D_MODEL), fixed),
            pl.BlockSpec((1, D_MODEL), fixed),
            pl.BlockSpec((D_MODEL, n), fixed),
        ],
        out_specs=[
            pl.BlockSpec((None, ATT_WIDTH, tm), col),
            pl.BlockSpec((None, tm, ATT_WIDTH), row),
            pl.BlockSpec((None, ATT_HEADS, tm // TKB, VAL_DIM, TKB),
                         lambda bi, i: (bi, 0, i, 0, 0)),
            pl.BlockSpec((None, tm, LRU_WIDTH), row),
            pl.BlockSpec((None, tm, LRU_WIDTH), row),
        ],
        out_shape=[
            jax.ShapeDtypeStruct((bsz, ATT_WIDTH, seq), BF16),
            jax.ShapeDtypeStruct((bsz, seq, ATT_WIDTH), BF16),
            jax.ShapeDtypeStruct((bsz, ATT_HEADS, seq // TKB, VAL_DIM, TKB), BF16),
            jax.ShapeDtypeStruct((bsz, seq, LRU_WIDTH), F32),
            jax.ShapeDtypeStruct((bsz, seq, LRU_WIDTH), F32),
        ],
        compiler_params=pltpu.CompilerParams(
            dimension_semantics=("parallel", "parallel"), vmem_limit_bytes=VMEM_LIMIT),
        name="ln_inproj",
    )(x, g, b, w_bf)


def _attn_kernel(qt_ref, k_ref, vt_ref, slope_ref, lam_ref, g_ref,
                 o_ref, q2_ref, kx_ref, qbias_ref, diag_ref, *stat_refs):
    qi = pl.program_id(2)
    tq = qt_ref.shape[1]
    n_strips = 2 * tq // SW
    blocks_per_q = tq // TKB
    cl = slope_ref[0:1, 0:1] * LOG2E

    @pl.when(qi == 0)
    def _():
        c = lax.broadcasted_iota(I32, (TKB, 128), 0)
        lane = lax.broadcasted_iota(I32, (TKB, 128), 1)
        w = c.astype(F32) * cl
        w_hi = w.astype(BF16).astype(F32)
        w_mid = (w - w_hi).astype(BF16).astype(F32)
        w_lo = w - w_hi - w_mid
        kx_ref[...] = jnp.where(
            lane == 0, w_hi, jnp.where(lane == 1, w_mid, jnp.where(lane == 2, w_lo, 0.0))
        ).astype(BF16)
        r1 = lax.broadcasted_iota(I32, (1, tq), 1)
        qbias_ref[...] = r1.astype(F32) * (-cl)
        ck0 = lax.broadcasted_iota(I32, (TKB, tq), 0)
        r = lax.broadcasted_iota(I32, (TKB, tq), 1)
        for kd in range(blocks_per_q):
            ck = ck0 + kd * TKB
            allowed = (ck >> CHUNK_SHIFT) <= (r >> CHUNK_SHIFT)
            bias = (jnp.abs(r - ck) + ck0).astype(F32) * (-cl)
            diag_ref[kd] = jnp.where(allowed, bias, NEG_BIG)

    m_refs = stat_refs[0:n_strips]
    l_refs = stat_refs[n_strips:2 * n_strips]
    acc_refs = stat_refs[2 * n_strips:3 * n_strips]
    for j in range(n_strips):
        m_refs[j][...] = jnp.full(m_refs[j].shape, NEG_BIG, F32)
        l_refs[j][...] = jnp.zeros(l_refs[j].shape, F32)
        acc_refs[j][...] = jnp.zeros(acc_refs[j].shape, F32)
    qt = qt_ref[...]
    frow = lax.broadcasted_iota(I32, qt.shape, 0)
    zero = jnp.zeros_like(qt)
    q2_ref[0:VAL_DIM, 0:tq] = jnp.where(frow < ATT_HEAD_DIM, qt, zero)
    q2_ref[0:VAL_DIM, tq:2 * tq] = jnp.where(frow >= ATT_HEAD_DIM, qt, zero)
    xrow = lax.broadcasted_iota(I32, (VAL_DIM, 2 * tq), 0)
    q2_ref[VAL_DIM:2 * VAL_DIM, :] = jnp.where(xrow < 3, 1.0, 0.0).astype(BF16)

    def load_keys(kb):
        k_blk = k_ref[pl.ds(pl.multiple_of(kb * TKB, TKB), TKB), :]
        return jnp.concatenate([k_blk, kx_ref[...]], axis=1), vt_ref[kb]

    def strip(j, k_ext, vt_blk, sc_bias, shift):
        ls = slice(j * SW, (j + 1) * SW)
        sc = jnp.dot(k_ext, q2_ref[:, ls], preferred_element_type=F32)
        if sc_bias is not None:
            sc = sc + sc_bias
        m_prev = m_refs[j][...]
        m_new = jnp.maximum(m_prev, jnp.max(sc, axis=0, keepdims=True) + shift)
        alpha = jnp.exp2(m_prev - m_new)
        p = jnp.exp2(sc - (m_new - shift))
        l_refs[j][...] = alpha * l_refs[j][...] + jnp.sum(p, axis=0, keepdims=True)
        pv = jnp.dot(vt_blk, p.astype(BF16), preferred_element_type=F32)
        acc_refs[j][...] = alpha * acc_refs[j][...] + pv
        m_refs[j][...] = m_new

    def off_diag_group(g, carry):
        for kd in range(blocks_per_q):
            kb = g * blocks_per_q + kd
            k_ext, vt_blk = load_keys(kb)
            block_shift = jnp.full((1, 1), qi * tq - kb * TKB, I32).astype(F32) * (-cl)
            for j in range(n_strips):
                jq = j % (tq // SW)
                strip(j, k_ext, vt_blk, None,
                      qbias_ref[:, jq * SW:(jq + 1) * SW] + block_shift)
        return carry
    lax.fori_loop(0, qi, off_diag_group, 0)

    for kd in range(blocks_per_q):
        k_ext, vt_blk = load_keys(qi * blocks_per_q + kd)
        for j in range(n_strips):
            jq = j % (tq // SW)
            if (jq + 1) * SW > kd * TKB:
                strip(j, k_ext, vt_blk, diag_ref[kd, :, jq * SW:(jq + 1) * SW], 0.0)

    lv = lam_ref[...]
    lam = (jnp.exp(jnp.sum(lv[0:1] * lv[1:2], axis=-1, keepdims=True))
           - jnp.exp(jnp.sum(lv[2:3] * lv[3:4], axis=-1, keepdims=True)) + LAM_INIT)
    o2 = [acc_refs[j][...] / l_refs[j][...] for j in range(n_strips)]
    half = n_strips // 2
    o = jnp.concatenate([o2[j] - lam * o2[half + j] for j in range(half)], axis=1)
    o = o * lax.rsqrt(jnp.mean(o * o, axis=0, keepdims=True) + SUBLN_EPS)
    o = o * (g_ref[...] * (1.0 - LAM_INIT))
    o_ref[...] = o.T.astype(o_ref.dtype)


def _attention(qt, k, vt, slopes, lamvec, subln_g_col):
    b, s, _ = k.shape
    nkb = s // TKB
    n_strips = 2 * TQ // SW
    return pl.pallas_call(
        _attn_kernel,
        grid=(b, ATT_HEADS, s // TQ),
        in_specs=[
            pl.BlockSpec((None, VAL_DIM, TQ), lambda bi, h, qi: (bi, h, qi)),
            pl.BlockSpec((None, s, VAL_DIM), lambda bi, h, qi: (bi, 0, h)),
            pl.BlockSpec((None, None, nkb, VAL_DIM, TKB), lambda bi, h, qi: (bi, h, 0, 0, 0)),
            pl.BlockSpec((None, 1, 128), lambda bi, h, qi: (h, 0, 0)),
            pl.BlockSpec((4, ATT_HEAD_DIM), lambda bi, h, qi: (0, 0)),
            pl.BlockSpec((VAL_DIM, 1), lambda bi, h, qi: (0, 0)),
        ],
        out_specs=pl.BlockSpec((None, TQ, VAL_DIM), lambda bi, h, qi: (bi, qi, h)),
        out_shape=jax.ShapeDtypeStruct((b, s, ATT_WIDTH), BF16),
        scratch_shapes=[
            pltpu.VMEM((2 * VAL_DIM, 2 * TQ), BF16),
            pltpu.VMEM((TKB, 128), BF16),
            pltpu.VMEM((1, TQ), F32),
            pltpu.VMEM((TQ // TKB, TKB, TQ), F32),
        ] +[pltpu.VMEM((1, SW), F32)] * (2 * n_strips) + [pltpu.VMEM((VAL_DIM, SW), F32)] * n_strips,
        compiler_params=pltpu.CompilerParams(
            dimension_semantics=("parallel", "parallel", "arbitrary"),
            vmem_limit_bytes=VMEM_LIMIT),
        name="diff_attention",
    )(qt, k, vt, slopes, lamvec, subln_g_col)


def _lru_kernel(xl_ref, gl_ref, cw_ref, cb_ref, wg_ref, bg_ref, lam_ref,
                o_ref, ext_ref, h_ref):
    c = pl.program_id(1)
    tc = xl_ref.shape[0]
    w = LRU_WIDTH

    @pl.when(c == 0)
    def _():
        ext_ref[0:8, :] = jnp.zeros((8, w), F32)
        h_ref[...] = jnp.zeros(h_ref.shape, F32)

    ext_ref[8:8 + tc, :] = xl_ref[...]
    cw = cw_ref[...]
    xc = cb_ref[...] + cw[0:1] * ext_ref[5:5 + tc, :]
    for j in range(1, CONV_WIDTH):
        xc = xc + cw[j:j + 1] * ext_ref[5 + j:5 + j + tc, :]
    tail = ext_ref[tc:tc + 8, :]

    gates = jnp.dot(xc.astype(BF16), wg_ref[...], preferred_element_type=F32) + bg_ref[...]
    r = jax.nn.sigmoid(gates[:, 0:w])
    ig = jax.nn.sigmoid(gates[:, w:2 * w])
    z = -lam_ref[...]
    softplus = jnp.maximum(z, 0.0) + jnp.log1p(jnp.exp(-jnp.abs(z)))
    log_a = (-LRU_C) * r * softplus
    a = jnp.exp(log_a)
    u = jnp.sqrt(-jnp.tanh(log_a) * (a * a + 1.0)) * (ig * xc)

    ng = tc // SCAN_GROUP
    a = a.reshape(ng, SCAN_GROUP, w)
    u = u.reshape(ng, SCAN_GROUP, w)
    pos = lax.broadcasted_iota(I32, a.shape, 1)
    d = 1
    while d < SCAN_GROUP:
        valid = pos >= d
        u = jnp.where(valid, a * pltpu.roll(u, d, axis=1) + u, u)
        a = jnp.where(valid, a * pltpu.roll(a, d, axis=1), a)
        d *= 2
    a = a.reshape(tc, w)
    u = u.reshape(tc, w)
    h_in = h_ref[0:1, :]
    groups = []
    for g in range(tc // SCAN_GROUP):
        rows = slice(g * SCAN_GROUP, (g + 1) * SCAN_GROUP)
        hg = u[rows] + a[rows] * h_in
        groups.append(hg)
        h_in = hg[SCAN_GROUP - 1:SCAN_GROUP, :]
    h = jnp.concatenate(groups, axis=0)

    gl = gl_ref[...]
    gelu = 0.5 * gl * (1.0 + jnp.tanh(math.sqrt(2.0 / math.pi) * (gl + 0.044715 * (gl * gl * gl))))
    o_ref[...] = (h * gelu).astype(o_ref.dtype)

    h_ref[0:1, :] = h[tc - 1:tc, :]
    ext_ref[0:8, :] = tail


def _lru(xl, gl, conv_w, conv_b, w_gate_bf, b_gate, lam):
    b, s, w = xl.shape
    tc = TC_LRU
    blk = lambda bi, c: (bi, c, 0)
    fixed = lambda bi, c: (0, 0)
    return pl.pallas_call(
        _lru_kernel,
        grid=(b, s // tc),
        in_specs=[
            pl.BlockSpec((None, tc, w), blk),
            pl.BlockSpec((None, tc, w), blk),
            pl.BlockSpec((CONV_WIDTH, w), fixed),
            pl.BlockSpec((1, w), fixed),
            pl.BlockSpec((w, 2 * w), fixed),
            pl.BlockSpec((1, 2 * w), fixed),
            pl.BlockSpec((1, w), fixed),
        ],
        out_specs=pl.BlockSpec((None, tc, w), blk),
        out_shape=jax.ShapeDtypeStruct((b, s, w), BF16),
        scratch_shapes=[pltpu.VMEM((tc + 8, w), F32), pltpu.VMEM((8, w), F32)],
        compiler_params=pltpu.CompilerParams(
            dimension_semantics=("parallel", "arbitrary"), vmem_limit_bytes=VMEM_LIMIT),
        name="rg_lru",
    )(xl, gl, conv_w, conv_b, w_gate_bf, b_gate, lam)


def _outproj_kernel(x_ref, g0_ref, b0_ref, att_ref, rec_ref, wo_ref, g1_ref, b1_ref,
                    wr_ref, br_ref, x1_ref, e4_ref, g4_ref, r4_ref, cnt_ref, carry_ref):
    i = pl.program_id(0)
    tm = x_ref.shape[0]

    @pl.when(i == 0)
    def _():
        carry_ref[...] = jnp.zeros(carry_ref.shape, F32)

    x0 = _layer_norm(x_ref[...], g0_ref[...], b0_ref[...])
    mixed = (jnp.dot(att_ref[...], wo_ref[0:ATT_WIDTH, :], preferred_element_type=F32)
             + jnp.dot(rec_ref[...], wo_ref[ATT_WIDTH:, :], preferred_element_type=F32))
    x1 = _layer_norm(ALPHA * x0 + mixed, g1_ref[...], b1_ref[...])
    _store_row_tiles(x1_ref, x1)

    x1_hi = x1.astype(BF16)
    x1_lo = (x1 - x1_hi.astype(F32)).astype(BF16)
    nt = (((1,), (1,)), ((), ()))
    ne = N_EXPERTS
    hi_terms = lax.dot_general(wr_ref[...], x1_hi, nt, preferred_element_type=F32)
    lo_term = lax.dot_general(wr_ref[0:ne, :], x1_lo, nt, preferred_element_type=F32)
    logits = hi_terms[0:ne] + hi_terms[ne:2 * ne] + lo_term + br_ref[...]

    erow = lax.broadcasted_iota(I32, logits.shape, 0)
    work = logits
    vals, idxs, hots = [], [], []
    for _ in range(TOP_K):
        mx = jnp.max(work, axis=0, keepdims=True)
        idx = jnp.min(jnp.where(work == mx, erow, ne), axis=0, keepdims=True)
        hot = erow == idx
        vals.append(mx)
        idxs.append(idx)
        hots.append(hot)
        work = jnp.where(hot, -jnp.inf, work)
    exps = [jnp.exp(v - vals[0]) for v in vals]
    denom = exps[0] + exps[1] + exps[2] + exps[3]

    mask = jnp.zeros(logits.shape, F32)
    for hot in hots:
        mask = mask + hot.astype(F32)
    tr = lax.broadcasted_iota(I32, (tm, tm), 0)
    tc = lax.broadcasted_iota(I32, (tm, tm), 1)
    earlier = jnp.where(tr < tc, 1.0, 0.0).astype(BF16)
    carry = carry_ref[:, 0:1]
    excl = jnp.dot(mask.astype(BF16), earlier, preferred_element_type=F32) + carry
    ranks = [jnp.sum(jnp.where(hot, excl, 0.0), axis=0, keepdims=True) for hot in hots]
    pad_i = jnp.zeros((8 - TOP_K, tm), I32)
    pad_f = jnp.zeros((8 - TOP_K, tm), F32)
    e4_ref[...] = jnp.concatenate(idxs + [pad_i], axis=0)
    g4_ref[...] = jnp.concatenate([ex / denom for ex in exps] + [pad_f], axis=0)
    r4_ref[...] = jnp.concatenate([r.astype(I32) for r in ranks] + [pad_i], axis=0)
    total = carry + jnp.sum(mask, axis=1, keepdims=True)
    carry_ref[...] = jnp.broadcast_to(total, carry_ref.shape)
    cnt_ref[...] = jnp.broadcast_to(total, cnt_ref.shape)


def _outproj_router(x2, g0, b0, att, rec, wo_bf, g1, b1, w_router, b_router):
    t = x2.shape[0]
    tm = TM_PROJ
    row = lambda i: (i, 0)
    col = lambda i: (0, i)
    fixed = lambda i: (0, 0)
    return pl.pallas_call(
        _outproj_kernel,
        grid=(t // tm,),
        in_specs=[
            pl.BlockSpec((tm, D_MODEL), row),
            pl.BlockSpec((1, D_MODEL), fixed),
            pl.BlockSpec((1, D_MODEL), fixed),
            pl.BlockSpec((tm, ATT_WIDTH), row),
            pl.BlockSpec((tm, LRU_WIDTH), row),
            pl.BlockSpec((D_MODEL, D_MODEL), fixed),
            pl.BlockSpec((1, D_MODEL), fixed),
            pl.BlockSpec((1, D_MODEL), fixed),
            pl.BlockSpec((2 * N_EXPERTS, D_MODEL), fixed),
            pl.BlockSpec((N_EXPERTS, 1), fixed),
        ],
        out_specs=[
            pl.BlockSpec((tm * LANE_TILES, 128), row),
            pl.BlockSpec((8, tm), col),
            pl.BlockSpec((8, tm), col),
            pl.BlockSpec((8, tm), col),
            pl.BlockSpec((N_EXPERTS, 128), fixed),
        ],
        out_shape=[
            jax.ShapeDtypeStruct((t * LANE_TILES, 128), F32),
            jax.ShapeDtypeStruct((8, t), I32),
            jax.ShapeDtypeStruct((8, t), F32),
            jax.ShapeDtypeStruct((8, t), I32),
            jax.ShapeDtypeStruct((N_EXPERTS, 128), F32),
        ],
        scratch_shapes=[pltpu.VMEM((N_EXPERTS, 128), F32)],
        compiler_params=pltpu.CompilerParams(
            dimension_semantics=("arbitrary",), vmem_limit_bytes=VMEM_LIMIT),
        name="outproj_ln_router",
    )(x2, g0, b0, att, rec, wo_bf, g1, b1, w_router, b_router)


def _tile_rows(row):
    return pl.ds(pl.multiple_of(row * LANE_TILES, LANE_TILES), LANE_TILES)


def _invert_kernel(dest_ref, rt_ref, zero_ref, sem):
    i = pl.program_id(0)
    tm = dest_ref.shape[1] // TOP_K

    @pl.when(i == 0)
    def _():
        zero_ref[...] = jnp.zeros(zero_ref.shape, I32)
        clear = pltpu.make_async_copy(zero_ref, rt_ref, sem)
        clear.start()
        clear.wait()

    def body(t, carry):
        for kk in range(TOP_K):
            rt_ref[dest_ref[0, t * TOP_K + kk]] = i * tm + t
        return carry
    lax.fori_loop(0, tm, body, 0, unroll=4)


def _invert(dest3, n_rows):
    nt, _, per = dest3.shape
    return pl.pallas_call(
        _invert_kernel,
        grid=(nt,),
        in_specs=[pl.BlockSpec((None, 1, per), lambda i: (i, 0, 0), memory_space=pltpu.SMEM)],
        out_specs=pl.BlockSpec(memory_space=pltpu.SMEM),
        out_shape=jax.ShapeDtypeStruct((n_rows,), I32),
        scratch_shapes=[pltpu.VMEM((n_rows,), I32), pltpu.SemaphoreType.DMA(())],
        compiler_params=pltpu.CompilerParams(dimension_semantics=("arbitrary",)),
        name="moe_invert",
    )(dest3)


def _expert_kernel(first_ref, nblk_ref, rt_ref, wgu_ref, bg_ref, bl_ref, wd_ref, bd_ref,
                   x_hbm, ys_hbm, wt_ref, wg_ref, wl_ref, wdb_ref, xb_ref, xbuf, ybuf, xsem, ysem):
    e = pl.program_id(0)
    n_exp = pl.num_programs(0)
    d, f2 = wgu_ref.shape
    f = f2 // 2
    ch = XPOSE_CHUNK
    first = first_ref[e]
    nblk = nblk_ref[e]
    n_live = first_ref[n_exp - 1] + nblk_ref[n_exp - 1]
    n_blocks = ys_hbm.shape[0] // (ROW_BLOCK * LANE_TILES)
    block_rows = ROW_BLOCK * LANE_TILES

    def gather_start(g, slot):
        for r in range(ROW_BLOCK):
            pltpu.make_async_copy(
                x_hbm.at[_tile_rows(rt_ref[g * ROW_BLOCK + r]), :],
                xbuf.at[slot, pl.ds(r * LANE_TILES, LANE_TILES), :],
                xsem.at[slot]).start()

    def gather_wait(slot):
        pltpu.make_async_copy(x_hbm.at[pl.ds(0, block_rows), :], xbuf.at[slot],
                              xsem.at[slot]).wait()

    def y_copy(g, slot):
        start = pl.multiple_of(g * block_rows, block_rows)
        return pltpu.make_async_copy(ybuf.at[slot], ys_hbm.at[pl.ds(start, block_rows), :],
                                     ysem.at[slot])

    @pl.when(jnp.logical_and(nblk > 0, first == 0))
    def _():
        gather_start(0, 0)

    @pl.when(nblk > 0)
    def _():
        for j in range(d // 128):
            rows = slice(j * 128, (j + 1) * 128)
            for c in range(f2 // ch):
                wt_ref[j, c * ch:(c + 1) * ch, :] = wgu_ref[rows, c * ch:(c + 1) * ch].T
            for c in range(f // ch):
                gate_rows = wt_ref[j, pl.ds(2 * c * ch, ch, stride=2), :]
                lin_rows = wt_ref[j, pl.ds(2 * c * ch + 1, ch, stride=2), :]
                wg_ref[rows, c * ch:(c + 1) * ch] = gate_rows.T.astype(BF16)
                wl_ref[rows, c * ch:(c + 1) * ch] = lin_rows.T.astype(BF16)
        wdb_ref[...] = wd_ref[...].astype(BF16)

    def block(b, carry):
        g = first + b
        for slot in range(2):
            @pl.when(g % 2 == slot)
            def _():
                block_body(g, slot)
        return carry

    def block_body(g, slot):
        @pl.when(g >= 2)
        def _():
            y_copy(g - 2, slot).wait()

        gather_wait(slot)
        xb_ref[...] = _load_row_tiles(xbuf.at[slot], ROW_BLOCK).astype(BF16)
        gather_start(jnp.minimum(g + 1, n_live - 1), 1 - slot)
        x = xb_ref[...]
        hg = jnp.dot(x, wg_ref[...], preferred_element_type=F32) + bg_ref[...]
        hl = jnp.dot(x, wl_ref[...], preferred_element_type=F32) + bl_ref[...]
        xg = jnp.minimum(hg, SWIGLU_LIMIT)
        xl = jnp.clip(hl, -SWIGLU_LIMIT, SWIGLU_LIMIT)
        act = xg * jax.nn.sigmoid(SWIGLU_ALPHA * xg) * (xl + 1.0)
        y = jnp.dot(act.astype(BF16), wdb_ref[...], preferred_element_type=F32) + bd_ref[...]
        _store_row_tiles(ybuf.at[slot], y)
        y_copy(g, slot).start()

    lax.fori_loop(0, nblk, block, 0)

    @pl.when(e == n_exp - 1)
    def _():
        gather_wait(n_live % 2)
        y_copy(0, 0).wait()
        y_copy(0, 1).wait()
        ybuf[0] = jnp.zeros(ybuf.shape[1:], F32)
        for t in range(N_EXPERTS):
            @pl.when(n_live + t < n_blocks)
            def _():
                y_copy(n_live + t, 0).start()
        for t in range(N_EXPERTS):
            @pl.when(n_live + t < n_blocks)
            def _():
                y_copy(n_live + t, 0).wait()


def _experts(first_blk, n_blk, row_tok, x1r, w_gu, bg, bl, w_down, bd):
    d = D_MODEL
    f2 = w_gu.shape[2]
    f = f2 // 2
    n_rows = row_tok.shape[0]
    wmap = lambda e, fb, nb_: (e, 0, 0)
    grid_spec = pltpu.PrefetchScalarGridSpec(
        num_scalar_prefetch=2,
        grid=(N_EXPERTS,),
        in_specs=[
            pl.BlockSpec(memory_space=pltpu.SMEM),
            pl.BlockSpec((None, d, f2), wmap),
            pl.BlockSpec((None, 1, f), wmap),
            pl.BlockSpec((None, 1, f), wmap),
            pl.BlockSpec((None, f, d), wmap),
            pl.BlockSpec((None, 1, d), wmap),
            pl.BlockSpec(memory_space=pl.ANY),
        ],
        out_specs=pl.BlockSpec(memory_space=pl.ANY),
        scratch_shapes=[
            pltpu.VMEM((d // 128, f2, 128), F32),
            pltpu.VMEM((d, f), BF16),
            pltpu.VMEM((d, f), BF16),
            pltpu.VMEM((f, d), BF16),
            pltpu.VMEM((ROW_BLOCK, d), BF16),
            pltpu.VMEM((2, ROW_BLOCK * LANE_TILES, 128), F32),
            pltpu.VMEM((2, ROW_BLOCK * LANE_TILES, 128), F32),
            pltpu.SemaphoreType.DMA((2,)),
            pltpu.SemaphoreType.DMA((2,)),
        ],
    )
    return pl.pallas_call(
        _expert_kernel,
        grid_spec=grid_spec,
        out_shape=jax.ShapeDtypeStruct((n_rows * LANE_TILES, 128), F32),
        compiler_params=pltpu.CompilerParams(
            dimension_semantics=("arbitrary",), vmem_limit_bytes=VMEM_LIMIT_EXPERTS),
        name="moe_experts",
    )(first_blk, n_blk, row_tok, w_gu, bg, bl, w_down, bd, x1r)


def _combine_start(ys_hbm, idx_ref, buf_ref, sem, tm):
    def body(t, carry):
        for kk in range(TOP_K):
            pltpu.make_async_copy(ys_hbm.at[_tile_rows(idx_ref[0, t * TOP_K + kk]), :],
                                  buf_ref.at[kk, _tile_rows(t), :], sem).start()
        return carry
    lax.fori_loop(0, tm, body, 0, unroll=2)


def _combine_wait(ys_hbm, buf_ref, sem, tm):
    for kk in range(TOP_K):
        pltpu.make_async_copy(ys_hbm.at[pl.ds(0, tm * LANE_TILES), :], buf_ref.at[kk], sem).wait()


def _combine_kernel(idx_cur, idx_nxt, x1_ref, g4_ref, g2_ref, b2_ref, ys_hbm,
                    o_ref, buf_ref, sem):
    i = pl.program_id(0)
    n = pl.num_programs(0)
    slot = i % 2
    tm = o_ref.shape[0]

    @pl.when(i == 0)
    def _():
        _combine_start(ys_hbm, idx_cur, buf_ref.at[0], sem.at[0], tm)

    @pl.when(i + 1 < n)
    def _():
        _combine_start(ys_hbm, idx_nxt, buf_ref.at[1 - slot], sem.at[1 - slot], tm)

    _combine_wait(ys_hbm, buf_ref.at[slot], sem.at[slot], tm)
    g4 = g4_ref[...]
    ffn = g4[:, 0:1] * _load_row_tiles(buf_ref.at[slot, 0], tm)
    for kk in range(1, TOP_K):
        ffn = ffn + g4[:, kk:kk + 1] * _load_row_tiles(buf_ref.at[slot, kk], tm)
    x1 = _load_row_tiles(x1_ref, tm)
    o_ref[...] = _layer_norm(ALPHA * x1 + ffn, g2_ref[...], b2_ref[...])


def _combine(dest3, x1r, g4, g2, b2, ys):
    d = D_MODEL
    t = x1r.shape[0] // LANE_TILES
    tm = TM_COMB
    nt = t // tm
    row = lambda i: (i, 0)
    fixed = lambda i: (0, 0)
    return pl.pallas_call(
        _combine_kernel,
        grid=(nt,),
        in_specs=[
            pl.BlockSpec((None, 1, tm * TOP_K), lambda i: (i, 0, 0), memory_space=pltpu.SMEM),
            pl.BlockSpec((None, 1, tm * TOP_K), lambda i: (jnp.minimum(i + 1, nt - 1), 0, 0),
                         memory_space=pltpu.SMEM),
            pl.BlockSpec((tm * LANE_TILES, 128), row),
            pl.BlockSpec((tm, TOP_K), row),
            pl.BlockSpec((1, d), fixed),
            pl.BlockSpec((1, d), fixed),
            pl.BlockSpec(memory_space=pl.ANY),
        ],
        out_specs=pl.BlockSpec((tm, d), row),
        out_shape=jax.ShapeDtypeStruct((t, d), F32),
        scratch_shapes=[pltpu.VMEM((2, TOP_K, tm * LANE_TILES, 128), F32),
                        pltpu.SemaphoreType.DMA((2,))],
        compiler_params=pltpu.CompilerParams(
            dimension_semantics=("arbitrary",), vmem_limit_bytes=VMEM_LIMIT),
        name="moe_combine_ln",
    )(dest3, dest3, x1r, g4, g2, b2, ys)


def kernel(x, ln0_g, ln0_b, w_in, conv_w, conv_b, w_rg_a, b_rg_a, w_rg_x, b_rg_x, lru_lambda, lam_q1, lam_k1, lam_q2, lam_k2, subln_g, w_out, ln1_g, ln1_b, w_router, b_router, w_gu, b_gu, w_down, b_down, ln2_g, ln2_b):
    bsz, seq, d = x.shape
    t = bsz * seq
    x2 = x.reshape(t, d)
    g0 = ln0_g.reshape(1, d)
    b0 = ln0_b.reshape(1, d)
    l = 0

    w_in_bf = w_in[l].astype(BF16)
    w_out_bf = w_out[l].astype(BF16)
    wa = jax.scipy.linalg.block_diag(*[w_rg_a[l, n] for n in range(LRU_BLOCKS)])
    wx = jax.scipy.linalg.block_diag(*[w_rg_x[l, n] for n in range(LRU_BLOCKS)])
    w_gate_bf = jnp.concatenate([wa, wx], axis=1).astype(BF16)
    b_gate = jnp.concatenate([b_rg_a[l].reshape(1, -1), b_rg_x[l].reshape(1, -1)], axis=1)
    lamvec = jnp.stack([lam_q1[l], lam_k1[l], lam_q2[l], lam_k2[l]]).astype(F32)
    slopes = jnp.asarray([2.0 ** (-8.0 * (i + 1) / ATT_HEADS) for i in range(ATT_HEADS)], F32)
    slopes = jnp.broadcast_to(slopes[:, None, None], (ATT_HEADS, 1, 128))
    bg = b_gu[l, :, None, 0::2]
    bl = b_gu[l, :, None, 1::2]
    bd = b_down[l][:, None, :]

    qt, k, vt, xl, gl = _inproj(x, g0, b0, w_in_bf)
    att = _attention(qt, k, vt, slopes, lamvec, subln_g[l].reshape(-1, 1))
    rec = _lru(xl, gl, conv_w[l], conv_b[l].reshape(1, -1), w_gate_bf, b_gate,
               lru_lambda[l].reshape(1, -1))
    wr_t = w_router[l].T.astype(F32)
    wr_hi = wr_t.astype(BF16)
    wr_lo = (wr_t - wr_hi.astype(F32)).astype(BF16)
    x1r, e4t, g4t, r4t, cnt = _outproj_router(
        x2, g0, b0, att.reshape(t, -1), rec.reshape(t, -1), w_out_bf,
        ln1_g[l].reshape(1, d), ln1_b[l].reshape(1, d),
        jnp.concatenate([wr_hi, wr_lo], axis=0), b_router[l].reshape(-1, 1).astype(F32))
    e4 = e4t[0:TOP_K].T
    g4 = g4t[0:TOP_K].T
    r4 = r4t[0:TOP_K].T

    n_assign = t * TOP_K
    nb = (n_assign + N_EXPERTS * (ROW_BLOCK - 1)) // ROW_BLOCK + 1
    counts = cnt[:, 0].astype(I32)
    padded = (counts + ROW_BLOCK - 1) // ROW_BLOCK * ROW_BLOCK
    pends = jnp.cumsum(padded)
    pstarts = pends - padded
    dest = pstarts[e4] + r4
    row_tok = _invert(dest.reshape(t // TM_INV, 1, TM_INV * TOP_K), nb * ROW_BLOCK)
    ys = _experts((pstarts // ROW_BLOCK).astype(I32), (padded // ROW_BLOCK).astype(I32),
                  row_tok, x1r, w_gu[l], bg, bl, w_down[l], bd)
    out = _combine(dest.reshape(t // TM_COMB, 1, TM_COMB * TOP_K), x1r, g4,
                   ln2_g[l].reshape(1, d), ln2_b[l].reshape(1, d), ys)
    return out.reshape(bsz, seq, d)
```

```python
import math

import jax
import jax.numpy as jnp
from jax import lax
from jax.experimental import pallas as pl
from jax.experimental.pallas import tpu as pltpu

F32 = jnp.float32
BF16 = jnp.bfloat16
I32 = jnp.int32

D_MODEL = 1024
ATT_WIDTH = 512
ATT_HEAD_DIM = 64
ATT_HEADS = 4
VAL_DIM = 2 * ATT_HEAD_DIM
LRU_WIDTH = 512
LRU_BLOCKS = 8
LRU_C = 8.0
CONV_WIDTH = 4
CHUNK_SHIFT = 6
N_EXPERTS = 32
TOP_K = 4
D_EXPERT = 1024
SWIGLU_LIMIT = 7.0
SWIGLU_ALPHA = 1.702
DEPTH = 1
ALPHA = (2.0 * DEPTH) ** 0.25
LN_EPS = 1e-5
SUBLN_EPS = 1e-5
LAM_INIT = 0.8 - 0.6 * math.exp(-0.3 * 0)
NEG_BIG = -1e30
LOG2E = math.log2(math.e)

TM_PROJ = 512
TQ = 1024
TKB = 128
SW = 256
TC_LRU = 512
SCAN_GROUP = 8
ROW_BLOCK = 256
TM_COMB = 128
TM_INV = 2048
XPOSE_CHUNK = 512

VMEM_LIMIT = 48 * 1024 * 1024
VMEM_LIMIT_EXPERTS = 56 * 1024 * 1024


LANE_TILES = D_MODEL // 128


def _store_row_tiles(ref, val):
    n = val.shape[0]
    for j in range(LANE_TILES):
        ref[pl.ds(j, n, stride=LANE_TILES), :] = val[:, j * 128:(j + 1) * 128]


def _load_row_tiles(ref, n):
    return jnp.concatenate(
        [ref[pl.ds(j, n, stride=LANE_TILES), :] for j in range(LANE_TILES)], axis=1)


def _layer_norm(x, g, b):
    mu = jnp.mean(x, axis=-1, keepdims=True)
    xc = x - mu
    var = jnp.mean(xc * xc, axis=-1, keepdims=True)
    return xc * lax.rsqrt(var + LN_EPS) * g + b


def _inproj_kernel(x_ref, g_ref, b_ref, w_ref, qt_ref, k_ref, vt_ref, xl_ref, gl_ref):
    x0 = _layer_norm(x_ref[...], g_ref[...], b_ref[...])
    p = jnp.dot(x0.astype(BF16), w_ref[...], preferred_element_type=F32)
    a = ATT_WIDTH
    qt_ref[...] = (p[:, 0:a] * (ATT_HEAD_DIM ** -0.5 * LOG2E)).T.astype(BF16)
    k_ref[...] = p[:, a:2 * a].astype(BF16)
    vt = p[:, 2 * a:3 * a].T.astype(BF16)
    for h in range(ATT_HEADS):
        for c in range(vt_ref.shape[1]):
            vt_ref[h, c] = vt[h * VAL_DIM:(h + 1) * VAL_DIM, c * TKB:(c + 1) * TKB]
    xl_ref[...] = p[:, 3 * a:3 * a + LRU_WIDTH]
    gl_ref[...] = p[:, 3 * a + LRU_WIDTH:]


def _inproj(x, g, b, w_bf):
    bsz, seq, _ = x.shape
    n = w_bf.shape[1]
    tm = TM_PROJ
    row = lambda bi, i: (bi, i, 0)
    col = lambda bi, i: (bi, 0, i)
    fixed = lambda bi, i: (0, 0)
    return pl.pallas_call(
        _inproj_kernel,
        grid=(bsz, seq // tm),
        in_specs=[
            pl.BlockSpec((None, tm, D_MODEL), row),
            pl.BlockSpec((1, D_MODEL), fixed),
            pl.BlockSpec((1, D_MODEL), fixed),
            pl.BlockSpec((D_MODEL, n), fixed),
        ],
        out_specs=[
            pl.BlockSpec((None, ATT_WIDTH, tm), col),
            pl.BlockSpec((None, tm, ATT_WIDTH), row),
            pl.BlockSpec((None, ATT_HEADS, tm // TKB, VAL_DIM, TKB),
                         lambda bi, i: (bi, 0, i, 0, 0)),
            pl.BlockSpec((None, tm, LRU_WIDTH), row),
            pl.BlockSpec((None, tm, LRU_WIDTH), row),
        ],
        out_shape=[
            jax.ShapeDtypeStruct((bsz, ATT_WIDTH, seq), BF16),
            jax.ShapeDtypeStruct((bsz, seq, ATT_WIDTH), BF16),
            jax.ShapeDtypeStruct((bsz, ATT_HEADS, seq // TKB, VAL_DIM, TKB), BF16),
            jax.ShapeDtypeStruct((bsz, seq, LRU_WIDTH), F32),
            jax.ShapeDtypeStruct((bsz, seq, LRU_WIDTH), F32),
        ],
        compiler_params=pltpu.CompilerParams(
            dimension_semantics=("parallel", "parallel"), vmem_limit_bytes=VMEM_LIMIT),
        name="ln_inproj",
    )(x, g, b, w_bf)


def _attn_kernel(qt_ref, k_ref, vt_ref, slope_ref, lam_ref, g_ref,
                 o_ref, q2_ref, kx_ref, qbias_ref, diag_ref, *stat_refs):
    qi = pl.program_id(2)
    tq = qt_ref.shape[1]
    n_strips = 2 * tq // SW
    blocks_per_q = tq // TKB
    cl = slope_ref[0:1, 0:1] * LOG2E

    @pl.when(qi == 0)
    def _():
        c = lax.broadcasted_iota(I32, (TKB, 128), 0)
        lane = lax.broadcasted_iota(I32, (TKB, 128), 1)
        w = c.astype(F32) * cl
        w_hi = w.astype(BF16).astype(F32)
        w_mid = (w - w_hi).astype(BF16).astype(F32)
        w_lo = w - w_hi - w_mid
        kx_ref[...] = jnp.where(
            lane == 0, w_hi, jnp.where(lane == 1, w_mid, jnp.where(lane == 2, w_lo, 0.0))
        ).astype(BF16)
        r1 = lax.broadcasted_iota(I32, (1, tq), 1)
        qbias_ref[...] = r1.astype(F32) * (-cl)
        ck0 = lax.broadcasted_iota(I32, (TKB, tq), 0)
        r = lax.broadcasted_iota(I32, (TKB, tq), 1)
        for kd in range(blocks_per_q):
            ck = ck0 + kd * TKB
            allowed = (ck >> CHUNK_SHIFT) <= (r >> CHUNK_SHIFT)
            bias = (jnp.abs(r - ck) + ck0).astype(F32) * (-cl)
            diag_ref[kd] = jnp.where(allowed, bias, NEG_BIG)

    m_refs = stat_refs[0:n_strips]
    l_refs = stat_refs[n_strips:2 * n_strips]
    acc_refs = stat_refs[2 * n_strips:3 * n_strips]
    for j in range(n_strips):
        m_refs[j][...] = jnp.full(m_refs[j].shape, NEG_BIG, F32)
        l_refs[j][...] = jnp.zeros(l_refs[j].shape, F32)
        acc_refs[j][...] = jnp.zeros(acc_refs[j].shape, F32)
    qt = qt_ref[...]
    frow = lax.broadcasted_iota(I32, qt.shape, 0)
    zero = jnp.zeros_like(qt)
    q2_ref[0:VAL_DIM, 0:tq] = jnp.where(frow < ATT_HEAD_DIM, qt, zero)
    q2_ref[0:VAL_DIM, tq:2 * tq] = jnp.where(frow >= ATT_HEAD_DIM, qt, zero)
    xrow = lax.broadcasted_iota(I32, (VAL_DIM, 2 * tq), 0)
    q2_ref[VAL_DIM:2 * VAL_DIM, :] = jnp.where(xrow < 3, 1.0, 0.0).astype(BF16)

    def load_keys(kb):
        k_blk = k_ref[pl.ds(pl.multiple_of(kb * TKB, TKB), TKB), :]
        return jnp.concatenate([k_blk, kx_ref[...]], axis=1), vt_ref[kb]

    def strip(j, k_ext, vt_blk, sc_bias, shift):
        ls = slice(j * SW, (j + 1) * SW)
        sc = jnp.dot(k_ext, q2_ref[:, ls], preferred_element_type=F32)
        if sc_bias is not None:
            sc = sc + sc_bias
        m_prev = m_refs[j][...]
        m_new = jnp.maximum(m_prev, jnp.max(sc, axis=0, keepdims=True) + shift)
        alpha = jnp.exp2(m_prev - m_new)
        p = jnp.exp2(sc - (m_new - shift))
        l_refs[j][...] = alpha * l_refs[j][...] + jnp.sum(p, axis=0, keepdims=True)
        pv = jnp.dot(vt_blk, p.astype(BF16), preferred_element_type=F32)
        acc_refs[j][...] = alpha * acc_refs[j][...] + pv
        m_refs[j][...] = m_new

    def off_diag_group(g, carry):
        for kd in range(blocks_per_q):
            kb = g * blocks_per_q + kd
            k_ext, vt_blk = load_keys(kb)
            block_shift = jnp.full((1, 1), qi * tq - kb * TKB, I32).astype(F32) * (-cl)
            for j in range(n_strips):
                jq = j % (tq // SW)
                strip(j, k_ext, vt_blk, None,
                      qbias_ref[:, jq * SW:(jq + 1) * SW] + block_shift)
        return carry
    lax.fori_loop(0, qi, off_diag_group, 0)

    for kd in range(blocks_per_q):
        k_ext, vt_blk = load_keys(qi * blocks_per_q + kd)
        for j in range(n_strips):
            jq = j % (tq // SW)
            if (jq + 1) * SW > kd * TKB:
                strip(j, k_ext, vt_blk, diag_ref[kd, :, jq * SW:(jq + 1) * SW], 0.0)

    lv = lam_ref[...]
    lam = (jnp.exp(jnp.sum(lv[0:1] * lv[1:2], axis=-1, keepdims=True))
           - jnp.exp(jnp.sum(lv[2:3] * lv[3:4], axis=-1, keepdims=True)) + LAM_INIT)
    o2 = [acc_refs[j][...] / l_refs[j][...] for j in range(n_strips)]
    half = n_strips // 2
    o = jnp.concatenate([o2[j] - lam * o2[half + j] for j in range(half)], axis=1)
    o = o * lax.rsqrt(jnp.mean(o * o, axis=0, keepdims=True) + SUBLN_EPS)
    o = o * (g_ref[...] * (1.0 - LAM_INIT))
    o_ref[...] = o.T.astype(o_ref.dtype)


def _attention(qt, k, vt, slopes, lamvec, subln_g_col):
    b, s, _ = k.shape
    nkb = s // TKB
    n_strips = 2 * TQ // SW
    return pl.pallas_call(
        _attn_kernel,
        grid=(b, ATT_HEADS, s // TQ),
        in_specs=[
            pl.BlockSpec((None, VAL_DIM, TQ), lambda bi, h, qi: (bi, h, qi)),
            pl.BlockSpec((None, s, VAL_DIM), lambda bi, h, qi: (bi, 0, h)),
            pl.BlockSpec((None, None, nkb, VAL_DIM, TKB), lambda bi, h, qi: (bi, h, 0, 0, 0)),
            pl.BlockSpec((None, 1, 128), lambda bi, h, qi: (h, 0, 0)),
            pl.BlockSpec((4, ATT_HEAD_DIM), lambda bi, h, qi: (0, 0)),
            pl.BlockSpec((VAL_DIM, 1), lambda bi, h, qi: (0, 0)),
        ],
        out_specs=pl.BlockSpec((None, TQ, VAL_DIM), lambda bi, h, qi: (bi, qi, h)),
        out_shape=jax.ShapeDtypeStruct((b, s, ATT_WIDTH), BF16),
        scratch_shapes=[
            pltpu.VMEM((2 * VAL_DIM, 2 * TQ), BF16),
            pltpu.VMEM((TKB, 128), BF16),
            pltpu.VMEM((1, TQ), F32),
            pltpu.VMEM((TQ // TKB, TKB, TQ), F32),
        ] +[pltpu.VMEM((1, SW), F32)] * (2 * n_strips) + [pltpu.VMEM((VAL_DIM, SW), F32)] * n_strips,
        compiler_params=pltpu.CompilerParams(
            dimension_semantics=("parallel", "parallel", "arbitrary"),
            vmem_limit_bytes=VMEM_LIMIT),
        name="diff_attention",
    )(qt, k, vt, slopes, lamvec, subln_g_col)


def _lru_kernel(xl_ref, gl_ref, cw_ref, cb_ref, wg_ref, bg_ref, lam_ref,
                o_ref, ext_ref, h_ref):
    c = pl.program_id(1)
    tc = xl_ref.shape[0]
    w = LRU_WIDTH

    @pl.when(c == 0)
    def _():
        ext_ref[0:8, :] = jnp.zeros((8, w), F32)
        h_ref[...] = jnp.zeros(h_ref.shape, F32)

    ext_ref[8:8 + tc, :] = xl_ref[...]
    cw = cw_ref[...]
    xc = cb_ref[...] + cw[0:1] * ext_ref[5:5 + tc, :]
    for j in range(1, CONV_WIDTH):
        xc = xc + cw[j:j + 1] * ext_ref[5 + j:5 + j + tc, :]
    tail = ext_ref[tc:tc + 8, :]

    gates = jnp.dot(xc.astype(BF16), wg_ref[...], preferred_element_type=F32) + bg_ref[...]
    r = jax.nn.sigmoid(gates[:, 0:w])
    ig = jax.nn.sigmoid(gates[:, w:2 * w])
    z = -lam_ref[...]
    softplus = jnp.maximum(z, 0.0) + jnp.log1p(jnp.exp(-jnp.abs(z)))
    log_a = (-LRU_C) * r * softplus
    a = jnp.exp(log_a)
    u = jnp.sqrt(-jnp.tanh(log_a) * (a * a + 1.0)) * (ig * xc)

    ng = tc // SCAN_GROUP
    a = a.reshape(ng, SCAN_GROUP, w)
    u = u.reshape(ng, SCAN_GROUP, w)
    pos = lax.broadcasted_iota(I32, a.shape, 1)
    d = 1
    while d < SCAN_GROUP:
        valid = pos >= d
        u = jnp.where(valid, a * pltpu.roll(u, d, axis=1) + u, u)
        a = jnp.where(valid, a * pltpu.roll(a, d, axis=1), a)
        d *= 2
    a = a.reshape(tc, w)
    u = u.reshape(tc, w)
    h_in = h_ref[0:1, :]
    groups = []
    for g in range(tc // SCAN_GROUP):
        rows = slice(g * SCAN_GROUP, (g + 1) * SCAN_GROUP)
        hg = u[rows] + a[rows] * h_in
        groups.append(hg)
        h_in = hg[SCAN_GROUP - 1:SCAN_GROUP, :]
    h = jnp.concatenate(groups, axis=0)

    gl = gl_ref[...]
    gelu = 0.5 * gl * (1.0 + jnp.tanh(math.sqrt(2.0 / math.pi) * (gl + 0.044715 * (gl * gl * gl))))
    o_ref[...] = (h * gelu).astype(o_ref.dtype)

    h_ref[0:1, :] = h[tc - 1:tc, :]
    ext_ref[0:8, :] = tail


def _lru(xl, gl, conv_w, conv_b, w_gate_bf, b_gate, lam):
    b, s, w = xl.shape
    tc = TC_LRU
    blk = lambda bi, c: (bi, c, 0)
    fixed = lambda bi, c: (0, 0)
    return pl.pallas_call(
        _lru_kernel,
        grid=(b, s // tc),
        in_specs=[
            pl.BlockSpec((None, tc, w), blk),
            pl.BlockSpec((None, tc, w), blk),
            pl.BlockSpec((CONV_WIDTH, w), fixed),
            pl.BlockSpec((1, w), fixed),
            pl.BlockSpec((w, 2 * w), fixed),
            pl.BlockSpec((1, 2 * w), fixed),
            pl.BlockSpec((1, w), fixed),
        ],
        out_specs=pl.BlockSpec((None, tc, w), blk),
        out_shape=jax.ShapeDtypeStruct((b, s, w), BF16),
        scratch_shapes=[pltpu.VMEM((tc + 8, w), F32), pltpu.VMEM((8, w), F32)],
        compiler_params=pltpu.CompilerParams(
            dimension_semantics=("parallel", "arbitrary"), vmem_limit_bytes=VMEM_LIMIT),
        name="rg_lru",
    )(xl, gl, conv_w, conv_b, w_gate_bf, b_gate, lam)


def _outproj_kernel(x_ref, g0_ref, b0_ref, att_ref, rec_ref, wo_ref, g1_ref, b1_ref,
                    wr_ref, br_ref, x1_ref, e4_ref, g4_ref, r4_ref, cnt_ref, carry_ref):
    i = pl.program_id(0)
    tm = x_ref.shape[0]

    @pl.when(i == 0)
    def _():
        carry_ref[...] = jnp.zeros(carry_ref.shape, F32)

    x0 = _layer_norm(x_ref[...], g0_ref[...], b0_ref[...])
    mixed = (jnp.dot(att_ref[...], wo_ref[0:ATT_WIDTH, :], preferred_element_type=F32)
             + jnp.dot(rec_ref[...], wo_ref[ATT_WIDTH:, :], preferred_element_type=F32))
    x1 = _layer_norm(ALPHA * x0 + mixed, g1_ref[...], b1_ref[...])
    _store_row_tiles(x1_ref, x1)

    x1_hi = x1.astype(BF16)
    x1_lo = (x1 - x1_hi.astype(F32)).astype(BF16)
    nt = (((1,), (1,)), ((), ()))
    ne = N_EXPERTS
    hi_terms = lax.dot_general(wr_ref[...], x1_hi, nt, preferred_element_type=F32)
    lo_term = lax.dot_general(wr_ref[0:ne, :], x1_lo, nt, preferred_element_type=F32)
    logits = hi_terms[0:ne] + hi_terms[ne:2 * ne] + lo_term + br_ref[...]

    erow = lax.broadcasted_iota(I32, logits.shape, 0)
    work = logits
    vals, idxs, hots = [], [], []
    for _ in range(TOP_K):
        mx = jnp.max(work, axis=0, keepdims=True)
        idx = jnp.min(jnp.where(work == mx, erow, ne), axis=0, keepdims=True)
        hot = erow == idx
        vals.append(mx)
        idxs.append(idx)
        hots.append(hot)
        work = jnp.where(hot, -jnp.inf, work)
    exps = [jnp.exp(v - vals[0]) for v in vals]
    denom = exps[0] + exps[1] + exps[2] + exps[3]

    mask = jnp.zeros(logits.shape, F32)
    for hot in hots:
        mask = mask + hot.astype(F32)
    tr = lax.broadcasted_iota(I32, (tm, tm), 0)
    tc = lax.broadcasted_iota(I32, (tm, tm), 1)
    earlier = jnp.where(tr < tc, 1.0, 0.0).astype(BF16)
    carry = carry_ref[:, 0:1]
    excl = jnp.dot(mask.astype(BF16), earlier, preferred_element_type=F32) + carry
    ranks = [jnp.sum(jnp.where(hot, excl, 0.0), axis=0, keepdims=True) for hot in hots]
    pad_i = jnp.zeros((8 - TOP_K, tm), I32)
    pad_f = jnp.zeros((8 - TOP_K, tm), F32)
    e4_ref[...] = jnp.concatenate(idxs + [pad_i], axis=0)
    g4_ref[...] = jnp.concatenate([ex / denom for ex in exps] + [pad_f], axis=0)
    r4_ref[...] = jnp.concatenate([r.astype(I32) for r in ranks] + [pad_i], axis=0)
    total = carry + jnp.sum(mask, axis=1, keepdims=True)
    carry_ref[...] = jnp.broadcast_to(total, carry_ref.shape)
    cnt_ref[...] = jnp.broadcast_to(total, cnt_ref.shape)


def _outproj_router(x2, g0, b0, att, rec, wo_bf, g1, b1, w_router, b_router):
    t = x2.shape[0]
    tm = TM_PROJ
    row = lambda i: (i, 0)
    col = lambda i: (0, i)
    fixed = lambda i: (0, 0)
    return pl.pallas_call(
        _outproj_kernel,
        grid=(t // tm,),
        in_specs=[
            pl.BlockSpec((tm, D_MODEL), row),
            pl.BlockSpec((1, D_MODEL), fixed),
            pl.BlockSpec((1, D_MODEL), fixed),
            pl.BlockSpec((tm, ATT_WIDTH), row),
            pl.BlockSpec((tm, LRU_WIDTH), row),
            pl.BlockSpec((D_MODEL, D_MODEL), fixed),
            pl.BlockSpec((1, D_MODEL), fixed),
            pl.BlockSpec((1, D_MODEL), fixed),
            pl.BlockSpec((2 * N_EXPERTS, D_MODEL), fixed),
            pl.BlockSpec((N_EXPERTS, 1), fixed),
        ],
        out_specs=[
            pl.BlockSpec((tm * LANE_TILES, 128), row),
            pl.BlockSpec((8, tm), col),
            pl.BlockSpec((8, tm), col),
            pl.BlockSpec((8, tm), col),
            pl.BlockSpec((N_EXPERTS, 128), fixed),
        ],
        out_shape=[
            jax.ShapeDtypeStruct((t * LANE_TILES, 128), F32),
            jax.ShapeDtypeStruct((8, t), I32),
            jax.ShapeDtypeStruct((8, t), F32),
            jax.ShapeDtypeStruct((8, t), I32),
            jax.ShapeDtypeStruct((N_EXPERTS, 128), F32),
        ],
        scratch_shapes=[pltpu.VMEM((N_EXPERTS, 128), F32)],
        compiler_params=pltpu.CompilerParams(
            dimension_semantics=("arbitrary",), vmem_limit_bytes=VMEM_LIMIT),
        name="outproj_ln_router",
    )(x2, g0, b0, att, rec, wo_bf, g1, b1, w_router, b_router)


def _tile_rows(row):
    return pl.ds(pl.multiple_of(row * LANE_TILES, LANE_TILES), LANE_TILES)


def _invert_kernel(dest_ref, rt_ref, zero_ref, sem):
    i = pl.program_id(0)
    tm = dest_ref.shape[1] // TOP_K

    @pl.when(i == 0)
    def _():
        zero_ref[...] = jnp.zeros(zero_ref.shape, I32)
        clear = pltpu.make_async_copy(zero_ref, rt_ref, sem)
        clear.start()
        clear.wait()

    def body(t, carry):
        for kk in range(TOP_K):
            rt_ref[dest_ref[0, t * TOP_K + kk]] = i * tm + t
        return carry
    lax.fori_loop(0, tm, body, 0, unroll=4)


def _invert(dest3, n_rows):
    nt, _, per = dest3.shape
    return pl.pallas_call(
        _invert_kernel,
        grid=(nt,),
        in_specs=[pl.BlockSpec((None, 1, per), lambda i: (i, 0, 0), memory_space=pltpu.SMEM)],
        out_specs=pl.BlockSpec(memory_space=pltpu.SMEM),
        out_shape=jax.ShapeDtypeStruct((n_rows,), I32),
        scratch_shapes=[pltpu.VMEM((n_rows,), I32), pltpu.SemaphoreType.DMA(())],
        compiler_params=pltpu.CompilerParams(dimension_semantics=("arbitrary",)),
        name="moe_invert",
    )(dest3)


def _expert_kernel(first_ref, nblk_ref, rt_ref, wgu_ref, bg_ref, bl_ref, wd_ref, bd_ref,
                   x_hbm, ys_hbm, wt_ref, wg_ref, wl_ref, wdb_ref, xb_ref, xbuf, ybuf, xsem, ysem):
    e = pl.program_id(0)
    n_exp = pl.num_programs(0)
    d, f2 = wgu_ref.shape
    f = f2 // 2
    ch = XPOSE_CHUNK
    first = first_ref[e]
    nblk = nblk_ref[e]
    n_live = first_ref[n_exp - 1] + nblk_ref[n_exp - 1]
    n_blocks = ys_hbm.shape[0] // (ROW_BLOCK * LANE_TILES)
    block_rows = ROW_BLOCK * LANE_TILES

    def gather_start(g, slot):
        for r in range(ROW_BLOCK):
            pltpu.make_async_copy(
                x_hbm.at[_tile_rows(rt_ref[g * ROW_BLOCK + r]), :],
                xbuf.at[slot, pl.ds(r * LANE_TILES, LANE_TILES), :],
                xsem.at[slot]).start()

    def gather_wait(slot):
        pltpu.make_async_copy(x_hbm.at[pl.ds(0, block_rows), :], xbuf.at[slot],
                              xsem.at[slot]).wait()

    def y_copy(g, slot):
        start = pl.multiple_of(g * block_rows, block_rows)
        return pltpu.make_async_copy(ybuf.at[slot], ys_hbm.at[pl.ds(start, block_rows), :],
                                     ysem.at[slot])

    @pl.when(jnp.logical_and(nblk > 0, first == 0))
    def _():
        gather_start(0, 0)

    @pl.when(nblk > 0)
    def _():
        for j in range(d // 128):
            rows = slice(j * 128, (j + 1) * 128)
            for c in range(f2 // ch):
                wt_ref[j, c * ch:(c + 1) * ch, :] = wgu_ref[rows, c * ch:(c + 1) * ch].T
            for c in range(f // ch):
                gate_rows = wt_ref[j, pl.ds(2 * c * ch, ch, stride=2), :]
                lin_rows = wt_ref[j, pl.ds(2 * c * ch + 1, ch, stride=2), :]
                wg_ref[c * ch:(c + 1) * ch, rows] = gate_rows.astype(BF16)
                wl_ref[c * ch:(c + 1) * ch, rows] = lin_rows.astype(BF16)
        wdb_ref[...] = wd_ref[...].astype(BF16)

    def block(b, carry):
        g = first + b
        for slot in range(2):
            @pl.when(g % 2 == slot)
            def _():
                block_body(g, slot)
        return carry

    def block_body(g, slot):
        @pl.when(g >= 2)
        def _():
            y_copy(g - 2, slot).wait()

        gather_wait(slot)
        xb_ref[...] = _load_row_tiles(xbuf.at[slot], ROW_BLOCK).astype(BF16)
        gather_start(jnp.minimum(g + 1, n_live - 1), 1 - slot)
        x = xb_ref[...]
        nt = (((1,), (1,)), ((), ()))
        hg = lax.dot_general(x, wg_ref[...], nt, preferred_element_type=F32) + bg_ref[...]
        hl = lax.dot_general(x, wl_ref[...], nt, preferred_element_type=F32) + bl_ref[...]
        xg = jnp.minimum(hg, SWIGLU_LIMIT)
        xl = jnp.clip(hl, -SWIGLU_LIMIT, SWIGLU_LIMIT)
        act = xg * jax.nn.sigmoid(SWIGLU_ALPHA * xg) * (xl + 1.0)
        y = jnp.dot(act.astype(BF16), wdb_ref[...], preferred_element_type=F32) + bd_ref[...]
        _store_row_tiles(ybuf.at[slot], y)
        y_copy(g, slot).start()

    lax.fori_loop(0, nblk, block, 0)

    @pl.when(e == n_exp - 1)
    def _():
        gather_wait(n_live % 2)
        y_copy(0, 0).wait()
        y_copy(0, 1).wait()
        ybuf[0] = jnp.zeros(ybuf.shape[1:], F32)
        for t in range(N_EXPERTS):
            @pl.when(n_live + t < n_blocks)
            def _():
                y_copy(n_live + t, 0).start()
        for t in range(N_EXPERTS):
            @pl.when(n_live + t < n_blocks)
            def _():
                y_copy(n_live + t, 0).wait()


def _experts(first_blk, n_blk, row_tok, x1r, w_gu, bg, bl, w_down, bd):
    d = D_MODEL
    f2 = w_gu.shape[2]
    f = f2 // 2
    n_rows = row_tok.shape[0]
    wmap = lambda e, fb, nb_: (e, 0, 0)
    grid_spec = pltpu.PrefetchScalarGridSpec(
        num_scalar_prefetch=2,
        grid=(N_EXPERTS,),
        in_specs=[
            pl.BlockSpec(memory_space=pltpu.SMEM),
            pl.BlockSpec((None, d, f2), wmap),
            pl.BlockSpec((None, 1, f), wmap),
            pl.BlockSpec((None, 1, f), wmap),
            pl.BlockSpec((None, f, d), wmap),
            pl.BlockSpec((None, 1, d), wmap),
            pl.BlockSpec(memory_space=pl.ANY),
        ],
        out_specs=pl.BlockSpec(memory_space=pl.ANY),
        scratch_shapes=[
            pltpu.VMEM((d // 128, f2, 128), F32),
            pltpu.VMEM((d, f), BF16),
            pltpu.VMEM((d, f), BF16),
            pltpu.VMEM((f, d), BF16),
            pltpu.VMEM((ROW_BLOCK, d), BF16),
            pltpu.VMEM((2, ROW_BLOCK * LANE_TILES, 128), F32),
            pltpu.VMEM((2, ROW_BLOCK * LANE_TILES, 128), F32),
            pltpu.SemaphoreType.DMA((2,)),
            pltpu.SemaphoreType.DMA((2,)),
        ],
    )
    return pl.pallas_call(
        _expert_kernel,
        grid_spec=grid_spec,
        out_shape=jax.ShapeDtypeStruct((n_rows * LANE_TILES, 128), F32),
        compiler_params=pltpu.CompilerParams(
            dimension_semantics=("arbitrary",), vmem_limit_bytes=VMEM_LIMIT_EXPERTS),
        name="moe_experts",
    )(first_blk, n_blk, row_tok, w_gu, bg, bl, w_down, bd, x1r)


def _combine_start(ys_hbm, idx_ref, buf_ref, sem, tm):
    def body(t, carry):
        for kk in range(TOP_K):
            pltpu.make_async_copy(ys_hbm.at[_tile_rows(idx_ref[0, t * TOP_K + kk]), :],
                                  buf_ref.at[kk, _tile_rows(t), :], sem).start()
        return carry
    lax.fori_loop(0, tm, body, 0, unroll=2)


def _combine_wait(ys_hbm, buf_ref, sem, tm):
    for kk in range(TOP_K):
        pltpu.make_async_copy(ys_hbm.at[pl.ds(0, tm * LANE_TILES), :], buf_ref.at[kk], sem).wait()


def _combine_kernel(idx_cur, idx_nxt, x1_ref, g4_ref, g2_ref, b2_ref, ys_hbm,
                    o_ref, buf_ref, sem):
    i = pl.program_id(0)
    n = pl.num_programs(0)
    slot = i % 2
    tm = o_ref.shape[0]

    @pl.when(i == 0)
    def _():
        _combine_start(ys_hbm, idx_cur, buf_ref.at[0], sem.at[0], tm)

    @pl.when(i + 1 < n)
    def _():
        _combine_start(ys_hbm, idx_nxt, buf_ref.at[1 - slot], sem.at[1 - slot], tm)

    _combine_wait(ys_hbm, buf_ref.at[slot], sem.at[slot], tm)
    g4 = g4_ref[...]
    ffn = g4[:, 0:1] * _load_row_tiles(buf_ref.at[slot, 0], tm)
    for kk in range(1, TOP_K):
        ffn = ffn + g4[:, kk:kk + 1] * _load_row_tiles(buf_ref.at[slot, kk], tm)
    x1 = _load_row_tiles(x1_ref, tm)
    o_ref[...] = _layer_norm(ALPHA * x1 + ffn, g2_ref[...], b2_ref[...])


def _combine(dest3, x1r, g4, g2, b2, ys):
    d = D_MODEL
    t = x1r.shape[0] // LANE_TILES
    tm = TM_COMB
    nt = t // tm
    row = lambda i: (i, 0)
    fixed = lambda i: (0, 0)
    return pl.pallas_call(
        _combine_kernel,
        grid=(nt,),
        in_specs=[
            pl.BlockSpec((None, 1, tm * TOP_K), lambda i: (i, 0, 0), memory_space=pltpu.SMEM),
            pl.BlockSpec((None, 1, tm * TOP_K), lambda i: (jnp.minimum(i + 1, nt - 1), 0, 0),
                         memory_space=pltpu.SMEM),
            pl.BlockSpec((tm * LANE_TILES, 128), row),
            pl.BlockSpec((tm, TOP_K), row),
            pl.BlockSpec((1, d), fixed),
            pl.BlockSpec((1, d), fixed),
            pl.BlockSpec(memory_space=pl.ANY),
        ],
        out_specs=pl.BlockSpec((tm, d), row),
        out_shape=jax.ShapeDtypeStruct((t, d), F32),
        scratch_shapes=[pltpu.VMEM((2, TOP_K, tm * LANE_TILES, 128), F32),
                        pltpu.SemaphoreType.DMA((2,))],
        compiler_params=pltpu.CompilerParams(
            dimension_semantics=("arbitrary",), vmem_limit_bytes=VMEM_LIMIT),
        name="moe_combine_ln",
    )(dest3, dest3, x1r, g4, g2, b2, ys)


def kernel(x, ln0_g, ln0_b, w_in, conv_w, conv_b, w_rg_a, b_rg_a, w_rg_x, b_rg_x, lru_lambda, lam_q1, lam_k1, lam_q2, lam_k2, subln_g, w_out, ln1_g, ln1_b, w_router, b_router, w_gu, b_gu, w_down, b_down, ln2_g, ln2_b):
    bsz, seq, d = x.shape
    t = bsz * seq
    x2 = x.reshape(t, d)
    g0 = ln0_g.reshape(1, d)
    b0 = ln0_b.reshape(1, d)
    l = 0

    w_in_bf = w_in[l].astype(BF16)
    w_out_bf = w_out[l].astype(BF16)
    wa = jax.scipy.linalg.block_diag(*[w_rg_a[l, n] for n in range(LRU_BLOCKS)])
    wx = jax.scipy.linalg.block_diag(*[w_rg_x[l, n] for n in range(LRU_BLOCKS)])
    w_gate_bf = jnp.concatenate([wa, wx], axis=1).astype(BF16)
    b_gate = jnp.concatenate([b_rg_a[l].reshape(1, -1), b_rg_x[l].reshape(1, -1)], axis=1)
    lamvec = jnp.stack([lam_q1[l], lam_k1[l], lam_q2[l], lam_k2[l]]).astype(F32)
    slopes = jnp.asarray([2.0 ** (-8.0 * (i + 1) / ATT_HEADS) for i in range(ATT_HEADS)], F32)
    slopes = jnp.broadcast_to(slopes[:, None, None], (ATT_HEADS, 1, 128))
    bg = b_gu[l, :, None, 0::2]
    bl = b_gu[l, :, None, 1::2]
    bd = b_down[l][:, None, :]

    qt, k, vt, xl, gl = _inproj(x, g0, b0, w_in_bf)
    att = _attention(qt, k, vt, slopes, lamvec, subln_g[l].reshape(-1, 1))
    rec = _lru(xl, gl, conv_w[l], conv_b[l].reshape(1, -1), w_gate_bf, b_gate,
               lru_lambda[l].reshape(1, -1))
    wr_t = w_router[l].T.astype(F32)
    wr_hi = wr_t.astype(BF16)
    wr_lo = (wr_t - wr_hi.astype(F32)).astype(BF16)
    x1r, e4t, g4t, r4t, cnt = _outproj_router(
        x2, g0, b0, att.reshape(t, -1), rec.reshape(t, -1), w_out_bf,
        ln1_g[l].reshape(1, d), ln1_b[l].reshape(1, d),
        jnp.concatenate([wr_hi, wr_lo], axis=0), b_router[l].reshape(-1, 1).astype(F32))
    e4 = e4t[0:TOP_K].T
    g4 = g4t[0:TOP_K].T
    r4 = r4t[0:TOP_K].T

    n_assign = t * TOP_K
    nb = (n_assign + N_EXPERTS * (ROW_BLOCK - 1)) // ROW_BLOCK + 1
    counts = cnt[:, 0].astype(I32)
    padded = (counts + ROW_BLOCK - 1) // ROW_BLOCK * ROW_BLOCK
    pends = jnp.cumsum(padded)
    pstarts = pends - padded
    dest = pstarts[e4] + r4
    row_tok = _invert(dest.reshape(t // TM_INV, 1, TM_INV * TOP_K), nb * ROW_BLOCK)
    ys = _experts((pstarts // ROW_BLOCK).astype(I32), (padded // ROW_BLOCK).astype(I32),
                  row_tok, x1r, w_gu[l], bg, bl, w_down[l], bd)
    out = _combine(dest.reshape(t // TM_COMB, 1, TM_COMB * TOP_K), x1r, g4,
                   ln2_g[l].reshape(1, d), ln2_b[l].reshape(1, d), ys)
    return out.reshape(bsz, seq, d)
```

```python
import math

import jax
import jax.numpy as jnp
from jax import lax
from jax.experimental import pallas as pl
from jax.experimental.pallas import tpu as pltpu

F32 = jnp.float32
BF16 = jnp.bfloat16
I32 = jnp.int32

D_MODEL = 1024
ATT_WIDTH = 512
ATT_HEAD_DIM = 64
ATT_HEADS = 4
VAL_DIM = 2 * ATT_HEAD_DIM
LRU_WIDTH = 512
LRU_BLOCKS = 8
LRU_C = 8.0
CONV_WIDTH = 4
CHUNK_SHIFT = 6
N_EXPERTS = 32
TOP_K = 4
D_EXPERT = 1024
SWIGLU_LIMIT = 7.0
SWIGLU_ALPHA = 1.702
DEPTH = 1
ALPHA = (2.0 * DEPTH) ** 0.25
LN_EPS = 1e-5
SUBLN_EPS = 1e-5
LAM_INIT = 0.8 - 0.6 * math.exp(-0.3 * 0)
NEG_BIG = -1e30
LOG2E = math.log2(math.e)

TM_PROJ = 512
TQ = 1024
TKB = 128
SW = 256
TC_LRU = 512
SCAN_GROUP = 8
ROW_BLOCK = 256
TM_COMB = 128
TM_INV = 2048
XPOSE_CHUNK = 512

VMEM_LIMIT = 48 * 1024 * 1024
VMEM_LIMIT_EXPERTS = 56 * 1024 * 1024


LANE_TILES = D_MODEL // 128


def _store_row_tiles(ref, val):
    n = val.shape[0]
    for j in range(LANE_TILES):
        ref[pl.ds(j, n, stride=LANE_TILES), :] = val[:, j * 128:(j + 1) * 128]


def _load_row_tiles(ref, n):
    return jnp.concatenate(
        [ref[pl.ds(j, n, stride=LANE_TILES), :] for j in range(LANE_TILES)], axis=1)


def _layer_norm(x, g, b):
    mu = jnp.mean(x, axis=-1, keepdims=True)
    xc = x - mu
    var = jnp.mean(xc * xc, axis=-1, keepdims=True)
    return xc * lax.rsqrt(var + LN_EPS) * g + b


def _inproj_kernel(x_ref, g_ref, b_ref, w_ref, qt_ref, k_ref, vt_ref, xl_ref, gl_ref):
    x0 = _layer_norm(x_ref[...], g_ref[...], b_ref[...])
    p = jnp.dot(x0.astype(BF16), w_ref[...], preferred_element_type=F32)
    a = ATT_WIDTH
    qt_ref[...] = (p[:, 0:a] * (ATT_HEAD_DIM ** -0.5 * LOG2E)).T.astype(BF16)
    k_ref[...] = p[:, a:2 * a].astype(BF16)
    vt = p[:, 2 * a:3 * a].T.astype(BF16)
    for h in range(ATT_HEADS):
        for c in range(vt_ref.shape[1]):
            vt_ref[h, c] = vt[h * VAL_DIM:(h + 1) * VAL_DIM, c * TKB:(c + 1) * TKB]
    xl_ref[...] = p[:, 3 * a:3 * a + LRU_WIDTH]
    gl_ref[...] = p[:, 3 * a + LRU_WIDTH:]


def _inproj(x, g, b, w_bf):
    bsz, seq, _ = x.shape
    n = w_bf.shape[1]
    tm = TM_PROJ
    row = lambda bi, i: (bi, i, 0)
    col = lambda bi, i: (bi, 0, i)
    fixed = lambda bi, i: (0, 0)
    return pl.pallas_call(
        _inproj_kernel,
        grid=(bsz, seq // tm),
        in_specs=[
            pl.BlockSpec((None, tm, D_MODEL), row),
            pl.BlockSpec((1, D_MODEL), fixed),
            pl.BlockSpec((1, D_MODEL), fixed),
            pl.BlockSpec((D_MODEL, n), fixed),
        ],
        out_specs=[
            pl.BlockSpec((None, ATT_WIDTH, tm), col),
            pl.BlockSpec((None, tm, ATT_WIDTH), row),
            pl.BlockSpec((None, ATT_HEADS, tm // TKB, VAL_DIM, TKB),
                         lambda bi, i: (bi, 0, i, 0, 0)),
            pl.BlockSpec((None, tm, LRU_WIDTH), row),
            pl.BlockSpec((None, tm, LRU_WIDTH), row),
        ],
        out_shape=[
            jax.ShapeDtypeStruct((bsz, ATT_WIDTH, seq), BF16),
            jax.ShapeDtypeStruct((bsz, seq, ATT_WIDTH), BF16),
            jax.ShapeDtypeStruct((bsz, ATT_HEADS, seq // TKB, VAL_DIM, TKB), BF16),
            jax.ShapeDtypeStruct((bsz, seq, LRU_WIDTH), F32),
            jax.ShapeDtypeStruct((bsz, seq, LRU_WIDTH), F32),
        ],
        compiler_params=pltpu.CompilerParams(
            dimension_semantics=("parallel", "parallel"), vmem_limit_bytes=VMEM_LIMIT),
        name="ln_inproj",
    )(x, g, b, w_bf)


def _attn_kernel(qt_ref, k_ref, vt_ref, slope_ref, lam_ref, g_ref,
                 o_ref, q2_ref, kx_ref, qbias_ref, diag_ref, *stat_refs):
    qi = pl.program_id(2)
    tq = qt_ref.shape[1]
    n_strips = 2 * tq // SW
    blocks_per_q = tq // TKB
    cl = slope_ref[0:1, 0:1] * LOG2E

    @pl.when(qi == 0)
    def _():
        c = lax.broadcasted_iota(I32, (TKB, 128), 0)
        lane = lax.broadcasted_iota(I32, (TKB, 128), 1)
        w = c.astype(F32) * cl
        w_hi = w.astype(BF16).astype(F32)
        w_mid = (w - w_hi).astype(BF16).astype(F32)
        w_lo = w - w_hi - w_mid
        kx_ref[...] = jnp.where(
            lane == 0, w_hi, jnp.where(lane == 1, w_mid, jnp.where(lane == 2, w_lo, 0.0))
        ).astype(BF16)
        r1 = lax.broadcasted_iota(I32, (1, tq), 1)
        qbias_ref[...] = r1.astype(F32) * (-cl)
        ck0 = lax.broadcasted_iota(I32, (TKB, tq), 0)
        r = lax.broadcasted_iota(I32, (TKB, tq), 1)
        for kd in range(blocks_per_q):
            ck = ck0 + kd * TKB
            allowed = (ck >> CHUNK_SHIFT) <= (r >> CHUNK_SHIFT)
            bias = (jnp.abs(r - ck) + ck0).astype(F32) * (-cl)
            diag_ref[kd] = jnp.where(allowed, bias, NEG_BIG)

    m_refs = stat_refs[0:n_strips]
    l_refs = stat_refs[n_strips:2 * n_strips]
    acc_refs = stat_refs[2 * n_strips:3 * n_strips]
    for j in range(n_strips):
        m_refs[j][...] = jnp.full(m_refs[j].shape, NEG_BIG, F32)
        l_refs[j][...] = jnp.zeros(l_refs[j].shape, F32)
        acc_refs[j][...] = jnp.zeros(acc_refs[j].shape, F32)
    qt = qt_ref[...]
    frow = lax.broadcasted_iota(I32, qt.shape, 0)
    zero = jnp.zeros_like(qt)
    q2_ref[0:VAL_DIM, 0:tq] = jnp.where(frow < ATT_HEAD_DIM, qt, zero)
    q2_ref[0:VAL_DIM, tq:2 * tq] = jnp.where(frow >= ATT_HEAD_DIM, qt, zero)
    xrow = lax.broadcasted_iota(I32, (VAL_DIM, 2 * tq), 0)
    q2_ref[VAL_DIM:2 * VAL_DIM, :] = jnp.where(xrow < 3, 1.0, 0.0).astype(BF16)

    def load_keys(kb):
        k_blk = k_ref[pl.ds(pl.multiple_of(kb * TKB, TKB), TKB), :]
        return jnp.concatenate([k_blk, kx_ref[...]], axis=1), vt_ref[kb]

    def strip(j, k_ext, vt_blk, sc_bias, shift):
        ls = slice(j * SW, (j + 1) * SW)
        sc = jnp.dot(k_ext, q2_ref[:, ls], preferred_element_type=F32)
        if sc_bias is not None:
            sc = sc + sc_bias
        m_prev = m_refs[j][...]
        m_new = jnp.maximum(m_prev, jnp.max(sc, axis=0, keepdims=True) + shift)
        alpha = jnp.exp2(m_prev - m_new)
        p = jnp.exp2(sc - (m_new - shift))
        l_refs[j][...] = alpha * l_refs[j][...] + jnp.sum(p, axis=0, keepdims=True)
        pv = jnp.dot(vt_blk, p.astype(BF16), preferred_element_type=F32)
        acc_refs[j][...] = alpha * acc_refs[j][...] + pv
        m_refs[j][...] = m_new

    def off_diag_group(g, carry):
        for kd in range(blocks_per_q):
            kb = g * blocks_per_q + kd
            k_ext, vt_blk = load_keys(kb)
            block_shift = jnp.full((1, 1), qi * tq - kb * TKB, I32).astype(F32) * (-cl)
            for j in range(n_strips):
                jq = j % (tq // SW)
                strip(j, k_ext, vt_blk, None,
                      qbias_ref[:, jq * SW:(jq + 1) * SW] + block_shift)
        return carry
    lax.fori_loop(0, qi, off_diag_group, 0)

    for kd in range(blocks_per_q):
        k_ext, vt_blk = load_keys(qi * blocks_per_q + kd)
        for j in range(n_strips):
            jq = j % (tq // SW)
            if (jq + 1) * SW > kd * TKB:
                strip(j, k_ext, vt_blk, diag_ref[kd, :, jq * SW:(jq + 1) * SW], 0.0)

    lv = lam_ref[...]
    lam = (jnp.exp(jnp.sum(lv[0:1] * lv[1:2], axis=-1, keepdims=True))
           - jnp.exp(jnp.sum(lv[2:3] * lv[3:4], axis=-1, keepdims=True)) + LAM_INIT)
    o2 = [acc_refs[j][...] / l_refs[j][...] for j in range(n_strips)]
    half = n_strips // 2
    o = jnp.concatenate([o2[j] - lam * o2[half + j] for j in range(half)], axis=1)
    o = o * lax.rsqrt(jnp.mean(o * o, axis=0, keepdims=True) + SUBLN_EPS)
    o = o * (g_ref[...] * (1.0 - LAM_INIT))
    o_ref[...] = o.T.astype(o_ref.dtype)


def _attention(qt, k, vt, slopes, lamvec, subln_g_col):
    b, s, _ = k.shape
    nkb = s // TKB
    n_strips = 2 * TQ // SW
    return pl.pallas_call(
        _attn_kernel,
        grid=(b, ATT_HEADS, s // TQ),
        in_specs=[
            pl.BlockSpec((None, VAL_DIM, TQ), lambda bi, h, qi: (bi, h, qi)),
            pl.BlockSpec((None, s, VAL_DIM), lambda bi, h, qi: (bi, 0, h)),
            pl.BlockSpec((None, None, nkb, VAL_DIM, TKB), lambda bi, h, qi: (bi, h, 0, 0, 0)),
            pl.BlockSpec((None, 1, 128), lambda bi, h, qi: (h, 0, 0)),
            pl.BlockSpec((4, ATT_HEAD_DIM), lambda bi, h, qi: (0, 0)),
            pl.BlockSpec((VAL_DIM, 1), lambda bi, h, qi: (0, 0)),
        ],
        out_specs=pl.BlockSpec((None, TQ, VAL_DIM), lambda bi, h, qi: (bi, qi, h)),
        out_shape=jax.ShapeDtypeStruct((b, s, ATT_WIDTH), BF16),
        scratch_shapes=[
            pltpu.VMEM((2 * VAL_DIM, 2 * TQ), BF16),
            pltpu.VMEM((TKB, 128), BF16),
            pltpu.VMEM((1, TQ), F32),
            pltpu.VMEM((TQ // TKB, TKB, TQ), F32),
        ] +[pltpu.VMEM((1, SW), F32)] * (2 * n_strips) + [pltpu.VMEM((VAL_DIM, SW), F32)] * n_strips,
        compiler_params=pltpu.CompilerParams(
            dimension_semantics=("parallel", "parallel", "arbitrary"),
            vmem_limit_bytes=VMEM_LIMIT),
        name="diff_attention",
    )(qt, k, vt, slopes, lamvec, subln_g_col)


def _lru_kernel(xl_ref, gl_ref, cw_ref, cb_ref, wg_ref, bg_ref, lam_ref,
                o_ref, ext_ref, h_ref):
    c = pl.program_id(1)
    tc = xl_ref.shape[0]
    w = LRU_WIDTH

    @pl.when(c == 0)
    def _():
        ext_ref[0:8, :] = jnp.zeros((8, w), F32)
        h_ref[...] = jnp.zeros(h_ref.shape, F32)

    ext_ref[8:8 + tc, :] = xl_ref[...]
    cw = cw_ref[...]
    xc = cb_ref[...] + cw[0:1] * ext_ref[5:5 + tc, :]
    for j in range(1, CONV_WIDTH):
        xc = xc + cw[j:j + 1] * ext_ref[5 + j:5 + j + tc, :]
    tail = ext_ref[tc:tc + 8, :]

    gates = jnp.dot(xc.astype(BF16), wg_ref[...], preferred_element_type=F32) + bg_ref[...]
    r = jax.nn.sigmoid(gates[:, 0:w])
    ig = jax.nn.sigmoid(gates[:, w:2 * w])
    z = -lam_ref[...]
    softplus = jnp.maximum(z, 0.0) + jnp.log1p(jnp.exp(-jnp.abs(z)))
    log_a = (-LRU_C) * r * softplus
    a = jnp.exp(log_a)
    u = jnp.sqrt(-jnp.tanh(log_a) * (a * a + 1.0)) * (ig * xc)

    ng = tc // SCAN_GROUP
    a = a.reshape(ng, SCAN_GROUP, w)
    u = u.reshape(ng, SCAN_GROUP, w)
    pos = lax.broadcasted_iota(I32, a.shape, 1)
    d = 1
    while d < SCAN_GROUP:
        valid = pos >= d
        u = jnp.where(valid, a * pltpu.roll(u, d, axis=1) + u, u)
        a = jnp.where(valid, a * pltpu.roll(a, d, axis=1), a)
        d *= 2
    a = a.reshape(tc, w)
    u = u.reshape(tc, w)
    h_in = h_ref[0:1, :]
    groups = []
    for g in range(tc // SCAN_GROUP):
        rows = slice(g * SCAN_GROUP, (g + 1) * SCAN_GROUP)
        hg = u[rows] + a[rows] * h_in
        groups.append(hg)
        h_in = hg[SCAN_GROUP - 1:SCAN_GROUP, :]
    h = jnp.concatenate(groups, axis=0)

    gl = gl_ref[...]
    gelu = 0.5 * gl * (1.0 + jnp.tanh(math.sqrt(2.0 / math.pi) * (gl + 0.044715 * (gl * gl * gl))))
    o_ref[...] = (h * gelu).astype(o_ref.dtype)

    h_ref[0:1, :] = h[tc - 1:tc, :]
    ext_ref[0:8, :] = tail


def _lru(xl, gl, conv_w, conv_b, w_gate_bf, b_gate, lam):
    b, s, w = xl.shape
    tc = TC_LRU
    blk = lambda bi, c: (bi, c, 0)
    fixed = lambda bi, c: (0, 0)
    return pl.pallas_call(
        _lru_kernel,
        grid=(b, s // tc),
        in_specs=[
            pl.BlockSpec((None, tc, w), blk),
            pl.BlockSpec((None, tc, w), blk),
            pl.BlockSpec((CONV_WIDTH, w), fixed),
            pl.BlockSpec((1, w), fixed),
            pl.BlockSpec((w, 2 * w), fixed),
            pl.BlockSpec((1, 2 * w), fixed),
            pl.BlockSpec((1, w), fixed),
        ],
        out_specs=pl.BlockSpec((None, tc, w), blk),
        out_shape=jax.ShapeDtypeStruct((b, s, w), BF16),
        scratch_shapes=[pltpu.VMEM((tc + 8, w), F32), pltpu.VMEM((8, w), F32)],
        compiler_params=pltpu.CompilerParams(
            dimension_semantics=("parallel", "arbitrary"), vmem_limit_bytes=VMEM_LIMIT),
        name="rg_lru",
    )(xl, gl, conv_w, conv_b, w_gate_bf, b_gate, lam)


def _outproj_kernel(x_ref, g0_ref, b0_ref, att_ref, rec_ref, wo_ref, g1_ref, b1_ref,
                    wr_ref, br_ref, x1_ref, e4_ref, g4_ref, r4_ref, cnt_ref, carry_ref):
    i = pl.program_id(0)
    tm = x_ref.shape[0]

    @pl.when(i == 0)
    def _():
        carry_ref[...] = jnp.zeros(carry_ref.shape, F32)

    x0 = _layer_norm(x_ref[...], g0_ref[...], b0_ref[...])
    mixed = (jnp.dot(att_ref[...], wo_ref[0:ATT_WIDTH, :], preferred_element_type=F32)
             + jnp.dot(rec_ref[...], wo_ref[ATT_WIDTH:, :], preferred_element_type=F32))
    x1 = _layer_norm(ALPHA * x0 + mixed, g1_ref[...], b1_ref[...])
    _store_row_tiles(x1_ref, x1)

    x1_hi = x1.astype(BF16)
    x1_lo = (x1 - x1_hi.astype(F32)).astype(BF16)
    nt = (((1,), (1,)), ((), ()))
    ne = N_EXPERTS
    hi_terms = lax.dot_general(wr_ref[...], x1_hi, nt, preferred_element_type=F32)
    lo_term = lax.dot_general(wr_ref[0:ne, :], x1_lo, nt, preferred_element_type=F32)
    logits = hi_terms[0:ne] + hi_terms[ne:2 * ne] + lo_term + br_ref[...]

    erow = lax.broadcasted_iota(I32, logits.shape, 0)
    work = logits
    vals, idxs, hots = [], [], []
    for _ in range(TOP_K):
        mx = jnp.max(work, axis=0, keepdims=True)
        idx = jnp.min(jnp.where(work == mx, erow, ne), axis=0, keepdims=True)
        hot = erow == idx
        vals.append(mx)
        idxs.append(idx)
        hots.append(hot)
        work = jnp.where(hot, -jnp.inf, work)
    exps = [jnp.exp(v - vals[0]) for v in vals]
    denom = exps[0] + exps[1] + exps[2] + exps[3]

    mask = jnp.zeros(logits.shape, F32)
    for hot in hots:
        mask = mask + hot.astype(F32)
    tr = lax.broadcasted_iota(I32, (tm, tm), 0)
    tc = lax.broadcasted_iota(I32, (tm, tm), 1)
    earlier = jnp.where(tr < tc, 1.0, 0.0).astype(BF16)
    carry = carry_ref[:, 0:1]
    excl = jnp.dot(mask.astype(BF16), earlier, preferred_element_type=F32) + carry
    ranks = [jnp.sum(jnp.where(hot, excl, 0.0), axis=0, keepdims=True) for hot in hots]
    pad_i = jnp.zeros((8 - TOP_K, tm), I32)
    pad_f = jnp.zeros((8 - TOP_K, tm), F32)
    e4_ref[...] = jnp.concatenate(idxs + [pad_i], axis=0)
    g4_ref[...] = jnp.concatenate([ex / denom for ex in exps] + [pad_f], axis=0)
    r4_ref[...] = jnp.concatenate([r.astype(I32) for r in ranks] + [pad_i], axis=0)
    total = carry + jnp.sum(mask, axis=1, keepdims=True)
    carry_ref[...] = jnp.broadcast_to(total, carry_ref.shape)
    cnt_ref[...] = jnp.broadcast_to(total, cnt_ref.shape)


def _outproj_router(x2, g0, b0, att, rec, wo_bf, g1, b1, w_router, b_router):
    t = x2.shape[0]
    tm = TM_PROJ
    row = lambda i: (i, 0)
    col = lambda i: (0, i)
    fixed = lambda i: (0, 0)
    return pl.pallas_call(
        _outproj_kernel,
        grid=(t // tm,),
        in_specs=[
            pl.BlockSpec((tm, D_MODEL), row),
            pl.BlockSpec((1, D_MODEL), fixed),
            pl.BlockSpec((1, D_MODEL), fixed),
            pl.BlockSpec((tm, ATT_WIDTH), row),
            pl.BlockSpec((tm, LRU_WIDTH), row),
            pl.BlockSpec((D_MODEL, D_MODEL), fixed),
            pl.BlockSpec((1, D_MODEL), fixed),
            pl.BlockSpec((1, D_MODEL), fixed),
            pl.BlockSpec((2 * N_EXPERTS, D_MODEL), fixed),
            pl.BlockSpec((N_EXPERTS, 1), fixed),
        ],
        out_specs=[
            pl.BlockSpec((tm * LANE_TILES, 128), row),
            pl.BlockSpec((8, tm), col),
            pl.BlockSpec((8, tm), col),
            pl.BlockSpec((8, tm), col),
            pl.BlockSpec((N_EXPERTS, 128), fixed),
        ],
        out_shape=[
            jax.ShapeDtypeStruct((t * LANE_TILES, 128), F32),
            jax.ShapeDtypeStruct((8, t), I32),
            jax.ShapeDtypeStruct((8, t), F32),
            jax.ShapeDtypeStruct((8, t), I32),
            jax.ShapeDtypeStruct((N_EXPERTS, 128), F32),
        ],
        scratch_shapes=[pltpu.VMEM((N_EXPERTS, 128), F32)],
        compiler_params=pltpu.CompilerParams(
            dimension_semantics=("arbitrary",), vmem_limit_bytes=VMEM_LIMIT),
        name="outproj_ln_router",
    )(x2, g0, b0, att, rec, wo_bf, g1, b1, w_router, b_router)


def _tile_rows(row):
    return pl.ds(pl.multiple_of(row * LANE_TILES, LANE_TILES), LANE_TILES)


def _invert_kernel(dest_ref, rt_ref, zero_ref, sem):
    i = pl.program_id(0)
    tm = dest_ref.shape[1]

    @pl.when(i == 0)
    def _():
        zero_ref[...] = jnp.zeros(zero_ref.shape, I32)
        clear = pltpu.make_async_copy(zero_ref, rt_ref, sem)
        clear.start()
        clear.wait()

    def body(t, carry):
        for kk in range(TOP_K):
            rt_ref[dest_ref[kk, t]] = i * tm + t
        return carry
    lax.fori_loop(0, tm, body, 0, unroll=4)


def _invert(dest_t, n_rows):
    rows, t = dest_t.shape
    return pl.pallas_call(
        _invert_kernel,
        grid=(t // TM_INV,),
        in_specs=[pl.BlockSpec((rows, TM_INV), lambda i: (0, i), memory_space=pltpu.SMEM)],
        out_specs=pl.BlockSpec(memory_space=pltpu.SMEM),
        out_shape=jax.ShapeDtypeStruct((n_rows,), I32),
        scratch_shapes=[pltpu.VMEM((n_rows,), I32), pltpu.SemaphoreType.DMA(())],
        compiler_params=pltpu.CompilerParams(dimension_semantics=("arbitrary",)),
        name="moe_invert",
    )(dest_t)


def _expert_kernel(first_ref, nblk_ref, rt_ref, wgu_ref, bg_ref, bl_ref, wd_ref, bd_ref,
                   x_hbm, ys_hbm, wt_ref, wg_ref, wl_ref, wdb_ref, xb_ref, xbuf, ybuf, xsem, ysem):
    e = pl.program_id(0)
    n_exp = pl.num_programs(0)
    d, f2 = wgu_ref.shape
    f = f2 // 2
    ch = XPOSE_CHUNK
    first = first_ref[e]
    nblk = nblk_ref[e]
    n_live = first_ref[n_exp - 1] + nblk_ref[n_exp - 1]
    n_blocks = ys_hbm.shape[0] // (ROW_BLOCK * LANE_TILES)
    block_rows = ROW_BLOCK * LANE_TILES

    def gather_start(g, slot):
        for r in range(ROW_BLOCK):
            pltpu.make_async_copy(
                x_hbm.at[_tile_rows(rt_ref[g * ROW_BLOCK + r]), :],
                xbuf.at[slot, pl.ds(r * LANE_TILES, LANE_TILES), :],
                xsem.at[slot]).start()

    def gather_wait(slot):
        pltpu.make_async_copy(x_hbm.at[pl.ds(0, block_rows), :], xbuf.at[slot],
                              xsem.at[slot]).wait()

    def y_copy(g, slot):
        start = pl.multiple_of(g * block_rows, block_rows)
        return pltpu.make_async_copy(ybuf.at[slot], ys_hbm.at[pl.ds(start, block_rows), :],
                                     ysem.at[slot])

    @pl.when(jnp.logical_and(nblk > 0, first == 0))
    def _():
        gather_start(0, 0)

    @pl.when(nblk > 0)
    def _():
        for j in range(d // 128):
            rows = slice(j * 128, (j + 1) * 128)
            for c in range(f2 // ch):
                wt_ref[j, c * ch:(c + 1) * ch, :] = wgu_ref[rows, c * ch:(c + 1) * ch].T
            for c in range(f // ch):
                gate_rows = wt_ref[j, pl.ds(2 * c * ch, ch, stride=2), :]
                lin_rows = wt_ref[j, pl.ds(2 * c * ch + 1, ch, stride=2), :]
                wg_ref[rows, c * ch:(c + 1) * ch] = gate_rows.T.astype(BF16)
                wl_ref[rows, c * ch:(c + 1) * ch] = lin_rows.T.astype(BF16)
        wdb_ref[...] = wd_ref[...].astype(BF16)

    def block(b, carry):
        g = first + b
        for slot in range(2):
            @pl.when(g % 2 == slot)
            def _():
                block_body(g, slot)
        return carry

    def block_body(g, slot):
        @pl.when(g >= 2)
        def _():
            y_copy(g - 2, slot).wait()

        gather_wait(slot)
        xb_ref[...] = _load_row_tiles(xbuf.at[slot], ROW_BLOCK).astype(BF16)
        gather_start(jnp.minimum(g + 1, n_live - 1), 1 - slot)
        x = xb_ref[...]
        hg = jnp.dot(x, wg_ref[...], preferred_element_type=F32) + bg_ref[...]
        hl = jnp.dot(x, wl_ref[...], preferred_element_type=F32) + bl_ref[...]
        xg = jnp.minimum(hg, SWIGLU_LIMIT)
        xl = jnp.clip(hl, -SWIGLU_LIMIT, SWIGLU_LIMIT)
        act = xg * jax.nn.sigmoid(SWIGLU_ALPHA * xg) * (xl + 1.0)
        y = jnp.dot(act.astype(BF16), wdb_ref[...], preferred_element_type=F32) + bd_ref[...]
        _store_row_tiles(ybuf.at[slot], y)
        y_copy(g, slot).start()

    lax.fori_loop(0, nblk, block, 0)

    @pl.when(e == n_exp - 1)
    def _():
        gather_wait(n_live % 2)
        y_copy(0, 0).wait()
        y_copy(0, 1).wait()
        ybuf[0] = jnp.zeros(ybuf.shape[1:], F32)
        for t in range(N_EXPERTS):
            @pl.when(n_live + t < n_blocks)
            def _():
                y_copy(n_live + t, 0).start()
        for t in range(N_EXPERTS):
            @pl.when(n_live + t < n_blocks)
            def _():
                y_copy(n_live + t, 0).wait()


def _experts(first_blk, n_blk, row_tok, x1r, w_gu, bg, bl, w_down, bd):
    d = D_MODEL
    f2 = w_gu.shape[2]
    f = f2 // 2
    n_rows = row_tok.shape[0]
    wmap = lambda e, fb, nb_: (e, 0, 0)
    grid_spec = pltpu.PrefetchScalarGridSpec(
        num_scalar_prefetch=2,
        grid=(N_EXPERTS,),
        in_specs=[
            pl.BlockSpec(memory_space=pltpu.SMEM),
            pl.BlockSpec((None, d, f2), wmap),
            pl.BlockSpec((None, 1, f), wmap),
            pl.BlockSpec((None, 1, f), wmap),
            pl.BlockSpec((None, f, d), wmap),
            pl.BlockSpec((None, 1, d), wmap),
            pl.BlockSpec(memory_space=pl.ANY),
        ],
        out_specs=pl.BlockSpec(memory_space=pl.ANY),
        scratch_shapes=[
            pltpu.VMEM((d // 128, f2, 128), F32),
            pltpu.VMEM((d, f), BF16),
            pltpu.VMEM((d, f), BF16),
            pltpu.VMEM((f, d), BF16),
            pltpu.VMEM((ROW_BLOCK, d), BF16),
            pltpu.VMEM((2, ROW_BLOCK * LANE_TILES, 128), F32),
            pltpu.VMEM((2, ROW_BLOCK * LANE_TILES, 128), F32),
            pltpu.SemaphoreType.DMA((2,)),
            pltpu.SemaphoreType.DMA((2,)),
        ],
    )
    return pl.pallas_call(
        _expert_kernel,
        grid_spec=grid_spec,
        out_shape=jax.ShapeDtypeStruct((n_rows * LANE_TILES, 128), F32),
        compiler_params=pltpu.CompilerParams(
            dimension_semantics=("arbitrary",), vmem_limit_bytes=VMEM_LIMIT_EXPERTS),
        name="moe_experts",
    )(first_blk, n_blk, row_tok, w_gu, bg, bl, w_down, bd, x1r)


def _combine_start(ys_hbm, idx_ref, buf_ref, sem, tm):
    def body(t, carry):
        for kk in range(TOP_K):
            pltpu.make_async_copy(ys_hbm.at[_tile_rows(idx_ref[kk, t]), :],
                                  buf_ref.at[kk, _tile_rows(t), :], sem).start()
        return carry
    lax.fori_loop(0, tm, body, 0, unroll=2)


def _combine_wait(ys_hbm, buf_ref, sem, tm):
    for kk in range(TOP_K):
        pltpu.make_async_copy(ys_hbm.at[pl.ds(0, tm * LANE_TILES), :], buf_ref.at[kk], sem).wait()


def _combine_kernel(idx_cur, idx_nxt, x1_ref, g4_ref, g2_ref, b2_ref, ys_hbm,
                    o_ref, buf_ref, sem):
    i = pl.program_id(0)
    n = pl.num_programs(0)
    slot = i % 2
    tm = o_ref.shape[0]

    @pl.when(i == 0)
    def _():
        _combine_start(ys_hbm, idx_cur, buf_ref.at[0], sem.at[0], tm)

    @pl.when(i + 1 < n)
    def _():
        _combine_start(ys_hbm, idx_nxt, buf_ref.at[1 - slot], sem.at[1 - slot], tm)

    _combine_wait(ys_hbm, buf_ref.at[slot], sem.at[slot], tm)
    g4 = g4_ref[...].T
    ffn = g4[:, 0:1] * _load_row_tiles(buf_ref.at[slot, 0], tm)
    for kk in range(1, TOP_K):
        ffn = ffn + g4[:, kk:kk + 1] * _load_row_tiles(buf_ref.at[slot, kk], tm)
    x1 = _load_row_tiles(x1_ref, tm)
    o_ref[...] = _layer_norm(ALPHA * x1 + ffn, g2_ref[...], b2_ref[...])


def _combine(dest_t, x1r, g4_t, g2, b2, ys):
    d = D_MODEL
    t = x1r.shape[0] // LANE_TILES
    tm = TM_COMB
    nt = t // tm
    rows = dest_t.shape[0]
    row = lambda i: (i, 0)
    col = lambda i: (0, i)
    fixed = lambda i: (0, 0)
    return pl.pallas_call(
        _combine_kernel,
        grid=(nt,),
        in_specs=[
            pl.BlockSpec((rows, tm), col, memory_space=pltpu.SMEM),
            pl.BlockSpec((rows, tm), lambda i: (0, jnp.minimum(i + 1, nt - 1)),
                         memory_space=pltpu.SMEM),
            pl.BlockSpec((tm * LANE_TILES, 128), row),
            pl.BlockSpec((rows, tm), col),
            pl.BlockSpec((1, d), fixed),
            pl.BlockSpec((1, d), fixed),
            pl.BlockSpec(memory_space=pl.ANY),
        ],
        out_specs=pl.BlockSpec((tm, d), row),
        out_shape=jax.ShapeDtypeStruct((t, d), F32),
        scratch_shapes=[pltpu.VMEM((2, TOP_K, tm * LANE_TILES, 128), F32),
                        pltpu.SemaphoreType.DMA((2,))],
        compiler_params=pltpu.CompilerParams(
            dimension_semantics=("arbitrary",), vmem_limit_bytes=VMEM_LIMIT),
        name="moe_combine_ln",
    )(dest_t, dest_t, x1r, g4_t, g2, b2, ys)


def kernel(x, ln0_g, ln0_b, w_in, conv_w, conv_b, w_rg_a, b_rg_a, w_rg_x, b_rg_x, lru_lambda, lam_q1, lam_k1, lam_q2, lam_k2, subln_g, w_out, ln1_g, ln1_b, w_router, b_router, w_gu, b_gu, w_down, b_down, ln2_g, ln2_b):
    bsz, seq, d = x.shape
    t = bsz * seq
    x2 = x.reshape(t, d)
    g0 = ln0_g.reshape(1, d)
    b0 = ln0_b.reshape(1, d)
    l = 0

    w_in_bf = w_in[l].astype(BF16)
    w_out_bf = w_out[l].astype(BF16)
    wa = jax.scipy.linalg.block_diag(*[w_rg_a[l, n] for n in range(LRU_BLOCKS)])
    wx = jax.scipy.linalg.block_diag(*[w_rg_x[l, n] for n in range(LRU_BLOCKS)])
    w_gate_bf = jnp.concatenate([wa, wx], axis=1).astype(BF16)
    b_gate = jnp.concatenate([b_rg_a[l].reshape(1, -1), b_rg_x[l].reshape(1, -1)], axis=1)
    lamvec = jnp.stack([lam_q1[l], lam_k1[l], lam_q2[l], lam_k2[l]]).astype(F32)
    slopes = jnp.asarray([2.0 ** (-8.0 * (i + 1) / ATT_HEADS) for i in range(ATT_HEADS)], F32)
    slopes = jnp.broadcast_to(slopes[:, None, None], (ATT_HEADS, 1, 128))
    bg = b_gu[l, :, None, 0::2]
    bl = b_gu[l, :, None, 1::2]
    bd = b_down[l][:, None, :]

    qt, k, vt, xl, gl = _inproj(x, g0, b0, w_in_bf)
    att = _attention(qt, k, vt, slopes, lamvec, subln_g[l].reshape(-1, 1))
    rec = _lru(xl, gl, conv_w[l], conv_b[l].reshape(1, -1), w_gate_bf, b_gate,
               lru_lambda[l].reshape(1, -1))
    wr_t = w_router[l].T.astype(F32)
    wr_hi = wr_t.astype(BF16)
    wr_lo = (wr_t - wr_hi.astype(F32)).astype(BF16)
    x1r, e4t, g4t, r4t, cnt = _outproj_router(
        x2, g0, b0, att.reshape(t, -1), rec.reshape(t, -1), w_out_bf,
        ln1_g[l].reshape(1, d), ln1_b[l].reshape(1, d),
        jnp.concatenate([wr_hi, wr_lo], axis=0), b_router[l].reshape(-1, 1).astype(F32))

    n_assign = t * TOP_K
    nb = (n_assign + N_EXPERTS * (ROW_BLOCK - 1)) // ROW_BLOCK + 1
    counts = cnt[:, 0].astype(I32)
    padded = (counts + ROW_BLOCK - 1) // ROW_BLOCK * ROW_BLOCK
    pends = jnp.cumsum(padded)
    pstarts = pends - padded
    dest_t = r4t
    for ei in range(N_EXPERTS):
        dest_t = dest_t + jnp.where(e4t == ei, pstarts[ei], 0)
    row_tok = _invert(dest_t, nb * ROW_BLOCK)
    ys = _experts((pstarts // ROW_BLOCK).astype(I32), (padded // ROW_BLOCK).astype(I32),
                  row_tok, x1r, w_gu[l], bg, bl, w_down[l], bd)
    out = _combine(dest_t, x1r, g4t, ln2_g[l].reshape(1, d), ln2_b[l].reshape(1, d), ys)
    return out.reshape(bsz, seq, d)
```

```python
import math

import jax
import jax.numpy as jnp
from jax import lax
from jax.experimental import pallas as pl
from jax.experimental.pallas import tpu as pltpu

F32 = jnp.float32
BF16 = jnp.bfloat16
I32 = jnp.int32

D_MODEL = 1024
ATT_WIDTH = 512
ATT_HEAD_DIM = 64
ATT_HEADS = 4
VAL_DIM = 2 * ATT_HEAD_DIM
LRU_WIDTH = 512
LRU_BLOCKS = 8
LRU_C = 8.0
CONV_WIDTH = 4
CHUNK_SHIFT = 6
N_EXPERTS = 32
TOP_K = 4
D_EXPERT = 1024
SWIGLU_LIMIT = 7.0
SWIGLU_ALPHA = 1.702
DEPTH = 1
ALPHA = (2.0 * DEPTH) ** 0.25
LN_EPS = 1e-5
SUBLN_EPS = 1e-5
LAM_INIT = 0.8 - 0.6 * math.exp(-0.3 * 0)
NEG_BIG = -1e30
LOG2E = math.log2(math.e)

TM_PROJ = 512
TQ = 1024
TKB = 128
SW = 256
TC_LRU = 512
SCAN_GROUP = 8
ROW_BLOCK = 256
TM_COMB = 128
TM_INV = 2048
XPOSE_CHUNK = 512

VMEM_LIMIT = 48 * 1024 * 1024
VMEM_LIMIT_EXPERTS = 56 * 1024 * 1024


LANE_TILES = D_MODEL // 128


def _store_row_tiles(ref, val):
    n = val.shape[0]
    for j in range(LANE_TILES):
        ref[pl.ds(j, n, stride=LANE_TILES), :] = val[:, j * 128:(j + 1) * 128]


def _load_row_tiles(ref, n):
    return jnp.concatenate(
        [ref[pl.ds(j, n, stride=LANE_TILES), :] for j in range(LANE_TILES)], axis=1)


def _layer_norm(x, g, b):
    mu = jnp.mean(x, axis=-1, keepdims=True)
    xc = x - mu
    var = jnp.mean(xc * xc, axis=-1, keepdims=True)
    return xc * lax.rsqrt(var + LN_EPS) * g + b


def _inproj_kernel(x_ref, g_ref, b_ref, w_ref, qt_ref, k_ref, vt_ref, xl_ref, gl_ref):
    x0 = _layer_norm(x_ref[...], g_ref[...], b_ref[...])
    p = jnp.dot(x0.astype(BF16), w_ref[...], preferred_element_type=F32)
    a = ATT_WIDTH
    qt_ref[...] = (p[:, 0:a] * (ATT_HEAD_DIM ** -0.5 * LOG2E)).T.astype(BF16)
    k_ref[...] = p[:, a:2 * a].astype(BF16)
    vt = p[:, 2 * a:3 * a].T.astype(BF16)
    for h in range(ATT_HEADS):
        for c in range(vt_ref.shape[1]):
            vt_ref[h, c] = vt[h * VAL_DIM:(h + 1) * VAL_DIM, c * TKB:(c + 1) * TKB]
    xl_ref[...] = p[:, 3 * a:3 * a + LRU_WIDTH]
    gl_ref[...] = p[:, 3 * a + LRU_WIDTH:]


def _inproj(x, g, b, w_bf):
    bsz, seq, _ = x.shape
    n = w_bf.shape[1]
    tm = TM_PROJ
    row = lambda bi, i: (bi, i, 0)
    col = lambda bi, i: (bi, 0, i)
    fixed = lambda bi, i: (0, 0)
    return pl.pallas_call(
        _inproj_kernel,
        grid=(bsz, seq // tm),
        in_specs=[
            pl.BlockSpec((None, tm, D_MODEL), row),
            pl.BlockSpec((1, D_MODEL), fixed),
            pl.BlockSpec((1, D_MODEL), fixed),
            pl.BlockSpec((D_MODEL, n), fixed),
        ],
        out_specs=[
            pl.BlockSpec((None, ATT_WIDTH, tm), col),
            pl.BlockSpec((None, tm, ATT_WIDTH), row),
            pl.BlockSpec((None, ATT_HEADS, tm // TKB, VAL_DIM, TKB),
                         lambda bi, i: (bi, 0, i, 0, 0)),
            pl.BlockSpec((None, tm, LRU_WIDTH), row),
            pl.BlockSpec((None, tm, LRU_WIDTH), row),
        ],
        out_shape=[
            jax.ShapeDtypeStruct((bsz, ATT_WIDTH, seq), BF16),
            jax.ShapeDtypeStruct((bsz, seq, ATT_WIDTH), BF16),
            jax.ShapeDtypeStruct((bsz, ATT_HEADS, seq // TKB, VAL_DIM, TKB), BF16),
            jax.ShapeDtypeStruct((bsz, seq, LRU_WIDTH), F32),
            jax.ShapeDtypeStruct((bsz, seq, LRU_WIDTH), F32),
        ],
        compiler_params=pltpu.CompilerParams(
            dimension_semantics=("parallel", "parallel"), vmem_limit_bytes=VMEM_LIMIT),
        name="ln_inproj",
    )(x, g, b, w_bf)


def _attn_kernel(qt_ref, k_ref, vt_ref, slope_ref, lam_ref, g_ref,
                 o_ref, q2_ref, kx_ref, qbias_ref, diag_ref, *stat_refs):
    qi = pl.program_id(2)
    tq = qt_ref.shape[1]
    n_strips = 2 * tq // SW
    blocks_per_q = tq // TKB
    cl = slope_ref[0:1, 0:1] * LOG2E

    @pl.when(qi == 0)
    def _():
        c = lax.broadcasted_iota(I32, (TKB, 128), 0)
        lane = lax.broadcasted_iota(I32, (TKB, 128), 1)
        w = c.astype(F32) * cl
        w_hi = w.astype(BF16).astype(F32)
        w_mid = (w - w_hi).astype(BF16).astype(F32)
        w_lo = w - w_hi - w_mid
        kx_ref[...] = jnp.where(
            lane == 0, w_hi, jnp.where(lane == 1, w_mid, jnp.where(lane == 2, w_lo, 0.0))
        ).astype(BF16)
        r1 = lax.broadcasted_iota(I32, (1, tq), 1)
        qbias_ref[...] = r1.astype(F32) * (-cl)
        ck0 = lax.broadcasted_iota(I32, (TKB, tq), 0)
        r = lax.broadcasted_iota(I32, (TKB, tq), 1)
        for kd in range(blocks_per_q):
            ck = ck0 + kd * TKB
            allowed = (ck >> CHUNK_SHIFT) <= (r >> CHUNK_SHIFT)
            bias = (jnp.abs(r - ck) + ck0).astype(F32) * (-cl)
            diag_ref[kd] = jnp.where(allowed, bias, NEG_BIG)

    m_refs = stat_refs[0:n_strips]
    l_refs = stat_refs[n_strips:2 * n_strips]
    acc_refs = stat_refs[2 * n_strips:3 * n_strips]
    for j in range(n_strips):
        m_refs[j][...] = jnp.full(m_refs[j].shape, NEG_BIG, F32)
        l_refs[j][...] = jnp.zeros(l_refs[j].shape, F32)
        acc_refs[j][...] = jnp.zeros(acc_refs[j].shape, F32)
    qt = qt_ref[...]
    frow = lax.broadcasted_iota(I32, qt.shape, 0)
    zero = jnp.zeros_like(qt)
    q2_ref[0:VAL_DIM, 0:tq] = jnp.where(frow < ATT_HEAD_DIM, qt, zero)
    q2_ref[0:VAL_DIM, tq:2 * tq] = jnp.where(frow >= ATT_HEAD_DIM, qt, zero)
    xrow = lax.broadcasted_iota(I32, (VAL_DIM, 2 * tq), 0)
    q2_ref[VAL_DIM:2 * VAL_DIM, :] = jnp.where(xrow < 3, 1.0, 0.0).astype(BF16)

    def load_keys(kb):
        k_blk = k_ref[pl.ds(pl.multiple_of(kb * TKB, TKB), TKB), :]
        return jnp.concatenate([k_blk, kx_ref[...]], axis=1), vt_ref[kb]

    def strip(j, k_ext, vt_blk, sc_bias, shift):
        ls = slice(j * SW, (j + 1) * SW)
        sc = jnp.dot(k_ext, q2_ref[:, ls], preferred_element_type=F32)
        if sc_bias is not None:
            sc = sc + sc_bias
        m_prev = m_refs[j][...]
        m_new = jnp.maximum(m_prev, jnp.max(sc, axis=0, keepdims=True) + shift)
        alpha = jnp.exp2(m_prev - m_new)
        p = jnp.exp2(sc - (m_new - shift))
        l_refs[j][...] = alpha * l_refs[j][...] + jnp.sum(p, axis=0, keepdims=True)
        pv = jnp.dot(vt_blk, p.astype(BF16), preferred_element_type=F32)
        acc_refs[j][...] = alpha * acc_refs[j][...] + pv
        m_refs[j][...] = m_new

    def off_diag_group(g, carry):
        for kd in range(blocks_per_q):
            kb = g * blocks_per_q + kd
            k_ext, vt_blk = load_keys(kb)
            block_shift = jnp.full((1, 1), qi * tq - kb * TKB, I32).astype(F32) * (-cl)
            for j in range(n_strips):
                jq = j % (tq // SW)
                strip(j, k_ext, vt_blk, None,
                      qbias_ref[:, jq * SW:(jq + 1) * SW] + block_shift)
        return carry
    lax.fori_loop(0, qi, off_diag_group, 0)

    for kd in range(blocks_per_q):
        k_ext, vt_blk = load_keys(qi * blocks_per_q + kd)
        for j in range(n_strips):
            jq = j % (tq // SW)
            cols = slice(jq * SW, (jq + 1) * SW)
            if (kd + 1) * TKB <= jq * SW:
                strip(j, k_ext, vt_blk, None, qbias_ref[:, cols] + cl * (kd * TKB))
            elif (jq + 1) * SW > kd * TKB:
                strip(j, k_ext, vt_blk, diag_ref[kd, :, cols], 0.0)

    lv = lam_ref[...]
    lam = (jnp.exp(jnp.sum(lv[0:1] * lv[1:2], axis=-1, keepdims=True))
           - jnp.exp(jnp.sum(lv[2:3] * lv[3:4], axis=-1, keepdims=True)) + LAM_INIT)
    o2 = [acc_refs[j][...] / l_refs[j][...] for j in range(n_strips)]
    half = n_strips // 2
    o = jnp.concatenate([o2[j] - lam * o2[half + j] for j in range(half)], axis=1)
    o = o * lax.rsqrt(jnp.mean(o * o, axis=0, keepdims=True) + SUBLN_EPS)
    o = o * (g_ref[...] * (1.0 - LAM_INIT))
    o_ref[...] = o.T.astype(o_ref.dtype)


def _attention(qt, k, vt, slopes, lamvec, subln_g_col):
    b, s, _ = k.shape
    nkb = s // TKB
    n_strips = 2 * TQ // SW
    return pl.pallas_call(
        _attn_kernel,
        grid=(b, ATT_HEADS, s // TQ),
        in_specs=[
            pl.BlockSpec((None, VAL_DIM, TQ), lambda bi, h, qi: (bi, h, qi)),
            pl.BlockSpec((None, s, VAL_DIM), lambda bi, h, qi: (bi, 0, h)),
            pl.BlockSpec((None, None, nkb, VAL_DIM, TKB), lambda bi, h, qi: (bi, h, 0, 0, 0)),
            pl.BlockSpec((None, 1, 128), lambda bi, h, qi: (h, 0, 0)),
            pl.BlockSpec((4, ATT_HEAD_DIM), lambda bi, h, qi: (0, 0)),
            pl.BlockSpec((VAL_DIM, 1), lambda bi, h, qi: (0, 0)),
        ],
        out_specs=pl.BlockSpec((None, TQ, VAL_DIM), lambda bi, h, qi: (bi, qi, h)),
        out_shape=jax.ShapeDtypeStruct((b, s, ATT_WIDTH), BF16),
        scratch_shapes=[
            pltpu.VMEM((2 * VAL_DIM, 2 * TQ), BF16),
            pltpu.VMEM((TKB, 128), BF16),
            pltpu.VMEM((1, TQ), F32),
            pltpu.VMEM((TQ // TKB, TKB, TQ), F32),
        ] +[pltpu.VMEM((1, SW), F32)] * (2 * n_strips) + [pltpu.VMEM((VAL_DIM, SW), F32)] * n_strips,
        compiler_params=pltpu.CompilerParams(
            dimension_semantics=("parallel", "parallel", "arbitrary"),
            vmem_limit_bytes=VMEM_LIMIT),
        name="diff_attention",
    )(qt, k, vt, slopes, lamvec, subln_g_col)


def _lru_kernel(xl_ref, gl_ref, cw_ref, cb_ref, wg_ref, bg_ref, lam_ref,
                o_ref, ext_ref, h_ref):
    c = pl.program_id(1)
    tc = xl_ref.shape[0]
    w = LRU_WIDTH

    @pl.when(c == 0)
    def _():
        ext_ref[0:8, :] = jnp.zeros((8, w), F32)
        h_ref[...] = jnp.zeros(h_ref.shape, F32)

    ext_ref[8:8 + tc, :] = xl_ref[...]
    cw = cw_ref[...]
    xc = cb_ref[...] + cw[0:1] * ext_ref[5:5 + tc, :]
    for j in range(1, CONV_WIDTH):
        xc = xc + cw[j:j + 1] * ext_ref[5 + j:5 + j + tc, :]
    tail = ext_ref[tc:tc + 8, :]

    gates = jnp.dot(xc.astype(BF16), wg_ref[...], preferred_element_type=F32) + bg_ref[...]
    r = jax.nn.sigmoid(gates[:, 0:w])
    ig = jax.nn.sigmoid(gates[:, w:2 * w])
    z = -lam_ref[...]
    softplus = jnp.maximum(z, 0.0) + jnp.log1p(jnp.exp(-jnp.abs(z)))
    log_a = (-LRU_C) * r * softplus
    a = jnp.exp(log_a)
    u = jnp.sqrt(-jnp.tanh(log_a) * (a * a + 1.0)) * (ig * xc)

    ng = tc // SCAN_GROUP
    a = a.reshape(ng, SCAN_GROUP, w)
    u = u.reshape(ng, SCAN_GROUP, w)
    pos = lax.broadcasted_iota(I32, a.shape, 1)
    d = 1
    while d < SCAN_GROUP:
        valid = pos >= d
        u = jnp.where(valid, a * pltpu.roll(u, d, axis=1) + u, u)
        a = jnp.where(valid, a * pltpu.roll(a, d, axis=1), a)
        d *= 2
    a = a.reshape(tc, w)
    u = u.reshape(tc, w)
    h_in = h_ref[0:1, :]
    groups = []
    for g in range(tc // SCAN_GROUP):
        rows = slice(g * SCAN_GROUP, (g + 1) * SCAN_GROUP)
        hg = u[rows] + a[rows] * h_in
        groups.append(hg)
        h_in = hg[SCAN_GROUP - 1:SCAN_GROUP, :]
    h = jnp.concatenate(groups, axis=0)

    gl = gl_ref[...]
    gelu = 0.5 * gl * (1.0 + jnp.tanh(math.sqrt(2.0 / math.pi) * (gl + 0.044715 * (gl * gl * gl))))
    o_ref[...] = (h * gelu).astype(o_ref.dtype)

    h_ref[0:1, :] = h[tc - 1:tc, :]
    ext_ref[0:8, :] = tail


def _lru(xl, gl, conv_w, conv_b, w_gate_bf, b_gate, lam):
    b, s, w = xl.shape
    tc = TC_LRU
    blk = lambda bi, c: (bi, c, 0)
    fixed = lambda bi, c: (0, 0)
    return pl.pallas_call(
        _lru_kernel,
        grid=(b, s // tc),
        in_specs=[
            pl.BlockSpec((None, tc, w), blk),
            pl.BlockSpec((None, tc, w), blk),
            pl.BlockSpec((CONV_WIDTH, w), fixed),
            pl.BlockSpec((1, w), fixed),
            pl.BlockSpec((w, 2 * w), fixed),
            pl.BlockSpec((1, 2 * w), fixed),
            pl.BlockSpec((1, w), fixed),
        ],
        out_specs=pl.BlockSpec((None, tc, w), blk),
        out_shape=jax.ShapeDtypeStruct((b, s, w), BF16),
        scratch_shapes=[pltpu.VMEM((tc + 8, w), F32), pltpu.VMEM((8, w), F32)],
        compiler_params=pltpu.CompilerParams(
            dimension_semantics=("parallel", "arbitrary"), vmem_limit_bytes=VMEM_LIMIT),
        name="rg_lru",
    )(xl, gl, conv_w, conv_b, w_gate_bf, b_gate, lam)


def _outproj_kernel(x_ref, g0_ref, b0_ref, att_ref, rec_ref, wo_ref, g1_ref, b1_ref,
                    wr_ref, br_ref, x1_ref, e4_ref, g4_ref, r4_ref, cnt_ref, carry_ref):
    i = pl.program_id(0)
    tm = x_ref.shape[0]

    @pl.when(i == 0)
    def _():
        carry_ref[...] = jnp.zeros(carry_ref.shape, F32)

    x0 = _layer_norm(x_ref[...], g0_ref[...], b0_ref[...])
    mixed = (jnp.dot(att_ref[...], wo_ref[0:ATT_WIDTH, :], preferred_element_type=F32)
             + jnp.dot(rec_ref[...], wo_ref[ATT_WIDTH:, :], preferred_element_type=F32))
    x1 = _layer_norm(ALPHA * x0 + mixed, g1_ref[...], b1_ref[...])
    _store_row_tiles(x1_ref, x1)

    x1_hi = x1.astype(BF16)
    x1_lo = (x1 - x1_hi.astype(F32)).astype(BF16)
    nt = (((1,), (1,)), ((), ()))
    ne = N_EXPERTS
    hi_terms = lax.dot_general(wr_ref[...], x1_hi, nt, preferred_element_type=F32)
    lo_term = lax.dot_general(wr_ref[0:ne, :], x1_lo, nt, preferred_element_type=F32)
    logits = hi_terms[0:ne] + hi_terms[ne:2 * ne] + lo_term + br_ref[...]

    erow = lax.broadcasted_iota(I32, logits.shape, 0)
    work = logits
    vals, idxs, hots = [], [], []
    for _ in range(TOP_K):
        mx = jnp.max(work, axis=0, keepdims=True)
        idx = jnp.min(jnp.where(work == mx, erow, ne), axis=0, keepdims=True)
        hot = erow == idx
        vals.append(mx)
        idxs.append(idx)
        hots.append(hot)
        work = jnp.where(hot, -jnp.inf, work)
    exps = [jnp.exp(v - vals[0]) for v in vals]
    denom = exps[0] + exps[1] + exps[2] + exps[3]

    mask = jnp.zeros(logits.shape, F32)
    for hot in hots:
        mask = mask + hot.astype(F32)
    tr = lax.broadcasted_iota(I32, (tm, tm), 0)
    tc = lax.broadcasted_iota(I32, (tm, tm), 1)
    earlier = jnp.where(tr < tc, 1.0, 0.0).astype(BF16)
    carry = carry_ref[:, 0:1]
    excl = jnp.dot(mask.astype(BF16), earlier, preferred_element_type=F32) + carry
    ranks = [jnp.sum(jnp.where(hot, excl, 0.0), axis=0, keepdims=True) for hot in hots]
    pad_i = jnp.zeros((8 - TOP_K, tm), I32)
    pad_f = jnp.zeros((8 - TOP_K, tm), F32)
    e4_ref[...] = jnp.concatenate(idxs + [pad_i], axis=0)
    g4_ref[...] = jnp.concatenate([ex / denom for ex in exps] + [pad_f], axis=0)
    r4_ref[...] = jnp.concatenate([r.astype(I32) for r in ranks] + [pad_i], axis=0)
    total = carry + jnp.sum(mask, axis=1, keepdims=True)
    carry_ref[...] = jnp.broadcast_to(total, carry_ref.shape)
    cnt_ref[...] = jnp.broadcast_to(total, cnt_ref.shape)


def _outproj_router(x2, g0, b0, att, rec, wo_bf, g1, b1, w_router, b_router):
    t = x2.shape[0]
    tm = TM_PROJ
    row = lambda i: (i, 0)
    col = lambda i: (0, i)
    fixed = lambda i: (0, 0)
    return pl.pallas_call(
        _outproj_kernel,
        grid=(t // tm,),
        in_specs=[
            pl.BlockSpec((tm, D_MODEL), row),
            pl.BlockSpec((1, D_MODEL), fixed),
            pl.BlockSpec((1, D_MODEL), fixed),
            pl.BlockSpec((tm, ATT_WIDTH), row),
            pl.BlockSpec((tm, LRU_WIDTH), row),
            pl.BlockSpec((D_MODEL, D_MODEL), fixed),
            pl.BlockSpec((1, D_MODEL), fixed),
            pl.BlockSpec((1, D_MODEL), fixed),
            pl.BlockSpec((2 * N_EXPERTS, D_MODEL), fixed),
            pl.BlockSpec((N_EXPERTS, 1), fixed),
        ],
        out_specs=[
            pl.BlockSpec((tm * LANE_TILES, 128), row),
            pl.BlockSpec((8, tm), col),
            pl.BlockSpec((8, tm), col),
            pl.BlockSpec((8, tm), col),
            pl.BlockSpec((N_EXPERTS, 128), fixed),
        ],
        out_shape=[
            jax.ShapeDtypeStruct((t * LANE_TILES, 128), F32),
            jax.ShapeDtypeStruct((8, t), I32),
            jax.ShapeDtypeStruct((8, t), F32),
            jax.ShapeDtypeStruct((8, t), I32),
            jax.ShapeDtypeStruct((N_EXPERTS, 128), F32),
        ],
        scratch_shapes=[pltpu.VMEM((N_EXPERTS, 128), F32)],
        compiler_params=pltpu.CompilerParams(
            dimension_semantics=("arbitrary",), vmem_limit_bytes=VMEM_LIMIT),
        name="outproj_ln_router",
    )(x2, g0, b0, att, rec, wo_bf, g1, b1, w_router, b_router)


def _tile_rows(row):
    return pl.ds(pl.multiple_of(row * LANE_TILES, LANE_TILES), LANE_TILES)


def _invert_kernel(dest_ref, rt_ref, zero_ref, sem):
    i = pl.program_id(0)
    tm = dest_ref.shape[1] // TOP_K

    @pl.when(i == 0)
    def _():
        zero_ref[...] = jnp.zeros(zero_ref.shape, I32)
        clear = pltpu.make_async_copy(zero_ref, rt_ref, sem)
        clear.start()
        clear.wait()

    def body(t, carry):
        for kk in range(TOP_K):
            rt_ref[dest_ref[0, t * TOP_K + kk]] = i * tm + t
        return carry
    lax.fori_loop(0, tm, body, 0, unroll=4)


def _invert(dest3, n_rows):
    nt, _, per = dest3.shape
    return pl.pallas_call(
        _invert_kernel,
        grid=(nt,),
        in_specs=[pl.BlockSpec((None, 1, per), lambda i: (i, 0, 0), memory_space=pltpu.SMEM)],
        out_specs=pl.BlockSpec(memory_space=pltpu.SMEM),
        out_shape=jax.ShapeDtypeStruct((n_rows,), I32),
        scratch_shapes=[pltpu.VMEM((n_rows,), I32), pltpu.SemaphoreType.DMA(())],
        compiler_params=pltpu.CompilerParams(dimension_semantics=("arbitrary",)),
        name="moe_invert",
    )(dest3)


def _expert_kernel(first_ref, nblk_ref, rt_ref, wgu_ref, bg_ref, bl_ref, wd_ref, bd_ref,
                   x_hbm, ys_hbm, wt_ref, wg_ref, wl_ref, wdb_ref, xb_ref, xbuf, ybuf, xsem, ysem):
    e = pl.program_id(0)
    n_exp = pl.num_programs(0)
    d, f2 = wgu_ref.shape
    f = f2 // 2
    ch = XPOSE_CHUNK
    first = first_ref[e]
    nblk = nblk_ref[e]
    n_live = first_ref[n_exp - 1] + nblk_ref[n_exp - 1]
    n_blocks = ys_hbm.shape[0] // (ROW_BLOCK * LANE_TILES)
    block_rows = ROW_BLOCK * LANE_TILES

    def gather_start(g, slot):
        for r in range(ROW_BLOCK):
            pltpu.make_async_copy(
                x_hbm.at[_tile_rows(rt_ref[g * ROW_BLOCK + r]), :],
                xbuf.at[slot, pl.ds(r * LANE_TILES, LANE_TILES), :],
                xsem.at[slot]).start()

    def gather_wait(slot):
        pltpu.make_async_copy(x_hbm.at[pl.ds(0, block_rows), :], xbuf.at[slot],
                              xsem.at[slot]).wait()

    def y_copy(g, slot):
        start = pl.multiple_of(g * block_rows, block_rows)
        return pltpu.make_async_copy(ybuf.at[slot], ys_hbm.at[pl.ds(start, block_rows), :],
                                     ysem.at[slot])

    @pl.when(jnp.logical_and(nblk > 0, first == 0))
    def _():
        gather_start(0, 0)

    @pl.when(nblk > 0)
    def _():
        for j in range(d // 128):
            rows = slice(j * 128, (j + 1) * 128)
            for c in range(f2 // ch):
                wt_ref[j, c * ch:(c + 1) * ch, :] = wgu_ref[rows, c * ch:(c + 1) * ch].T
            for c in range(f // ch):
                gate_rows = wt_ref[j, pl.ds(2 * c * ch, ch, stride=2), :]
                lin_rows = wt_ref[j, pl.ds(2 * c * ch + 1, ch, stride=2), :]
                wg_ref[rows, c * ch:(c + 1) * ch] = gate_rows.T.astype(BF16)
                wl_ref[rows, c * ch:(c + 1) * ch] = lin_rows.T.astype(BF16)
        wdb_ref[...] = wd_ref[...].astype(BF16)

    def block(b, carry):
        g = first + b
        for slot in range(2):
            @pl.when(g % 2 == slot)
            def _():
                block_body(g, slot)
        return carry

    def block_body(g, slot):
        @pl.when(g >= 2)
        def _():
            y_copy(g - 2, slot).wait()

        gather_wait(slot)
        xb_ref[...] = _load_row_tiles(xbuf.at[slot], ROW_BLOCK).astype(BF16)
        gather_start(jnp.minimum(g + 1, n_live - 1), 1 - slot)
        x = xb_ref[...]
        hg = jnp.dot(x, wg_ref[...], preferred_element_type=F32) + bg_ref[...]
        hl = jnp.dot(x, wl_ref[...], preferred_element_type=F32) + bl_ref[...]
        xg = jnp.minimum(hg, SWIGLU_LIMIT)
        xl = jnp.clip(hl, -SWIGLU_LIMIT, SWIGLU_LIMIT)
        act = xg * jax.nn.sigmoid(SWIGLU_ALPHA * xg) * (xl + 1.0)
        y = jnp.dot(act.astype(BF16), wdb_ref[...], preferred_element_type=F32) + bd_ref[...]
        _store_row_tiles(ybuf.at[slot], y)
        y_copy(g, slot).start()

    lax.fori_loop(0, nblk, block, 0)

    @pl.when(e == n_exp - 1)
    def _():
        gather_wait(n_live % 2)
        y_copy(0, 0).wait()
        y_copy(0, 1).wait()
        ybuf[0] = jnp.zeros(ybuf.shape[1:], F32)
        for t in range(N_EXPERTS):
            @pl.when(n_live + t < n_blocks)
            def _():
                y_copy(n_live + t, 0).start()
        for t in range(N_EXPERTS):
            @pl.when(n_live + t < n_blocks)
            def _():
                y_copy(n_live + t, 0).wait()


def _experts(first_blk, n_blk, row_tok, x1r, w_gu, bg, bl, w_down, bd):
    d = D_MODEL
    f2 = w_gu.shape[2]
    f = f2 // 2
    n_rows = row_tok.shape[0]
    wmap = lambda e, fb, nb_: (e, 0, 0)
    grid_spec = pltpu.PrefetchScalarGridSpec(
        num_scalar_prefetch=2,
        grid=(N_EXPERTS,),
        in_specs=[
            pl.BlockSpec(memory_space=pltpu.SMEM),
            pl.BlockSpec((None, d, f2), wmap),
            pl.BlockSpec((None, 1, f), wmap),
            pl.BlockSpec((None, 1, f), wmap),
            pl.BlockSpec((None, f, d), wmap),
            pl.BlockSpec((None, 1, d), wmap),
            pl.BlockSpec(memory_space=pl.ANY),
        ],
        out_specs=pl.BlockSpec(memory_space=pl.ANY),
        scratch_shapes=[
            pltpu.VMEM((d // 128, f2, 128), F32),
            pltpu.VMEM((d, f), BF16),
            pltpu.VMEM((d, f), BF16),
            pltpu.VMEM((f, d), BF16),
            pltpu.VMEM((ROW_BLOCK, d), BF16),
            pltpu.VMEM((2, ROW_BLOCK * LANE_TILES, 128), F32),
            pltpu.VMEM((2, ROW_BLOCK * LANE_TILES, 128), F32),
            pltpu.SemaphoreType.DMA((2,)),
            pltpu.SemaphoreType.DMA((2,)),
        ],
    )
    return pl.pallas_call(
        _expert_kernel,
        grid_spec=grid_spec,
        out_shape=jax.ShapeDtypeStruct((n_rows * LANE_TILES, 128), F32),
        compiler_params=pltpu.CompilerParams(
            dimension_semantics=("arbitrary",), vmem_limit_bytes=VMEM_LIMIT_EXPERTS),
        name="moe_experts",
    )(first_blk, n_blk, row_tok, w_gu, bg, bl, w_down, bd, x1r)


def _combine_start(ys_hbm, idx_ref, buf_ref, sem, tm):
    def body(t, carry):
        for kk in range(TOP_K):
            pltpu.make_async_copy(ys_hbm.at[_tile_rows(idx_ref[0, t * TOP_K + kk]), :],
                                  buf_ref.at[kk, _tile_rows(t), :], sem).start()
        return carry
    lax.fori_loop(0, tm, body, 0, unroll=2)


def _combine_wait(ys_hbm, buf_ref, sem, tm):
    for kk in range(TOP_K):
        pltpu.make_async_copy(ys_hbm.at[pl.ds(0, tm * LANE_TILES), :], buf_ref.at[kk], sem).wait()


def _combine_kernel(idx_cur, idx_nxt, x1_ref, g4_ref, g2_ref, b2_ref, ys_hbm,
                    o_ref, buf_ref, sem):
    i = pl.program_id(0)
    n = pl.num_programs(0)
    slot = i % 2
    tm = o_ref.shape[0]

    @pl.when(i == 0)
    def _():
        _combine_start(ys_hbm, idx_cur, buf_ref.at[0], sem.at[0], tm)

    @pl.when(i + 1 < n)
    def _():
        _combine_start(ys_hbm, idx_nxt, buf_ref.at[1 - slot], sem.at[1 - slot], tm)

    _combine_wait(ys_hbm, buf_ref.at[slot], sem.at[slot], tm)
    g4 = g4_ref[...]
    ffn = g4[:, 0:1] * _load_row_tiles(buf_ref.at[slot, 0], tm)
    for kk in range(1, TOP_K):
        ffn = ffn + g4[:, kk:kk + 1] * _load_row_tiles(buf_ref.at[slot, kk], tm)
    x1 = _load_row_tiles(x1_ref, tm)
    o_ref[...] = _layer_norm(ALPHA * x1 + ffn, g2_ref[...], b2_ref[...])


def _combine(dest3, x1r, g4, g2, b2, ys):
    d = D_MODEL
    t = x1r.shape[0] // LANE_TILES
    tm = TM_COMB
    nt = t // tm
    row = lambda i: (i, 0)
    fixed = lambda i: (0, 0)
    return pl.pallas_call(
        _combine_kernel,
        grid=(nt,),
        in_specs=[
            pl.BlockSpec((None, 1, tm * TOP_K), lambda i: (i, 0, 0), memory_space=pltpu.SMEM),
            pl.BlockSpec((None, 1, tm * TOP_K), lambda i: (jnp.minimum(i + 1, nt - 1), 0, 0),
                         memory_space=pltpu.SMEM),
            pl.BlockSpec((tm * LANE_TILES, 128), row),
            pl.BlockSpec((tm, TOP_K), row),
            pl.BlockSpec((1, d), fixed),
            pl.BlockSpec((1, d), fixed),
            pl.BlockSpec(memory_space=pl.ANY),
        ],
        out_specs=pl.BlockSpec((tm, d), row),
        out_shape=jax.ShapeDtypeStruct((t, d), F32),
        scratch_shapes=[pltpu.VMEM((2, TOP_K, tm * LANE_TILES, 128), F32),
                        pltpu.SemaphoreType.DMA((2,))],
        compiler_params=pltpu.CompilerParams(
            dimension_semantics=("arbitrary",), vmem_limit_bytes=VMEM_LIMIT),
        name="moe_combine_ln",
    )(dest3, dest3, x1r, g4, g2, b2, ys)


def kernel(x, ln0_g, ln0_b, w_in, conv_w, conv_b, w_rg_a, b_rg_a, w_rg_x, b_rg_x, lru_lambda, lam_q1, lam_k1, lam_q2, lam_k2, subln_g, w_out, ln1_g, ln1_b, w_router, b_router, w_gu, b_gu, w_down, b_down, ln2_g, ln2_b):
    bsz, seq, d = x.shape
    t = bsz * seq
    x2 = x.reshape(t, d)
    g0 = ln0_g.reshape(1, d)
    b0 = ln0_b.reshape(1, d)
    l = 0

    w_in_bf = w_in[l].astype(BF16)
    w_out_bf = w_out[l].astype(BF16)
    wa = jax.scipy.linalg.block_diag(*[w_rg_a[l, n] for n in range(LRU_BLOCKS)])
    wx = jax.scipy.linalg.block_diag(*[w_rg_x[l, n] for n in range(LRU_BLOCKS)])
    w_gate_bf = jnp.concatenate([wa, wx], axis=1).astype(BF16)
    b_gate = jnp.concatenate([b_rg_a[l].reshape(1, -1), b_rg_x[l].reshape(1, -1)], axis=1)
    lamvec = jnp.stack([lam_q1[l], lam_k1[l], lam_q2[l], lam_k2[l]]).astype(F32)
    slopes = jnp.asarray([2.0 ** (-8.0 * (i + 1) / ATT_HEADS) for i in range(ATT_HEADS)], F32)
    slopes = jnp.broadcast_to(slopes[:, None, None], (ATT_HEADS, 1, 128))
    bg = b_gu[l, :, None, 0::2]
    bl = b_gu[l, :, None, 1::2]
    bd = b_down[l][:, None, :]

    qt, k, vt, xl, gl = _inproj(x, g0, b0, w_in_bf)
    att = _attention(qt, k, vt, slopes, lamvec, subln_g[l].reshape(-1, 1))
    rec = _lru(xl, gl, conv_w[l], conv_b[l].reshape(1, -1), w_gate_bf, b_gate,
               lru_lambda[l].reshape(1, -1))
    wr_t = w_router[l].T.astype(F32)
    wr_hi = wr_t.astype(BF16)
    wr_lo = (wr_t - wr_hi.astype(F32)).astype(BF16)
    x1r, e4t, g4t, r4t, cnt = _outproj_router(
        x2, g0, b0, att.reshape(t, -1), rec.reshape(t, -1), w_out_bf,
        ln1_g[l].reshape(1, d), ln1_b[l].reshape(1, d),
        jnp.concatenate([wr_hi, wr_lo], axis=0), b_router[l].reshape(-1, 1).astype(F32))
    e4 = e4t[0:TOP_K].T
    g4 = g4t[0:TOP_K].T
    r4 = r4t[0:TOP_K].T

    n_assign = t * TOP_K
    nb = (n_assign + N_EXPERTS * (ROW_BLOCK - 1)) // ROW_BLOCK + 1
    counts = cnt[:, 0].astype(I32)
    padded = (counts + ROW_BLOCK - 1) // ROW_BLOCK * ROW_BLOCK
    pends = jnp.cumsum(padded)
    pstarts = pends - padded
    dest = pstarts[e4] + r4
    row_tok = _invert(dest.reshape(t // TM_INV, 1, TM_INV * TOP_K), nb * ROW_BLOCK)
    ys = _experts((pstarts // ROW_BLOCK).astype(I32), (padded // ROW_BLOCK).astype(I32),
                  row_tok, x1r, w_gu[l], bg, bl, w_down[l], bd)
    out = _combine(dest.reshape(t // TM_COMB, 1, TM_COMB * TOP_K), x1r, g4,
                   ln2_g[l].reshape(1, d), ln2_b[l].reshape(1, d), ys)
    return out.reshape(bsz, seq, d)
```

```python
import math

import jax
import jax.numpy as jnp
from jax import lax
from jax.experimental import pallas as pl
from jax.experimental.pallas import tpu as pltpu

F32 = jnp.float32
BF16 = jnp.bfloat16
I32 = jnp.int32

D_MODEL = 1024
ATT_WIDTH = 512
ATT_HEAD_DIM = 64
ATT_HEADS = 4
VAL_DIM = 2 * ATT_HEAD_DIM
LRU_WIDTH = 512
LRU_BLOCKS = 8
LRU_C = 8.0
CONV_WIDTH = 4
CHUNK_SHIFT = 6
N_EXPERTS = 32
TOP_K = 4
D_EXPERT = 1024
SWIGLU_LIMIT = 7.0
SWIGLU_ALPHA = 1.702
DEPTH = 1
ALPHA = (2.0 * DEPTH) ** 0.25
LN_EPS = 1e-5
SUBLN_EPS = 1e-5
LAM_INIT = 0.8 - 0.6 * math.exp(-0.3 * 0)
NEG_BIG = -1e30
LOG2E = math.log2(math.e)

TM_PROJ = 512
TQ = 1024
TKB = 128
SW = 256
TC_LRU = 512
SCAN_GROUP = 8
ROW_BLOCK = 256
TM_COMB = 128
TM_INV = 2048
GATHER_DMA_PRIORITY = 1
XPOSE_CHUNK = 512

VMEM_LIMIT = 48 * 1024 * 1024
VMEM_LIMIT_EXPERTS = 56 * 1024 * 1024


LANE_TILES = D_MODEL // 128


def _store_row_tiles(ref, val):
    n = val.shape[0]
    for j in range(LANE_TILES):
        ref[pl.ds(j, n, stride=LANE_TILES), :] = val[:, j * 128:(j + 1) * 128]


def _load_row_tiles(ref, n):
    return jnp.concatenate(
        [ref[pl.ds(j, n, stride=LANE_TILES), :] for j in range(LANE_TILES)], axis=1)


def _layer_norm(x, g, b):
    mu = jnp.mean(x, axis=-1, keepdims=True)
    xc = x - mu
    var = jnp.mean(xc * xc, axis=-1, keepdims=True)
    return xc * lax.rsqrt(var + LN_EPS) * g + b


def _inproj_kernel(x_ref, g_ref, b_ref, w_ref, qt_ref, k_ref, vt_ref, xl_ref, gl_ref):
    x0 = _layer_norm(x_ref[...], g_ref[...], b_ref[...])
    p = jnp.dot(x0.astype(BF16), w_ref[...], preferred_element_type=F32)
    a = ATT_WIDTH
    qt_ref[...] = (p[:, 0:a] * (ATT_HEAD_DIM ** -0.5 * LOG2E)).T.astype(BF16)
    k_ref[...] = p[:, a:2 * a].astype(BF16)
    vt = p[:, 2 * a:3 * a].T.astype(BF16)
    for h in range(ATT_HEADS):
        for c in range(vt_ref.shape[1]):
            vt_ref[h, c] = vt[h * VAL_DIM:(h + 1) * VAL_DIM, c * TKB:(c + 1) * TKB]
    xl_ref[...] = p[:, 3 * a:3 * a + LRU_WIDTH]
    gl_ref[...] = p[:, 3 * a + LRU_WIDTH:]


def _inproj(x, g, b, w_bf):
    bsz, seq, _ = x.shape
    n = w_bf.shape[1]
    tm = TM_PROJ
    row = lambda bi, i: (bi, i, 0)
    col = lambda bi, i: (bi, 0, i)
    fixed = lambda bi, i: (0, 0)
    return pl.pallas_call(
        _inproj_kernel,
        grid=(bsz, seq // tm),
        in_specs=[
            pl.BlockSpec((None, tm, D_MODEL), row),
            pl.BlockSpec((1, D_MODEL), fixed),
            pl.BlockSpec((1, D_MODEL), fixed),
            pl.BlockSpec((D_MODEL, n), fixed),
        ],
        out_specs=[
            pl.BlockSpec((None, ATT_WIDTH, tm), col),
            pl.BlockSpec((None, tm, ATT_WIDTH), row),
            pl.BlockSpec((None, ATT_HEADS, tm // TKB, VAL_DIM, TKB),
                         lambda bi, i: (bi, 0, i, 0, 0)),
            pl.BlockSpec((None, tm, LRU_WIDTH), row),
            pl.BlockSpec((None, tm, LRU_WIDTH), row),
        ],
        out_shape=[
            jax.ShapeDtypeStruct((bsz, ATT_WIDTH, seq), BF16),
            jax.ShapeDtypeStruct((bsz, seq, ATT_WIDTH), BF16),
            jax.ShapeDtypeStruct((bsz, ATT_HEADS, seq // TKB, VAL_DIM, TKB), BF16),
            jax.ShapeDtypeStruct((bsz, seq, LRU_WIDTH), F32),
            jax.ShapeDtypeStruct((bsz, seq, LRU_WIDTH), F32),
        ],
        compiler_params=pltpu.CompilerParams(
            dimension_semantics=("parallel", "parallel"), vmem_limit_bytes=VMEM_LIMIT),
        name="ln_inproj",
    )(x, g, b, w_bf)


def _attn_kernel(qt_ref, k_ref, vt_ref, slope_ref, lam_ref, g_ref,
                 o_ref, q2_ref, kx_ref, qbias_ref, diag_ref, *stat_refs):
    qi = pl.program_id(2)
    tq = qt_ref.shape[1]
    n_strips = 2 * tq // SW
    blocks_per_q = tq // TKB
    cl = slope_ref[0:1, 0:1] * LOG2E

    @pl.when(qi == 0)
    def _():
        c = lax.broadcasted_iota(I32, (TKB, 128), 0)
        lane = lax.broadcasted_iota(I32, (TKB, 128), 1)
        w = c.astype(F32) * cl
        w_hi = w.astype(BF16).astype(F32)
        w_mid = (w - w_hi).astype(BF16).astype(F32)
        w_lo = w - w_hi - w_mid
        kx_ref[...] = jnp.where(
            lane == 0, w_hi, jnp.where(lane == 1, w_mid, jnp.where(lane == 2, w_lo, 0.0))
        ).astype(BF16)
        r1 = lax.broadcasted_iota(I32, (1, tq), 1)
        qbias_ref[...] = r1.astype(F32) * (-cl)
        ck0 = lax.broadcasted_iota(I32, (TKB, tq), 0)
        r = lax.broadcasted_iota(I32, (TKB, tq), 1)
        for kd in range(blocks_per_q):
            ck = ck0 + kd * TKB
            allowed = (ck >> CHUNK_SHIFT) <= (r >> CHUNK_SHIFT)
            bias = (jnp.abs(r - ck) + ck0).astype(F32) * (-cl)
            diag_ref[kd] = jnp.where(allowed, bias, NEG_BIG)

    m_refs = stat_refs[0:n_strips]
    l_refs = stat_refs[n_strips:2 * n_strips]
    acc_refs = stat_refs[2 * n_strips:3 * n_strips]
    for j in range(n_strips):
        m_refs[j][...] = jnp.full(m_refs[j].shape, NEG_BIG, F32)
        l_refs[j][...] = jnp.zeros(l_refs[j].shape, F32)
        acc_refs[j][...] = jnp.zeros(acc_refs[j].shape, F32)
    qt = qt_ref[...]
    frow = lax.broadcasted_iota(I32, qt.shape, 0)
    zero = jnp.zeros_like(qt)
    q2_ref[0:VAL_DIM, 0:tq] = jnp.where(frow < ATT_HEAD_DIM, qt, zero)
    q2_ref[0:VAL_DIM, tq:2 * tq] = jnp.where(frow >= ATT_HEAD_DIM, qt, zero)
    xrow = lax.broadcasted_iota(I32, (VAL_DIM, 2 * tq), 0)
    q2_ref[VAL_DIM:2 * VAL_DIM, :] = jnp.where(xrow < 3, 1.0, 0.0).astype(BF16)

    def load_keys(kb):
        k_blk = k_ref[pl.ds(pl.multiple_of(kb * TKB, TKB), TKB), :]
        return jnp.concatenate([k_blk, kx_ref[...]], axis=1), vt_ref[kb]

    def strip(j, k_ext, vt_blk, sc_bias, shift):
        ls = slice(j * SW, (j + 1) * SW)
        sc = jnp.dot(k_ext, q2_ref[:, ls], preferred_element_type=F32)
        if sc_bias is not None:
            sc = sc + sc_bias
        m_prev = m_refs[j][...]
        m_new = jnp.maximum(m_prev, jnp.max(sc, axis=0, keepdims=True) + shift)
        alpha = jnp.exp2(m_prev - m_new)
        p = jnp.exp2(sc - (m_new - shift))
        l_refs[j][...] = alpha * l_refs[j][...] + jnp.sum(p, axis=0, keepdims=True)
        pv = jnp.dot(vt_blk, p.astype(BF16), preferred_element_type=F32)
        acc_refs[j][...] = alpha * acc_refs[j][...] + pv
        m_refs[j][...] = m_new

    def off_diag_group(g, carry):
        for kd in range(blocks_per_q):
            kb = g * blocks_per_q + kd
            k_ext, vt_blk = load_keys(kb)
            block_shift = jnp.full((1, 1), qi * tq - kb * TKB, I32).astype(F32) * (-cl)
            for j in range(n_strips):
                jq = j % (tq // SW)
                strip(j, k_ext, vt_blk, None,
                      qbias_ref[:, jq * SW:(jq + 1) * SW] + block_shift)
        return carry
    lax.fori_loop(0, qi, off_diag_group, 0)

    for kd in range(blocks_per_q):
        k_ext, vt_blk = load_keys(qi * blocks_per_q + kd)
        for j in range(n_strips):
            jq = j % (tq // SW)
            cols = slice(jq * SW, (jq + 1) * SW)
            if (kd + 1) * TKB <= jq * SW:
                strip(j, k_ext, vt_blk, None, qbias_ref[:, cols] + cl * (kd * TKB))
            elif (jq + 1) * SW > kd * TKB:
                strip(j, k_ext, vt_blk, diag_ref[kd, :, cols], 0.0)

    lv = lam_ref[...]
    lam = (jnp.exp(jnp.sum(lv[0:1] * lv[1:2], axis=-1, keepdims=True))
           - jnp.exp(jnp.sum(lv[2:3] * lv[3:4], axis=-1, keepdims=True)) + LAM_INIT)
    o2 = [acc_refs[j][...] / l_refs[j][...] for j in range(n_strips)]
    half = n_strips // 2
    o = jnp.concatenate([o2[j] - lam * o2[half + j] for j in range(half)], axis=1)
    o = o * lax.rsqrt(jnp.mean(o * o, axis=0, keepdims=True) + SUBLN_EPS)
    o = o * (g_ref[...] * (1.0 - LAM_INIT))
    o_ref[...] = o.T.astype(o_ref.dtype)


def _attention(qt, k, vt, slopes, lamvec, subln_g_col):
    b, s, _ = k.shape
    nkb = s // TKB
    n_strips = 2 * TQ // SW
    return pl.pallas_call(
        _attn_kernel,
        grid=(b, ATT_HEADS, s // TQ),
        in_specs=[
            pl.BlockSpec((None, VAL_DIM, TQ), lambda bi, h, qi: (bi, h, qi)),
            pl.BlockSpec((None, s, VAL_DIM), lambda bi, h, qi: (bi, 0, h)),
            pl.BlockSpec((None, None, nkb, VAL_DIM, TKB), lambda bi, h, qi: (bi, h, 0, 0, 0)),
            pl.BlockSpec((None, 1, 128), lambda bi, h, qi: (h, 0, 0)),
            pl.BlockSpec((4, ATT_HEAD_DIM), lambda bi, h, qi: (0, 0)),
            pl.BlockSpec((VAL_DIM, 1), lambda bi, h, qi: (0, 0)),
        ],
        out_specs=pl.BlockSpec((None, TQ, VAL_DIM), lambda bi, h, qi: (bi, qi, h)),
        out_shape=jax.ShapeDtypeStruct((b, s, ATT_WIDTH), BF16),
        scratch_shapes=[
            pltpu.VMEM((2 * VAL_DIM, 2 * TQ), BF16),
            pltpu.VMEM((TKB, 128), BF16),
            pltpu.VMEM((1, TQ), F32),
            pltpu.VMEM((TQ // TKB, TKB, TQ), F32),
        ] +[pltpu.VMEM((1, SW), F32)] * (2 * n_strips) + [pltpu.VMEM((VAL_DIM, SW), F32)] * n_strips,
        compiler_params=pltpu.CompilerParams(
            dimension_semantics=("parallel", "parallel", "arbitrary"),
            vmem_limit_bytes=VMEM_LIMIT),
        name="diff_attention",
    )(qt, k, vt, slopes, lamvec, subln_g_col)


def _lru_kernel(xl_ref, gl_ref, cw_ref, cb_ref, wg_ref, bg_ref, lam_ref,
                o_ref, ext_ref, h_ref):
    c = pl.program_id(1)
    tc = xl_ref.shape[0]
    w = LRU_WIDTH

    @pl.when(c == 0)
    def _():
        ext_ref[0:8, :] = jnp.zeros((8, w), F32)
        h_ref[...] = jnp.zeros(h_ref.shape, F32)

    ext_ref[8:8 + tc, :] = xl_ref[...]
    cw = cw_ref[...]
    xc = cb_ref[...] + cw[0:1] * ext_ref[5:5 + tc, :]
    for j in range(1, CONV_WIDTH):
        xc = xc + cw[j:j + 1] * ext_ref[5 + j:5 + j + tc, :]
    tail = ext_ref[tc:tc + 8, :]

    gates = jnp.dot(xc.astype(BF16), wg_ref[...], preferred_element_type=F32) + bg_ref[...]
    r = jax.nn.sigmoid(gates[:, 0:w])
    ig = jax.nn.sigmoid(gates[:, w:2 * w])
    z = -lam_ref[...]
    softplus = jnp.maximum(z, 0.0) + jnp.log1p(jnp.exp(-jnp.abs(z)))
    log_a = (-LRU_C) * r * softplus
    a = jnp.exp(log_a)
    u = jnp.sqrt(-jnp.tanh(log_a) * (a * a + 1.0)) * (ig * xc)

    ng = tc // SCAN_GROUP
    a = a.reshape(ng, SCAN_GROUP, w)
    u = u.reshape(ng, SCAN_GROUP, w)
    pos = lax.broadcasted_iota(I32, a.shape, 1)
    d = 1
    while d < SCAN_GROUP:
        valid = pos >= d
        u = jnp.where(valid, a * pltpu.roll(u, d, axis=1) + u, u)
        a = jnp.where(valid, a * pltpu.roll(a, d, axis=1), a)
        d *= 2
    a = a.reshape(tc, w)
    u = u.reshape(tc, w)
    h_in = h_ref[0:1, :]
    groups = []
    for g in range(tc // SCAN_GROUP):
        rows = slice(g * SCAN_GROUP, (g + 1) * SCAN_GROUP)
        hg = u[rows] + a[rows] * h_in
        groups.append(hg)
        h_in = hg[SCAN_GROUP - 1:SCAN_GROUP, :]
    h = jnp.concatenate(groups, axis=0)

    gl = gl_ref[...]
    gelu = 0.5 * gl * (1.0 + jnp.tanh(math.sqrt(2.0 / math.pi) * (gl + 0.044715 * (gl * gl * gl))))
    o_ref[...] = (h * gelu).astype(o_ref.dtype)

    h_ref[0:1, :] = h[tc - 1:tc, :]
    ext_ref[0:8, :] = tail


def _lru(xl, gl, conv_w, conv_b, w_gate_bf, b_gate, lam):
    b, s, w = xl.shape
    tc = TC_LRU
    blk = lambda bi, c: (bi, c, 0)
    fixed = lambda bi, c: (0, 0)
    return pl.pallas_call(
        _lru_kernel,
        grid=(b, s // tc),
        in_specs=[
            pl.BlockSpec((None, tc, w), blk),
            pl.BlockSpec((None, tc, w), blk),
            pl.BlockSpec((CONV_WIDTH, w), fixed),
            pl.BlockSpec((1, w), fixed),
            pl.BlockSpec((w, 2 * w), fixed),
            pl.BlockSpec((1, 2 * w), fixed),
            pl.BlockSpec((1, w), fixed),
        ],
        out_specs=pl.BlockSpec((None, tc, w), blk),
        out_shape=jax.ShapeDtypeStruct((b, s, w), BF16),
        scratch_shapes=[pltpu.VMEM((tc + 8, w), F32), pltpu.VMEM((8, w), F32)],
        compiler_params=pltpu.CompilerParams(
            dimension_semantics=("parallel", "arbitrary"), vmem_limit_bytes=VMEM_LIMIT),
        name="rg_lru",
    )(xl, gl, conv_w, conv_b, w_gate_bf, b_gate, lam)


def _outproj_kernel(x_ref, g0_ref, b0_ref, att_ref, rec_ref, wo_ref, g1_ref, b1_ref,
                    wr_ref, br_ref, x1_ref, e4_ref, g4_ref, r4_ref, cnt_ref, carry_ref):
    i = pl.program_id(0)
    tm = x_ref.shape[0]

    @pl.when(i == 0)
    def _():
        carry_ref[...] = jnp.zeros(carry_ref.shape, F32)

    x0 = _layer_norm(x_ref[...], g0_ref[...], b0_ref[...])
    mixed = (jnp.dot(att_ref[...], wo_ref[0:ATT_WIDTH, :], preferred_element_type=F32)
             + jnp.dot(rec_ref[...], wo_ref[ATT_WIDTH:, :], preferred_element_type=F32))
    x1 = _layer_norm(ALPHA * x0 + mixed, g1_ref[...], b1_ref[...])
    _store_row_tiles(x1_ref, x1)

    x1_hi = x1.astype(BF16)
    x1_lo = (x1 - x1_hi.astype(F32)).astype(BF16)
    nt = (((1,), (1,)), ((), ()))
    ne = N_EXPERTS
    hi_terms = lax.dot_general(wr_ref[...], x1_hi, nt, preferred_element_type=F32)
    lo_term = lax.dot_general(wr_ref[0:ne, :], x1_lo, nt, preferred_element_type=F32)
    logits = hi_terms[0:ne] + hi_terms[ne:2 * ne] + lo_term + br_ref[...]

    erow = lax.broadcasted_iota(I32, logits.shape, 0)
    work = logits
    vals, idxs, hots = [], [], []
    for _ in range(TOP_K):
        mx = jnp.max(work, axis=0, keepdims=True)
        idx = jnp.min(jnp.where(work == mx, erow, ne), axis=0, keepdims=True)
        hot = erow == idx
        vals.append(mx)
        idxs.append(idx)
        hots.append(hot)
        work = jnp.where(hot, -jnp.inf, work)
    exps = [jnp.exp(v - vals[0]) for v in vals]
    denom = exps[0] + exps[1] + exps[2] + exps[3]

    mask = jnp.zeros(logits.shape, F32)
    for hot in hots:
        mask = mask + hot.astype(F32)
    tr = lax.broadcasted_iota(I32, (tm, tm), 0)
    tc = lax.broadcasted_iota(I32, (tm, tm), 1)
    earlier = jnp.where(tr < tc, 1.0, 0.0).astype(BF16)
    carry = carry_ref[:, 0:1]
    excl = jnp.dot(mask.astype(BF16), earlier, preferred_element_type=F32) + carry
    ranks = [jnp.sum(jnp.where(hot, excl, 0.0), axis=0, keepdims=True) for hot in hots]
    pad_i = jnp.zeros((8 - TOP_K, tm), I32)
    pad_f = jnp.zeros((8 - TOP_K, tm), F32)
    e4_ref[...] = jnp.concatenate(idxs + [pad_i], axis=0)
    g4_ref[...] = jnp.concatenate([ex / denom for ex in exps] + [pad_f], axis=0)
    r4_ref[...] = jnp.concatenate([r.astype(I32) for r in ranks] + [pad_i], axis=0)
    total = carry + jnp.sum(mask, axis=1, keepdims=True)
    carry_ref[...] = jnp.broadcast_to(total, carry_ref.shape)
    cnt_ref[...] = jnp.broadcast_to(total, cnt_ref.shape)


def _outproj_router(x2, g0, b0, att, rec, wo_bf, g1, b1, w_router, b_router):
    t = x2.shape[0]
    tm = TM_PROJ
    row = lambda i: (i, 0)
    col = lambda i: (0, i)
    fixed = lambda i: (0, 0)
    return pl.pallas_call(
        _outproj_kernel,
        grid=(t // tm,),
        in_specs=[
            pl.BlockSpec((tm, D_MODEL), row),
            pl.BlockSpec((1, D_MODEL), fixed),
            pl.BlockSpec((1, D_MODEL), fixed),
            pl.BlockSpec((tm, ATT_WIDTH), row),
            pl.BlockSpec((tm, LRU_WIDTH), row),
            pl.BlockSpec((D_MODEL, D_MODEL), fixed),
            pl.BlockSpec((1, D_MODEL), fixed),
            pl.BlockSpec((1, D_MODEL), fixed),
            pl.BlockSpec((2 * N_EXPERTS, D_MODEL), fixed),
            pl.BlockSpec((N_EXPERTS, 1), fixed),
        ],
        out_specs=[
            pl.BlockSpec((tm * LANE_TILES, 128), row),
            pl.BlockSpec((8, tm), col),
            pl.BlockSpec((8, tm), col),
            pl.BlockSpec((8, tm), col),
            pl.BlockSpec((N_EXPERTS, 128), fixed),
        ],
        out_shape=[
            jax.ShapeDtypeStruct((t * LANE_TILES, 128), F32),
            jax.ShapeDtypeStruct((8, t), I32),
            jax.ShapeDtypeStruct((8, t), F32),
            jax.ShapeDtypeStruct((8, t), I32),
            jax.ShapeDtypeStruct((N_EXPERTS, 128), F32),
        ],
        scratch_shapes=[pltpu.VMEM((N_EXPERTS, 128), F32)],
        compiler_params=pltpu.CompilerParams(
            dimension_semantics=("arbitrary",), vmem_limit_bytes=VMEM_LIMIT),
        name="outproj_ln_router",
    )(x2, g0, b0, att, rec, wo_bf, g1, b1, w_router, b_router)


def _tile_rows(row):
    return pl.ds(pl.multiple_of(row * LANE_TILES, LANE_TILES), LANE_TILES)


def _invert_kernel(dest_ref, rt_ref, zero_ref, sem):
    i = pl.program_id(0)
    tm = dest_ref.shape[1] // TOP_K

    @pl.when(i == 0)
    def _():
        zero_ref[...] = jnp.zeros(zero_ref.shape, I32)
        clear = pltpu.make_async_copy(zero_ref, rt_ref, sem)
        clear.start()
        clear.wait()

    def body(t, carry):
        for kk in range(TOP_K):
            rt_ref[dest_ref[0, t * TOP_K + kk]] = i * tm + t
        return carry
    lax.fori_loop(0, tm, body, 0, unroll=4)


def _invert(dest3, n_rows):
    nt, _, per = dest3.shape
    return pl.pallas_call(
        _invert_kernel,
        grid=(nt,),
        in_specs=[pl.BlockSpec((None, 1, per), lambda i: (i, 0, 0), memory_space=pltpu.SMEM)],
        out_specs=pl.BlockSpec(memory_space=pltpu.SMEM),
        out_shape=jax.ShapeDtypeStruct((n_rows,), I32),
        scratch_shapes=[pltpu.VMEM((n_rows,), I32), pltpu.SemaphoreType.DMA(())],
        compiler_params=pltpu.CompilerParams(dimension_semantics=("arbitrary",)),
        name="moe_invert",
    )(dest3)


def _expert_kernel(first_ref, nblk_ref, rt_ref, wgu_ref, bg_ref, bl_ref, wd_ref, bd_ref,
                   x_hbm, ys_hbm, wt_ref, wg_ref, wl_ref, wdb_ref, xb_ref, xbuf, ybuf, xsem, ysem):
    e = pl.program_id(0)
    n_exp = pl.num_programs(0)
    d, f2 = wgu_ref.shape
    f = f2 // 2
    ch = XPOSE_CHUNK
    first = first_ref[e]
    nblk = nblk_ref[e]
    n_live = first_ref[n_exp - 1] + nblk_ref[n_exp - 1]
    n_blocks = ys_hbm.shape[0] // (ROW_BLOCK * LANE_TILES)
    block_rows = ROW_BLOCK * LANE_TILES

    def gather_start(g, slot):
        for r in range(ROW_BLOCK):
            pltpu.make_async_copy(
                x_hbm.at[_tile_rows(rt_ref[g * ROW_BLOCK + r]), :],
                xbuf.at[slot, pl.ds(r * LANE_TILES, LANE_TILES), :],
                xsem.at[slot]).start(priority=GATHER_DMA_PRIORITY)

    def gather_wait(slot):
        pltpu.make_async_copy(x_hbm.at[pl.ds(0, block_rows), :], xbuf.at[slot],
                              xsem.at[slot]).wait()

    def y_copy(g, slot):
        start = pl.multiple_of(g * block_rows, block_rows)
        return pltpu.make_async_copy(ybuf.at[slot], ys_hbm.at[pl.ds(start, block_rows), :],
                                     ysem.at[slot])

    @pl.when(jnp.logical_and(nblk > 0, first == 0))
    def _():
        gather_start(0, 0)

    @pl.when(nblk > 0)
    def _():
        for j in range(d // 128):
            rows = slice(j * 128, (j + 1) * 128)
            for c in range(f2 // ch):
                wt_ref[j, c * ch:(c + 1) * ch, :] = wgu_ref[rows, c * ch:(c + 1) * ch].T
            for c in range(f // ch):
                gate_rows = wt_ref[j, pl.ds(2 * c * ch, ch, stride=2), :]
                lin_rows = wt_ref[j, pl.ds(2 * c * ch + 1, ch, stride=2), :]
                wg_ref[rows, c * ch:(c + 1) * ch] = gate_rows.T.astype(BF16)
                wl_ref[rows, c * ch:(c + 1) * ch] = lin_rows.T.astype(BF16)
        wdb_ref[...] = wd_ref[...].astype(BF16)

    def block(b, carry):
        g = first + b
        for slot in range(2):
            @pl.when(g % 2 == slot)
            def _():
                block_body(g, slot)
        return carry

    def block_body(g, slot):
        @pl.when(g >= 2)
        def _():
            y_copy(g - 2, slot).wait()

        gather_wait(slot)
        xb_ref[...] = _load_row_tiles(xbuf.at[slot], ROW_BLOCK).astype(BF16)
        gather_start(jnp.minimum(g + 1, n_live - 1), 1 - slot)
        x = xb_ref[...]
        hg = jnp.dot(x, wg_ref[...], preferred_element_type=F32) + bg_ref[...]
        hl = jnp.dot(x, wl_ref[...], preferred_element_type=F32) + bl_ref[...]
        xg = jnp.minimum(hg, SWIGLU_LIMIT)
        xl = jnp.clip(hl, -SWIGLU_LIMIT, SWIGLU_LIMIT)
        act = xg * jax.nn.sigmoid(SWIGLU_ALPHA * xg) * (xl + 1.0)
        y = jnp.dot(act.astype(BF16), wdb_ref[...], preferred_element_type=F32) + bd_ref[...]
        _store_row_tiles(ybuf.at[slot], y)
        y_copy(g, slot).start()

    lax.fori_loop(0, nblk, block, 0)

    @pl.when(e == n_exp - 1)
    def _():
        gather_wait(n_live % 2)
        y_copy(0, 0).wait()
        y_copy(0, 1).wait()
        ybuf[0] = jnp.zeros(ybuf.shape[1:], F32)
        for t in range(N_EXPERTS):
            @pl.when(n_live + t < n_blocks)
            def _():
                y_copy(n_live + t, 0).start()
        for t in range(N_EXPERTS):
            @pl.when(n_live + t < n_blocks)
            def _():
                y_copy(n_live + t, 0).wait()


def _experts(first_blk, n_blk, row_tok, x1r, w_gu, bg, bl, w_down, bd):
    d = D_MODEL
    f2 = w_gu.shape[2]
    f = f2 // 2
    n_rows = row_tok.shape[0]
    wmap = lambda e, fb, nb_: (e, 0, 0)
    grid_spec = pltpu.PrefetchScalarGridSpec(
        num_scalar_prefetch=2,
        grid=(N_EXPERTS,),
        in_specs=[
            pl.BlockSpec(memory_space=pltpu.SMEM),
            pl.BlockSpec((None, d, f2), wmap),
            pl.BlockSpec((None, 1, f), wmap),
            pl.BlockSpec((None, 1, f), wmap),
            pl.BlockSpec((None, f, d), wmap),
            pl.BlockSpec((None, 1, d), wmap),
            pl.BlockSpec(memory_space=pl.ANY),
        ],
        out_specs=pl.BlockSpec(memory_space=pl.ANY),
        scratch_shapes=[
            pltpu.VMEM((d // 128, f2, 128), F32),
            pltpu.VMEM((d, f), BF16),
            pltpu.VMEM((d, f), BF16),
            pltpu.VMEM((f, d), BF16),
            pltpu.VMEM((ROW_BLOCK, d), BF16),
            pltpu.VMEM((2, ROW_BLOCK * LANE_TILES, 128), F32),
            pltpu.VMEM((2, ROW_BLOCK * LANE_TILES, 128), F32),
            pltpu.SemaphoreType.DMA((2,)),
            pltpu.SemaphoreType.DMA((2,)),
        ],
    )
    return pl.pallas_call(
        _expert_kernel,
        grid_spec=grid_spec,
        out_shape=jax.ShapeDtypeStruct((n_rows * LANE_TILES, 128), F32),
        compiler_params=pltpu.CompilerParams(
            dimension_semantics=("arbitrary",), vmem_limit_bytes=VMEM_LIMIT_EXPERTS),
        name="moe_experts",
    )(first_blk, n_blk, row_tok, w_gu, bg, bl, w_down, bd, x1r)


def _combine_start(ys_hbm, idx_ref, buf_ref, sem, tm):
    def body(t, carry):
        for kk in range(TOP_K):
            pltpu.make_async_copy(ys_hbm.at[_tile_rows(idx_ref[0, t * TOP_K + kk]), :],
                                  buf_ref.at[kk, _tile_rows(t), :], sem).start()
        return carry
    lax.fori_loop(0, tm, body, 0, unroll=2)


def _combine_wait(ys_hbm, buf_ref, sem, tm):
    for kk in range(TOP_K):
        pltpu.make_async_copy(ys_hbm.at[pl.ds(0, tm * LANE_TILES), :], buf_ref.at[kk], sem).wait()


def _combine_kernel(idx_cur, idx_nxt, x1_ref, g4_ref, g2_ref, b2_ref, ys_hbm,
                    o_ref, buf_ref, sem):
    i = pl.program_id(0)
    n = pl.num_programs(0)
    slot = i % 2
    tm = o_ref.shape[0]

    @pl.when(i == 0)
    def _():
        _combine_start(ys_hbm, idx_cur, buf_ref.at[0], sem.at[0], tm)

    @pl.when(i + 1 < n)
    def _():
        _combine_start(ys_hbm, idx_nxt, buf_ref.at[1 - slot], sem.at[1 - slot], tm)

    _combine_wait(ys_hbm, buf_ref.at[slot], sem.at[slot], tm)
    g4 = g4_ref[...]
    ffn = g4[:, 0:1] * _load_row_tiles(buf_ref.at[slot, 0], tm)
    for kk in range(1, TOP_K):
        ffn = ffn + g4[:, kk:kk + 1] * _load_row_tiles(buf_ref.at[slot, kk], tm)
    x1 = _load_row_tiles(x1_ref, tm)
    o_ref[...] = _layer_norm(ALPHA * x1 + ffn, g2_ref[...], b2_ref[...])


def _combine(dest3, x1r, g4, g2, b2, ys):
    d = D_MODEL
    t = x1r.shape[0] // LANE_TILES
    tm = TM_COMB
    nt = t // tm
    row = lambda i: (i, 0)
    fixed = lambda i: (0, 0)
    return pl.pallas_call(
        _combine_kernel,
        grid=(nt,),
        in_specs=[
            pl.BlockSpec((None, 1, tm * TOP_K), lambda i: (i, 0, 0), memory_space=pltpu.SMEM),
            pl.BlockSpec((None, 1, tm * TOP_K), lambda i: (jnp.minimum(i + 1, nt - 1), 0, 0),
                         memory_space=pltpu.SMEM),
            pl.BlockSpec((tm * LANE_TILES, 128), row),
            pl.BlockSpec((tm, TOP_K), row),
            pl.BlockSpec((1, d), fixed),
            pl.BlockSpec((1, d), fixed),
            pl.BlockSpec(memory_space=pl.ANY),
        ],
        out_specs=pl.BlockSpec((tm, d), row),
        out_shape=jax.ShapeDtypeStruct((t, d), F32),
        scratch_shapes=[pltpu.VMEM((2, TOP_K, tm * LANE_TILES, 128), F32),
                        pltpu.SemaphoreType.DMA((2,))],
        compiler_params=pltpu.CompilerParams(
            dimension_semantics=("arbitrary",), vmem_limit_bytes=VMEM_LIMIT),
        name="moe_combine_ln",
    )(dest3, dest3, x1r, g4, g2, b2, ys)


def kernel(x, ln0_g, ln0_b, w_in, conv_w, conv_b, w_rg_a, b_rg_a, w_rg_x, b_rg_x, lru_lambda, lam_q1, lam_k1, lam_q2, lam_k2, subln_g, w_out, ln1_g, ln1_b, w_router, b_router, w_gu, b_gu, w_down, b_down, ln2_g, ln2_b):
    bsz, seq, d = x.shape
    t = bsz * seq
    x2 = x.reshape(t, d)
    g0 = ln0_g.reshape(1, d)
    b0 = ln0_b.reshape(1, d)
    l = 0

    w_in_bf = w_in[l].astype(BF16)
    w_out_bf = w_out[l].astype(BF16)
    wa = jax.scipy.linalg.block_diag(*[w_rg_a[l, n] for n in range(LRU_BLOCKS)])
    wx = jax.scipy.linalg.block_diag(*[w_rg_x[l, n] for n in range(LRU_BLOCKS)])
    w_gate_bf = jnp.concatenate([wa, wx], axis=1).astype(BF16)
    b_gate = jnp.concatenate([b_rg_a[l].reshape(1, -1), b_rg_x[l].reshape(1, -1)], axis=1)
    lamvec = jnp.stack([lam_q1[l], lam_k1[l], lam_q2[l], lam_k2[l]]).astype(F32)
    slopes = jnp.asarray([2.0 ** (-8.0 * (i + 1) / ATT_HEADS) for i in range(ATT_HEADS)], F32)
    slopes = jnp.broadcast_to(slopes[:, None, None], (ATT_HEADS, 1, 128))
    bg = b_gu[l, :, None, 0::2]
    bl = b_gu[l, :, None, 1::2]
    bd = b_down[l][:, None, :]

    qt, k, vt, xl, gl = _inproj(x, g0, b0, w_in_bf)
    att = _attention(qt, k, vt, slopes, lamvec, subln_g[l].reshape(-1, 1))
    rec = _lru(xl, gl, conv_w[l], conv_b[l].reshape(1, -1), w_gate_bf, b_gate,
               lru_lambda[l].reshape(1, -1))
    wr_t = w_router[l].T.astype(F32)
    wr_hi = wr_t.astype(BF16)
    wr_lo = (wr_t - wr_hi.astype(F32)).astype(BF16)
    x1r, e4t, g4t, r4t, cnt = _outproj_router(
        x2, g0, b0, att.reshape(t, -1), rec.reshape(t, -1), w_out_bf,
        ln1_g[l].reshape(1, d), ln1_b[l].reshape(1, d),
        jnp.concatenate([wr_hi, wr_lo], axis=0), b_router[l].reshape(-1, 1).astype(F32))
    e4 = e4t[0:TOP_K].T
    g4 = g4t[0:TOP_K].T
    r4 = r4t[0:TOP_K].T

    n_assign = t * TOP_K
    nb = (n_assign + N_EXPERTS * (ROW_BLOCK - 1)) // ROW_BLOCK + 1
    counts = cnt[:, 0].astype(I32)
    padded = (counts + ROW_BLOCK - 1) // ROW_BLOCK * ROW_BLOCK
    pends = jnp.cumsum(padded)
    pstarts = pends - padded
    dest = pstarts[e4] + r4
    row_tok = _invert(dest.reshape(t // TM_INV, 1, TM_INV * TOP_K), nb * ROW_BLOCK)
    ys = _experts((pstarts // ROW_BLOCK).astype(I32), (padded // ROW_BLOCK).astype(I32),
                  row_tok, x1r, w_gu[l], bg, bl, w_down[l], bd)
    out = _combine(dest.reshape(t // TM_COMB, 1, TM_COMB * TOP_K), x1r, g4,
                   ln2_g[l].reshape(1, d), ln2_b[l].reshape(1, d), ys)
    return out.reshape(bsz, seq, d)
```

```python
import math

import jax
import jax.numpy as jnp
from jax import lax
from jax.experimental import pallas as pl
from jax.experimental.pallas import tpu as pltpu

F32 = jnp.float32
BF16 = jnp.bfloat16
I32 = jnp.int32

D_MODEL = 1024
ATT_WIDTH = 512
ATT_HEAD_DIM = 64
ATT_HEADS = 4
VAL_DIM = 2 * ATT_HEAD_DIM
LRU_WIDTH = 512
LRU_BLOCKS = 8
LRU_C = 8.0
CONV_WIDTH = 4
CHUNK_SHIFT = 6
N_EXPERTS = 32
TOP_K = 4
D_EXPERT = 1024
SWIGLU_LIMIT = 7.0
SWIGLU_ALPHA = 1.702
DEPTH = 1
ALPHA = (2.0 * DEPTH) ** 0.25
LN_EPS = 1e-5
SUBLN_EPS = 1e-5
LAM_INIT = 0.8 - 0.6 * math.exp(-0.3 * 0)
NEG_BIG = -1e30
LOG2E = math.log2(math.e)

TM_PROJ = 512
TQ = 1024
TKB = 128
SW = 256
TC_LRU = 512
SCAN_GROUP = 8
ROW_BLOCK = 256
TM_COMB = 128
TM_INV = 2048
XPOSE_CHUNK = 512

VMEM_LIMIT = 48 * 1024 * 1024
VMEM_LIMIT_EXPERTS = 56 * 1024 * 1024

LANES = 128
LANE_TILES = D_MODEL // LANES


def _store_row_tiles(ref, val):
    n = val.shape[0]
    for j in range(LANE_TILES):
        ref[pl.ds(j, n, stride=LANE_TILES), :] = val[:, j * LANES:(j + 1) * LANES]


def _load_row_tiles(ref, n):
    return jnp.concatenate(
        [ref[pl.ds(j, n, stride=LANE_TILES), :] for j in range(LANE_TILES)], axis=1)


def _layer_norm(x, g, b):
    mu = jnp.mean(x, axis=-1, keepdims=True)
    xc = x - mu
    var = jnp.mean(xc * xc, axis=-1, keepdims=True)
    return xc * lax.rsqrt(var + LN_EPS) * g + b


def _inproj_kernel(x_ref, g_ref, b_ref, w_ref, qt_ref, k_ref, vt_ref, xl_ref, gl_ref):
    x0 = _layer_norm(x_ref[...], g_ref[...], b_ref[...])
    p = jnp.dot(x0.astype(BF16), w_ref[...], preferred_element_type=F32)
    a = ATT_WIDTH
    qt_ref[...] = (p[:, 0:a] * (ATT_HEAD_DIM ** -0.5 * LOG2E)).T.astype(BF16)
    k_ref[...] = p[:, a:2 * a].astype(BF16)
    vt = p[:, 2 * a:3 * a].T.astype(BF16)
    for h in range(ATT_HEADS):
        for c in range(vt_ref.shape[1]):
            vt_ref[h, c] = vt[h * VAL_DIM:(h + 1) * VAL_DIM, c * TKB:(c + 1) * TKB]
    xl_ref[...] = p[:, 3 * a:3 * a + LRU_WIDTH]
    gl_ref[...] = p[:, 3 * a + LRU_WIDTH:]


def _inproj(x, g, b, w_bf):
    bsz, seq, _ = x.shape
    n = w_bf.shape[1]
    tm = TM_PROJ
    row = lambda bi, i: (bi, i, 0)
    col = lambda bi, i: (bi, 0, i)
    fixed = lambda bi, i: (0, 0)
    return pl.pallas_call(
        _inproj_kernel,
        grid=(bsz, seq // tm),
        in_specs=[
            pl.BlockSpec((None, tm, D_MODEL), row),
            pl.BlockSpec((1, D_MODEL), fixed),
            pl.BlockSpec((1, D_MODEL), fixed),
            pl.BlockSpec((D_MODEL, n), fixed),
        ],
        out_specs=[
            pl.BlockSpec((None, ATT_WIDTH, tm), col),
            pl.BlockSpec((None, tm, ATT_WIDTH), row),
            pl.BlockSpec((None, ATT_HEADS, tm // TKB, VAL_DIM, TKB),
                         lambda bi, i: (bi, 0, i, 0, 0)),
            pl.BlockSpec((None, tm, LRU_WIDTH), row),
            pl.BlockSpec((None, tm, LRU_WIDTH), row),
        ],
        out_shape=[
            jax.ShapeDtypeStruct((bsz, ATT_WIDTH, seq), BF16),
            jax.ShapeDtypeStruct((bsz, seq, ATT_WIDTH), BF16),
            jax.ShapeDtypeStruct((bsz, ATT_HEADS, seq // TKB, VAL_DIM, TKB), BF16),
            jax.ShapeDtypeStruct((bsz, seq, LRU_WIDTH), F32),
            jax.ShapeDtypeStruct((bsz, seq, LRU_WIDTH), F32),
        ],
        compiler_params=pltpu.CompilerParams(
            dimension_semantics=("parallel", "parallel"), vmem_limit_bytes=VMEM_LIMIT),
        name="ln_inproj",
    )(x, g, b, w_bf)


def _attn_kernel(qt_ref, k_ref, vt_ref, slope_ref, lam_ref, g_ref,
                 o_ref, q2_ref, kx_ref, qbias_ref, diag_ref, *stat_refs):
    qi = pl.program_id(2)
    tq = qt_ref.shape[1]
    n_strips = 2 * tq // SW
    blocks_per_q = tq // TKB
    cl = slope_ref[0:1, 0:1] * LOG2E

    @pl.when(qi == 0)
    def _():
        c = lax.broadcasted_iota(I32, (TKB, LANES), 0)
        lane = lax.broadcasted_iota(I32, (TKB, LANES), 1)
        w = c.astype(F32) * cl
        w_hi = w.astype(BF16).astype(F32)
        w_mid = (w - w_hi).astype(BF16).astype(F32)
        w_lo = w - w_hi - w_mid
        kx_ref[...] = jnp.where(
            lane == 0, w_hi, jnp.where(lane == 1, w_mid, jnp.where(lane == 2, w_lo, 0.0))
        ).astype(BF16)
        r1 = lax.broadcasted_iota(I32, (1, tq), 1)
        qbias_ref[...] = r1.astype(F32) * (-cl)
        ck0 = lax.broadcasted_iota(I32, (TKB, tq), 0)
        r = lax.broadcasted_iota(I32, (TKB, tq), 1)
        for kd in range(blocks_per_q):
            ck = ck0 + kd * TKB
            allowed = (ck >> CHUNK_SHIFT) <= (r >> CHUNK_SHIFT)
            bias = (jnp.abs(r - ck) + ck0).astype(F32) * (-cl)
            diag_ref[kd] = jnp.where(allowed, bias, NEG_BIG)

    m_refs = stat_refs[0:n_strips]
    l_refs = stat_refs[n_strips:2 * n_strips]
    acc_refs = stat_refs[2 * n_strips:3 * n_strips]
    for j in range(n_strips):
        m_refs[j][...] = jnp.full(m_refs[j].shape, NEG_BIG, F32)
        l_refs[j][...] = jnp.zeros(l_refs[j].shape, F32)
        acc_refs[j][...] = jnp.zeros(acc_refs[j].shape, F32)
    qt = qt_ref[...]
    frow = lax.broadcasted_iota(I32, qt.shape, 0)
    zero = jnp.zeros_like(qt)
    q2_ref[0:VAL_DIM, 0:tq] = jnp.where(frow < ATT_HEAD_DIM, qt, zero)
    q2_ref[0:VAL_DIM, tq:2 * tq] = jnp.where(frow >= ATT_HEAD_DIM, qt, zero)
    xrow = lax.broadcasted_iota(I32, (VAL_DIM, 2 * tq), 0)
    q2_ref[VAL_DIM:2 * VAL_DIM, :] = jnp.where(xrow < 3, 1.0, 0.0).astype(BF16)

    def load_keys(kb):
        k_blk = k_ref[pl.ds(pl.multiple_of(kb * TKB, TKB), TKB), :]
        return jnp.concatenate([k_blk, kx_ref[...]], axis=1), vt_ref[kb]

    def strip(j, k_ext, vt_blk, sc_bias, shift):
        ls = slice(j * SW, (j + 1) * SW)
        sc = jnp.dot(k_ext, q2_ref[:, ls], preferred_element_type=F32)
        if sc_bias is not None:
            sc = sc + sc_bias
        m_prev = m_refs[j][...]
        m_new = jnp.maximum(m_prev, jnp.max(sc, axis=0, keepdims=True) + shift)
        alpha = jnp.exp2(m_prev - m_new)
        p = jnp.exp2(sc - (m_new - shift))
        l_refs[j][...] = alpha * l_refs[j][...] + jnp.sum(p, axis=0, keepdims=True)
        pv = jnp.dot(vt_blk, p.astype(BF16), preferred_element_type=F32)
        acc_refs[j][...] = alpha * acc_refs[j][...] + pv
        m_refs[j][...] = m_new

    def off_diag_group(g, carry):
        for kd in range(blocks_per_q):
            kb = g * blocks_per_q + kd
            k_ext, vt_blk = load_keys(kb)
            block_shift = jnp.full((1, 1), qi * tq - kb * TKB, I32).astype(F32) * (-cl)
            for j in range(n_strips):
                jq = j % (tq // SW)
                strip(j, k_ext, vt_blk, None,
                      qbias_ref[:, jq * SW:(jq + 1) * SW] + block_shift)
        return carry
    lax.fori_loop(0, qi, off_diag_group, 0)

    for kd in range(blocks_per_q):
        k_ext, vt_blk = load_keys(qi * blocks_per_q + kd)
        for j in range(n_strips):
            jq = j % (tq // SW)
            cols = slice(jq * SW, (jq + 1) * SW)
            if (kd + 1) * TKB <= jq * SW:
                strip(j, k_ext, vt_blk, None, qbias_ref[:, cols] + cl * (kd * TKB))
            elif (jq + 1) * SW > kd * TKB:
                strip(j, k_ext, vt_blk, diag_ref[kd, :, cols], 0.0)

    lv = lam_ref[...]
    lam = (jnp.exp(jnp.sum(lv[0:1] * lv[1:2], axis=-1, keepdims=True))
           - jnp.exp(jnp.sum(lv[2:3] * lv[3:4], axis=-1, keepdims=True)) + LAM_INIT)
    o2 = [acc_refs[j][...] / l_refs[j][...] for j in range(n_strips)]
    half = n_strips // 2
    o = jnp.concatenate([o2[j] - lam * o2[half + j] for j in range(half)], axis=1)
    o = o * lax.rsqrt(jnp.mean(o * o, axis=0, keepdims=True) + SUBLN_EPS)
    o = o * (g_ref[...] * (1.0 - LAM_INIT))
    o_ref[...] = o.T.astype(o_ref.dtype)


def _attention(qt, k, vt, slopes, lamvec, subln_g_col):
    b, s, _ = k.shape
    nkb = s // TKB
    n_strips = 2 * TQ // SW
    return pl.pallas_call(
        _attn_kernel,
        grid=(b, ATT_HEADS, s // TQ),
        in_specs=[
            pl.BlockSpec((None, VAL_DIM, TQ), lambda bi, h, qi: (bi, h, qi)),
            pl.BlockSpec((None, s, VAL_DIM), lambda bi, h, qi: (bi, 0, h)),
            pl.BlockSpec((None, None, nkb, VAL_DIM, TKB), lambda bi, h, qi: (bi, h, 0, 0, 0)),
            pl.BlockSpec((None, 1, LANES), lambda bi, h, qi: (h, 0, 0)),
            pl.BlockSpec((4, ATT_HEAD_DIM), lambda bi, h, qi: (0, 0)),
            pl.BlockSpec((VAL_DIM, 1), lambda bi, h, qi: (0, 0)),
        ],
        out_specs=pl.BlockSpec((None, TQ, VAL_DIM), lambda bi, h, qi: (bi, qi, h)),
        out_shape=jax.ShapeDtypeStruct((b, s, ATT_WIDTH), BF16),
        scratch_shapes=[
            pltpu.VMEM((2 * VAL_DIM, 2 * TQ), BF16),
            pltpu.VMEM((TKB, LANES), BF16),
            pltpu.VMEM((1, TQ), F32),
            pltpu.VMEM((TQ // TKB, TKB, TQ), F32),
        ] +[pltpu.VMEM((1, SW), F32)] * (2 * n_strips) + [pltpu.VMEM((VAL_DIM, SW), F32)] * n_strips,
        compiler_params=pltpu.CompilerParams(
            dimension_semantics=("parallel", "parallel", "arbitrary"),
            vmem_limit_bytes=VMEM_LIMIT),
        name="diff_attention",
    )(qt, k, vt, slopes, lamvec, subln_g_col)


def _lru_kernel(xl_ref, gl_ref, cw_ref, cb_ref, wg_ref, bg_ref, lam_ref,
                o_ref, ext_ref, h_ref):
    c = pl.program_id(1)
    tc = xl_ref.shape[0]
    w = LRU_WIDTH

    @pl.when(c == 0)
    def _():
        ext_ref[0:8, :] = jnp.zeros((8, w), F32)
        h_ref[...] = jnp.zeros(h_ref.shape, F32)

    ext_ref[8:8 + tc, :] = xl_ref[...]
    cw = cw_ref[...]
    xc = cb_ref[...] + cw[0:1] * ext_ref[5:5 + tc, :]
    for j in range(1, CONV_WIDTH):
        xc = xc + cw[j:j + 1] * ext_ref[5 + j:5 + j + tc, :]
    tail = ext_ref[tc:tc + 8, :]

    gates = jnp.dot(xc.astype(BF16), wg_ref[...], preferred_element_type=F32) + bg_ref[...]
    r = jax.nn.sigmoid(gates[:, 0:w])
    ig = jax.nn.sigmoid(gates[:, w:2 * w])
    z = -lam_ref[...]
    softplus = jnp.maximum(z, 0.0) + jnp.log1p(jnp.exp(-jnp.abs(z)))
    log_a = (-LRU_C) * r * softplus
    a = jnp.exp(log_a)
    u = jnp.sqrt(-jnp.tanh(log_a) * (a * a + 1.0)) * (ig * xc)

    ng = tc // SCAN_GROUP
    a = a.reshape(ng, SCAN_GROUP, w)
    u = u.reshape(ng, SCAN_GROUP, w)
    pos = lax.broadcasted_iota(I32, a.shape, 1)
    d = 1
    while d < SCAN_GROUP:
        valid = pos >= d
        u = jnp.where(valid, a * pltpu.roll(u, d, axis=1) + u, u)
        a = jnp.where(valid, a * pltpu.roll(a, d, axis=1), a)
        d *= 2
    a = a.reshape(tc, w)
    u = u.reshape(tc, w)
    h_in = h_ref[0:1, :]
    groups = []
    for g in range(tc // SCAN_GROUP):
        rows = slice(g * SCAN_GROUP, (g + 1) * SCAN_GROUP)
        hg = u[rows] + a[rows] * h_in
        groups.append(hg)
        h_in = hg[SCAN_GROUP - 1:SCAN_GROUP, :]
    h = jnp.concatenate(groups, axis=0)

    gl = gl_ref[...]
    gelu = 0.5 * gl * (1.0 + jnp.tanh(math.sqrt(2.0 / math.pi) * (gl + 0.044715 * (gl * gl * gl))))
    o_ref[...] = (h * gelu).astype(o_ref.dtype)

    h_ref[0:1, :] = h[tc - 1:tc, :]
    ext_ref[0:8, :] = tail


def _lru(xl, gl, conv_w, conv_b, w_gate_bf, b_gate, lam):
    b, s, w = xl.shape
    tc = TC_LRU
    blk = lambda bi, c: (bi, c, 0)
    fixed = lambda bi, c: (0, 0)
    return pl.pallas_call(
        _lru_kernel,
        grid=(b, s // tc),
        in_specs=[
            pl.BlockSpec((None, tc, w), blk),
            pl.BlockSpec((None, tc, w), blk),
            pl.BlockSpec((CONV_WIDTH, w), fixed),
            pl.BlockSpec((1, w), fixed),
            pl.BlockSpec((w, 2 * w), fixed),
            pl.BlockSpec((1, 2 * w), fixed),
            pl.BlockSpec((1, w), fixed),
        ],
        out_specs=pl.BlockSpec((None, tc, w), blk),
        out_shape=jax.ShapeDtypeStruct((b, s, w), BF16),
        scratch_shapes=[pltpu.VMEM((tc + 8, w), F32), pltpu.VMEM((8, w), F32)],
        compiler_params=pltpu.CompilerParams(
            dimension_semantics=("parallel", "arbitrary"), vmem_limit_bytes=VMEM_LIMIT),
        name="rg_lru",
    )(xl, gl, conv_w, conv_b, w_gate_bf, b_gate, lam)


def _outproj_kernel(x_ref, g0_ref, b0_ref, att_ref, rec_ref, wo_ref, g1_ref, b1_ref,
                    wr_ref, br_ref, x1_ref, e4_ref, g4_ref, r4_ref, cnt_ref, carry_ref):
    i = pl.program_id(0)
    tm = x_ref.shape[0]

    @pl.when(i == 0)
    def _():
        carry_ref[...] = jnp.zeros(carry_ref.shape, F32)

    x0 = _layer_norm(x_ref[...], g0_ref[...], b0_ref[...])
    mixed = (jnp.dot(att_ref[...], wo_ref[0:ATT_WIDTH, :], preferred_element_type=F32)
             + jnp.dot(rec_ref[...], wo_ref[ATT_WIDTH:, :], preferred_element_type=F32))
    x1 = _layer_norm(ALPHA * x0 + mixed, g1_ref[...], b1_ref[...])
    _store_row_tiles(x1_ref, x1)

    x1_hi = x1.astype(BF16)
    x1_lo = (x1 - x1_hi.astype(F32)).astype(BF16)
    nt = (((1,), (1,)), ((), ()))
    ne = N_EXPERTS
    hi_terms = lax.dot_general(wr_ref[...], x1_hi, nt, preferred_element_type=F32)
    lo_term = lax.dot_general(wr_ref[0:ne, :], x1_lo, nt, preferred_element_type=F32)
    logits = hi_terms[0:ne] + hi_terms[ne:2 * ne] + lo_term + br_ref[...]

    erow = lax.broadcasted_iota(I32, logits.shape, 0)
    work = logits
    vals, idxs, hots = [], [], []
    for _ in range(TOP_K):
        mx = jnp.max(work, axis=0, keepdims=True)
        idx = jnp.min(jnp.where(work == mx, erow, ne), axis=0, keepdims=True)
        hot = erow == idx
        vals.append(mx)
        idxs.append(idx)
        hots.append(hot)
        work = jnp.where(hot, -jnp.inf, work)
    exps = [jnp.exp(v - vals[0]) for v in vals]
    denom = exps[0] + exps[1] + exps[2] + exps[3]

    mask = jnp.zeros(logits.shape, F32)
    for hot in hots:
        mask = mask + hot.astype(F32)
    tr = lax.broadcasted_iota(I32, (tm, tm), 0)
    tc = lax.broadcasted_iota(I32, (tm, tm), 1)
    earlier = jnp.where(tr < tc, 1.0, 0.0).astype(BF16)
    carry = carry_ref[:, 0:1]
    excl = jnp.dot(mask.astype(BF16), earlier, preferred_element_type=F32) + carry
    ranks = [jnp.sum(jnp.where(hot, excl, 0.0), axis=0, keepdims=True) for hot in hots]
    pad_i = jnp.zeros((8 - TOP_K, tm), I32)
    pad_f = jnp.zeros((8 - TOP_K, tm), F32)
    e4_ref[...] = jnp.concatenate(idxs + [pad_i], axis=0)
    g4_ref[...] = jnp.concatenate([ex / denom for ex in exps] + [pad_f], axis=0)
    r4_ref[...] = jnp.concatenate([r.astype(I32) for r in ranks] + [pad_i], axis=0)
    total = carry + jnp.sum(mask, axis=1, keepdims=True)
    carry_ref[...] = jnp.broadcast_to(total, carry_ref.shape)
    cnt_ref[...] = jnp.broadcast_to(total, cnt_ref.shape)


def _outproj_router(x2, g0, b0, att, rec, wo_bf, g1, b1, w_router, b_router):
    t = x2.shape[0]
    tm = TM_PROJ
    row = lambda i: (i, 0)
    col = lambda i: (0, i)
    fixed = lambda i: (0, 0)
    return pl.pallas_call(
        _outproj_kernel,
        grid=(t // tm,),
        in_specs=[
            pl.BlockSpec((tm, D_MODEL), row),
            pl.BlockSpec((1, D_MODEL), fixed),
            pl.BlockSpec((1, D_MODEL), fixed),
            pl.BlockSpec((tm, ATT_WIDTH), row),
            pl.BlockSpec((tm, LRU_WIDTH), row),
            pl.BlockSpec((D_MODEL, D_MODEL), fixed),
            pl.BlockSpec((1, D_MODEL), fixed),
            pl.BlockSpec((1, D_MODEL), fixed),
            pl.BlockSpec((2 * N_EXPERTS, D_MODEL), fixed),
            pl.BlockSpec((N_EXPERTS, 1), fixed),
        ],
        out_specs=[
            pl.BlockSpec((tm * LANE_TILES, LANES), row),
            pl.BlockSpec((8, tm), col),
            pl.BlockSpec((8, tm), col),
            pl.BlockSpec((8, tm), col),
            pl.BlockSpec((N_EXPERTS, LANES), fixed),
        ],
        out_shape=[
            jax.ShapeDtypeStruct((t * LANE_TILES, LANES), F32),
            jax.ShapeDtypeStruct((8, t), I32),
            jax.ShapeDtypeStruct((8, t), F32),
            jax.ShapeDtypeStruct((8, t), I32),
            jax.ShapeDtypeStruct((N_EXPERTS, LANES), F32),
        ],
        scratch_shapes=[pltpu.VMEM((N_EXPERTS, LANES), F32)],
        compiler_params=pltpu.CompilerParams(
            dimension_semantics=("arbitrary",), vmem_limit_bytes=VMEM_LIMIT),
        name="outproj_ln_router",
    )(x2, g0, b0, att, rec, wo_bf, g1, b1, w_router, b_router)


def _tile_rows(row):
    return pl.ds(pl.multiple_of(row * LANE_TILES, LANE_TILES), LANE_TILES)


def _invert_kernel(dest_ref, rt_ref, zero_ref, sem):
    i = pl.program_id(0)
    tm = dest_ref.shape[1] // TOP_K

    @pl.when(i == 0)
    def _():
        zero_ref[...] = jnp.zeros(zero_ref.shape, I32)
        clear = pltpu.make_async_copy(zero_ref, rt_ref, sem)
        clear.start()
        clear.wait()

    def body(t, carry):
        for kk in range(TOP_K):
            rt_ref[dest_ref[0, t * TOP_K + kk]] = i * tm + t
        return carry
    lax.fori_loop(0, tm, body, 0, unroll=4)


def _invert(dest3, n_rows):
    nt, _, per = dest3.shape
    return pl.pallas_call(
        _invert_kernel,
        grid=(nt,),
        in_specs=[pl.BlockSpec((None, 1, per), lambda i: (i, 0, 0), memory_space=pltpu.SMEM)],
        out_specs=pl.BlockSpec(memory_space=pltpu.SMEM),
        out_shape=jax.ShapeDtypeStruct((n_rows,), I32),
        scratch_shapes=[pltpu.VMEM((n_rows,), I32), pltpu.SemaphoreType.DMA(())],
        compiler_params=pltpu.CompilerParams(dimension_semantics=("arbitrary",)),
        name="moe_invert",
    )(dest3)


def _expert_kernel(first_ref, nblk_ref, rt_ref, wgu_ref, bg_ref, bl_ref, wd_ref, bd_ref,
                   x_hbm, ys_hbm, wt_ref, wg_ref, wl_ref, wdb_ref, xb_ref, xbuf, ybuf, xsem, ysem):
    e = pl.program_id(0)
    n_exp = pl.num_programs(0)
    d, f2 = wgu_ref.shape
    f = f2 // 2
    ch = XPOSE_CHUNK
    first = first_ref[e]
    nblk = nblk_ref[e]
    n_live = first_ref[n_exp - 1] + nblk_ref[n_exp - 1]
    n_blocks = ys_hbm.shape[0] // (ROW_BLOCK * LANE_TILES)
    block_rows = ROW_BLOCK * LANE_TILES

    def gather_start(g, slot):
        for r in range(ROW_BLOCK):
            pltpu.make_async_copy(
                x_hbm.at[_tile_rows(rt_ref[g * ROW_BLOCK + r]), :],
                xbuf.at[slot, pl.ds(r * LANE_TILES, LANE_TILES), :],
                xsem.at[slot]).start()

    def gather_wait(slot):
        pltpu.make_async_copy(x_hbm.at[pl.ds(0, block_rows), :], xbuf.at[slot],
                              xsem.at[slot]).wait()

    def y_copy(g, slot):
        start = pl.multiple_of(g * block_rows, block_rows)
        return pltpu.make_async_copy(ybuf.at[slot], ys_hbm.at[pl.ds(start, block_rows), :],
                                     ysem.at[slot])

    @pl.when(jnp.logical_and(nblk > 0, first == 0))
    def _():
        gather_start(0, 0)

    @pl.when(nblk > 0)
    def _():
        for j in range(d // LANES):
            rows = slice(j * LANES, (j + 1) * LANES)
            for c in range(f2 // ch):
                wt_ref[j, c * ch:(c + 1) * ch, :] = wgu_ref[rows, c * ch:(c + 1) * ch].T
            for c in range(f // ch):
                gate_rows = wt_ref[j, pl.ds(2 * c * ch, ch, stride=2), :]
                lin_rows = wt_ref[j, pl.ds(2 * c * ch + 1, ch, stride=2), :]
                wg_ref[rows, c * ch:(c + 1) * ch] = gate_rows.T.astype(BF16)
                wl_ref[rows, c * ch:(c + 1) * ch] = lin_rows.T.astype(BF16)
        wdb_ref[...] = wd_ref[...].astype(BF16)

    def block(b, carry):
        g = first + b
        for slot in range(2):
            @pl.when(g % 2 == slot)
            def _():
                block_body(g, slot)
        return carry

    def block_body(g, slot):
        @pl.when(g >= 2)
        def _():
            y_copy(g - 2, slot).wait()

        gather_wait(slot)
        xb_ref[...] = _load_row_tiles(xbuf.at[slot], ROW_BLOCK).astype(BF16)
        gather_start(jnp.minimum(g + 1, n_live - 1), 1 - slot)
        x = xb_ref[...]
        hg = jnp.dot(x, wg_ref[...], preferred_element_type=F32) + bg_ref[...]
        hl = jnp.dot(x, wl_ref[...], preferred_element_type=F32) + bl_ref[...]
        xg = jnp.minimum(hg, SWIGLU_LIMIT)
        xl = jnp.clip(hl, -SWIGLU_LIMIT, SWIGLU_LIMIT)
        act = xg * jax.nn.sigmoid(SWIGLU_ALPHA * xg) * (xl + 1.0)
        y = jnp.dot(act.astype(BF16), wdb_ref[...], preferred_element_type=F32) + bd_ref[...]
        _store_row_tiles(ybuf.at[slot], y)
        y_copy(g, slot).start()

    lax.fori_loop(0, nblk, block, 0)

    @pl.when(e == n_exp - 1)
    def _():
        gather_wait(n_live % 2)
        y_copy(0, 0).wait()
        y_copy(0, 1).wait()
        ybuf[0] = jnp.zeros(ybuf.shape[1:], F32)
        for t in range(N_EXPERTS):
            @pl.when(n_live + t < n_blocks)
            def _():
                y_copy(n_live + t, 0).start()
        for t in range(N_EXPERTS):
            @pl.when(n_live + t < n_blocks)
            def _():
                y_copy(n_live + t, 0).wait()


def _experts(first_blk, n_blk, row_tok, x1r, w_gu, bg, bl, w_down, bd):
    d = D_MODEL
    f2 = w_gu.shape[2]
    f = f2 // 2
    n_rows = row_tok.shape[0]
    wmap = lambda e, fb, nb_: (e, 0, 0)
    grid_spec = pltpu.PrefetchScalarGridSpec(
        num_scalar_prefetch=2,
        grid=(N_EXPERTS,),
        in_specs=[
            pl.BlockSpec(memory_space=pltpu.SMEM),
            pl.BlockSpec((None, d, f2), wmap),
            pl.BlockSpec((None, 1, f), wmap),
            pl.BlockSpec((None, 1, f), wmap),
            pl.BlockSpec((None, f, d), wmap),
            pl.BlockSpec((None, 1, d), wmap),
            pl.BlockSpec(memory_space=pl.ANY),
        ],
        out_specs=pl.BlockSpec(memory_space=pl.ANY),
        scratch_shapes=[
            pltpu.VMEM((d // LANES, f2, LANES), F32),
            pltpu.VMEM((d, f), BF16),
            pltpu.VMEM((d, f), BF16),
            pltpu.VMEM((f, d), BF16),
            pltpu.VMEM((ROW_BLOCK, d), BF16),
            pltpu.VMEM((2, ROW_BLOCK * LANE_TILES, LANES), F32),
            pltpu.VMEM((2, ROW_BLOCK * LANE_TILES, LANES), F32),
            pltpu.SemaphoreType.DMA((2,)),
            pltpu.SemaphoreType.DMA((2,)),
        ],
    )
    return pl.pallas_call(
        _expert_kernel,
        grid_spec=grid_spec,
        out_shape=jax.ShapeDtypeStruct((n_rows * LANE_TILES, LANES), F32),
        compiler_params=pltpu.CompilerParams(
            dimension_semantics=("arbitrary",), vmem_limit_bytes=VMEM_LIMIT_EXPERTS),
        name="moe_experts",
    )(first_blk, n_blk, row_tok, w_gu, bg, bl, w_down, bd, x1r)


def _combine_start(ys_hbm, idx_ref, buf_ref, sem, tm):
    def body(t, carry):
        for kk in range(TOP_K):
            pltpu.make_async_copy(ys_hbm.at[_tile_rows(idx_ref[0, t * TOP_K + kk]), :],
                                  buf_ref.at[kk, _tile_rows(t), :], sem).start()
        return carry
    lax.fori_loop(0, tm, body, 0, unroll=2)


def _combine_wait(ys_hbm, buf_ref, sem, tm):
    for kk in range(TOP_K):
        pltpu.make_async_copy(ys_hbm.at[pl.ds(0, tm * LANE_TILES), :], buf_ref.at[kk], sem).wait()


def _combine_kernel(idx_cur, idx_nxt, x1_ref, g4_ref, g2_ref, b2_ref, ys_hbm,
                    o_ref, buf_ref, sem):
    i = pl.program_id(0)
    n = pl.num_programs(0)
    slot = i % 2
    tm = o_ref.shape[0]

    @pl.when(i == 0)
    def _():
        _combine_start(ys_hbm, idx_cur, buf_ref.at[0], sem.at[0], tm)

    @pl.when(i + 1 < n)
    def _():
        _combine_start(ys_hbm, idx_nxt, buf_ref.at[1 - slot], sem.at[1 - slot], tm)

    _combine_wait(ys_hbm, buf_ref.at[slot], sem.at[slot], tm)
    g4 = g4_ref[...]
    ffn = g4[:, 0:1] * _load_row_tiles(buf_ref.at[slot, 0], tm)
    for kk in range(1, TOP_K):
        ffn = ffn + g4[:, kk:kk + 1] * _load_row_tiles(buf_ref.at[slot, kk], tm)
    x1 = _load_row_tiles(x1_ref, tm)
    o_ref[...] = _layer_norm(ALPHA * x1 + ffn, g2_ref[...], b2_ref[...])


def _combine(dest3, x1r, g4, g2, b2, ys):
    d = D_MODEL
    t = x1r.shape[0] // LANE_TILES
    tm = TM_COMB
    nt = t // tm
    row = lambda i: (i, 0)
    fixed = lambda i: (0, 0)
    return pl.pallas_call(
        _combine_kernel,
        grid=(nt,),
        in_specs=[
            pl.BlockSpec((None, 1, tm * TOP_K), lambda i: (i, 0, 0), memory_space=pltpu.SMEM),
            pl.BlockSpec((None, 1, tm * TOP_K), lambda i: (jnp.minimum(i + 1, nt - 1), 0, 0),
                         memory_space=pltpu.SMEM),
            pl.BlockSpec((tm * LANE_TILES, LANES), row),
            pl.BlockSpec((tm, TOP_K), row),
            pl.BlockSpec((1, d), fixed),
            pl.BlockSpec((1, d), fixed),
            pl.BlockSpec(memory_space=pl.ANY),
        ],
        out_specs=pl.BlockSpec((tm, d), row),
        out_shape=jax.ShapeDtypeStruct((t, d), F32),
        scratch_shapes=[pltpu.VMEM((2, TOP_K, tm * LANE_TILES, LANES), F32),
                        pltpu.SemaphoreType.DMA((2,))],
        compiler_params=pltpu.CompilerParams(
            dimension_semantics=("arbitrary",), vmem_limit_bytes=VMEM_LIMIT),
        name="moe_combine_ln",
    )(dest3, dest3, x1r, g4, g2, b2, ys)


def kernel(x, ln0_g, ln0_b, w_in, conv_w, conv_b, w_rg_a, b_rg_a, w_rg_x, b_rg_x, lru_lambda, lam_q1, lam_k1, lam_q2, lam_k2, subln_g, w_out, ln1_g, ln1_b, w_router, b_router, w_gu, b_gu, w_down, b_down, ln2_g, ln2_b):
    bsz, seq, d = x.shape
    t = bsz * seq
    x2 = x.reshape(t, d)
    g0 = ln0_g.reshape(1, d)
    b0 = ln0_b.reshape(1, d)
    l = 0

    w_in_bf = w_in[l].astype(BF16)
    w_out_bf = w_out[l].astype(BF16)
    wa = jax.scipy.linalg.block_diag(*[w_rg_a[l, n] for n in range(LRU_BLOCKS)])
    wx = jax.scipy.linalg.block_diag(*[w_rg_x[l, n] for n in range(LRU_BLOCKS)])
    w_gate_bf = jnp.concatenate([wa, wx], axis=1).astype(BF16)
    b_gate = jnp.concatenate([b_rg_a[l].reshape(1, -1), b_rg_x[l].reshape(1, -1)], axis=1)
    lamvec = jnp.stack([lam_q1[l], lam_k1[l], lam_q2[l], lam_k2[l]]).astype(F32)
    slopes = jnp.asarray([2.0 ** (-8.0 * (i + 1) / ATT_HEADS) for i in range(ATT_HEADS)], F32)
    slopes = jnp.broadcast_to(slopes[:, None, None], (ATT_HEADS, 1, LANES))
    bg = b_gu[l, :, None, 0::2]
    bl = b_gu[l, :, None, 1::2]
    bd = b_down[l][:, None, :]

    qt, k, vt, xl, gl = _inproj(x, g0, b0, w_in_bf)
    att = _attention(qt, k, vt, slopes, lamvec, subln_g[l].reshape(-1, 1))
    rec = _lru(xl, gl, conv_w[l], conv_b[l].reshape(1, -1), w_gate_bf, b_gate,
               lru_lambda[l].reshape(1, -1))
    wr_t = w_router[l].T.astype(F32)
    wr_hi = wr_t.astype(BF16)
    wr_lo = (wr_t - wr_hi.astype(F32)).astype(BF16)
    x1r, e4t, g4t, r4t, cnt = _outproj_router(
        x2, g0, b0, att.reshape(t, -1), rec.reshape(t, -1), w_out_bf,
        ln1_g[l].reshape(1, d), ln1_b[l].reshape(1, d),
        jnp.concatenate([wr_hi, wr_lo], axis=0), b_router[l].reshape(-1, 1).astype(F32))
    e4 = e4t[0:TOP_K].T
    g4 = g4t[0:TOP_K].T
    r4 = r4t[0:TOP_K].T

    n_assign = t * TOP_K
    nb = (n_assign + N_EXPERTS * (ROW_BLOCK - 1)) // ROW_BLOCK + 1
    counts = cnt[:, 0].astype(I32)
    padded = (counts + ROW_BLOCK - 1) // ROW_BLOCK * ROW_BLOCK
    pends = jnp.cumsum(padded)
    pstarts = pends - padded
    dest = pstarts[e4] + r4
    row_tok = _invert(dest.reshape(t // TM_INV, 1, TM_INV * TOP_K), nb * ROW_BLOCK)
    ys = _experts((pstarts // ROW_BLOCK).astype(I32), (padded // ROW_BLOCK).astype(I32),
                  row_tok, x1r, w_gu[l], bg, bl, w_down[l], bd)
    out = _combine(dest.reshape(t // TM_COMB, 1, TM_COMB * TOP_K), x1r, g4,
                   ln2_g[l].reshape(1, d), ln2_b[l].reshape(1, d), ys)
    return out.reshape(bsz, seq, d)
```

```python
import math

import jax
import jax.numpy as jnp
from jax import lax
from jax.experimental import pallas as pl
from jax.experimental.pallas import tpu as pltpu

F32 = jnp.float32
BF16 = jnp.bfloat16
I32 = jnp.int32

D_MODEL = 1024
ATT_WIDTH = 512
ATT_HEAD_DIM = 64
ATT_HEADS = 4
VAL_DIM = 2 * ATT_HEAD_DIM
LRU_WIDTH = 512
LRU_BLOCKS = 8
LRU_C = 8.0
CONV_WIDTH = 4
CHUNK_SHIFT = 6
N_EXPERTS = 32
TOP_K = 4
D_EXPERT = 1024
SWIGLU_LIMIT = 7.0
SWIGLU_ALPHA = 1.702
DEPTH = 1
ALPHA = (2.0 * DEPTH) ** 0.25
LN_EPS = 1e-5
SUBLN_EPS = 1e-5
LAM_INIT = 0.8 - 0.6 * math.exp(-0.3 * 0)
NEG_BIG = -1e30
LOG2E = math.log2(math.e)

TM_PROJ = 512
TQ = 1024
TKB = 128
SW = 256
TC_LRU = 512
SCAN_GROUP = 8
ROW_BLOCK = 256
TM_COMB = 128
TM_INV = 2048
XPOSE_CHUNK = 256

VMEM_LIMIT = 48 * 1024 * 1024
VMEM_LIMIT_EXPERTS = 56 * 1024 * 1024

LANES = 128
LANE_TILES = D_MODEL // LANES


def _store_row_tiles(ref, val):
    n = val.shape[0]
    for j in range(LANE_TILES):
        ref[pl.ds(j, n, stride=LANE_TILES), :] = val[:, j * LANES:(j + 1) * LANES]


def _load_row_tiles(ref, n):
    return jnp.concatenate(
        [ref[pl.ds(j, n, stride=LANE_TILES), :] for j in range(LANE_TILES)], axis=1)


def _layer_norm(x, g, b):
    mu = jnp.mean(x, axis=-1, keepdims=True)
    xc = x - mu
    var = jnp.mean(xc * xc, axis=-1, keepdims=True)
    return xc * lax.rsqrt(var + LN_EPS) * g + b


def _inproj_kernel(x_ref, g_ref, b_ref, w_ref, qt_ref, k_ref, vt_ref, xl_ref, gl_ref):
    x0 = _layer_norm(x_ref[...], g_ref[...], b_ref[...])
    p = jnp.dot(x0.astype(BF16), w_ref[...], preferred_element_type=F32)
    a = ATT_WIDTH
    qt_ref[...] = (p[:, 0:a] * (ATT_HEAD_DIM ** -0.5 * LOG2E)).T.astype(BF16)
    k_ref[...] = p[:, a:2 * a].astype(BF16)
    vt = p[:, 2 * a:3 * a].T.astype(BF16)
    for h in range(ATT_HEADS):
        for c in range(vt_ref.shape[1]):
            vt_ref[h, c] = vt[h * VAL_DIM:(h + 1) * VAL_DIM, c * TKB:(c + 1) * TKB]
    xl_ref[...] = p[:, 3 * a:3 * a + LRU_WIDTH]
    gl_ref[...] = p[:, 3 * a + LRU_WIDTH:]


def _inproj(x, g, b, w_bf):
    bsz, seq, _ = x.shape
    n = w_bf.shape[1]
    tm = TM_PROJ
    row = lambda bi, i: (bi, i, 0)
    col = lambda bi, i: (bi, 0, i)
    fixed = lambda bi, i: (0, 0)
    return pl.pallas_call(
        _inproj_kernel,
        grid=(bsz, seq // tm),
        in_specs=[
            pl.BlockSpec((None, tm, D_MODEL), row),
            pl.BlockSpec((1, D_MODEL), fixed),
            pl.BlockSpec((1, D_MODEL), fixed),
            pl.BlockSpec((D_MODEL, n), fixed),
        ],
        out_specs=[
            pl.BlockSpec((None, ATT_WIDTH, tm), col),
            pl.BlockSpec((None, tm, ATT_WIDTH), row),
            pl.BlockSpec((None, ATT_HEADS, tm // TKB, VAL_DIM, TKB),
                         lambda bi, i: (bi, 0, i, 0, 0)),
            pl.BlockSpec((None, tm, LRU_WIDTH), row),
            pl.BlockSpec((None, tm, LRU_WIDTH), row),
        ],
        out_shape=[
            jax.ShapeDtypeStruct((bsz, ATT_WIDTH, seq), BF16),
            jax.ShapeDtypeStruct((bsz, seq, ATT_WIDTH), BF16),
            jax.ShapeDtypeStruct((bsz, ATT_HEADS, seq // TKB, VAL_DIM, TKB), BF16),
            jax.ShapeDtypeStruct((bsz, seq, LRU_WIDTH), F32),
            jax.ShapeDtypeStruct((bsz, seq, LRU_WIDTH), F32),
        ],
        compiler_params=pltpu.CompilerParams(
            dimension_semantics=("parallel", "parallel"), vmem_limit_bytes=VMEM_LIMIT),
        name="ln_inproj",
    )(x, g, b, w_bf)


def _attn_kernel(qt_ref, k_ref, vt_ref, slope_ref, lam_ref, g_ref,
                 o_ref, q2_ref, kx_ref, qbias_ref, diag_ref, *stat_refs):
    qi = pl.program_id(2)
    tq = qt_ref.shape[1]
    n_strips = 2 * tq // SW
    blocks_per_q = tq // TKB
    cl = slope_ref[0:1, 0:1] * LOG2E

    @pl.when(qi == 0)
    def _():
        c = lax.broadcasted_iota(I32, (TKB, LANES), 0)
        lane = lax.broadcasted_iota(I32, (TKB, LANES), 1)
        w = c.astype(F32) * cl
        w_hi = w.astype(BF16).astype(F32)
        w_mid = (w - w_hi).astype(BF16).astype(F32)
        w_lo = w - w_hi - w_mid
        kx_ref[...] = jnp.where(
            lane == 0, w_hi, jnp.where(lane == 1, w_mid, jnp.where(lane == 2, w_lo, 0.0))
        ).astype(BF16)
        r1 = lax.broadcasted_iota(I32, (1, tq), 1)
        qbias_ref[...] = r1.astype(F32) * (-cl)
        ck0 = lax.broadcasted_iota(I32, (TKB, tq), 0)
        r = lax.broadcasted_iota(I32, (TKB, tq), 1)
        for kd in range(blocks_per_q):
            ck = ck0 + kd * TKB
            allowed = (ck >> CHUNK_SHIFT) <= (r >> CHUNK_SHIFT)
            bias = (jnp.abs(r - ck) + ck0).astype(F32) * (-cl)
            diag_ref[kd] = jnp.where(allowed, bias, NEG_BIG)

    m_refs = stat_refs[0:n_strips]
    l_refs = stat_refs[n_strips:2 * n_strips]
    acc_refs = stat_refs[2 * n_strips:3 * n_strips]
    for j in range(n_strips):
        m_refs[j][...] = jnp.full(m_refs[j].shape, NEG_BIG, F32)
        l_refs[j][...] = jnp.zeros(l_refs[j].shape, F32)
        acc_refs[j][...] = jnp.zeros(acc_refs[j].shape, F32)
    qt = qt_ref[...]
    frow = lax.broadcasted_iota(I32, qt.shape, 0)
    zero = jnp.zeros_like(qt)
    q2_ref[0:VAL_DIM, 0:tq] = jnp.where(frow < ATT_HEAD_DIM, qt, zero)
    q2_ref[0:VAL_DIM, tq:2 * tq] = jnp.where(frow >= ATT_HEAD_DIM, qt, zero)
    xrow = lax.broadcasted_iota(I32, (VAL_DIM, 2 * tq), 0)
    q2_ref[VAL_DIM:2 * VAL_DIM, :] = jnp.where(xrow < 3, 1.0, 0.0).astype(BF16)

    def load_keys(kb):
        k_blk = k_ref[pl.ds(pl.multiple_of(kb * TKB, TKB), TKB), :]
        return jnp.concatenate([k_blk, kx_ref[...]], axis=1), vt_ref[kb]

    def strip(j, k_ext, vt_blk, sc_bias, shift):
        ls = slice(j * SW, (j + 1) * SW)
        sc = jnp.dot(k_ext, q2_ref[:, ls], preferred_element_type=F32)
        if sc_bias is not None:
            sc = sc + sc_bias
        m_prev = m_refs[j][...]
        m_new = jnp.maximum(m_prev, jnp.max(sc, axis=0, keepdims=True) + shift)
        alpha = jnp.exp2(m_prev - m_new)
        p = jnp.exp2(sc - (m_new - shift))
        l_refs[j][...] = alpha * l_refs[j][...] + jnp.sum(p, axis=0, keepdims=True)
        pv = jnp.dot(vt_blk, p.astype(BF16), preferred_element_type=F32)
        acc_refs[j][...] = alpha * acc_refs[j][...] + pv
        m_refs[j][...] = m_new

    def off_diag_group(g, carry):
        for kd in range(blocks_per_q):
            kb = g * blocks_per_q + kd
            k_ext, vt_blk = load_keys(kb)
            block_shift = jnp.full((1, 1), qi * tq - kb * TKB, I32).astype(F32) * (-cl)
            for j in range(n_strips):
                jq = j % (tq // SW)
                strip(j, k_ext, vt_blk, None,
                      qbias_ref[:, jq * SW:(jq + 1) * SW] + block_shift)
        return carry
    lax.fori_loop(0, qi, off_diag_group, 0)

    for kd in range(blocks_per_q):
        k_ext, vt_blk = load_keys(qi * blocks_per_q + kd)
        for j in range(n_strips):
            jq = j % (tq // SW)
            cols = slice(jq * SW, (jq + 1) * SW)
            if (kd + 1) * TKB <= jq * SW:
                strip(j, k_ext, vt_blk, None, qbias_ref[:, cols] + cl * (kd * TKB))
            elif (jq + 1) * SW > kd * TKB:
                strip(j, k_ext, vt_blk, diag_ref[kd, :, cols], 0.0)

    lv = lam_ref[...]
    lam = (jnp.exp(jnp.sum(lv[0:1] * lv[1:2], axis=-1, keepdims=True))
           - jnp.exp(jnp.sum(lv[2:3] * lv[3:4], axis=-1, keepdims=True)) + LAM_INIT)
    o2 = [acc_refs[j][...] / l_refs[j][...] for j in range(n_strips)]
    half = n_strips // 2
    o = jnp.concatenate([o2[j] - lam * o2[half + j] for j in range(half)], axis=1)
    o = o * lax.rsqrt(jnp.mean(o * o, axis=0, keepdims=True) + SUBLN_EPS)
    o = o * (g_ref[...] * (1.0 - LAM_INIT))
    o_ref[...] = o.T.astype(o_ref.dtype)


def _attention(qt, k, vt, slopes, lamvec, subln_g_col):
    b, s, _ = k.shape
    nkb = s // TKB
    n_strips = 2 * TQ // SW
    return pl.pallas_call(
        _attn_kernel,
        grid=(b, ATT_HEADS, s // TQ),
        in_specs=[
            pl.BlockSpec((None, VAL_DIM, TQ), lambda bi, h, qi: (bi, h, qi)),
            pl.BlockSpec((None, s, VAL_DIM), lambda bi, h, qi: (bi, 0, h)),
            pl.BlockSpec((None, None, nkb, VAL_DIM, TKB), lambda bi, h, qi: (bi, h, 0, 0, 0)),
            pl.BlockSpec((None, 1, LANES), lambda bi, h, qi: (h, 0, 0)),
            pl.BlockSpec((4, ATT_HEAD_DIM), lambda bi, h, qi: (0, 0)),
            pl.BlockSpec((VAL_DIM, 1), lambda bi, h, qi: (0, 0)),
        ],
        out_specs=pl.BlockSpec((None, TQ, VAL_DIM), lambda bi, h, qi: (bi, qi, h)),
        out_shape=jax.ShapeDtypeStruct((b, s, ATT_WIDTH), BF16),
        scratch_shapes=[
            pltpu.VMEM((2 * VAL_DIM, 2 * TQ), BF16),
            pltpu.VMEM((TKB, LANES), BF16),
            pltpu.VMEM((1, TQ), F32),
            pltpu.VMEM((TQ // TKB, TKB, TQ), F32),
        ] +[pltpu.VMEM((1, SW), F32)] * (2 * n_strips) + [pltpu.VMEM((VAL_DIM, SW), F32)] * n_strips,
        compiler_params=pltpu.CompilerParams(
            dimension_semantics=("parallel", "parallel", "arbitrary"),
            vmem_limit_bytes=VMEM_LIMIT),
        name="diff_attention",
    )(qt, k, vt, slopes, lamvec, subln_g_col)


def _lru_kernel(xl_ref, gl_ref, cw_ref, cb_ref, wg_ref, bg_ref, lam_ref,
                o_ref, ext_ref, h_ref):
    c = pl.program_id(1)
    tc = xl_ref.shape[0]
    w = LRU_WIDTH

    @pl.when(c == 0)
    def _():
        ext_ref[0:8, :] = jnp.zeros((8, w), F32)
        h_ref[...] = jnp.zeros(h_ref.shape, F32)

    ext_ref[8:8 + tc, :] = xl_ref[...]
    cw = cw_ref[...]
    xc = cb_ref[...] + cw[0:1] * ext_ref[5:5 + tc, :]
    for j in range(1, CONV_WIDTH):
        xc = xc + cw[j:j + 1] * ext_ref[5 + j:5 + j + tc, :]
    tail = ext_ref[tc:tc + 8, :]

    gates = jnp.dot(xc.astype(BF16), wg_ref[...], preferred_element_type=F32) + bg_ref[...]
    r = jax.nn.sigmoid(gates[:, 0:w])
    ig = jax.nn.sigmoid(gates[:, w:2 * w])
    z = -lam_ref[...]
    softplus = jnp.maximum(z, 0.0) + jnp.log1p(jnp.exp(-jnp.abs(z)))
    log_a = (-LRU_C) * r * softplus
    a = jnp.exp(log_a)
    u = jnp.sqrt(-jnp.tanh(log_a) * (a * a + 1.0)) * (ig * xc)

    ng = tc // SCAN_GROUP
    a = a.reshape(ng, SCAN_GROUP, w)
    u = u.reshape(ng, SCAN_GROUP, w)
    pos = lax.broadcasted_iota(I32, a.shape, 1)
    d = 1
    while d < SCAN_GROUP:
        valid = pos >= d
        u = jnp.where(valid, a * pltpu.roll(u, d, axis=1) + u, u)
        a = jnp.where(valid, a * pltpu.roll(a, d, axis=1), a)
        d *= 2
    a = a.reshape(tc, w)
    u = u.reshape(tc, w)
    h_in = h_ref[0:1, :]
    groups = []
    for g in range(tc // SCAN_GROUP):
        rows = slice(g * SCAN_GROUP, (g + 1) * SCAN_GROUP)
        hg = u[rows] + a[rows] * h_in
        groups.append(hg)
        h_in = hg[SCAN_GROUP - 1:SCAN_GROUP, :]
    h = jnp.concatenate(groups, axis=0)

    gl = gl_ref[...]
    gelu = 0.5 * gl * (1.0 + jnp.tanh(math.sqrt(2.0 / math.pi) * (gl + 0.044715 * (gl * gl * gl))))
    o_ref[...] = (h * gelu).astype(o_ref.dtype)

    h_ref[0:1, :] = h[tc - 1:tc, :]
    ext_ref[0:8, :] = tail


def _lru(xl, gl, conv_w, conv_b, w_gate_bf, b_gate, lam):
    b, s, w = xl.shape
    tc = TC_LRU
    blk = lambda bi, c: (bi, c, 0)
    fixed = lambda bi, c: (0, 0)
    return pl.pallas_call(
        _lru_kernel,
        grid=(b, s // tc),
        in_specs=[
            pl.BlockSpec((None, tc, w), blk),
            pl.BlockSpec((None, tc, w), blk),
            pl.BlockSpec((CONV_WIDTH, w), fixed),
            pl.BlockSpec((1, w), fixed),
            pl.BlockSpec((w, 2 * w), fixed),
            pl.BlockSpec((1, 2 * w), fixed),
            pl.BlockSpec((1, w), fixed),
        ],
        out_specs=pl.BlockSpec((None, tc, w), blk),
        out_shape=jax.ShapeDtypeStruct((b, s, w), BF16),
        scratch_shapes=[pltpu.VMEM((tc + 8, w), F32), pltpu.VMEM((8, w), F32)],
        compiler_params=pltpu.CompilerParams(
            dimension_semantics=("parallel", "arbitrary"), vmem_limit_bytes=VMEM_LIMIT),
        name="rg_lru",
    )(xl, gl, conv_w, conv_b, w_gate_bf, b_gate, lam)


def _outproj_kernel(x_ref, g0_ref, b0_ref, att_ref, rec_ref, wo_ref, g1_ref, b1_ref,
                    wr_ref, br_ref, x1_ref, e4_ref, g4_ref, r4_ref, cnt_ref, carry_ref):
    i = pl.program_id(0)
    tm = x_ref.shape[0]

    @pl.when(i == 0)
    def _():
        carry_ref[...] = jnp.zeros(carry_ref.shape, F32)

    x0 = _layer_norm(x_ref[...], g0_ref[...], b0_ref[...])
    mixed = (jnp.dot(att_ref[...], wo_ref[0:ATT_WIDTH, :], preferred_element_type=F32)
             + jnp.dot(rec_ref[...], wo_ref[ATT_WIDTH:, :], preferred_element_type=F32))
    x1 = _layer_norm(ALPHA * x0 + mixed, g1_ref[...], b1_ref[...])
    _store_row_tiles(x1_ref, x1)

    x1_hi = x1.astype(BF16)
    x1_lo = (x1 - x1_hi.astype(F32)).astype(BF16)
    nt = (((1,), (1,)), ((), ()))
    ne = N_EXPERTS
    hi_terms = lax.dot_general(wr_ref[...], x1_hi, nt, preferred_element_type=F32)
    lo_term = lax.dot_general(wr_ref[0:ne, :], x1_lo, nt, preferred_element_type=F32)
    logits = hi_terms[0:ne] + hi_terms[ne:2 * ne] + lo_term + br_ref[...]

    erow = lax.broadcasted_iota(I32, logits.shape, 0)
    work = logits
    vals, idxs, hots = [], [], []
    for _ in range(TOP_K):
        mx = jnp.max(work, axis=0, keepdims=True)
        idx = jnp.min(jnp.where(work == mx, erow, ne), axis=0, keepdims=True)
        hot = erow == idx
        vals.append(mx)
        idxs.append(idx)
        hots.append(hot)
        work = jnp.where(hot, -jnp.inf, work)
    exps = [jnp.exp(v - vals[0]) for v in vals]
    denom = exps[0] + exps[1] + exps[2] + exps[3]

    mask = jnp.zeros(logits.shape, F32)
    for hot in hots:
        mask = mask + hot.astype(F32)
    tr = lax.broadcasted_iota(I32, (tm, tm), 0)
    tc = lax.broadcasted_iota(I32, (tm, tm), 1)
    earlier = jnp.where(tr < tc, 1.0, 0.0).astype(BF16)
    carry = carry_ref[:, 0:1]
    excl = jnp.dot(mask.astype(BF16), earlier, preferred_element_type=F32) + carry
    ranks = [jnp.sum(jnp.where(hot, excl, 0.0), axis=0, keepdims=True) for hot in hots]
    pad_i = jnp.zeros((8 - TOP_K, tm), I32)
    pad_f = jnp.zeros((8 - TOP_K, tm), F32)
    e4_ref[...] = jnp.concatenate(idxs + [pad_i], axis=0)
    g4_ref[...] = jnp.concatenate([ex / denom for ex in exps] + [pad_f], axis=0)
    r4_ref[...] = jnp.concatenate([r.astype(I32) for r in ranks] + [pad_i], axis=0)
    total = carry + jnp.sum(mask, axis=1, keepdims=True)
    carry_ref[...] = jnp.broadcast_to(total, carry_ref.shape)
    cnt_ref[...] = jnp.broadcast_to(total, cnt_ref.shape)


def _outproj_router(x2, g0, b0, att, rec, wo_bf, g1, b1, w_router, b_router):
    t = x2.shape[0]
    tm = TM_PROJ
    row = lambda i: (i, 0)
    col = lambda i: (0, i)
    fixed = lambda i: (0, 0)
    return pl.pallas_call(
        _outproj_kernel,
        grid=(t // tm,),
        in_specs=[
            pl.BlockSpec((tm, D_MODEL), row),
            pl.BlockSpec((1, D_MODEL), fixed),
            pl.BlockSpec((1, D_MODEL), fixed),
            pl.BlockSpec((tm, ATT_WIDTH), row),
            pl.BlockSpec((tm, LRU_WIDTH), row),
            pl.BlockSpec((D_MODEL, D_MODEL), fixed),
            pl.BlockSpec((1, D_MODEL), fixed),
            pl.BlockSpec((1, D_MODEL), fixed),
            pl.BlockSpec((2 * N_EXPERTS, D_MODEL), fixed),
            pl.BlockSpec((N_EXPERTS, 1), fixed),
        ],
        out_specs=[
            pl.BlockSpec((tm * LANE_TILES, LANES), row),
            pl.BlockSpec((8, tm), col),
            pl.BlockSpec((8, tm), col),
            pl.BlockSpec((8, tm), col),
            pl.BlockSpec((N_EXPERTS, LANES), fixed),
        ],
        out_shape=[
            jax.ShapeDtypeStruct((t * LANE_TILES, LANES), F32),
            jax.ShapeDtypeStruct((8, t), I32),
            jax.ShapeDtypeStruct((8, t), F32),
            jax.ShapeDtypeStruct((8, t), I32),
            jax.ShapeDtypeStruct((N_EXPERTS, LANES), F32),
        ],
        scratch_shapes=[pltpu.VMEM((N_EXPERTS, LANES), F32)],
        compiler_params=pltpu.CompilerParams(
            dimension_semantics=("arbitrary",), vmem_limit_bytes=VMEM_LIMIT),
        name="outproj_ln_router",
    )(x2, g0, b0, att, rec, wo_bf, g1, b1, w_router, b_router)


def _tile_rows(row):
    return pl.ds(pl.multiple_of(row * LANE_TILES, LANE_TILES), LANE_TILES)


def _invert_kernel(dest_ref, rt_ref, zero_ref, sem):
    i = pl.program_id(0)
    tm = dest_ref.shape[1] // TOP_K

    @pl.when(i == 0)
    def _():
        zero_ref[...] = jnp.zeros(zero_ref.shape, I32)
        clear = pltpu.make_async_copy(zero_ref, rt_ref, sem)
        clear.start()
        clear.wait()

    def body(t, carry):
        for kk in range(TOP_K):
            rt_ref[dest_ref[0, t * TOP_K + kk]] = i * tm + t
        return carry
    lax.fori_loop(0, tm, body, 0, unroll=4)


def _invert(dest3, n_rows):
    nt, _, per = dest3.shape
    return pl.pallas_call(
        _invert_kernel,
        grid=(nt,),
        in_specs=[pl.BlockSpec((None, 1, per), lambda i: (i, 0, 0), memory_space=pltpu.SMEM)],
        out_specs=pl.BlockSpec(memory_space=pltpu.SMEM),
        out_shape=jax.ShapeDtypeStruct((n_rows,), I32),
        scratch_shapes=[pltpu.VMEM((n_rows,), I32), pltpu.SemaphoreType.DMA(())],
        compiler_params=pltpu.CompilerParams(dimension_semantics=("arbitrary",)),
        name="moe_invert",
    )(dest3)


def _expert_kernel(first_ref, nblk_ref, rt_ref, wgu_ref, bg_ref, bl_ref, wd_ref, bd_ref,
                   x_hbm, ys_hbm, wg_ref, wl_ref, wdb_ref, xb_ref, xbuf, ybuf, xsem, ysem):
    e = pl.program_id(0)
    n_exp = pl.num_programs(0)
    d, f2 = wgu_ref.shape
    f = f2 // 2
    ch = XPOSE_CHUNK
    first = first_ref[e]
    nblk = nblk_ref[e]
    n_live = first_ref[n_exp - 1] + nblk_ref[n_exp - 1]
    n_blocks = ys_hbm.shape[0] // (ROW_BLOCK * LANE_TILES)
    block_rows = ROW_BLOCK * LANE_TILES

    def gather_start(g, slot):
        for r in range(ROW_BLOCK):
            pltpu.make_async_copy(
                x_hbm.at[_tile_rows(rt_ref[g * ROW_BLOCK + r]), :],
                xbuf.at[slot, pl.ds(r * LANE_TILES, LANE_TILES), :],
                xsem.at[slot]).start()

    def gather_wait(slot):
        pltpu.make_async_copy(x_hbm.at[pl.ds(0, block_rows), :], xbuf.at[slot],
                              xsem.at[slot]).wait()

    def y_copy(g, slot):
        start = pl.multiple_of(g * block_rows, block_rows)
        return pltpu.make_async_copy(ybuf.at[slot], ys_hbm.at[pl.ds(start, block_rows), :],
                                     ysem.at[slot])

    @pl.when(jnp.logical_and(nblk > 0, first == 0))
    def _():
        gather_start(0, 0)

    @pl.when(nblk > 0)
    def _():
        src = lax.broadcasted_iota(I32, (2 * ch, 2 * ch), 0)
        dst = lax.broadcasted_iota(I32, (2 * ch, 2 * ch), 1)
        wanted = jnp.where(dst < ch, 2 * dst, 2 * (dst - ch) + 1)
        select = jnp.where(src == wanted, 1.0, 0.0).astype(BF16)
        for c in range(f // ch):
            w_c = wgu_ref[:, 2 * c * ch:2 * (c + 1) * ch].astype(BF16)
            split = jnp.dot(w_c, select, preferred_element_type=F32)
            wg_ref[:, c * ch:(c + 1) * ch] = split[:, 0:ch].astype(BF16)
            wl_ref[:, c * ch:(c + 1) * ch] = split[:, ch:2 * ch].astype(BF16)
        wdb_ref[...] = wd_ref[...].astype(BF16)

    def block(b, carry):
        g = first + b
        for slot in range(2):
            @pl.when(g % 2 == slot)
            def _():
                block_body(g, slot)
        return carry

    def block_body(g, slot):
        @pl.when(g >= 2)
        def _():
            y_copy(g - 2, slot).wait()

        gather_wait(slot)
        xb_ref[...] = _load_row_tiles(xbuf.at[slot], ROW_BLOCK).astype(BF16)
        gather_start(jnp.minimum(g + 1, n_live - 1), 1 - slot)
        x = xb_ref[...]
        hg = jnp.dot(x, wg_ref[...], preferred_element_type=F32) + bg_ref[...]
        hl = jnp.dot(x, wl_ref[...], preferred_element_type=F32) + bl_ref[...]
        xg = jnp.minimum(hg, SWIGLU_LIMIT)
        xl = jnp.clip(hl, -SWIGLU_LIMIT, SWIGLU_LIMIT)
        act = xg * jax.nn.sigmoid(SWIGLU_ALPHA * xg) * (xl + 1.0)
        y = jnp.dot(act.astype(BF16), wdb_ref[...], preferred_element_type=F32) + bd_ref[...]
        _store_row_tiles(ybuf.at[slot], y)
        y_copy(g, slot).start()

    lax.fori_loop(0, nblk, block, 0)

    @pl.when(e == n_exp - 1)
    def _():
        gather_wait(n_live % 2)
        y_copy(0, 0).wait()
        y_copy(0, 1).wait()
        ybuf[0] = jnp.zeros(ybuf.shape[1:], F32)
        for t in range(N_EXPERTS):
            @pl.when(n_live + t < n_blocks)
            def _():
                y_copy(n_live + t, 0).start()
        for t in range(N_EXPERTS):
            @pl.when(n_live + t < n_blocks)
            def _():
                y_copy(n_live + t, 0).wait()


def _experts(first_blk, n_blk, row_tok, x1r, w_gu, bg, bl, w_down, bd):
    d = D_MODEL
    f2 = w_gu.shape[2]
    f = f2 // 2
    n_rows = row_tok.shape[0]
    wmap = lambda e, fb, nb_: (e, 0, 0)
    grid_spec = pltpu.PrefetchScalarGridSpec(
        num_scalar_prefetch=2,
        grid=(N_EXPERTS,),
        in_specs=[
            pl.BlockSpec(memory_space=pltpu.SMEM),
            pl.BlockSpec((None, d, f2), wmap),
            pl.BlockSpec((None, 1, f), wmap),
            pl.BlockSpec((None, 1, f), wmap),
            pl.BlockSpec((None, f, d), wmap),
            pl.BlockSpec((None, 1, d), wmap),
            pl.BlockSpec(memory_space=pl.ANY),
        ],
        out_specs=pl.BlockSpec(memory_space=pl.ANY),
        scratch_shapes=[
            pltpu.VMEM((d, f), BF16),
            pltpu.VMEM((d, f), BF16),
            pltpu.VMEM((f, d), BF16),
            pltpu.VMEM((ROW_BLOCK, d), BF16),
            pltpu.VMEM((2, ROW_BLOCK * LANE_TILES, LANES), F32),
            pltpu.VMEM((2, ROW_BLOCK * LANE_TILES, LANES), F32),
            pltpu.SemaphoreType.DMA((2,)),
            pltpu.SemaphoreType.DMA((2,)),
        ],
    )
    return pl.pallas_call(
        _expert_kernel,
        grid_spec=grid_spec,
        out_shape=jax.ShapeDtypeStruct((n_rows * LANE_TILES, LANES), F32),
        compiler_params=pltpu.CompilerParams(
            dimension_semantics=("arbitrary",), vmem_limit_bytes=VMEM_LIMIT_EXPERTS),
        name="moe_experts",
    )(first_blk, n_blk, row_tok, w_gu, bg, bl, w_down, bd, x1r)


def _combine_start(ys_hbm, idx_ref, buf_ref, sem, tm):
    def body(t, carry):
        for kk in range(TOP_K):
            pltpu.make_async_copy(ys_hbm.at[_tile_rows(idx_ref[0, t * TOP_K + kk]), :],
                                  buf_ref.at[kk, _tile_rows(t), :], sem).start()
        return carry
    lax.fori_loop(0, tm, body, 0, unroll=2)


def _combine_wait(ys_hbm, buf_ref, sem, tm):
    for kk in range(TOP_K):
        pltpu.make_async_copy(ys_hbm.at[pl.ds(0, tm * LANE_TILES), :], buf_ref.at[kk], sem).wait()


def _combine_kernel(idx_cur, idx_nxt, x1_ref, g4_ref, g2_ref, b2_ref, ys_hbm,
                    o_ref, buf_ref, sem):
    i = pl.program_id(0)
    n = pl.num_programs(0)
    slot = i % 2
    tm = o_ref.shape[0]

    @pl.when(i == 0)
    def _():
        _combine_start(ys_hbm, idx_cur, buf_ref.at[0], sem.at[0], tm)

    @pl.when(i + 1 < n)
    def _():
        _combine_start(ys_hbm, idx_nxt, buf_ref.at[1 - slot], sem.at[1 - slot], tm)

    _combine_wait(ys_hbm, buf_ref.at[slot], sem.at[slot], tm)
    g4 = g4_ref[...]
    ffn = g4[:, 0:1] * _load_row_tiles(buf_ref.at[slot, 0], tm)
    for kk in range(1, TOP_K):
        ffn = ffn + g4[:, kk:kk + 1] * _load_row_tiles(buf_ref.at[slot, kk], tm)
    x1 = _load_row_tiles(x1_ref, tm)
    o_ref[...] = _layer_norm(ALPHA * x1 + ffn, g2_ref[...], b2_ref[...])


def _combine(dest3, x1r, g4, g2, b2, ys):
    d = D_MODEL
    t = x1r.shape[0] // LANE_TILES
    tm = TM_COMB
    nt = t // tm
    row = lambda i: (i, 0)
    fixed = lambda i: (0, 0)
    return pl.pallas_call(
        _combine_kernel,
        grid=(nt,),
        in_specs=[
            pl.BlockSpec((None, 1, tm * TOP_K), lambda i: (i, 0, 0), memory_space=pltpu.SMEM),
            pl.BlockSpec((None, 1, tm * TOP_K), lambda i: (jnp.minimum(i + 1, nt - 1), 0, 0),
                         memory_space=pltpu.SMEM),
            pl.BlockSpec((tm * LANE_TILES, LANES), row),
            pl.BlockSpec((tm, TOP_K), row),
            pl.BlockSpec((1, d), fixed),
            pl.BlockSpec((1, d), fixed),
            pl.BlockSpec(memory_space=pl.ANY),
        ],
        out_specs=pl.BlockSpec((tm, d), row),
        out_shape=jax.ShapeDtypeStruct((t, d), F32),
        scratch_shapes=[pltpu.VMEM((2, TOP_K, tm * LANE_TILES, LANES), F32),
                        pltpu.SemaphoreType.DMA((2,))],
        compiler_params=pltpu.CompilerParams(
            dimension_semantics=("arbitrary",), vmem_limit_bytes=VMEM_LIMIT),
        name="moe_combine_ln",
    )(dest3, dest3, x1r, g4, g2, b2, ys)


def kernel(x, ln0_g, ln0_b, w_in, conv_w, conv_b, w_rg_a, b_rg_a, w_rg_x, b_rg_x, lru_lambda, lam_q1, lam_k1, lam_q2, lam_k2, subln_g, w_out, ln1_g, ln1_b, w_router, b_router, w_gu, b_gu, w_down, b_down, ln2_g, ln2_b):
    bsz, seq, d = x.shape
    t = bsz * seq
    x2 = x.reshape(t, d)
    g0 = ln0_g.reshape(1, d)
    b0 = ln0_b.reshape(1, d)
    l = 0

    w_in_bf = w_in[l].astype(BF16)
    w_out_bf = w_out[l].astype(BF16)
    wa = jax.scipy.linalg.block_diag(*[w_rg_a[l, n] for n in range(LRU_BLOCKS)])
    wx = jax.scipy.linalg.block_diag(*[w_rg_x[l, n] for n in range(LRU_BLOCKS)])
    w_gate_bf = jnp.concatenate([wa, wx], axis=1).astype(BF16)
    b_gate = jnp.concatenate([b_rg_a[l].reshape(1, -1), b_rg_x[l].reshape(1, -1)], axis=1)
    lamvec = jnp.stack([lam_q1[l], lam_k1[l], lam_q2[l], lam_k2[l]]).astype(F32)
    slopes = jnp.asarray([2.0 ** (-8.0 * (i + 1) / ATT_HEADS) for i in range(ATT_HEADS)], F32)
    slopes = jnp.broadcast_to(slopes[:, None, None], (ATT_HEADS, 1, LANES))
    bg = b_gu[l, :, None, 0::2]
    bl = b_gu[l, :, None, 1::2]
    bd = b_down[l][:, None, :]

    qt, k, vt, xl, gl = _inproj(x, g0, b0, w_in_bf)
    att = _attention(qt, k, vt, slopes, lamvec, subln_g[l].reshape(-1, 1))
    rec = _lru(xl, gl, conv_w[l], conv_b[l].reshape(1, -1), w_gate_bf, b_gate,
               lru_lambda[l].reshape(1, -1))
    wr_t = w_router[l].T.astype(F32)
    wr_hi = wr_t.astype(BF16)
    wr_lo = (wr_t - wr_hi.astype(F32)).astype(BF16)
    x1r, e4t, g4t, r4t, cnt = _outproj_router(
        x2, g0, b0, att.reshape(t, -1), rec.reshape(t, -1), w_out_bf,
        ln1_g[l].reshape(1, d), ln1_b[l].reshape(1, d),
        jnp.concatenate([wr_hi, wr_lo], axis=0), b_router[l].reshape(-1, 1).astype(F32))
    e4 = e4t[0:TOP_K].T
    g4 = g4t[0:TOP_K].T
    r4 = r4t[0:TOP_K].T

    n_assign = t * TOP_K
    nb = (n_assign + N_EXPERTS * (ROW_BLOCK - 1)) // ROW_BLOCK + 1
    counts = cnt[:, 0].astype(I32)
    padded = (counts + ROW_BLOCK - 1) // ROW_BLOCK * ROW_BLOCK
    pends = jnp.cumsum(padded)
    pstarts = pends - padded
    dest = pstarts[e4] + r4
    row_tok = _invert(dest.reshape(t // TM_INV, 1, TM_INV * TOP_K), nb * ROW_BLOCK)
    ys = _experts((pstarts // ROW_BLOCK).astype(I32), (padded // ROW_BLOCK).astype(I32),
                  row_tok, x1r, w_gu[l], bg, bl, w_down[l], bd)
    out = _combine(dest.reshape(t // TM_COMB, 1, TM_COMB * TOP_K), x1r, g4,
                   ln2_g[l].reshape(1, d), ln2_b[l].reshape(1, d), ys)
    return out.reshape(bsz, seq, d)
```

```python
import math

import jax
import jax.numpy as jnp
from jax import lax
from jax.experimental import pallas as pl
from jax.experimental.pallas import tpu as pltpu

F32 = jnp.float32
BF16 = jnp.bfloat16
I32 = jnp.int32

D_MODEL = 1024
ATT_WIDTH = 512
ATT_HEAD_DIM = 64
ATT_HEADS = 4
VAL_DIM = 2 * ATT_HEAD_DIM
LRU_WIDTH = 512
LRU_BLOCKS = 8
LRU_C = 8.0
CONV_WIDTH = 4
CHUNK_SHIFT = 6
N_EXPERTS = 32
TOP_K = 4
D_EXPERT = 1024
SWIGLU_LIMIT = 7.0
SWIGLU_ALPHA = 1.702
DEPTH = 1
ALPHA = (2.0 * DEPTH) ** 0.25
LN_EPS = 1e-5
SUBLN_EPS = 1e-5
LAM_INIT = 0.8 - 0.6 * math.exp(-0.3 * 0)
NEG_BIG = -1e30
LOG2E = math.log2(math.e)

TM_PROJ = 512
TQ = 1024
TKB = 128
SW = 256
TC_LRU = 512
SCAN_GROUP = 8
ROW_BLOCK = 256
TM_COMB = 128
TM_INV = 2048
XPOSE_CHUNK = 128

VMEM_LIMIT = 48 * 1024 * 1024
VMEM_LIMIT_EXPERTS = 56 * 1024 * 1024

LANES = 128
LANE_TILES = D_MODEL // LANES


def _store_row_tiles(ref, val):
    n = val.shape[0]
    for j in range(LANE_TILES):
        ref[pl.ds(j, n, stride=LANE_TILES), :] = val[:, j * LANES:(j + 1) * LANES]


def _load_row_tiles(ref, n):
    return jnp.concatenate(
        [ref[pl.ds(j, n, stride=LANE_TILES), :] for j in range(LANE_TILES)], axis=1)


def _layer_norm(x, g, b):
    mu = jnp.mean(x, axis=-1, keepdims=True)
    xc = x - mu
    var = jnp.mean(xc * xc, axis=-1, keepdims=True)
    return xc * lax.rsqrt(var + LN_EPS) * g + b


def _inproj_kernel(x_ref, g_ref, b_ref, w_ref, qt_ref, k_ref, vt_ref, xl_ref, gl_ref):
    x0 = _layer_norm(x_ref[...], g_ref[...], b_ref[...])
    p = jnp.dot(x0.astype(BF16), w_ref[...], preferred_element_type=F32)
    a = ATT_WIDTH
    qt_ref[...] = (p[:, 0:a] * (ATT_HEAD_DIM ** -0.5 * LOG2E)).T.astype(BF16)
    k_ref[...] = p[:, a:2 * a].astype(BF16)
    vt = p[:, 2 * a:3 * a].T.astype(BF16)
    for h in range(ATT_HEADS):
        for c in range(vt_ref.shape[1]):
            vt_ref[h, c] = vt[h * VAL_DIM:(h + 1) * VAL_DIM, c * TKB:(c + 1) * TKB]
    xl_ref[...] = p[:, 3 * a:3 * a + LRU_WIDTH]
    gl_ref[...] = p[:, 3 * a + LRU_WIDTH:]


def _inproj(x, g, b, w_bf):
    bsz, seq, _ = x.shape
    n = w_bf.shape[1]
    tm = TM_PROJ
    row = lambda bi, i: (bi, i, 0)
    col = lambda bi, i: (bi, 0, i)
    fixed = lambda bi, i: (0, 0)
    return pl.pallas_call(
        _inproj_kernel,
        grid=(bsz, seq // tm),
        in_specs=[
            pl.BlockSpec((None, tm, D_MODEL), row),
            pl.BlockSpec((1, D_MODEL), fixed),
            pl.BlockSpec((1, D_MODEL), fixed),
            pl.BlockSpec((D_MODEL, n), fixed),
        ],
        out_specs=[
            pl.BlockSpec((None, ATT_WIDTH, tm), col),
            pl.BlockSpec((None, tm, ATT_WIDTH), row),
            pl.BlockSpec((None, ATT_HEADS, tm // TKB, VAL_DIM, TKB),
                         lambda bi, i: (bi, 0, i, 0, 0)),
            pl.BlockSpec((None, tm, LRU_WIDTH), row),
            pl.BlockSpec((None, tm, LRU_WIDTH), row),
        ],
        out_shape=[
            jax.ShapeDtypeStruct((bsz, ATT_WIDTH, seq), BF16),
            jax.ShapeDtypeStruct((bsz, seq, ATT_WIDTH), BF16),
            jax.ShapeDtypeStruct((bsz, ATT_HEADS, seq // TKB, VAL_DIM, TKB), BF16),
            jax.ShapeDtypeStruct((bsz, seq, LRU_WIDTH), F32),
            jax.ShapeDtypeStruct((bsz, seq, LRU_WIDTH), F32),
        ],
        compiler_params=pltpu.CompilerParams(
            dimension_semantics=("parallel", "parallel"), vmem_limit_bytes=VMEM_LIMIT),
        name="ln_inproj",
    )(x, g, b, w_bf)


def _attn_kernel(qt_ref, k_ref, vt_ref, slope_ref, lam_ref, g_ref,
                 o_ref, q2_ref, kx_ref, qbias_ref, diag_ref, *stat_refs):
    qi = pl.program_id(2)
    tq = qt_ref.shape[1]
    n_strips = 2 * tq // SW
    blocks_per_q = tq // TKB
    cl = slope_ref[0:1, 0:1] * LOG2E

    @pl.when(qi == 0)
    def _():
        c = lax.broadcasted_iota(I32, (TKB, LANES), 0)
        lane = lax.broadcasted_iota(I32, (TKB, LANES), 1)
        w = c.astype(F32) * cl
        w_hi = w.astype(BF16).astype(F32)
        w_mid = (w - w_hi).astype(BF16).astype(F32)
        w_lo = w - w_hi - w_mid
        kx_ref[...] = jnp.where(
            lane == 0, w_hi, jnp.where(lane == 1, w_mid, jnp.where(lane == 2, w_lo, 0.0))
        ).astype(BF16)
        r1 = lax.broadcasted_iota(I32, (1, tq), 1)
        qbias_ref[...] = r1.astype(F32) * (-cl)
        ck0 = lax.broadcasted_iota(I32, (TKB, tq), 0)
        r = lax.broadcasted_iota(I32, (TKB, tq), 1)
        for kd in range(blocks_per_q):
            ck = ck0 + kd * TKB
            allowed = (ck >> CHUNK_SHIFT) <= (r >> CHUNK_SHIFT)
            bias = (jnp.abs(r - ck) + ck0).astype(F32) * (-cl)
            diag_ref[kd] = jnp.where(allowed, bias, NEG_BIG)

    m_refs = stat_refs[0:n_strips]
    l_refs = stat_refs[n_strips:2 * n_strips]
    acc_refs = stat_refs[2 * n_strips:3 * n_strips]
    for j in range(n_strips):
        m_refs[j][...] = jnp.full(m_refs[j].shape, NEG_BIG, F32)
        l_refs[j][...] = jnp.zeros(l_refs[j].shape, F32)
        acc_refs[j][...] = jnp.zeros(acc_refs[j].shape, F32)
    qt = qt_ref[...]
    frow = lax.broadcasted_iota(I32, qt.shape, 0)
    zero = jnp.zeros_like(qt)
    q2_ref[0:VAL_DIM, 0:tq] = jnp.where(frow < ATT_HEAD_DIM, qt, zero)
    q2_ref[0:VAL_DIM, tq:2 * tq] = jnp.where(frow >= ATT_HEAD_DIM, qt, zero)
    xrow = lax.broadcasted_iota(I32, (VAL_DIM, 2 * tq), 0)
    q2_ref[VAL_DIM:2 * VAL_DIM, :] = jnp.where(xrow < 3, 1.0, 0.0).astype(BF16)

    def load_keys(kb):
        k_blk = k_ref[pl.ds(pl.multiple_of(kb * TKB, TKB), TKB), :]
        return jnp.concatenate([k_blk, kx_ref[...]], axis=1), vt_ref[kb]

    def strip(j, k_ext, vt_blk, sc_bias, shift):
        ls = slice(j * SW, (j + 1) * SW)
        sc = jnp.dot(k_ext, q2_ref[:, ls], preferred_element_type=F32)
        if sc_bias is not None:
            sc = sc + sc_bias
        m_prev = m_refs[j][...]
        m_new = jnp.maximum(m_prev, jnp.max(sc, axis=0, keepdims=True) + shift)
        alpha = jnp.exp2(m_prev - m_new)
        p = jnp.exp2(sc - (m_new - shift))
        l_refs[j][...] = alpha * l_refs[j][...] + jnp.sum(p, axis=0, keepdims=True)
        pv = jnp.dot(vt_blk, p.astype(BF16), preferred_element_type=F32)
        acc_refs[j][...] = alpha * acc_refs[j][...] + pv
        m_refs[j][...] = m_new

    def off_diag_group(g, carry):
        for kd in range(blocks_per_q):
            kb = g * blocks_per_q + kd
            k_ext, vt_blk = load_keys(kb)
            block_shift = jnp.full((1, 1), qi * tq - kb * TKB, I32).astype(F32) * (-cl)
            for j in range(n_strips):
                jq = j % (tq // SW)
                strip(j, k_ext, vt_blk, None,
                      qbias_ref[:, jq * SW:(jq + 1) * SW] + block_shift)
        return carry
    lax.fori_loop(0, qi, off_diag_group, 0)

    for kd in range(blocks_per_q):
        k_ext, vt_blk = load_keys(qi * blocks_per_q + kd)
        for j in range(n_strips):
            jq = j % (tq // SW)
            cols = slice(jq * SW, (jq + 1) * SW)
            if (kd + 1) * TKB <= jq * SW:
                strip(j, k_ext, vt_blk, None, qbias_ref[:, cols] + cl * (kd * TKB))
            elif (jq + 1) * SW > kd * TKB:
                strip(j, k_ext, vt_blk, diag_ref[kd, :, cols], 0.0)

    lv = lam_ref[...]
    lam = (jnp.exp(jnp.sum(lv[0:1] * lv[1:2], axis=-1, keepdims=True))
           - jnp.exp(jnp.sum(lv[2:3] * lv[3:4], axis=-1, keepdims=True)) + LAM_INIT)
    o2 = [acc_refs[j][...] / l_refs[j][...] for j in range(n_strips)]
    half = n_strips // 2
    o = jnp.concatenate([o2[j] - lam * o2[half + j] for j in range(half)], axis=1)
    o = o * lax.rsqrt(jnp.mean(o * o, axis=0, keepdims=True) + SUBLN_EPS)
    o = o * (g_ref[...] * (1.0 - LAM_INIT))
    o_ref[...] = o.T.astype(o_ref.dtype)


def _attention(qt, k, vt, slopes, lamvec, subln_g_col):
    b, s, _ = k.shape
    nkb = s // TKB
    n_strips = 2 * TQ // SW
    return pl.pallas_call(
        _attn_kernel,
        grid=(b, ATT_HEADS, s // TQ),
        in_specs=[
            pl.BlockSpec((None, VAL_DIM, TQ), lambda bi, h, qi: (bi, h, qi)),
            pl.BlockSpec((None, s, VAL_DIM), lambda bi, h, qi: (bi, 0, h)),
            pl.BlockSpec((None, None, nkb, VAL_DIM, TKB), lambda bi, h, qi: (bi, h, 0, 0, 0)),
            pl.BlockSpec((None, 1, LANES), lambda bi, h, qi: (h, 0, 0)),
            pl.BlockSpec((4, ATT_HEAD_DIM), lambda bi, h, qi: (0, 0)),
            pl.BlockSpec((VAL_DIM, 1), lambda bi, h, qi: (0, 0)),
        ],
        out_specs=pl.BlockSpec((None, TQ, VAL_DIM), lambda bi, h, qi: (bi, qi, h)),
        out_shape=jax.ShapeDtypeStruct((b, s, ATT_WIDTH), BF16),
        scratch_shapes=[
            pltpu.VMEM((2 * VAL_DIM, 2 * TQ), BF16),
            pltpu.VMEM((TKB, LANES), BF16),
            pltpu.VMEM((1, TQ), F32),
            pltpu.VMEM((TQ // TKB, TKB, TQ), F32),
        ] +[pltpu.VMEM((1, SW), F32)] * (2 * n_strips) + [pltpu.VMEM((VAL_DIM, SW), F32)] * n_strips,
        compiler_params=pltpu.CompilerParams(
            dimension_semantics=("parallel", "parallel", "arbitrary"),
            vmem_limit_bytes=VMEM_LIMIT),
        name="diff_attention",
    )(qt, k, vt, slopes, lamvec, subln_g_col)


def _lru_kernel(xl_ref, gl_ref, cw_ref, cb_ref, wg_ref, bg_ref, lam_ref,
                o_ref, ext_ref, h_ref):
    c = pl.program_id(1)
    tc = xl_ref.shape[0]
    w = LRU_WIDTH

    @pl.when(c == 0)
    def _():
        ext_ref[0:8, :] = jnp.zeros((8, w), F32)
        h_ref[...] = jnp.zeros(h_ref.shape, F32)

    ext_ref[8:8 + tc, :] = xl_ref[...]
    cw = cw_ref[...]
    xc = cb_ref[...] + cw[0:1] * ext_ref[5:5 + tc, :]
    for j in range(1, CONV_WIDTH):
        xc = xc + cw[j:j + 1] * ext_ref[5 + j:5 + j + tc, :]
    tail = ext_ref[tc:tc + 8, :]

    gates = jnp.dot(xc.astype(BF16), wg_ref[...], preferred_element_type=F32) + bg_ref[...]
    r = jax.nn.sigmoid(gates[:, 0:w])
    ig = jax.nn.sigmoid(gates[:, w:2 * w])
    z = -lam_ref[...]
    softplus = jnp.maximum(z, 0.0) + jnp.log1p(jnp.exp(-jnp.abs(z)))
    log_a = (-LRU_C) * r * softplus
    a = jnp.exp(log_a)
    u = jnp.sqrt(-jnp.tanh(log_a) * (a * a + 1.0)) * (ig * xc)

    ng = tc // SCAN_GROUP
    a = a.reshape(ng, SCAN_GROUP, w)
    u = u.reshape(ng, SCAN_GROUP, w)
    pos = lax.broadcasted_iota(I32, a.shape, 1)
    d = 1
    while d < SCAN_GROUP:
        valid = pos >= d
        u = jnp.where(valid, a * pltpu.roll(u, d, axis=1) + u, u)
        a = jnp.where(valid, a * pltpu.roll(a, d, axis=1), a)
        d *= 2
    a = a.reshape(tc, w)
    u = u.reshape(tc, w)
    h_in = h_ref[0:1, :]
    groups = []
    for g in range(tc // SCAN_GROUP):
        rows = slice(g * SCAN_GROUP, (g + 1) * SCAN_GROUP)
        hg = u[rows] + a[rows] * h_in
        groups.append(hg)
        h_in = hg[SCAN_GROUP - 1:SCAN_GROUP, :]
    h = jnp.concatenate(groups, axis=0)

    gl = gl_ref[...]
    gelu = 0.5 * gl * (1.0 + jnp.tanh(math.sqrt(2.0 / math.pi) * (gl + 0.044715 * (gl * gl * gl))))
    o_ref[...] = (h * gelu).astype(o_ref.dtype)

    h_ref[0:1, :] = h[tc - 1:tc, :]
    ext_ref[0:8, :] = tail


def _lru(xl, gl, conv_w, conv_b, w_gate_bf, b_gate, lam):
    b, s, w = xl.shape
    tc = TC_LRU
    blk = lambda bi, c: (bi, c, 0)
    fixed = lambda bi, c: (0, 0)
    return pl.pallas_call(
        _lru_kernel,
        grid=(b, s // tc),
        in_specs=[
            pl.BlockSpec((None, tc, w), blk),
            pl.BlockSpec((None, tc, w), blk),
            pl.BlockSpec((CONV_WIDTH, w), fixed),
            pl.BlockSpec((1, w), fixed),
            pl.BlockSpec((w, 2 * w), fixed),
            pl.BlockSpec((1, 2 * w), fixed),
            pl.BlockSpec((1, w), fixed),
        ],
        out_specs=pl.BlockSpec((None, tc, w), blk),
        out_shape=jax.ShapeDtypeStruct((b, s, w), BF16),
        scratch_shapes=[pltpu.VMEM((tc + 8, w), F32), pltpu.VMEM((8, w), F32)],
        compiler_params=pltpu.CompilerParams(
            dimension_semantics=("parallel", "arbitrary"), vmem_limit_bytes=VMEM_LIMIT),
        name="rg_lru",
    )(xl, gl, conv_w, conv_b, w_gate_bf, b_gate, lam)


def _outproj_kernel(x_ref, g0_ref, b0_ref, att_ref, rec_ref, wo_ref, g1_ref, b1_ref,
                    wr_ref, br_ref, x1_ref, e4_ref, g4_ref, r4_ref, cnt_ref, carry_ref):
    i = pl.program_id(0)
    tm = x_ref.shape[0]

    @pl.when(i == 0)
    def _():
        carry_ref[...] = jnp.zeros(carry_ref.shape, F32)

    x0 = _layer_norm(x_ref[...], g0_ref[...], b0_ref[...])
    mixed = (jnp.dot(att_ref[...], wo_ref[0:ATT_WIDTH, :], preferred_element_type=F32)
             + jnp.dot(rec_ref[...], wo_ref[ATT_WIDTH:, :], preferred_element_type=F32))
    x1 = _layer_norm(ALPHA * x0 + mixed, g1_ref[...], b1_ref[...])
    _store_row_tiles(x1_ref, x1)

    x1_hi = x1.astype(BF16)
    x1_lo = (x1 - x1_hi.astype(F32)).astype(BF16)
    nt = (((1,), (1,)), ((), ()))
    ne = N_EXPERTS
    hi_terms = lax.dot_general(wr_ref[...], x1_hi, nt, preferred_element_type=F32)
    lo_term = lax.dot_general(wr_ref[0:ne, :], x1_lo, nt, preferred_element_type=F32)
    logits = hi_terms[0:ne] + hi_terms[ne:2 * ne] + lo_term + br_ref[...]

    erow = lax.broadcasted_iota(I32, logits.shape, 0)
    work = logits
    vals, idxs, hots = [], [], []
    for _ in range(TOP_K):
        mx = jnp.max(work, axis=0, keepdims=True)
        idx = jnp.min(jnp.where(work == mx, erow, ne), axis=0, keepdims=True)
        hot = erow == idx
        vals.append(mx)
        idxs.append(idx)
        hots.append(hot)
        work = jnp.where(hot, -jnp.inf, work)
    exps = [jnp.exp(v - vals[0]) for v in vals]
    denom = exps[0] + exps[1] + exps[2] + exps[3]

    mask = jnp.zeros(logits.shape, F32)
    for hot in hots:
        mask = mask + hot.astype(F32)
    tr = lax.broadcasted_iota(I32, (tm, tm), 0)
    tc = lax.broadcasted_iota(I32, (tm, tm), 1)
    earlier = jnp.where(tr < tc, 1.0, 0.0).astype(BF16)
    carry = carry_ref[:, 0:1]
    excl = jnp.dot(mask.astype(BF16), earlier, preferred_element_type=F32) + carry
    ranks = [jnp.sum(jnp.where(hot, excl, 0.0), axis=0, keepdims=True) for hot in hots]
    pad_i = jnp.zeros((8 - TOP_K, tm), I32)
    pad_f = jnp.zeros((8 - TOP_K, tm), F32)
    e4_ref[...] = jnp.concatenate(idxs + [pad_i], axis=0)
    g4_ref[...] = jnp.concatenate([ex / denom for ex in exps] + [pad_f], axis=0)
    r4_ref[...] = jnp.concatenate([r.astype(I32) for r in ranks] + [pad_i], axis=0)
    total = carry + jnp.sum(mask, axis=1, keepdims=True)
    carry_ref[...] = jnp.broadcast_to(total, carry_ref.shape)
    cnt_ref[...] = jnp.broadcast_to(total, cnt_ref.shape)


def _outproj_router(x2, g0, b0, att, rec, wo_bf, g1, b1, w_router, b_router):
    t = x2.shape[0]
    tm = TM_PROJ
    row = lambda i: (i, 0)
    col = lambda i: (0, i)
    fixed = lambda i: (0, 0)
    return pl.pallas_call(
        _outproj_kernel,
        grid=(t // tm,),
        in_specs=[
            pl.BlockSpec((tm, D_MODEL), row),
            pl.BlockSpec((1, D_MODEL), fixed),
            pl.BlockSpec((1, D_MODEL), fixed),
            pl.BlockSpec((tm, ATT_WIDTH), row),
            pl.BlockSpec((tm, LRU_WIDTH), row),
            pl.BlockSpec((D_MODEL, D_MODEL), fixed),
            pl.BlockSpec((1, D_MODEL), fixed),
            pl.BlockSpec((1, D_MODEL), fixed),
            pl.BlockSpec((2 * N_EXPERTS, D_MODEL), fixed),
            pl.BlockSpec((N_EXPERTS, 1), fixed),
        ],
        out_specs=[
            pl.BlockSpec((tm * LANE_TILES, LANES), row),
            pl.BlockSpec((8, tm), col),
            pl.BlockSpec((8, tm), col),
            pl.BlockSpec((8, tm), col),
            pl.BlockSpec((N_EXPERTS, LANES), fixed),
        ],
        out_shape=[
            jax.ShapeDtypeStruct((t * LANE_TILES, LANES), F32),
            jax.ShapeDtypeStruct((8, t), I32),
            jax.ShapeDtypeStruct((8, t), F32),
            jax.ShapeDtypeStruct((8, t), I32),
            jax.ShapeDtypeStruct((N_EXPERTS, LANES), F32),
        ],
        scratch_shapes=[pltpu.VMEM((N_EXPERTS, LANES), F32)],
        compiler_params=pltpu.CompilerParams(
            dimension_semantics=("arbitrary",), vmem_limit_bytes=VMEM_LIMIT),
        name="outproj_ln_router",
    )(x2, g0, b0, att, rec, wo_bf, g1, b1, w_router, b_router)


def _tile_rows(row):
    return pl.ds(pl.multiple_of(row * LANE_TILES, LANE_TILES), LANE_TILES)


def _invert_kernel(dest_ref, rt_ref, zero_ref, sem):
    i = pl.program_id(0)
    tm = dest_ref.shape[1] // TOP_K

    @pl.when(i == 0)
    def _():
        zero_ref[...] = jnp.zeros(zero_ref.shape, I32)
        clear = pltpu.make_async_copy(zero_ref, rt_ref, sem)
        clear.start()
        clear.wait()

    def body(t, carry):
        for kk in range(TOP_K):
            rt_ref[dest_ref[0, t * TOP_K + kk]] = i * tm + t
        return carry
    lax.fori_loop(0, tm, body, 0, unroll=4)


def _invert(dest3, n_rows):
    nt, _, per = dest3.shape
    return pl.pallas_call(
        _invert_kernel,
        grid=(nt,),
        in_specs=[pl.BlockSpec((None, 1, per), lambda i: (i, 0, 0), memory_space=pltpu.SMEM)],
        out_specs=pl.BlockSpec(memory_space=pltpu.SMEM),
        out_shape=jax.ShapeDtypeStruct((n_rows,), I32),
        scratch_shapes=[pltpu.VMEM((n_rows,), I32), pltpu.SemaphoreType.DMA(())],
        compiler_params=pltpu.CompilerParams(dimension_semantics=("arbitrary",)),
        name="moe_invert",
    )(dest3)


def _expert_kernel(first_ref, nblk_ref, rt_ref, wgu_ref, bg_ref, bl_ref, wd_ref, bd_ref,
                   x_hbm, ys_hbm, wg_ref, wl_ref, wdb_ref, xb_ref, xbuf, ybuf, xsem, ysem):
    e = pl.program_id(0)
    n_exp = pl.num_programs(0)
    d, f2 = wgu_ref.shape
    f = f2 // 2
    ch = XPOSE_CHUNK
    first = first_ref[e]
    nblk = nblk_ref[e]
    n_live = first_ref[n_exp - 1] + nblk_ref[n_exp - 1]
    n_blocks = ys_hbm.shape[0] // (ROW_BLOCK * LANE_TILES)
    block_rows = ROW_BLOCK * LANE_TILES

    def gather_start(g, slot):
        for r in range(ROW_BLOCK):
            pltpu.make_async_copy(
                x_hbm.at[_tile_rows(rt_ref[g * ROW_BLOCK + r]), :],
                xbuf.at[slot, pl.ds(r * LANE_TILES, LANE_TILES), :],
                xsem.at[slot]).start()

    def gather_wait(slot):
        pltpu.make_async_copy(x_hbm.at[pl.ds(0, block_rows), :], xbuf.at[slot],
                              xsem.at[slot]).wait()

    def y_copy(g, slot):
        start = pl.multiple_of(g * block_rows, block_rows)
        return pltpu.make_async_copy(ybuf.at[slot], ys_hbm.at[pl.ds(start, block_rows), :],
                                     ysem.at[slot])

    @pl.when(jnp.logical_and(nblk > 0, first == 0))
    def _():
        gather_start(0, 0)

    @pl.when(nblk > 0)
    def _():
        src = lax.broadcasted_iota(I32, (2 * ch, 2 * ch), 0)
        dst = lax.broadcasted_iota(I32, (2 * ch, 2 * ch), 1)
        wanted = jnp.where(dst < ch, 2 * dst, 2 * (dst - ch) + 1)
        select = jnp.where(src == wanted, 1.0, 0.0).astype(BF16)
        for c in range(f // ch):
            w_c = wgu_ref[:, 2 * c * ch:2 * (c + 1) * ch].astype(BF16)
            split = jnp.dot(w_c, select, preferred_element_type=F32)
            wg_ref[:, c * ch:(c + 1) * ch] = split[:, 0:ch].astype(BF16)
            wl_ref[:, c * ch:(c + 1) * ch] = split[:, ch:2 * ch].astype(BF16)
        wdb_ref[...] = wd_ref[...].astype(BF16)

    def block(b, carry):
        g = first + b
        for slot in range(2):
            @pl.when(g % 2 == slot)
            def _():
                block_body(g, slot)
        return carry

    def block_body(g, slot):
        @pl.when(g >= 2)
        def _():
            y_copy(g - 2, slot).wait()

        gather_wait(slot)
        xb_ref[...] = _load_row_tiles(xbuf.at[slot], ROW_BLOCK).astype(BF16)
        gather_start(jnp.minimum(g + 1, n_live - 1), 1 - slot)
        x = xb_ref[...]
        hg = jnp.dot(x, wg_ref[...], preferred_element_type=F32) + bg_ref[...]
        hl = jnp.dot(x, wl_ref[...], preferred_element_type=F32) + bl_ref[...]
        xg = jnp.minimum(hg, SWIGLU_LIMIT)
        xl = jnp.clip(hl, -SWIGLU_LIMIT, SWIGLU_LIMIT)
        act = xg * jax.nn.sigmoid(SWIGLU_ALPHA * xg) * (xl + 1.0)
        y = jnp.dot(act.astype(BF16), wdb_ref[...], preferred_element_type=F32) + bd_ref[...]
        _store_row_tiles(ybuf.at[slot], y)
        y_copy(g, slot).start()

    lax.fori_loop(0, nblk, block, 0)

    @pl.when(e == n_exp - 1)
    def _():
        gather_wait(n_live % 2)
        y_copy(0, 0).wait()
        y_copy(0, 1).wait()
        ybuf[0] = jnp.zeros(ybuf.shape[1:], F32)
        for t in range(N_EXPERTS):
            @pl.when(n_live + t < n_blocks)
            def _():
                y_copy(n_live + t, 0).start()
        for t in range(N_EXPERTS):
            @pl.when(n_live + t < n_blocks)
            def _():
                y_copy(n_live + t, 0).wait()


def _experts(first_blk, n_blk, row_tok, x1r, w_gu, bg, bl, w_down, bd):
    d = D_MODEL
    f2 = w_gu.shape[2]
    f = f2 // 2
    n_rows = row_tok.shape[0]
    wmap = lambda e, fb, nb_: (e, 0, 0)
    grid_spec = pltpu.PrefetchScalarGridSpec(
        num_scalar_prefetch=2,
        grid=(N_EXPERTS,),
        in_specs=[
            pl.BlockSpec(memory_space=pltpu.SMEM),
            pl.BlockSpec((None, d, f2), wmap),
            pl.BlockSpec((None, 1, f), wmap),
            pl.BlockSpec((None, 1, f), wmap),
            pl.BlockSpec((None, f, d), wmap),
            pl.BlockSpec((None, 1, d), wmap),
            pl.BlockSpec(memory_space=pl.ANY),
        ],
        out_specs=pl.BlockSpec(memory_space=pl.ANY),
        scratch_shapes=[
            pltpu.VMEM((d, f), BF16),
            pltpu.VMEM((d, f), BF16),
            pltpu.VMEM((f, d), BF16),
            pltpu.VMEM((ROW_BLOCK, d), BF16),
            pltpu.VMEM((2, ROW_BLOCK * LANE_TILES, LANES), F32),
            pltpu.VMEM((2, ROW_BLOCK * LANE_TILES, LANES), F32),
            pltpu.SemaphoreType.DMA((2,)),
            pltpu.SemaphoreType.DMA((2,)),
        ],
    )
    return pl.pallas_call(
        _expert_kernel,
        grid_spec=grid_spec,
        out_shape=jax.ShapeDtypeStruct((n_rows * LANE_TILES, LANES), F32),
        compiler_params=pltpu.CompilerParams(
            dimension_semantics=("arbitrary",), vmem_limit_bytes=VMEM_LIMIT_EXPERTS),
        name="moe_experts",
    )(first_blk, n_blk, row_tok, w_gu, bg, bl, w_down, bd, x1r)


def _combine_start(ys_hbm, idx_ref, buf_ref, sem, tm):
    def body(t, carry):
        for kk in range(TOP_K):
            pltpu.make_async_copy(ys_hbm.at[_tile_rows(idx_ref[0, t * TOP_K + kk]), :],
                                  buf_ref.at[kk, _tile_rows(t), :], sem).start()
        return carry
    lax.fori_loop(0, tm, body, 0, unroll=2)


def _combine_wait(ys_hbm, buf_ref, sem, tm):
    for kk in range(TOP_K):
        pltpu.make_async_copy(ys_hbm.at[pl.ds(0, tm * LANE_TILES), :], buf_ref.at[kk], sem).wait()


def _combine_kernel(idx_cur, idx_nxt, x1_ref, g4_ref, g2_ref, b2_ref, ys_hbm,
                    o_ref, buf_ref, sem):
    i = pl.program_id(0)
    n = pl.num_programs(0)
    slot = i % 2
    tm = o_ref.shape[0]

    @pl.when(i == 0)
    def _():
        _combine_start(ys_hbm, idx_cur, buf_ref.at[0], sem.at[0], tm)

    @pl.when(i + 1 < n)
    def _():
        _combine_start(ys_hbm, idx_nxt, buf_ref.at[1 - slot], sem.at[1 - slot], tm)

    _combine_wait(ys_hbm, buf_ref.at[slot], sem.at[slot], tm)
    g4 = g4_ref[...]
    ffn = g4[:, 0:1] * _load_row_tiles(buf_ref.at[slot, 0], tm)
    for kk in range(1, TOP_K):
        ffn = ffn + g4[:, kk:kk + 1] * _load_row_tiles(buf_ref.at[slot, kk], tm)
    x1 = _load_row_tiles(x1_ref, tm)
    o_ref[...] = _layer_norm(ALPHA * x1 + ffn, g2_ref[...], b2_ref[...])


def _combine(dest3, x1r, g4, g2, b2, ys):
    d = D_MODEL
    t = x1r.shape[0] // LANE_TILES
    tm = TM_COMB
    nt = t // tm
    row = lambda i: (i, 0)
    fixed = lambda i: (0, 0)
    return pl.pallas_call(
        _combine_kernel,
        grid=(nt,),
        in_specs=[
            pl.BlockSpec((None, 1, tm * TOP_K), lambda i: (i, 0, 0), memory_space=pltpu.SMEM),
            pl.BlockSpec((None, 1, tm * TOP_K), lambda i: (jnp.minimum(i + 1, nt - 1), 0, 0),
                         memory_space=pltpu.SMEM),
            pl.BlockSpec((tm * LANE_TILES, LANES), row),
            pl.BlockSpec((tm, TOP_K), row),
            pl.BlockSpec((1, d), fixed),
            pl.BlockSpec((1, d), fixed),
            pl.BlockSpec(memory_space=pl.ANY),
        ],
        out_specs=pl.BlockSpec((tm, d), row),
        out_shape=jax.ShapeDtypeStruct((t, d), F32),
        scratch_shapes=[pltpu.VMEM((2, TOP_K, tm * LANE_TILES, LANES), F32),
                        pltpu.SemaphoreType.DMA((2,))],
        compiler_params=pltpu.CompilerParams(
            dimension_semantics=("arbitrary",), vmem_limit_bytes=VMEM_LIMIT),
        name="moe_combine_ln",
    )(dest3, dest3, x1r, g4, g2, b2, ys)


def kernel(x, ln0_g, ln0_b, w_in, conv_w, conv_b, w_rg_a, b_rg_a, w_rg_x, b_rg_x, lru_lambda, lam_q1, lam_k1, lam_q2, lam_k2, subln_g, w_out, ln1_g, ln1_b, w_router, b_router, w_gu, b_gu, w_down, b_down, ln2_g, ln2_b):
    bsz, seq, d = x.shape
    t = bsz * seq
    x2 = x.reshape(t, d)
    g0 = ln0_g.reshape(1, d)
    b0 = ln0_b.reshape(1, d)
    l = 0

    w_in_bf = w_in[l].astype(BF16)
    w_out_bf = w_out[l].astype(BF16)
    wa = jax.scipy.linalg.block_diag(*[w_rg_a[l, n] for n in range(LRU_BLOCKS)])
    wx = jax.scipy.linalg.block_diag(*[w_rg_x[l, n] for n in range(LRU_BLOCKS)])
    w_gate_bf = jnp.concatenate([wa, wx], axis=1).astype(BF16)
    b_gate = jnp.concatenate([b_rg_a[l].reshape(1, -1), b_rg_x[l].reshape(1, -1)], axis=1)
    lamvec = jnp.stack([lam_q1[l], lam_k1[l], lam_q2[l], lam_k2[l]]).astype(F32)
    slopes = jnp.asarray([2.0 ** (-8.0 * (i + 1) / ATT_HEADS) for i in range(ATT_HEADS)], F32)
    slopes = jnp.broadcast_to(slopes[:, None, None], (ATT_HEADS, 1, LANES))
    bg = b_gu[l, :, None, 0::2]
    bl = b_gu[l, :, None, 1::2]
    bd = b_down[l][:, None, :]

    qt, k, vt, xl, gl = _inproj(x, g0, b0, w_in_bf)
    att = _attention(qt, k, vt, slopes, lamvec, subln_g[l].reshape(-1, 1))
    rec = _lru(xl, gl, conv_w[l], conv_b[l].reshape(1, -1), w_gate_bf, b_gate,
               lru_lambda[l].reshape(1, -1))
    wr_t = w_router[l].T.astype(F32)
    wr_hi = wr_t.astype(BF16)
    wr_lo = (wr_t - wr_hi.astype(F32)).astype(BF16)
    x1r, e4t, g4t, r4t, cnt = _outproj_router(
        x2, g0, b0, att.reshape(t, -1), rec.reshape(t, -1), w_out_bf,
        ln1_g[l].reshape(1, d), ln1_b[l].reshape(1, d),
        jnp.concatenate([wr_hi, wr_lo], axis=0), b_router[l].reshape(-1, 1).astype(F32))
    e4 = e4t[0:TOP_K].T
    g4 = g4t[0:TOP_K].T
    r4 = r4t[0:TOP_K].T

    n_assign = t * TOP_K
    nb = (n_assign + N_EXPERTS * (ROW_BLOCK - 1)) // ROW_BLOCK + 1
    counts = cnt[:, 0].astype(I32)
    padded = (counts + ROW_BLOCK - 1) // ROW_BLOCK * ROW_BLOCK
    pends = jnp.cumsum(padded)
    pstarts = pends - padded
    dest = pstarts[e4] + r4
    row_tok = _invert(dest.reshape(t // TM_INV, 1, TM_INV * TOP_K), nb * ROW_BLOCK)
    ys = _experts((pstarts // ROW_BLOCK).astype(I32), (padded // ROW_BLOCK).astype(I32),
                  row_tok, x1r, w_gu[l], bg, bl, w_down[l], bd)
    out = _combine(dest.reshape(t // TM_COMB, 1, TM_COMB * TOP_K), x1r, g4,
                   ln2_g[l].reshape(1, d), ln2_b[l].reshape(1, d), ys)
    return out.reshape(bsz, seq, d)
```

```python
import math

import jax
import jax.numpy as jnp
from jax import lax
from jax.experimental import pallas as pl
from jax.experimental.pallas import tpu as pltpu

F32 = jnp.float32
BF16 = jnp.bfloat16
I32 = jnp.int32

D_MODEL = 1024
ATT_WIDTH = 512
ATT_HEAD_DIM = 64
ATT_HEADS = 4
VAL_DIM = 2 * ATT_HEAD_DIM
LRU_WIDTH = 512
LRU_BLOCKS = 8
LRU_C = 8.0
CONV_WIDTH = 4
CHUNK_SHIFT = 6
N_EXPERTS = 32
TOP_K = 4
D_EXPERT = 1024
SWIGLU_LIMIT = 7.0
SWIGLU_ALPHA = 1.702
DEPTH = 1
ALPHA = (2.0 * DEPTH) ** 0.25
LN_EPS = 1e-5
SUBLN_EPS = 1e-5
LAM_INIT = 0.8 - 0.6 * math.exp(-0.3 * 0)
NEG_BIG = -1e30
LOG2E = math.log2(math.e)

TM_PROJ = 512
TQ = 1024
TKB = 128
SW = 256
TC_LRU = 512
SCAN_GROUP = 8
ROW_BLOCK = 256
TM_COMB = 128
TM_INV = 2048
XPOSE_CHUNK = 256

VMEM_LIMIT = 48 * 1024 * 1024
VMEM_LIMIT_EXPERTS = 56 * 1024 * 1024

LANES = 128
LANE_TILES = D_MODEL // LANES


def _store_row_tiles(ref, val):
    n = val.shape[0]
    for j in range(LANE_TILES):
        ref[pl.ds(j, n, stride=LANE_TILES), :] = val[:, j * LANES:(j + 1) * LANES]


def _load_row_tiles(ref, n):
    return jnp.concatenate(
        [ref[pl.ds(j, n, stride=LANE_TILES), :] for j in range(LANE_TILES)], axis=1)


def _layer_norm(x, g, b):
    mu = jnp.mean(x, axis=-1, keepdims=True)
    xc = x - mu
    var = jnp.mean(xc * xc, axis=-1, keepdims=True)
    return xc * lax.rsqrt(var + LN_EPS) * g + b


def _inproj_kernel(x_ref, g_ref, b_ref, w_ref, qt_ref, k_ref, vt_ref, xl_ref, gl_ref):
    x0 = _layer_norm(x_ref[...], g_ref[...], b_ref[...])
    p = jnp.dot(x0.astype(BF16), w_ref[...], preferred_element_type=F32)
    a = ATT_WIDTH
    qt_ref[...] = (p[:, 0:a] * (ATT_HEAD_DIM ** -0.5 * LOG2E)).T.astype(BF16)
    k_ref[...] = p[:, a:2 * a].astype(BF16)
    vt = p[:, 2 * a:3 * a].T.astype(BF16)
    for h in range(ATT_HEADS):
        for c in range(vt_ref.shape[1]):
            vt_ref[h, c] = vt[h * VAL_DIM:(h + 1) * VAL_DIM, c * TKB:(c + 1) * TKB]
    xl_ref[...] = p[:, 3 * a:3 * a + LRU_WIDTH]
    gl_ref[...] = p[:, 3 * a + LRU_WIDTH:]


def _inproj(x, g, b, w_bf):
    bsz, seq, _ = x.shape
    n = w_bf.shape[1]
    tm = TM_PROJ
    row = lambda bi, i: (bi, i, 0)
    col = lambda bi, i: (bi, 0, i)
    fixed = lambda bi, i: (0, 0)
    return pl.pallas_call(
        _inproj_kernel,
        grid=(bsz, seq // tm),
        in_specs=[
            pl.BlockSpec((None, tm, D_MODEL), row),
            pl.BlockSpec((1, D_MODEL), fixed),
            pl.BlockSpec((1, D_MODEL), fixed),
            pl.BlockSpec((D_MODEL, n), fixed),
        ],
        out_specs=[
            pl.BlockSpec((None, ATT_WIDTH, tm), col),
            pl.BlockSpec((None, tm, ATT_WIDTH), row),
            pl.BlockSpec((None, ATT_HEADS, tm // TKB, VAL_DIM, TKB),
                         lambda bi, i: (bi, 0, i, 0, 0)),
            pl.BlockSpec((None, tm, LRU_WIDTH), row),
            pl.BlockSpec((None, tm, LRU_WIDTH), row),
        ],
        out_shape=[
            jax.ShapeDtypeStruct((bsz, ATT_WIDTH, seq), BF16),
            jax.ShapeDtypeStruct((bsz, seq, ATT_WIDTH), BF16),
            jax.ShapeDtypeStruct((bsz, ATT_HEADS, seq // TKB, VAL_DIM, TKB), BF16),
            jax.ShapeDtypeStruct((bsz, seq, LRU_WIDTH), F32),
            jax.ShapeDtypeStruct((bsz, seq, LRU_WIDTH), F32),
        ],
        compiler_params=pltpu.CompilerParams(
            dimension_semantics=("parallel", "parallel"), vmem_limit_bytes=VMEM_LIMIT),
        name="ln_inproj",
    )(x, g, b, w_bf)


def _attn_kernel(qt_ref, k_ref, vt_ref, slope_ref, lam_ref, g_ref,
                 o_ref, q2_ref, kx_ref, qbias_ref, diag_ref, *stat_refs):
    qi = pl.program_id(2)
    tq = qt_ref.shape[1]
    n_strips = 2 * tq // SW
    blocks_per_q = tq // TKB
    cl = slope_ref[0:1, 0:1] * LOG2E

    @pl.when(qi == 0)
    def _():
        c = lax.broadcasted_iota(I32, (TKB, LANES), 0)
        lane = lax.broadcasted_iota(I32, (TKB, LANES), 1)
        w = c.astype(F32) * cl
        w_hi = w.astype(BF16).astype(F32)
        w_mid = (w - w_hi).astype(BF16).astype(F32)
        w_lo = w - w_hi - w_mid
        kx_ref[...] = jnp.where(
            lane == 0, w_hi, jnp.where(lane == 1, w_mid, jnp.where(lane == 2, w_lo, 0.0))
        ).astype(BF16)
        r1 = lax.broadcasted_iota(I32, (1, tq), 1)
        qbias_ref[...] = r1.astype(F32) * (-cl)
        ck0 = lax.broadcasted_iota(I32, (TKB, tq), 0)
        r = lax.broadcasted_iota(I32, (TKB, tq), 1)
        for kd in range(blocks_per_q):
            ck = ck0 + kd * TKB
            allowed = (ck >> CHUNK_SHIFT) <= (r >> CHUNK_SHIFT)
            bias = (jnp.abs(r - ck) + ck0).astype(F32) * (-cl)
            diag_ref[kd] = jnp.where(allowed, bias, NEG_BIG)

    m_refs = stat_refs[0:n_strips]
    l_refs = stat_refs[n_strips:2 * n_strips]
    acc_refs = stat_refs[2 * n_strips:3 * n_strips]
    for j in range(n_strips):
        m_refs[j][...] = jnp.full(m_refs[j].shape, NEG_BIG, F32)
        l_refs[j][...] = jnp.zeros(l_refs[j].shape, F32)
        acc_refs[j][...] = jnp.zeros(acc_refs[j].shape, F32)
    qt = qt_ref[...]
    frow = lax.broadcasted_iota(I32, qt.shape, 0)
    zero = jnp.zeros_like(qt)
    q2_ref[0:VAL_DIM, 0:tq] = jnp.where(frow < ATT_HEAD_DIM, qt, zero)
    q2_ref[0:VAL_DIM, tq:2 * tq] = jnp.where(frow >= ATT_HEAD_DIM, qt, zero)
    xrow = lax.broadcasted_iota(I32, (VAL_DIM, 2 * tq), 0)
    q2_ref[VAL_DIM:2 * VAL_DIM, :] = jnp.where(xrow < 3, 1.0, 0.0).astype(BF16)

    def load_keys(kb):
        k_blk = k_ref[pl.ds(pl.multiple_of(kb * TKB, TKB), TKB), :]
        return jnp.concatenate([k_blk, kx_ref[...]], axis=1), vt_ref[kb]

    def strip(j, k_ext, vt_blk, sc_bias, shift):
        ls = slice(j * SW, (j + 1) * SW)
        sc = jnp.dot(k_ext, q2_ref[:, ls], preferred_element_type=F32)
        if sc_bias is not None:
            sc = sc + sc_bias
        m_prev = m_refs[j][...]
        m_new = jnp.maximum(m_prev, jnp.max(sc, axis=0, keepdims=True) + shift)
        alpha = jnp.exp2(m_prev - m_new)
        p = jnp.exp2(sc - (m_new - shift))
        l_refs[j][...] = alpha * l_refs[j][...] + jnp.sum(p, axis=0, keepdims=True)
        pv = jnp.dot(vt_blk, p.astype(BF16), preferred_element_type=F32)
        acc_refs[j][...] = alpha * acc_refs[j][...] + pv
        m_refs[j][...] = m_new

    def off_diag_group(g, carry):
        for kd in range(blocks_per_q):
            kb = g * blocks_per_q + kd
            k_ext, vt_blk = load_keys(kb)
            block_shift = jnp.full((1, 1), qi * tq - kb * TKB, I32).astype(F32) * (-cl)
            for j in range(n_strips):
                jq = j % (tq // SW)
                strip(j, k_ext, vt_blk, None,
                      qbias_ref[:, jq * SW:(jq + 1) * SW] + block_shift)
        return carry
    lax.fori_loop(0, qi, off_diag_group, 0)

    for kd in range(blocks_per_q):
        k_ext, vt_blk = load_keys(qi * blocks_per_q + kd)
        for j in range(n_strips):
            jq = j % (tq // SW)
            cols = slice(jq * SW, (jq + 1) * SW)
            if (kd + 1) * TKB <= jq * SW:
                strip(j, k_ext, vt_blk, None, qbias_ref[:, cols] + cl * (kd * TKB))
            elif (jq + 1) * SW > kd * TKB:
                strip(j, k_ext, vt_blk, diag_ref[kd, :, cols], 0.0)

    lv = lam_ref[...]
    lam = (jnp.exp(jnp.sum(lv[0:1] * lv[1:2], axis=-1, keepdims=True))
           - jnp.exp(jnp.sum(lv[2:3] * lv[3:4], axis=-1, keepdims=True)) + LAM_INIT)
    o2 = [acc_refs[j][...] / l_refs[j][...] for j in range(n_strips)]
    half = n_strips // 2
    o = jnp.concatenate([o2[j] - lam * o2[half + j] for j in range(half)], axis=1)
    o = o * lax.rsqrt(jnp.mean(o * o, axis=0, keepdims=True) + SUBLN_EPS)
    o = o * (g_ref[...] * (1.0 - LAM_INIT))
    o_ref[...] = o.T.astype(o_ref.dtype)


def _attention(qt, k, vt, slopes, lamvec, subln_g_col):
    b, s, _ = k.shape
    nkb = s // TKB
    n_strips = 2 * TQ // SW
    return pl.pallas_call(
        _attn_kernel,
        grid=(b, ATT_HEADS, s // TQ),
        in_specs=[
            pl.BlockSpec((None, VAL_DIM, TQ), lambda bi, h, qi: (bi, h, qi)),
            pl.BlockSpec((None, s, VAL_DIM), lambda bi, h, qi: (bi, 0, h)),
            pl.BlockSpec((None, None, nkb, VAL_DIM, TKB), lambda bi, h, qi: (bi, h, 0, 0, 0)),
            pl.BlockSpec((None, 1, LANES), lambda bi, h, qi: (h, 0, 0)),
            pl.BlockSpec((4, ATT_HEAD_DIM), lambda bi, h, qi: (0, 0)),
            pl.BlockSpec((VAL_DIM, 1), lambda bi, h, qi: (0, 0)),
        ],
        out_specs=pl.BlockSpec((None, TQ, VAL_DIM), lambda bi, h, qi: (bi, qi, h)),
        out_shape=jax.ShapeDtypeStruct((b, s, ATT_WIDTH), BF16),
        scratch_shapes=[
            pltpu.VMEM((2 * VAL_DIM, 2 * TQ), BF16),
            pltpu.VMEM((TKB, LANES), BF16),
            pltpu.VMEM((1, TQ), F32),
            pltpu.VMEM((TQ // TKB, TKB, TQ), F32),
        ] +[pltpu.VMEM((1, SW), F32)] * (2 * n_strips) + [pltpu.VMEM((VAL_DIM, SW), F32)] * n_strips,
        compiler_params=pltpu.CompilerParams(
            dimension_semantics=("parallel", "parallel", "arbitrary"),
            vmem_limit_bytes=VMEM_LIMIT),
        name="diff_attention",
    )(qt, k, vt, slopes, lamvec, subln_g_col)


def _lru_kernel(xl_ref, gl_ref, cw_ref, cb_ref, wg_ref, bg_ref, lam_ref,
                o_ref, ext_ref, h_ref):
    c = pl.program_id(1)
    tc = xl_ref.shape[0]
    w = LRU_WIDTH

    @pl.when(c == 0)
    def _():
        ext_ref[0:8, :] = jnp.zeros((8, w), F32)
        h_ref[...] = jnp.zeros(h_ref.shape, F32)

    ext_ref[8:8 + tc, :] = xl_ref[...]
    cw = cw_ref[...]
    xc = cb_ref[...] + cw[0:1] * ext_ref[5:5 + tc, :]
    for j in range(1, CONV_WIDTH):
        xc = xc + cw[j:j + 1] * ext_ref[5 + j:5 + j + tc, :]
    tail = ext_ref[tc:tc + 8, :]

    gates = jnp.dot(xc.astype(BF16), wg_ref[...], preferred_element_type=F32) + bg_ref[...]
    r = jax.nn.sigmoid(gates[:, 0:w])
    ig = jax.nn.sigmoid(gates[:, w:2 * w])
    z = -lam_ref[...]
    softplus = jnp.maximum(z, 0.0) + jnp.log1p(jnp.exp(-jnp.abs(z)))
    log_a = (-LRU_C) * r * softplus
    a = jnp.exp(log_a)
    u = jnp.sqrt(-jnp.tanh(log_a) * (a * a + 1.0)) * (ig * xc)

    ng = tc // SCAN_GROUP
    a = a.reshape(ng, SCAN_GROUP, w)
    u = u.reshape(ng, SCAN_GROUP, w)
    pos = lax.broadcasted_iota(I32, a.shape, 1)
    d = 1
    while d < SCAN_GROUP:
        valid = pos >= d
        u = jnp.where(valid, a * pltpu.roll(u, d, axis=1) + u, u)
        a = jnp.where(valid, a * pltpu.roll(a, d, axis=1), a)
        d *= 2
    a = a.reshape(tc, w)
    u = u.reshape(tc, w)
    h_in = h_ref[0:1, :]
    groups = []
    for g in range(tc // SCAN_GROUP):
        rows = slice(g * SCAN_GROUP, (g + 1) * SCAN_GROUP)
        hg = u[rows] + a[rows] * h_in
        groups.append(hg)
        h_in = hg[SCAN_GROUP - 1:SCAN_GROUP, :]
    h = jnp.concatenate(groups, axis=0)

    gl = gl_ref[...]
    gelu = 0.5 * gl * (1.0 + jnp.tanh(math.sqrt(2.0 / math.pi) * (gl + 0.044715 * (gl * gl * gl))))
    o_ref[...] = (h * gelu).astype(o_ref.dtype)

    h_ref[0:1, :] = h[tc - 1:tc, :]
    ext_ref[0:8, :] = tail


def _lru(xl, gl, conv_w, conv_b, w_gate_bf, b_gate, lam):
    b, s, w = xl.shape
    tc = TC_LRU
    blk = lambda bi, c: (bi, c, 0)
    fixed = lambda bi, c: (0, 0)
    return pl.pallas_call(
        _lru_kernel,
        grid=(b, s // tc),
        in_specs=[
            pl.BlockSpec((None, tc, w), blk),
            pl.BlockSpec((None, tc, w), blk),
            pl.BlockSpec((CONV_WIDTH, w), fixed),
            pl.BlockSpec((1, w), fixed),
            pl.BlockSpec((w, 2 * w), fixed),
            pl.BlockSpec((1, 2 * w), fixed),
            pl.BlockSpec((1, w), fixed),
        ],
        out_specs=pl.BlockSpec((None, tc, w), blk),
        out_shape=jax.ShapeDtypeStruct((b, s, w), BF16),
        scratch_shapes=[pltpu.VMEM((tc + 8, w), F32), pltpu.VMEM((8, w), F32)],
        compiler_params=pltpu.CompilerParams(
            dimension_semantics=("parallel", "arbitrary"), vmem_limit_bytes=VMEM_LIMIT),
        name="rg_lru",
    )(xl, gl, conv_w, conv_b, w_gate_bf, b_gate, lam)


def _outproj_kernel(x_ref, g0_ref, b0_ref, att_ref, rec_ref, wo_ref, g1_ref, b1_ref,
                    wr_ref, br_ref, x1_ref, e4_ref, g4_ref, r4_ref, cnt_ref, carry_ref):
    i = pl.program_id(0)
    tm = x_ref.shape[0]

    @pl.when(i == 0)
    def _():
        carry_ref[...] = jnp.zeros(carry_ref.shape, F32)

    x0 = _layer_norm(x_ref[...], g0_ref[...], b0_ref[...])
    mixed = (jnp.dot(att_ref[...], wo_ref[0:ATT_WIDTH, :], preferred_element_type=F32)
             + jnp.dot(rec_ref[...], wo_ref[ATT_WIDTH:, :], preferred_element_type=F32))
    x1 = _layer_norm(ALPHA * x0 + mixed, g1_ref[...], b1_ref[...])
    _store_row_tiles(x1_ref, x1)

    x1_hi = x1.astype(BF16)
    x1_lo = (x1 - x1_hi.astype(F32)).astype(BF16)
    nt = (((1,), (1,)), ((), ()))
    ne = N_EXPERTS
    hi_terms = lax.dot_general(wr_ref[...], x1_hi, nt, preferred_element_type=F32)
    lo_term = lax.dot_general(wr_ref[0:ne, :], x1_lo, nt, preferred_element_type=F32)
    logits = hi_terms[0:ne] + hi_terms[ne:2 * ne] + lo_term + br_ref[...]

    erow = lax.broadcasted_iota(I32, logits.shape, 0)
    work = logits
    vals, idxs, hots = [], [], []
    for _ in range(TOP_K):
        mx = jnp.max(work, axis=0, keepdims=True)
        idx = jnp.min(jnp.where(work == mx, erow, ne), axis=0, keepdims=True)
        hot = erow == idx
        vals.append(mx)
        idxs.append(idx)
        hots.append(hot)
        work = jnp.where(hot, -jnp.inf, work)
    exps = [jnp.exp(v - vals[0]) for v in vals]
    denom = exps[0] + exps[1] + exps[2] + exps[3]

    mask = jnp.zeros(logits.shape, F32)
    for hot in hots:
        mask = mask + hot.astype(F32)
    tr = lax.broadcasted_iota(I32, (tm, tm), 0)
    tc = lax.broadcasted_iota(I32, (tm, tm), 1)
    earlier = jnp.where(tr < tc, 1.0, 0.0).astype(BF16)
    carry = carry_ref[:, 0:1]
    excl = jnp.dot(mask.astype(BF16), earlier, preferred_element_type=F32) + carry
    ranks = [jnp.sum(jnp.where(hot, excl, 0.0), axis=0, keepdims=True) for hot in hots]
    pad_i = jnp.zeros((8 - TOP_K, tm), I32)
    pad_f = jnp.zeros((8 - TOP_K, tm), F32)
    e4_ref[...] = jnp.concatenate(idxs + [pad_i], axis=0)
    g4_ref[...] = jnp.concatenate([ex / denom for ex in exps] + [pad_f], axis=0)
    r4_ref[...] = jnp.concatenate([r.astype(I32) for r in ranks] + [pad_i], axis=0)
    total = carry + jnp.sum(mask, axis=1, keepdims=True)
    carry_ref[...] = jnp.broadcast_to(total, carry_ref.shape)
    cnt_ref[...] = jnp.broadcast_to(total, cnt_ref.shape)


def _outproj_router(x2, g0, b0, att, rec, wo_bf, g1, b1, w_router, b_router):
    t = x2.shape[0]
    tm = TM_PROJ
    row = lambda i: (i, 0)
    col = lambda i: (0, i)
    fixed = lambda i: (0, 0)
    return pl.pallas_call(
        _outproj_kernel,
        grid=(t // tm,),
        in_specs=[
            pl.BlockSpec((tm, D_MODEL), row),
            pl.BlockSpec((1, D_MODEL), fixed),
            pl.BlockSpec((1, D_MODEL), fixed),
            pl.BlockSpec((tm, ATT_WIDTH), row),
            pl.BlockSpec((tm, LRU_WIDTH), row),
            pl.BlockSpec((D_MODEL, D_MODEL), fixed),
            pl.BlockSpec((1, D_MODEL), fixed),
            pl.BlockSpec((1, D_MODEL), fixed),
            pl.BlockSpec((2 * N_EXPERTS, D_MODEL), fixed),
            pl.BlockSpec((N_EXPERTS, 1), fixed),
        ],
        out_specs=[
            pl.BlockSpec((tm * LANE_TILES, LANES), row),
            pl.BlockSpec((8, tm), col),
            pl.BlockSpec((8, tm), col),
            pl.BlockSpec((8, tm), col),
            pl.BlockSpec((N_EXPERTS, LANES), fixed),
        ],
        out_shape=[
            jax.ShapeDtypeStruct((t * LANE_TILES, LANES), F32),
            jax.ShapeDtypeStruct((8, t), I32),
            jax.ShapeDtypeStruct((8, t), F32),
            jax.ShapeDtypeStruct((8, t), I32),
            jax.ShapeDtypeStruct((N_EXPERTS, LANES), F32),
        ],
        scratch_shapes=[pltpu.VMEM((N_EXPERTS, LANES), F32)],
        compiler_params=pltpu.CompilerParams(
            dimension_semantics=("arbitrary",), vmem_limit_bytes=VMEM_LIMIT),
        name="outproj_ln_router",
    )(x2, g0, b0, att, rec, wo_bf, g1, b1, w_router, b_router)


def _tile_rows(row):
    return pl.ds(pl.multiple_of(row * LANE_TILES, LANE_TILES), LANE_TILES)


def _invert_kernel(dest_ref, rt_ref, zero_ref, sem):
    i = pl.program_id(0)
    tm = dest_ref.shape[1] // TOP_K

    @pl.when(i == 0)
    def _():
        zero_ref[...] = jnp.zeros(zero_ref.shape, I32)
        clear = pltpu.make_async_copy(zero_ref, rt_ref, sem)
        clear.start()
        clear.wait()

    def body(t, carry):
        for kk in range(TOP_K):
            rt_ref[dest_ref[0, t * TOP_K + kk]] = i * tm + t
        return carry
    lax.fori_loop(0, tm, body, 0, unroll=4)


def _invert(dest3, n_rows):
    nt, _, per = dest3.shape
    return pl.pallas_call(
        _invert_kernel,
        grid=(nt,),
        in_specs=[pl.BlockSpec((None, 1, per), lambda i: (i, 0, 0), memory_space=pltpu.SMEM)],
        out_specs=pl.BlockSpec(memory_space=pltpu.SMEM),
        out_shape=jax.ShapeDtypeStruct((n_rows,), I32),
        scratch_shapes=[pltpu.VMEM((n_rows,), I32), pltpu.SemaphoreType.DMA(())],
        compiler_params=pltpu.CompilerParams(dimension_semantics=("arbitrary",)),
        name="moe_invert",
    )(dest3)


def _expert_kernel(first_ref, nblk_ref, rt_ref, wgu_ref, bg_ref, bl_ref, wd_ref, bd_ref,
                   x_hbm, ys_hbm, wg_ref, wl_ref, wdb_ref, xb_ref, xbuf, ybuf, xsem, ysem):
    e = pl.program_id(0)
    n_exp = pl.num_programs(0)
    d, f2 = wgu_ref.shape
    f = f2 // 2
    ch = XPOSE_CHUNK
    first = first_ref[e]
    nblk = nblk_ref[e]
    n_live = first_ref[n_exp - 1] + nblk_ref[n_exp - 1]
    n_blocks = ys_hbm.shape[0] // (ROW_BLOCK * LANE_TILES)
    block_rows = ROW_BLOCK * LANE_TILES

    def gather_start(g, slot):
        for r in range(ROW_BLOCK):
            pltpu.make_async_copy(
                x_hbm.at[_tile_rows(rt_ref[g * ROW_BLOCK + r]), :],
                xbuf.at[slot, pl.ds(r * LANE_TILES, LANE_TILES), :],
                xsem.at[slot]).start()

    def gather_wait(slot):
        pltpu.make_async_copy(x_hbm.at[pl.ds(0, block_rows), :], xbuf.at[slot],
                              xsem.at[slot]).wait()

    def y_copy(g, slot):
        start = pl.multiple_of(g * block_rows, block_rows)
        return pltpu.make_async_copy(ybuf.at[slot], ys_hbm.at[pl.ds(start, block_rows), :],
                                     ysem.at[slot])

    @pl.when(jnp.logical_and(nblk > 0, first == 0))
    def _():
        gather_start(0, 0)

    @pl.when(nblk > 0)
    def _():
        src = lax.broadcasted_iota(I32, (2 * ch, 2 * ch), 0)
        dst = lax.broadcasted_iota(I32, (2 * ch, 2 * ch), 1)
        wanted = jnp.where(dst < ch, 2 * dst, 2 * (dst - ch) + 1)
        select = jnp.where(src == wanted, 1.0, 0.0).astype(BF16)
        for c in range(f // ch):
            w_c = wgu_ref[:, 2 * c * ch:2 * (c + 1) * ch].astype(BF16)
            split = jnp.dot(w_c, select, preferred_element_type=F32)
            wg_ref[:, c * ch:(c + 1) * ch] = split[:, 0:ch].astype(BF16)
            wl_ref[:, c * ch:(c + 1) * ch] = split[:, ch:2 * ch].astype(BF16)
        wdb_ref[...] = wd_ref[...].astype(BF16)

    def block(b, carry):
        g = first + b
        for slot in range(2):
            @pl.when(g % 2 == slot)
            def _():
                block_body(g, slot)
        return carry

    def block_body(g, slot):
        @pl.when(g >= 2)
        def _():
            y_copy(g - 2, slot).wait()

        gather_wait(slot)
        xb_ref[...] = _load_row_tiles(xbuf.at[slot], ROW_BLOCK).astype(BF16)
        gather_start(jnp.minimum(g + 1, n_live - 1), 1 - slot)
        x = xb_ref[...]
        hg = jnp.dot(x, wg_ref[...], preferred_element_type=F32) + bg_ref[...]
        hl = jnp.dot(x, wl_ref[...], preferred_element_type=F32) + bl_ref[...]
        xg = jnp.minimum(hg, SWIGLU_LIMIT)
        xl = jnp.clip(hl, -SWIGLU_LIMIT, SWIGLU_LIMIT)
        act = xg * jax.nn.sigmoid(SWIGLU_ALPHA * xg) * (xl + 1.0)
        y = jnp.dot(act.astype(BF16), wdb_ref[...], preferred_element_type=F32) + bd_ref[...]
        _store_row_tiles(ybuf.at[slot], y)
        y_copy(g, slot).start()

    lax.fori_loop(0, nblk, block, 0)

    @pl.when(e == n_exp - 1)
    def _():
        gather_wait(n_live % 2)
        y_copy(0, 0).wait()
        y_copy(0, 1).wait()
        ybuf[0] = jnp.zeros(ybuf.shape[1:], F32)
        for t in range(N_EXPERTS):
            @pl.when(n_live + t < n_blocks)
            def _():
                y_copy(n_live + t, 0).start()
        for t in range(N_EXPERTS):
            @pl.when(n_live + t < n_blocks)
            def _():
                y_copy(n_live + t, 0).wait()


def _experts(first_blk, n_blk, row_tok, x1r, w_gu, bg, bl, w_down, bd):
    d = D_MODEL
    f2 = w_gu.shape[2]
    f = f2 // 2
    n_rows = row_tok.shape[0]
    wmap = lambda e, fb, nb_: (e, 0, 0)
    grid_spec = pltpu.PrefetchScalarGridSpec(
        num_scalar_prefetch=2,
        grid=(N_EXPERTS,),
        in_specs=[
            pl.BlockSpec(memory_space=pltpu.SMEM),
            pl.BlockSpec((None, d, f2), wmap),
            pl.BlockSpec((None, 1, f), wmap),
            pl.BlockSpec((None, 1, f), wmap),
            pl.BlockSpec((None, f, d), wmap),
            pl.BlockSpec((None, 1, d), wmap),
            pl.BlockSpec(memory_space=pl.ANY),
        ],
        out_specs=pl.BlockSpec(memory_space=pl.ANY),
        scratch_shapes=[
            pltpu.VMEM((d, f), BF16),
            pltpu.VMEM((d, f), BF16),
            pltpu.VMEM((f, d), BF16),
            pltpu.VMEM((ROW_BLOCK, d), BF16),
            pltpu.VMEM((2, ROW_BLOCK * LANE_TILES, LANES), F32),
            pltpu.VMEM((2, ROW_BLOCK * LANE_TILES, LANES), F32),
            pltpu.SemaphoreType.DMA((2,)),
            pltpu.SemaphoreType.DMA((2,)),
        ],
    )
    return pl.pallas_call(
        _expert_kernel,
        grid_spec=grid_spec,
        out_shape=jax.ShapeDtypeStruct((n_rows * LANE_TILES, LANES), F32),
        compiler_params=pltpu.CompilerParams(
            dimension_semantics=("arbitrary",), vmem_limit_bytes=VMEM_LIMIT_EXPERTS),
        name="moe_experts",
    )(first_blk, n_blk, row_tok, w_gu, bg, bl, w_down, bd, x1r)


def _combine_start(ys_hbm, idx_ref, buf_ref, sem, tm):
    def body(t, carry):
        for kk in range(TOP_K):
            pltpu.make_async_copy(ys_hbm.at[_tile_rows(idx_ref[0, t * TOP_K + kk]), :],
                                  buf_ref.at[kk, _tile_rows(t), :], sem).start()
        return carry
    lax.fori_loop(0, tm, body, 0, unroll=8)


def _combine_wait(ys_hbm, buf_ref, sem, tm):
    for kk in range(TOP_K):
        pltpu.make_async_copy(ys_hbm.at[pl.ds(0, tm * LANE_TILES), :], buf_ref.at[kk], sem).wait()


def _combine_kernel(idx_cur, idx_nxt, x1_ref, g4_ref, g2_ref, b2_ref, ys_hbm,
                    o_ref, buf_ref, sem):
    i = pl.program_id(0)
    n = pl.num_programs(0)
    slot = i % 2
    tm = o_ref.shape[0]

    @pl.when(i == 0)
    def _():
        _combine_start(ys_hbm, idx_cur, buf_ref.at[0], sem.at[0], tm)

    @pl.when(i + 1 < n)
    def _():
        _combine_start(ys_hbm, idx_nxt, buf_ref.at[1 - slot], sem.at[1 - slot], tm)

    _combine_wait(ys_hbm, buf_ref.at[slot], sem.at[slot], tm)
    g4 = g4_ref[...]
    ffn = g4[:, 0:1] * _load_row_tiles(buf_ref.at[slot, 0], tm)
    for kk in range(1, TOP_K):
        ffn = ffn + g4[:, kk:kk + 1] * _load_row_tiles(buf_ref.at[slot, kk], tm)
    x1 = _load_row_tiles(x1_ref, tm)
    o_ref[...] = _layer_norm(ALPHA * x1 + ffn, g2_ref[...], b2_ref[...])


def _combine(dest3, x1r, g4, g2, b2, ys):
    d = D_MODEL
    t = x1r.shape[0] // LANE_TILES
    tm = TM_COMB
    nt = t // tm
    row = lambda i: (i, 0)
    fixed = lambda i: (0, 0)
    return pl.pallas_call(
        _combine_kernel,
        grid=(nt,),
        in_specs=[
            pl.BlockSpec((None, 1, tm * TOP_K), lambda i: (i, 0, 0), memory_space=pltpu.SMEM),
            pl.BlockSpec((None, 1, tm * TOP_K), lambda i: (jnp.minimum(i + 1, nt - 1), 0, 0),
                         memory_space=pltpu.SMEM),
            pl.BlockSpec((tm * LANE_TILES, LANES), row),
            pl.BlockSpec((tm, TOP_K), row),
            pl.BlockSpec((1, d), fixed),
            pl.BlockSpec((1, d), fixed),
            pl.BlockSpec(memory_space=pl.ANY),
        ],
        out_specs=pl.BlockSpec((tm, d), row),
        out_shape=jax.ShapeDtypeStruct((t, d), F32),
        scratch_shapes=[pltpu.VMEM((2, TOP_K, tm * LANE_TILES, LANES), F32),
                        pltpu.SemaphoreType.DMA((2,))],
        compiler_params=pltpu.CompilerParams(
            dimension_semantics=("arbitrary",), vmem_limit_bytes=VMEM_LIMIT),
        name="moe_combine_ln",
    )(dest3, dest3, x1r, g4, g2, b2, ys)


def kernel(x, ln0_g, ln0_b, w_in, conv_w, conv_b, w_rg_a, b_rg_a, w_rg_x, b_rg_x, lru_lambda, lam_q1, lam_k1, lam_q2, lam_k2, subln_g, w_out, ln1_g, ln1_b, w_router, b_router, w_gu, b_gu, w_down, b_down, ln2_g, ln2_b):
    bsz, seq, d = x.shape
    t = bsz * seq
    x2 = x.reshape(t, d)
    g0 = ln0_g.reshape(1, d)
    b0 = ln0_b.reshape(1, d)
    l = 0

    w_in_bf = w_in[l].astype(BF16)
    w_out_bf = w_out[l].astype(BF16)
    wa = jax.scipy.linalg.block_diag(*[w_rg_a[l, n] for n in range(LRU_BLOCKS)])
    wx = jax.scipy.linalg.block_diag(*[w_rg_x[l, n] for n in range(LRU_BLOCKS)])
    w_gate_bf = jnp.concatenate([wa, wx], axis=1).astype(BF16)
    b_gate = jnp.concatenate([b_rg_a[l].reshape(1, -1), b_rg_x[l].reshape(1, -1)], axis=1)
    lamvec = jnp.stack([lam_q1[l], lam_k1[l], lam_q2[l], lam_k2[l]]).astype(F32)
    slopes = jnp.asarray([2.0 ** (-8.0 * (i + 1) / ATT_HEADS) for i in range(ATT_HEADS)], F32)
    slopes = jnp.broadcast_to(slopes[:, None, None], (ATT_HEADS, 1, LANES))
    bg = b_gu[l, :, None, 0::2]
    bl = b_gu[l, :, None, 1::2]
    bd = b_down[l][:, None, :]

    qt, k, vt, xl, gl = _inproj(x, g0, b0, w_in_bf)
    att = _attention(qt, k, vt, slopes, lamvec, subln_g[l].reshape(-1, 1))
    rec = _lru(xl, gl, conv_w[l], conv_b[l].reshape(1, -1), w_gate_bf, b_gate,
               lru_lambda[l].reshape(1, -1))
    wr_t = w_router[l].T.astype(F32)
    wr_hi = wr_t.astype(BF16)
    wr_lo = (wr_t - wr_hi.astype(F32)).astype(BF16)
    x1r, e4t, g4t, r4t, cnt = _outproj_router(
        x2, g0, b0, att.reshape(t, -1), rec.reshape(t, -1), w_out_bf,
        ln1_g[l].reshape(1, d), ln1_b[l].reshape(1, d),
        jnp.concatenate([wr_hi, wr_lo], axis=0), b_router[l].reshape(-1, 1).astype(F32))
    e4 = e4t[0:TOP_K].T
    g4 = g4t[0:TOP_K].T
    r4 = r4t[0:TOP_K].T

    n_assign = t * TOP_K
    nb = (n_assign + N_EXPERTS * (ROW_BLOCK - 1)) // ROW_BLOCK + 1
    counts = cnt[:, 0].astype(I32)
    padded = (counts + ROW_BLOCK - 1) // ROW_BLOCK * ROW_BLOCK
    pends = jnp.cumsum(padded)
    pstarts = pends - padded
    dest = pstarts[e4] + r4
    row_tok = _invert(dest.reshape(t // TM_INV, 1, TM_INV * TOP_K), nb * ROW_BLOCK)
    ys = _experts((pstarts // ROW_BLOCK).astype(I32), (padded // ROW_BLOCK).astype(I32),
                  row_tok, x1r, w_gu[l], bg, bl, w_down[l], bd)
    out = _combine(dest.reshape(t // TM_COMB, 1, TM_COMB * TOP_K), x1r, g4,
                   ln2_g[l].reshape(1, d), ln2_b[l].reshape(1, d), ys)
    return out.reshape(bsz, seq, d)
```

```python
import math

import jax
import jax.numpy as jnp
from jax import lax
from jax.experimental import pallas as pl
from jax.experimental.pallas import tpu as pltpu

F32 = jnp.float32
BF16 = jnp.bfloat16
I32 = jnp.int32

D_MODEL = 1024
ATT_WIDTH = 512
ATT_HEAD_DIM = 64
ATT_HEADS = 4
VAL_DIM = 2 * ATT_HEAD_DIM
LRU_WIDTH = 512
LRU_BLOCKS = 8
LRU_C = 8.0
CONV_WIDTH = 4
CHUNK_SHIFT = 6
N_EXPERTS = 32
TOP_K = 4
D_EXPERT = 1024
SWIGLU_LIMIT = 7.0
SWIGLU_ALPHA = 1.702
DEPTH = 1
ALPHA = (2.0 * DEPTH) ** 0.25
LN_EPS = 1e-5
SUBLN_EPS = 1e-5
LAM_INIT = 0.8 - 0.6 * math.exp(-0.3 * 0)
NEG_BIG = -1e30
LOG2E = math.log2(math.e)

TM_PROJ = 512
TQ = 2048
TKB = 128
SW = 256
TC_LRU = 512
SCAN_GROUP = 8
ROW_BLOCK = 256
TM_COMB = 128
TM_INV = 2048
XPOSE_CHUNK = 256

VMEM_LIMIT = 48 * 1024 * 1024
VMEM_LIMIT_EXPERTS = 56 * 1024 * 1024

LANES = 128
LANE_TILES = D_MODEL // LANES


def _store_row_tiles(ref, val):
    n = val.shape[0]
    for j in range(LANE_TILES):
        ref[pl.ds(j, n, stride=LANE_TILES), :] = val[:, j * LANES:(j + 1) * LANES]


def _load_row_tiles(ref, n):
    return jnp.concatenate(
        [ref[pl.ds(j, n, stride=LANE_TILES), :] for j in range(LANE_TILES)], axis=1)


def _layer_norm(x, g, b):
    mu = jnp.mean(x, axis=-1, keepdims=True)
    xc = x - mu
    var = jnp.mean(xc * xc, axis=-1, keepdims=True)
    return xc * lax.rsqrt(var + LN_EPS) * g + b


def _inproj_kernel(x_ref, g_ref, b_ref, w_ref, qt_ref, k_ref, vt_ref, xl_ref, gl_ref):
    x0 = _layer_norm(x_ref[...], g_ref[...], b_ref[...])
    p = jnp.dot(x0.astype(BF16), w_ref[...], preferred_element_type=F32)
    a = ATT_WIDTH
    qt_ref[...] = (p[:, 0:a] * (ATT_HEAD_DIM ** -0.5 * LOG2E)).T.astype(BF16)
    k_ref[...] = p[:, a:2 * a].astype(BF16)
    vt = p[:, 2 * a:3 * a].T.astype(BF16)
    for h in range(ATT_HEADS):
        for c in range(vt_ref.shape[1]):
            vt_ref[h, c] = vt[h * VAL_DIM:(h + 1) * VAL_DIM, c * TKB:(c + 1) * TKB]
    xl_ref[...] = p[:, 3 * a:3 * a + LRU_WIDTH]
    gl_ref[...] = p[:, 3 * a + LRU_WIDTH:]


def _inproj(x, g, b, w_bf):
    bsz, seq, _ = x.shape
    n = w_bf.shape[1]
    tm = TM_PROJ
    row = lambda bi, i: (bi, i, 0)
    col = lambda bi, i: (bi, 0, i)
    fixed = lambda bi, i: (0, 0)
    return pl.pallas_call(
        _inproj_kernel,
        grid=(bsz, seq // tm),
        in_specs=[
            pl.BlockSpec((None, tm, D_MODEL), row),
            pl.BlockSpec((1, D_MODEL), fixed),
            pl.BlockSpec((1, D_MODEL), fixed),
            pl.BlockSpec((D_MODEL, n), fixed),
        ],
        out_specs=[
            pl.BlockSpec((None, ATT_WIDTH, tm), col),
            pl.BlockSpec((None, tm, ATT_WIDTH), row),
            pl.BlockSpec((None, ATT_HEADS, tm // TKB, VAL_DIM, TKB),
                         lambda bi, i: (bi, 0, i, 0, 0)),
            pl.BlockSpec((None, tm, LRU_WIDTH), row),
            pl.BlockSpec((None, tm, LRU_WIDTH), row),
        ],
        out_shape=[
            jax.ShapeDtypeStruct((bsz, ATT_WIDTH, seq), BF16),
            jax.ShapeDtypeStruct((bsz, seq, ATT_WIDTH), BF16),
            jax.ShapeDtypeStruct((bsz, ATT_HEADS, seq // TKB, VAL_DIM, TKB), BF16),
            jax.ShapeDtypeStruct((bsz, seq, LRU_WIDTH), F32),
            jax.ShapeDtypeStruct((bsz, seq, LRU_WIDTH), F32),
        ],
        compiler_params=pltpu.CompilerParams(
            dimension_semantics=("parallel", "parallel"), vmem_limit_bytes=VMEM_LIMIT),
        name="ln_inproj",
    )(x, g, b, w_bf)


def _attn_kernel(qt_ref, k_ref, vt_ref, slope_ref, lam_ref, g_ref,
                 o_ref, q2_ref, kx_ref, qbias_ref, diag_ref, *stat_refs):
    qi = pl.program_id(2)
    tq = qt_ref.shape[1]
    n_strips = 2 * tq // SW
    blocks_per_q = tq // TKB
    cl = slope_ref[0:1, 0:1] * LOG2E

    @pl.when(qi == 0)
    def _():
        c = lax.broadcasted_iota(I32, (TKB, LANES), 0)
        lane = lax.broadcasted_iota(I32, (TKB, LANES), 1)
        w = c.astype(F32) * cl
        w_hi = w.astype(BF16).astype(F32)
        w_mid = (w - w_hi).astype(BF16).astype(F32)
        w_lo = w - w_hi - w_mid
        kx_ref[...] = jnp.where(
            lane == 0, w_hi, jnp.where(lane == 1, w_mid, jnp.where(lane == 2, w_lo, 0.0))
        ).astype(BF16)
        r1 = lax.broadcasted_iota(I32, (1, tq), 1)
        qbias_ref[...] = r1.astype(F32) * (-cl)
        ck0 = lax.broadcasted_iota(I32, (TKB, tq), 0)
        r = lax.broadcasted_iota(I32, (TKB, tq), 1)
        for kd in range(blocks_per_q):
            ck = ck0 + kd * TKB
            allowed = (ck >> CHUNK_SHIFT) <= (r >> CHUNK_SHIFT)
            bias = (jnp.abs(r - ck) + ck0).astype(F32) * (-cl)
            diag_ref[kd] = jnp.where(allowed, bias, NEG_BIG)

    m_refs = stat_refs[0:n_strips]
    l_refs = stat_refs[n_strips:2 * n_strips]
    acc_refs = stat_refs[2 * n_strips:3 * n_strips]
    for j in range(n_strips):
        m_refs[j][...] = jnp.full(m_refs[j].shape, NEG_BIG, F32)
        l_refs[j][...] = jnp.zeros(l_refs[j].shape, F32)
        acc_refs[j][...] = jnp.zeros(acc_refs[j].shape, F32)
    qt = qt_ref[...]
    frow = lax.broadcasted_iota(I32, qt.shape, 0)
    zero = jnp.zeros_like(qt)
    q2_ref[0:VAL_DIM, 0:tq] = jnp.where(frow < ATT_HEAD_DIM, qt, zero)
    q2_ref[0:VAL_DIM, tq:2 * tq] = jnp.where(frow >= ATT_HEAD_DIM, qt, zero)
    xrow = lax.broadcasted_iota(I32, (VAL_DIM, 2 * tq), 0)
    q2_ref[VAL_DIM:2 * VAL_DIM, :] = jnp.where(xrow < 3, 1.0, 0.0).astype(BF16)

    def load_keys(kb):
        k_blk = k_ref[pl.ds(pl.multiple_of(kb * TKB, TKB), TKB), :]
        return jnp.concatenate([k_blk, kx_ref[...]], axis=1), vt_ref[kb]

    def strip(j, k_ext, vt_blk, sc_bias, shift):
        ls = slice(j * SW, (j + 1) * SW)
        sc = jnp.dot(k_ext, q2_ref[:, ls], preferred_element_type=F32)
        if sc_bias is not None:
            sc = sc + sc_bias
        m_prev = m_refs[j][...]
        m_new = jnp.maximum(m_prev, jnp.max(sc, axis=0, keepdims=True) + shift)
        alpha = jnp.exp2(m_prev - m_new)
        p = jnp.exp2(sc - (m_new - shift))
        l_refs[j][...] = alpha * l_refs[j][...] + jnp.sum(p, axis=0, keepdims=True)
        pv = jnp.dot(vt_blk, p.astype(BF16), preferred_element_type=F32)
        acc_refs[j][...] = alpha * acc_refs[j][...] + pv
        m_refs[j][...] = m_new

    def off_diag_group(g, carry):
        for kd in range(blocks_per_q):
            kb = g * blocks_per_q + kd
            k_ext, vt_blk = load_keys(kb)
            block_shift = jnp.full((1, 1), qi * tq - kb * TKB, I32).astype(F32) * (-cl)
            for j in range(n_strips):
                jq = j % (tq // SW)
                strip(j, k_ext, vt_blk, None,
                      qbias_ref[:, jq * SW:(jq + 1) * SW] + block_shift)
        return carry
    lax.fori_loop(0, qi, off_diag_group, 0)

    for kd in range(blocks_per_q):
        k_ext, vt_blk = load_keys(qi * blocks_per_q + kd)
        for j in range(n_strips):
            jq = j % (tq // SW)
            cols = slice(jq * SW, (jq + 1) * SW)
            if (kd + 1) * TKB <= jq * SW:
                strip(j, k_ext, vt_blk, None, qbias_ref[:, cols] + cl * (kd * TKB))
            elif (jq + 1) * SW > kd * TKB:
                strip(j, k_ext, vt_blk, diag_ref[kd, :, cols], 0.0)

    lv = lam_ref[...]
    lam = (jnp.exp(jnp.sum(lv[0:1] * lv[1:2], axis=-1, keepdims=True))
           - jnp.exp(jnp.sum(lv[2:3] * lv[3:4], axis=-1, keepdims=True)) + LAM_INIT)
    o2 = [acc_refs[j][...] / l_refs[j][...] for j in range(n_strips)]
    half = n_strips // 2
    o = jnp.concatenate([o2[j] - lam * o2[half + j] for j in range(half)], axis=1)
    o = o * lax.rsqrt(jnp.mean(o * o, axis=0, keepdims=True) + SUBLN_EPS)
    o = o * (g_ref[...] * (1.0 - LAM_INIT))
    o_ref[...] = o.T.astype(o_ref.dtype)


def _attention(qt, k, vt, slopes, lamvec, subln_g_col):
    b, s, _ = k.shape
    nkb = s // TKB
    n_strips = 2 * TQ // SW
    return pl.pallas_call(
        _attn_kernel,
        grid=(b, ATT_HEADS, s // TQ),
        in_specs=[
            pl.BlockSpec((None, VAL_DIM, TQ), lambda bi, h, qi: (bi, h, qi)),
            pl.BlockSpec((None, s, VAL_DIM), lambda bi, h, qi: (bi, 0, h)),
            pl.BlockSpec((None, None, nkb, VAL_DIM, TKB), lambda bi, h, qi: (bi, h, 0, 0, 0)),
            pl.BlockSpec((None, 1, LANES), lambda bi, h, qi: (h, 0, 0)),
            pl.BlockSpec((4, ATT_HEAD_DIM), lambda bi, h, qi: (0, 0)),
            pl.BlockSpec((VAL_DIM, 1), lambda bi, h, qi: (0, 0)),
        ],
        out_specs=pl.BlockSpec((None, TQ, VAL_DIM), lambda bi, h, qi: (bi, qi, h)),
        out_shape=jax.ShapeDtypeStruct((b, s, ATT_WIDTH), BF16),
        scratch_shapes=[
            pltpu.VMEM((2 * VAL_DIM, 2 * TQ), BF16),
            pltpu.VMEM((TKB, LANES), BF16),
            pltpu.VMEM((1, TQ), F32),
            pltpu.VMEM((TQ // TKB, TKB, TQ), F32),
        ] +[pltpu.VMEM((1, SW), F32)] * (2 * n_strips) + [pltpu.VMEM((VAL_DIM, SW), F32)] * n_strips,
        compiler_params=pltpu.CompilerParams(
            dimension_semantics=("parallel", "parallel", "arbitrary"),
            vmem_limit_bytes=VMEM_LIMIT),
        name="diff_attention",
    )(qt, k, vt, slopes, lamvec, subln_g_col)


def _lru_kernel(xl_ref, gl_ref, cw_ref, cb_ref, wg_ref, bg_ref, lam_ref,
                o_ref, ext_ref, h_ref):
    c = pl.program_id(1)
    tc = xl_ref.shape[0]
    w = LRU_WIDTH

    @pl.when(c == 0)
    def _():
        ext_ref[0:8, :] = jnp.zeros((8, w), F32)
        h_ref[...] = jnp.zeros(h_ref.shape, F32)

    ext_ref[8:8 + tc, :] = xl_ref[...]
    cw = cw_ref[...]
    xc = cb_ref[...] + cw[0:1] * ext_ref[5:5 + tc, :]
    for j in range(1, CONV_WIDTH):
        xc = xc + cw[j:j + 1] * ext_ref[5 + j:5 + j + tc, :]
    tail = ext_ref[tc:tc + 8, :]

    gates = jnp.dot(xc.astype(BF16), wg_ref[...], preferred_element_type=F32) + bg_ref[...]
    r = jax.nn.sigmoid(gates[:, 0:w])
    ig = jax.nn.sigmoid(gates[:, w:2 * w])
    z = -lam_ref[...]
    softplus = jnp.maximum(z, 0.0) + jnp.log1p(jnp.exp(-jnp.abs(z)))
    log_a = (-LRU_C) * r * softplus
    a = jnp.exp(log_a)
    u = jnp.sqrt(-jnp.tanh(log_a) * (a * a + 1.0)) * (ig * xc)

    ng = tc // SCAN_GROUP
    a = a.reshape(ng, SCAN_GROUP, w)
    u = u.reshape(ng, SCAN_GROUP, w)
    pos = lax.broadcasted_iota(I32, a.shape, 1)
    d = 1
    while d < SCAN_GROUP:
        valid = pos >= d
        u = jnp.where(valid, a * pltpu.roll(u, d, axis=1) + u, u)
        a = jnp.where(valid, a * pltpu.roll(a, d, axis=1), a)
        d *= 2
    a = a.reshape(tc, w)
    u = u.reshape(tc, w)
    h_in = h_ref[0:1, :]
    groups = []
    for g in range(tc // SCAN_GROUP):
        rows = slice(g * SCAN_GROUP, (g + 1) * SCAN_GROUP)
        hg = u[rows] + a[rows] * h_in
        groups.append(hg)
        h_in = hg[SCAN_GROUP - 1:SCAN_GROUP, :]
    h = jnp.concatenate(groups, axis=0)

    gl = gl_ref[...]
    gelu = 0.5 * gl * (1.0 + jnp.tanh(math.sqrt(2.0 / math.pi) * (gl + 0.044715 * (gl * gl * gl))))
    o_ref[...] = (h * gelu).astype(o_ref.dtype)

    h_ref[0:1, :] = h[tc - 1:tc, :]
    ext_ref[0:8, :] = tail


def _lru(xl, gl, conv_w, conv_b, w_gate_bf, b_gate, lam):
    b, s, w = xl.shape
    tc = TC_LRU
    blk = lambda bi, c: (bi, c, 0)
    fixed = lambda bi, c: (0, 0)
    return pl.pallas_call(
        _lru_kernel,
        grid=(b, s // tc),
        in_specs=[
            pl.BlockSpec((None, tc, w), blk),
            pl.BlockSpec((None, tc, w), blk),
            pl.BlockSpec((CONV_WIDTH, w), fixed),
            pl.BlockSpec((1, w), fixed),
            pl.BlockSpec((w, 2 * w), fixed),
            pl.BlockSpec((1, 2 * w), fixed),
            pl.BlockSpec((1, w), fixed),
        ],
        out_specs=pl.BlockSpec((None, tc, w), blk),
        out_shape=jax.ShapeDtypeStruct((b, s, w), BF16),
        scratch_shapes=[pltpu.VMEM((tc + 8, w), F32), pltpu.VMEM((8, w), F32)],
        compiler_params=pltpu.CompilerParams(
            dimension_semantics=("parallel", "arbitrary"), vmem_limit_bytes=VMEM_LIMIT),
        name="rg_lru",
    )(xl, gl, conv_w, conv_b, w_gate_bf, b_gate, lam)


def _outproj_kernel(x_ref, g0_ref, b0_ref, att_ref, rec_ref, wo_ref, g1_ref, b1_ref,
                    wr_ref, br_ref, x1_ref, e4_ref, g4_ref, r4_ref, cnt_ref, carry_ref):
    i = pl.program_id(0)
    tm = x_ref.shape[0]

    @pl.when(i == 0)
    def _():
        carry_ref[...] = jnp.zeros(carry_ref.shape, F32)

    x0 = _layer_norm(x_ref[...], g0_ref[...], b0_ref[...])
    mixed = (jnp.dot(att_ref[...], wo_ref[0:ATT_WIDTH, :], preferred_element_type=F32)
             + jnp.dot(rec_ref[...], wo_ref[ATT_WIDTH:, :], preferred_element_type=F32))
    x1 = _layer_norm(ALPHA * x0 + mixed, g1_ref[...], b1_ref[...])
    _store_row_tiles(x1_ref, x1)

    x1_hi = x1.astype(BF16)
    x1_lo = (x1 - x1_hi.astype(F32)).astype(BF16)
    nt = (((1,), (1,)), ((), ()))
    ne = N_EXPERTS
    hi_terms = lax.dot_general(wr_ref[...], x1_hi, nt, preferred_element_type=F32)
    lo_term = lax.dot_general(wr_ref[0:ne, :], x1_lo, nt, preferred_element_type=F32)
    logits = hi_terms[0:ne] + hi_terms[ne:2 * ne] + lo_term + br_ref[...]

    erow = lax.broadcasted_iota(I32, logits.shape, 0)
    work = logits
    vals, idxs, hots = [], [], []
    for _ in range(TOP_K):
        mx = jnp.max(work, axis=0, keepdims=True)
        idx = jnp.min(jnp.where(work == mx, erow, ne), axis=0, keepdims=True)
        hot = erow == idx
        vals.append(mx)
        idxs.append(idx)
        hots.append(hot)
        work = jnp.where(hot, -jnp.inf, work)
    exps = [jnp.exp(v - vals[0]) for v in vals]
    denom = exps[0] + exps[1] + exps[2] + exps[3]

    mask = jnp.zeros(logits.shape, F32)
    for hot in hots:
        mask = mask + hot.astype(F32)
    tr = lax.broadcasted_iota(I32, (tm, tm), 0)
    tc = lax.broadcasted_iota(I32, (tm, tm), 1)
    earlier = jnp.where(tr < tc, 1.0, 0.0).astype(BF16)
    carry = carry_ref[:, 0:1]
    excl = jnp.dot(mask.astype(BF16), earlier, preferred_element_type=F32) + carry
    ranks = [jnp.sum(jnp.where(hot, excl, 0.0), axis=0, keepdims=True) for hot in hots]
    pad_i = jnp.zeros((8 - TOP_K, tm), I32)
    pad_f = jnp.zeros((8 - TOP_K, tm), F32)
    e4_ref[...] = jnp.concatenate(idxs + [pad_i], axis=0)
    g4_ref[...] = jnp.concatenate([ex / denom for ex in exps] + [pad_f], axis=0)
    r4_ref[...] = jnp.concatenate([r.astype(I32) for r in ranks] + [pad_i], axis=0)
    total = carry + jnp.sum(mask, axis=1, keepdims=True)
    carry_ref[...] = jnp.broadcast_to(total, carry_ref.shape)
    cnt_ref[...] = jnp.broadcast_to(total, cnt_ref.shape)


def _outproj_router(x2, g0, b0, att, rec, wo_bf, g1, b1, w_router, b_router):
    t = x2.shape[0]
    tm = TM_PROJ
    row = lambda i: (i, 0)
    col = lambda i: (0, i)
    fixed = lambda i: (0, 0)
    return pl.pallas_call(
        _outproj_kernel,
        grid=(t // tm,),
        in_specs=[
            pl.BlockSpec((tm, D_MODEL), row),
            pl.BlockSpec((1, D_MODEL), fixed),
            pl.BlockSpec((1, D_MODEL), fixed),
            pl.BlockSpec((tm, ATT_WIDTH), row),
            pl.BlockSpec((tm, LRU_WIDTH), row),
            pl.BlockSpec((D_MODEL, D_MODEL), fixed),
            pl.BlockSpec((1, D_MODEL), fixed),
            pl.BlockSpec((1, D_MODEL), fixed),
            pl.BlockSpec((2 * N_EXPERTS, D_MODEL), fixed),
            pl.BlockSpec((N_EXPERTS, 1), fixed),
        ],
        out_specs=[
            pl.BlockSpec((tm * LANE_TILES, LANES), row),
            pl.BlockSpec((8, tm), col),
            pl.BlockSpec((8, tm), col),
            pl.BlockSpec((8, tm), col),
            pl.BlockSpec((N_EXPERTS, LANES), fixed),
        ],
        out_shape=[
            jax.ShapeDtypeStruct((t * LANE_TILES, LANES), F32),
            jax.ShapeDtypeStruct((8, t), I32),
            jax.ShapeDtypeStruct((8, t), F32),
            jax.ShapeDtypeStruct((8, t), I32),
            jax.ShapeDtypeStruct((N_EXPERTS, LANES), F32),
        ],
        scratch_shapes=[pltpu.VMEM((N_EXPERTS, LANES), F32)],
        compiler_params=pltpu.CompilerParams(
            dimension_semantics=("arbitrary",), vmem_limit_bytes=VMEM_LIMIT),
        name="outproj_ln_router",
    )(x2, g0, b0, att, rec, wo_bf, g1, b1, w_router, b_router)


def _tile_rows(row):
    return pl.ds(pl.multiple_of(row * LANE_TILES, LANE_TILES), LANE_TILES)


def _invert_kernel(dest_ref, rt_ref, zero_ref, sem):
    i = pl.program_id(0)
    tm = dest_ref.shape[1] // TOP_K

    @pl.when(i == 0)
    def _():
        zero_ref[...] = jnp.zeros(zero_ref.shape, I32)
        clear = pltpu.make_async_copy(zero_ref, rt_ref, sem)
        clear.start()
        clear.wait()

    def body(t, carry):
        for kk in range(TOP_K):
            rt_ref[dest_ref[0, t * TOP_K + kk]] = i * tm + t
        return carry
    lax.fori_loop(0, tm, body, 0, unroll=4)


def _invert(dest3, n_rows):
    nt, _, per = dest3.shape
    return pl.pallas_call(
        _invert_kernel,
        grid=(nt,),
        in_specs=[pl.BlockSpec((None, 1, per), lambda i: (i, 0, 0), memory_space=pltpu.SMEM)],
        out_specs=pl.BlockSpec(memory_space=pltpu.SMEM),
        out_shape=jax.ShapeDtypeStruct((n_rows,), I32),
        scratch_shapes=[pltpu.VMEM((n_rows,), I32), pltpu.SemaphoreType.DMA(())],
        compiler_params=pltpu.CompilerParams(dimension_semantics=("arbitrary",)),
        name="moe_invert",
    )(dest3)


def _expert_kernel(first_ref, nblk_ref, rt_ref, wgu_ref, bg_ref, bl_ref, wd_ref, bd_ref,
                   x_hbm, ys_hbm, wg_ref, wl_ref, wdb_ref, xb_ref, xbuf, ybuf, xsem, ysem):
    e = pl.program_id(0)
    n_exp = pl.num_programs(0)
    d, f2 = wgu_ref.shape
    f = f2 // 2
    ch = XPOSE_CHUNK
    first = first_ref[e]
    nblk = nblk_ref[e]
    n_live = first_ref[n_exp - 1] + nblk_ref[n_exp - 1]
    n_blocks = ys_hbm.shape[0] // (ROW_BLOCK * LANE_TILES)
    block_rows = ROW_BLOCK * LANE_TILES

    def gather_start(g, slot):
        for r in range(ROW_BLOCK):
            pltpu.make_async_copy(
                x_hbm.at[_tile_rows(rt_ref[g * ROW_BLOCK + r]), :],
                xbuf.at[slot, pl.ds(r * LANE_TILES, LANE_TILES), :],
                xsem.at[slot]).start()

    def gather_wait(slot):
        pltpu.make_async_copy(x_hbm.at[pl.ds(0, block_rows), :], xbuf.at[slot],
                              xsem.at[slot]).wait()

    def y_copy(g, slot):
        start = pl.multiple_of(g * block_rows, block_rows)
        return pltpu.make_async_copy(ybuf.at[slot], ys_hbm.at[pl.ds(start, block_rows), :],
                                     ysem.at[slot])

    @pl.when(jnp.logical_and(nblk > 0, first == 0))
    def _():
        gather_start(0, 0)

    @pl.when(nblk > 0)
    def _():
        src = lax.broadcasted_iota(I32, (2 * ch, 2 * ch), 0)
        dst = lax.broadcasted_iota(I32, (2 * ch, 2 * ch), 1)
        wanted = jnp.where(dst < ch, 2 * dst, 2 * (dst - ch) + 1)
        select = jnp.where(src == wanted, 1.0, 0.0).astype(BF16)
        for c in range(f // ch):
            w_c = wgu_ref[:, 2 * c * ch:2 * (c + 1) * ch].astype(BF16)
            split = jnp.dot(w_c, select, preferred_element_type=F32)
            wg_ref[:, c * ch:(c + 1) * ch] = split[:, 0:ch].astype(BF16)
            wl_ref[:, c * ch:(c + 1) * ch] = split[:, ch:2 * ch].astype(BF16)
        wdb_ref[...] = wd_ref[...].astype(BF16)

    def block(b, carry):
        g = first + b
        for slot in range(2):
            @pl.when(g % 2 == slot)
            def _():
                block_body(g, slot)
        return carry

    def block_body(g, slot):
        @pl.when(g >= 2)
        def _():
            y_copy(g - 2, slot).wait()

        gather_wait(slot)
        xb_ref[...] = _load_row_tiles(xbuf.at[slot], ROW_BLOCK).astype(BF16)
        gather_start(jnp.minimum(g + 1, n_live - 1), 1 - slot)
        x = xb_ref[...]
        hg = jnp.dot(x, wg_ref[...], preferred_element_type=F32) + bg_ref[...]
        hl = jnp.dot(x, wl_ref[...], preferred_element_type=F32) + bl_ref[...]
        xg = jnp.minimum(hg, SWIGLU_LIMIT)
        xl = jnp.clip(hl, -SWIGLU_LIMIT, SWIGLU_LIMIT)
        act = xg * jax.nn.sigmoid(SWIGLU_ALPHA * xg) * (xl + 1.0)
        y = jnp.dot(act.astype(BF16), wdb_ref[...], preferred_element_type=F32) + bd_ref[...]
        _store_row_tiles(ybuf.at[slot], y)
        y_copy(g, slot).start()

    lax.fori_loop(0, nblk, block, 0)

    @pl.when(e == n_exp - 1)
    def _():
        gather_wait(n_live % 2)
        y_copy(0, 0).wait()
        y_copy(0, 1).wait()
        ybuf[0] = jnp.zeros(ybuf.shape[1:], F32)
        for t in range(N_EXPERTS):
            @pl.when(n_live + t < n_blocks)
            def _():
                y_copy(n_live + t, 0).start()
        for t in range(N_EXPERTS):
            @pl.when(n_live + t < n_blocks)
            def _():
                y_copy(n_live + t, 0).wait()


def _experts(first_blk, n_blk, row_tok, x1r, w_gu, bg, bl, w_down, bd):
    d = D_MODEL
    f2 = w_gu.shape[2]
    f = f2 // 2
    n_rows = row_tok.shape[0]
    wmap = lambda e, fb, nb_: (e, 0, 0)
    grid_spec = pltpu.PrefetchScalarGridSpec(
        num_scalar_prefetch=2,
        grid=(N_EXPERTS,),
        in_specs=[
            pl.BlockSpec(memory_space=pltpu.SMEM),
            pl.BlockSpec((None, d, f2), wmap),
            pl.BlockSpec((None, 1, f), wmap),
            pl.BlockSpec((None, 1, f), wmap),
            pl.BlockSpec((None, f, d), wmap),
            pl.BlockSpec((None, 1, d), wmap),
            pl.BlockSpec(memory_space=pl.ANY),
        ],
        out_specs=pl.BlockSpec(memory_space=pl.ANY),
        scratch_shapes=[
            pltpu.VMEM((d, f), BF16),
            pltpu.VMEM((d, f), BF16),
            pltpu.VMEM((f, d), BF16),
            pltpu.VMEM((ROW_BLOCK, d), BF16),
            pltpu.VMEM((2, ROW_BLOCK * LANE_TILES, LANES), F32),
            pltpu.VMEM((2, ROW_BLOCK * LANE_TILES, LANES), F32),
            pltpu.SemaphoreType.DMA((2,)),
            pltpu.SemaphoreType.DMA((2,)),
        ],
    )
    return pl.pallas_call(
        _expert_kernel,
        grid_spec=grid_spec,
        out_shape=jax.ShapeDtypeStruct((n_rows * LANE_TILES, LANES), F32),
        compiler_params=pltpu.CompilerParams(
            dimension_semantics=("arbitrary",), vmem_limit_bytes=VMEM_LIMIT_EXPERTS),
        name="moe_experts",
    )(first_blk, n_blk, row_tok, w_gu, bg, bl, w_down, bd, x1r)


def _combine_start(ys_hbm, idx_ref, buf_ref, sem, tm):
    def body(t, carry):
        for kk in range(TOP_K):
            pltpu.make_async_copy(ys_hbm.at[_tile_rows(idx_ref[0, t * TOP_K + kk]), :],
                                  buf_ref.at[kk, _tile_rows(t), :], sem).start()
        return carry
    lax.fori_loop(0, tm, body, 0, unroll=2)


def _combine_wait(ys_hbm, buf_ref, sem, tm):
    for kk in range(TOP_K):
        pltpu.make_async_copy(ys_hbm.at[pl.ds(0, tm * LANE_TILES), :], buf_ref.at[kk], sem).wait()


def _combine_kernel(idx_cur, idx_nxt, x1_ref, g4_ref, g2_ref, b2_ref, ys_hbm,
                    o_ref, buf_ref, sem):
    i = pl.program_id(0)
    n = pl.num_programs(0)
    slot = i % 2
    tm = o_ref.shape[0]

    @pl.when(i == 0)
    def _():
        _combine_start(ys_hbm, idx_cur, buf_ref.at[0], sem.at[0], tm)

    @pl.when(i + 1 < n)
    def _():
        _combine_start(ys_hbm, idx_nxt, buf_ref.at[1 - slot], sem.at[1 - slot], tm)

    _combine_wait(ys_hbm, buf_ref.at[slot], sem.at[slot], tm)
    g4 = g4_ref[...]
    ffn = g4[:, 0:1] * _load_row_tiles(buf_ref.at[slot, 0], tm)
    for kk in range(1, TOP_K):
        ffn = ffn + g4[:, kk:kk + 1] * _load_row_tiles(buf_ref.at[slot, kk], tm)
    x1 = _load_row_tiles(x1_ref, tm)
    o_ref[...] = _layer_norm(ALPHA * x1 + ffn, g2_ref[...], b2_ref[...])


def _combine(dest3, x1r, g4, g2, b2, ys):
    d = D_MODEL
    t = x1r.shape[0] // LANE_TILES
    tm = TM_COMB
    nt = t // tm
    row = lambda i: (i, 0)
    fixed = lambda i: (0, 0)
    return pl.pallas_call(
        _combine_kernel,
        grid=(nt,),
        in_specs=[
            pl.BlockSpec((None, 1, tm * TOP_K), lambda i: (i, 0, 0), memory_space=pltpu.SMEM),
            pl.BlockSpec((None, 1, tm * TOP_K), lambda i: (jnp.minimum(i + 1, nt - 1), 0, 0),
                         memory_space=pltpu.SMEM),
            pl.BlockSpec((tm * LANE_TILES, LANES), row),
            pl.BlockSpec((tm, TOP_K), row),
            pl.BlockSpec((1, d), fixed),
            pl.BlockSpec((1, d), fixed),
            pl.BlockSpec(memory_space=pl.ANY),
        ],
        out_specs=pl.BlockSpec((tm, d), row),
        out_shape=jax.ShapeDtypeStruct((t, d), F32),
        scratch_shapes=[pltpu.VMEM((2, TOP_K, tm * LANE_TILES, LANES), F32),
                        pltpu.SemaphoreType.DMA((2,))],
        compiler_params=pltpu.CompilerParams(
            dimension_semantics=("arbitrary",), vmem_limit_bytes=VMEM_LIMIT),
        name="moe_combine_ln",
    )(dest3, dest3, x1r, g4, g2, b2, ys)


def kernel(x, ln0_g, ln0_b, w_in, conv_w, conv_b, w_rg_a, b_rg_a, w_rg_x, b_rg_x, lru_lambda, lam_q1, lam_k1, lam_q2, lam_k2, subln_g, w_out, ln1_g, ln1_b, w_router, b_router, w_gu, b_gu, w_down, b_down, ln2_g, ln2_b):
    bsz, seq, d = x.shape
    t = bsz * seq
    x2 = x.reshape(t, d)
    g0 = ln0_g.reshape(1, d)
    b0 = ln0_b.reshape(1, d)
    l = 0

    w_in_bf = w_in[l].astype(BF16)
    w_out_bf = w_out[l].astype(BF16)
    wa = jax.scipy.linalg.block_diag(*[w_rg_a[l, n] for n in range(LRU_BLOCKS)])
    wx = jax.scipy.linalg.block_diag(*[w_rg_x[l, n] for n in range(LRU_BLOCKS)])
    w_gate_bf = jnp.concatenate([wa, wx], axis=1).astype(BF16)
    b_gate = jnp.concatenate([b_rg_a[l].reshape(1, -1), b_rg_x[l].reshape(1, -1)], axis=1)
    lamvec = jnp.stack([lam_q1[l], lam_k1[l], lam_q2[l], lam_k2[l]]).astype(F32)
    slopes = jnp.asarray([2.0 ** (-8.0 * (i + 1) / ATT_HEADS) for i in range(ATT_HEADS)], F32)
    slopes = jnp.broadcast_to(slopes[:, None, None], (ATT_HEADS, 1, LANES))
    bg = b_gu[l, :, None, 0::2]
    bl = b_gu[l, :, None, 1::2]
    bd = b_down[l][:, None, :]

    qt, k, vt, xl, gl = _inproj(x, g0, b0, w_in_bf)
    att = _attention(qt, k, vt, slopes, lamvec, subln_g[l].reshape(-1, 1))
    rec = _lru(xl, gl, conv_w[l], conv_b[l].reshape(1, -1), w_gate_bf, b_gate,
               lru_lambda[l].reshape(1, -1))
    wr_t = w_router[l].T.astype(F32)
    wr_hi = wr_t.astype(BF16)
    wr_lo = (wr_t - wr_hi.astype(F32)).astype(BF16)
    x1r, e4t, g4t, r4t, cnt = _outproj_router(
        x2, g0, b0, att.reshape(t, -1), rec.reshape(t, -1), w_out_bf,
        ln1_g[l].reshape(1, d), ln1_b[l].reshape(1, d),
        jnp.concatenate([wr_hi, wr_lo], axis=0), b_router[l].reshape(-1, 1).astype(F32))
    e4 = e4t[0:TOP_K].T
    g4 = g4t[0:TOP_K].T
    r4 = r4t[0:TOP_K].T

    n_assign = t * TOP_K
    nb = (n_assign + N_EXPERTS * (ROW_BLOCK - 1)) // ROW_BLOCK + 1
    counts = cnt[:, 0].astype(I32)
    padded = (counts + ROW_BLOCK - 1) // ROW_BLOCK * ROW_BLOCK
    pends = jnp.cumsum(padded)
    pstarts = pends - padded
    dest = pstarts[e4] + r4
    row_tok = _invert(dest.reshape(t // TM_INV, 1, TM_INV * TOP_K), nb * ROW_BLOCK)
    ys = _experts((pstarts // ROW_BLOCK).astype(I32), (padded // ROW_BLOCK).astype(I32),
                  row_tok, x1r, w_gu[l], bg, bl, w_down[l], bd)
    out = _combine(dest.reshape(t // TM_COMB, 1, TM_COMB * TOP_K), x1r, g4,
                   ln2_g[l].reshape(1, d), ln2_b[l].reshape(1, d), ys)
    return out.reshape(bsz, seq, d)
```
